```python
import math
import jax
import jax.numpy as jnp
from jax import lax
import numpy as np

D_MODEL = 2048
BATCH = 4
SEQ = 4096
DEPTH = 2
DEC_BATCH = 16
DEC_SEQ = 16
PAST_LEN = 2048

CHUNK = 64
N_BRANCH = 4
BRANCH_W = D_MODEL // 4
CONV_WIDTH = 3
CONV_DIM = BRANCH_W
RET_HEADS = 4
RET_DK = BRANCH_W // RET_HEADS
RET_DV = BRANCH_W // RET_HEADS
DIFF_HEADS = 4
DIFF_D = BRANCH_W // (2 * DIFF_HEADS)
DIFF_DV = 2 * DIFF_D
MEM_TOKENS = 256
MEM_HEADS = 4
MEM_HD = BRANCH_W // MEM_HEADS
D_FF = 11 * D_MODEL // 4
REL_BUCKETS = 32
REL_MAX_DIST = 128
Q_BLOCK = 128
LN_EPS = 1e-5
ROPE_BASE = 10000.0
NEG_INF = -1e30
DEEPNORM_ALPHA = (2 * DEPTH) ** 0.25
DEEPNORM_BETA = (8 * DEPTH) ** -0.25
IN_SPLITS = (CONV_DIM, CONV_DIM, CONV_DIM,
             RET_HEADS * RET_DK, RET_HEADS * RET_DK, RET_HEADS * RET_DV, RET_HEADS * RET_DV,
             DIFF_HEADS * 2 * DIFF_D, DIFF_HEADS * 2 * DIFF_D, DIFF_HEADS * DIFF_DV,
             MEM_HEADS * MEM_HD)
IN_COLS = sum(IN_SPLITS)

kernel_name = 'hybrid_stream_encoder_step'

F32 = jnp.float32


def layer_norm(x, g, b):
    x32 = x.astype(F32)
    mu = jnp.mean(x32, -1, keepdims=True)
    var = jnp.mean(jnp.square(x32 - mu), -1, keepdims=True)
    return ((x32 - mu) * lax.rsqrt(var + LN_EPS) * g.astype(F32) + b.astype(F32)).astype(x.dtype)


def rms_norm(x, g):
    x32 = x.astype(F32)
    return x32 * lax.rsqrt(jnp.mean(jnp.square(x32), -1, keepdims=True) + LN_EPS) * g.astype(F32)


def head_norm(x, g):
    mu = jnp.mean(x, -1, keepdims=True)
    var = jnp.mean(jnp.square(x - mu), -1, keepdims=True)
    return (x - mu) * lax.rsqrt(var + LN_EPS) * g.astype(F32)


def swiglu(x, w_up, w_down):
    a, b = jnp.split(x @ w_up, 2, axis=-1)
    return (jax.nn.silu(a) * b) @ w_down


def split_cols(x, sizes):
    out, start = [], 0
    for s in sizes:
        out.append(x[..., start:start + s])
        start += s
    return out


def rope(x, pos):
    half = x.shape[-1] // 2
    inv = ROPE_BASE ** (-jnp.arange(half, dtype=F32) / half)
    ang = pos.astype(F32)[:, None] * inv[None, :]
    cos = jnp.cos(ang)[None, :, None, :]
    sin = jnp.sin(ang)[None, :, None, :]
    x1 = x[..., :half].astype(F32)
    x2 = x[..., half:].astype(F32)
    return jnp.concatenate([x1 * cos - x2 * sin, x2 * cos + x1 * sin], -1).astype(x.dtype)


def t5_bucket(rel):
    nb = REL_BUCKETS // 2
    max_exact = nb // 2
    n = jnp.abs(rel)
    nf = jnp.maximum(n, 1).astype(F32)
    large = max_exact + (jnp.log(nf / max_exact) / math.log(REL_MAX_DIST / max_exact)
                         * (nb - max_exact)).astype(jnp.int32)
    large = jnp.minimum(large, nb - 1)
    return jnp.where(rel > 0, nb, 0) + jnp.where(n < max_exact, n, large)


def short_conv_branch(b_gate, c_gate, h, conv_w, conv_prev):
    u = c_gate * h
    bsz, length, _ = u.shape
    if conv_prev is None:
        prev = jnp.zeros((bsz, CONV_WIDTH - 1, CONV_DIM), u.dtype)
    else:
        prev = conv_prev.astype(u.dtype)
    up = jnp.concatenate([prev, u], axis=1)
    z = sum(conv_w[j] * up[:, j:j + length] for j in range(CONV_WIDTH))
    return b_gate * z, up[:, length:]


def retention(q, k, v, s0):
    bsz, length, heads, _ = q.shape
    dv = v.shape[-1]
    c = CHUNK if length % CHUNK == 0 else length
    n = length // c
    log_g = jnp.log1p(-jnp.exp2(-5.0 - jnp.arange(heads, dtype=F32)))
    idx = jnp.arange(c, dtype=F32)
    rel = idx[:, None] - idx[None, :]
    intra = jnp.where(rel[None] >= 0, jnp.exp(jnp.maximum(rel, 0.0)[None] * log_g[:, None, None]), 0.0)
    q_dec = jnp.exp((idx[:, None] + 1.0) * log_g[None, :])
    k_dec = jnp.exp((c - 1.0 - idx)[:, None] * log_g[None, :])
    c_dec = jnp.exp(c * log_g)

    def to_chunks(t):
        return jnp.moveaxis(t.astype(F32).reshape(bsz, n, c, heads, t.shape[-1]), 1, 0)

    def step(s, xs):
        qc, kc, vc = xs
        att = jnp.einsum('bihd,bjhd->bhij', qc, kc) * intra
        o = (jnp.einsum('bhij,bjhe->bihe', att, vc)
             + jnp.einsum('bihd,bhde->bihe', qc, s) * q_dec[None, :, :, None])
        s = s * c_dec[:, None, None] + jnp.einsum('bjhd,bjhe->bhde', kc * k_dec[None, :, :, None], vc)
        return s, o

    s_fin, o = lax.scan(step, s0.astype(F32), (to_chunks(q), to_chunks(k), to_chunks(v)))
    return jnp.moveaxis(o, 0, 1).reshape(bsz, length, heads, dv), s_fin


def diff_attend(q, k, v, q_pos, k_pos, rel_bias, lam):
    s = jnp.einsum('bqhcd,bkhcd->bchqk', q, k).astype(F32) * (DIFF_D ** -0.5)
    bias = jnp.transpose(rel_bias[t5_bucket(k_pos[None, :] - q_pos[:, None])], (2, 0, 1)).astype(F32)
    allowed = (k_pos[None, :] // CHUNK) <= (q_pos[:, None] // CHUNK)
    s = jnp.where(allowed, s + bias, NEG_INF)
    p = jax.nn.softmax(s, axis=-1)
    a = p[:, 0] - lam * p[:, 1]
    return jnp.einsum('bhqk,bkhe->bqhe', a.astype(v.dtype), v)


def diff_attention(q, k, v, q_pos, k_pos, rel_bias, lam):
    bsz, sq = q.shape[:2]
    if sq <= Q_BLOCK or sq % Q_BLOCK:
        return diff_attend(q, k, v, q_pos, k_pos, rel_bias, lam)
    nb = sq // Q_BLOCK
    qb = jnp.moveaxis(q.reshape((bsz, nb, Q_BLOCK) + q.shape[2:]), 1, 0)
    pb = q_pos.reshape(nb, Q_BLOCK)
    ob = lax.map(lambda a: diff_attend(a[0], k, v, a[1], k_pos, rel_bias, lam), (qb, pb))
    return jnp.moveaxis(ob, 0, 1).reshape((bsz, sq) + ob.shape[3:])


def mem_attention(q, mk, mv):
    s = jnp.einsum('bqhd,bkhd->bhqk', q, mk).astype(F32) * (MEM_HD ** -0.5)
    p = jax.nn.softmax(s, axis=-1)
    return jnp.einsum('bhqk,bkhd->bqhd', p.astype(mv.dtype), mv)


def memory_kv(mem, w_mem_kv):
    k, v = jnp.split(mem @ w_mem_kv, 2, axis=-1)
    shape = mem.shape[:2] + (MEM_HEADS, MEM_HD)
    return k.reshape(shape), v.reshape(shape)


def token_mixer(h, pos, layer_idx, lw, rel_bias, conv_prev, ret_prev, k_past, v_past, past_pos, mem_k, mem_v):
    bsz, length, _ = h.shape
    (cb, cc, ch, rq, rk, rv, rg, dq, dk, dv, mq) = split_cols(h @ lw['w_in'], IN_SPLITS)
    y_a, conv_new = short_conv_branch(cb, cc, ch, lw['conv_w'], conv_prev)
    rq = rope(rq.reshape(bsz, length, RET_HEADS, RET_DK), pos)
    rk = rope(rk.reshape(bsz, length, RET_HEADS, RET_DK), pos) * (RET_DK ** -0.5)
    rv = rv.reshape(bsz, length, RET_HEADS, RET_DV)
    s0 = jnp.zeros((bsz, RET_HEADS, RET_DK, RET_DV), F32) if ret_prev is None else ret_prev
    ro, ret_new = retention(rq, rk, rv, s0)
    ro = head_norm(ro, lw['ret_gn_g'].reshape(RET_HEADS, RET_DV))
    y_b = jax.nn.silu(rg) * ro.reshape(bsz, length, RET_HEADS * RET_DV).astype(h.dtype)
    dk_rows = dk.reshape(bsz, length, DIFF_HEADS, 2 * DIFF_D)
    dv_rows = dv.reshape(bsz, length, DIFF_HEADS, DIFF_DV)
    if k_past is None:
        k_all, v_all, k_pos = dk_rows, dv_rows, pos
    else:
        k_all = jnp.concatenate([k_past.astype(dk_rows.dtype), dk_rows], axis=1)
        v_all = jnp.concatenate([v_past.astype(dv_rows.dtype), dv_rows], axis=1)
        k_pos = jnp.concatenate([past_pos, pos])
    lam_init = 0.8 - 0.6 * math.exp(-0.3 * layer_idx)
    lp = lw['diff_lambda'].astype(F32)
    lam = jnp.exp(jnp.sum(lp[0] * lp[1])) - jnp.exp(jnp.sum(lp[2] * lp[3])) + lam_init
    do = diff_attention(dq.reshape(bsz, length, DIFF_HEADS, 2, DIFF_D),
                        k_all.reshape(bsz, -1, DIFF_HEADS, 2, DIFF_D), v_all,
                        pos, k_pos, rel_bias, lam)
    do = rms_norm(do, lw['diff_subln_g']) * (1.0 - lam_init)
    y_c = do.reshape(bsz, length, DIFF_HEADS * DIFF_DV).astype(h.dtype)
    y_d = mem_attention(mq.reshape(bsz, length, MEM_HEADS, MEM_HD), mem_k, mem_v).reshape(bsz, length, BRANCH_W)
    merged = 0
    for i, y in enumerate((y_a, y_b, y_c, y_d)):
        merged = merged + jax.nn.sigmoid(h @ lw['w_gate'][i] + lw['b_gate'][i]) * (y @ lw['w_branch'][i])
    return merged @ lw['w_o'], (conv_new, ret_new, dk_rows, dv_rows)


def encoder_layer(x, pos, layer_idx, lw, rel_bias, conv_prev, ret_prev, k_past, v_past, past_pos, mem_k, mem_v):
    x = layer_norm(DEEPNORM_ALPHA * x + 0.5 * swiglu(x, lw['ffn1_w_up'], lw['ffn1_w_down']), lw['ln1_g'], lw['ln1_b'])
    mix, new_state = token_mixer(x, pos, layer_idx, lw, rel_bias, conv_prev, ret_prev, k_past, v_past, past_pos, mem_k, mem_v)
    x = layer_norm(DEEPNORM_ALPHA * x + mix, lw['ln2_g'], lw['ln2_b'])
    x = layer_norm(DEEPNORM_ALPHA * x + 0.5 * swiglu(x, lw['ffn2_w_up'], lw['ffn2_w_down']), lw['ln3_g'], lw['ln3_b'])
    return x, new_state


def setup_inputs(seed: int = 0) -> dict:
    key = jax.random.key(seed)
    ks = jax.random.split(key, 32)
    beta = DEEPNORM_BETA

    def nrm(k, shape, scale):
        return jax.random.normal(k, shape, F32) * scale

    return {
        'x_prompt': nrm(ks[0], (BATCH, SEQ, D_MODEL), 1.0),
        'x_sample': nrm(ks[1], (DEC_BATCH, DEC_SEQ, D_MODEL), 1.0),
        'state_conv': nrm(ks[2], (DEPTH, DEC_BATCH, CONV_WIDTH - 1, CONV_DIM), 1.0),
        'state_ret': nrm(ks[3], (DEPTH, DEC_BATCH, RET_HEADS, RET_DK, RET_DV), 1.0),
        'cache_diff_k': nrm(ks[4], (DEPTH, DEC_BATCH, PAST_LEN, DIFF_HEADS, 2 * DIFF_D), 1.0),
        'cache_diff_v': nrm(ks[5], (DEPTH, DEC_BATCH, PAST_LEN, DIFF_HEADS, DIFF_DV), 1.0),
        'cache_mem_k': nrm(ks[6], (DEPTH, DEC_BATCH, MEM_TOKENS, MEM_HEADS, MEM_HD), 1.0),
        'cache_mem_v': nrm(ks[7], (DEPTH, DEC_BATCH, MEM_TOKENS, MEM_HEADS, MEM_HD), 1.0),
        'mem_prompt': nrm(ks[8], (BATCH, MEM_TOKENS, D_MODEL), 1.0),
        'ffn1_w_up': nrm(ks[9], (DEPTH, D_MODEL, 2 * D_FF), beta * D_MODEL ** -0.5),
        'ffn1_w_down': nrm(ks[10], (DEPTH, D_FF, D_MODEL), beta * D_FF ** -0.5),
        'ln1_g': 1.0 + nrm(ks[11], (DEPTH, D_MODEL), 0.02),
        'ln1_b': nrm(ks[12], (DEPTH, D_MODEL), 0.02),
        'w_in': nrm(ks[13], (DEPTH, D_MODEL, IN_COLS), D_MODEL ** -0.5),
        'conv_w': nrm(ks[14], (DEPTH, CONV_WIDTH, CONV_DIM), CONV_WIDTH ** -0.5),
        'ret_gn_g': 1.0 + nrm(ks[15], (DEPTH, RET_HEADS * RET_DV), 0.02),
        'diff_lambda': nrm(ks[16], (DEPTH, 4, DIFF_D), 0.1),
        'diff_subln_g': 1.0 + nrm(ks[17], (DEPTH, DIFF_DV), 0.02),
        'w_mem_kv': nrm(ks[18], (DEPTH, D_MODEL, 2 * MEM_HEADS * MEM_HD), D_MODEL ** -0.5),
        'w_branch': nrm(ks[19], (DEPTH, N_BRANCH, BRANCH_W, D_MODEL), beta * BRANCH_W ** -0.5),
        'w_gate': nrm(ks[20], (DEPTH, N_BRANCH, D_MODEL, D_MODEL), D_MODEL ** -0.5),
        'b_gate': nrm(ks[21], (DEPTH, N_BRANCH, D_MODEL), 0.02),
        'w_o': nrm(ks[22], (DEPTH, D_MODEL, D_MODEL), beta * D_MODEL ** -0.5),
        'ln2_g': 1.0 + nrm(ks[23], (DEPTH, D_MODEL), 0.02),
        'ln2_b': nrm(ks[24], (DEPTH, D_MODEL), 0.02),
        'ffn2_w_up': nrm(ks[25], (DEPTH, D_MODEL, 2 * D_FF), beta * D_MODEL ** -0.5),
        'ffn2_w_down': nrm(ks[26], (DEPTH, D_FF, D_MODEL), beta * D_FF ** -0.5),
        'ln3_g': 1.0 + nrm(ks[27], (DEPTH, D_MODEL), 0.02),
        'ln3_b': nrm(ks[28], (DEPTH, D_MODEL), 0.02),
        'rel_bias': nrm(ks[29], (REL_BUCKETS, DIFF_HEADS), 0.5),
    }


def reference(x_prompt, x_sample, state_conv, state_ret, cache_diff_k, cache_diff_v, cache_mem_k, cache_mem_v,
              mem_prompt, ffn1_w_up, ffn1_w_down, ln1_g, ln1_b, w_in, conv_w, ret_gn_g, diff_lambda,
              diff_subln_g, w_mem_kv, w_branch, w_gate, b_gate, w_o, ln2_g, ln2_b, ffn2_w_up, ffn2_w_down,
              ln3_g, ln3_b, rel_bias):
    pos_p = jnp.arange(SEQ, dtype=jnp.int32)
    pos_s = PAST_LEN + jnp.arange(DEC_SEQ, dtype=jnp.int32)
    past_pos = jnp.arange(PAST_LEN, dtype=jnp.int32)
    yp, ys = x_prompt, x_sample
    conv_p, ret_p, dk_p, dv_p, mk_p, mv_p = [], [], [], [], [], []
    conv_s, ret_s, dk_s, dv_s = [], [], [], []
    for l in range(DEPTH):
        lw = {
            'ffn1_w_up': ffn1_w_up[l], 'ffn1_w_down': ffn1_w_down[l], 'ln1_g': ln1_g[l], 'ln1_b': ln1_b[l],
            'w_in': w_in[l], 'conv_w': conv_w[l], 'ret_gn_g': ret_gn_g[l], 'diff_lambda': diff_lambda[l],
            'diff_subln_g': diff_subln_g[l], 'w_branch': w_branch[l], 'w_gate': w_gate[l], 'b_gate': b_gate[l],
            'w_o': w_o[l], 'ln2_g': ln2_g[l], 'ln2_b': ln2_b[l], 'ffn2_w_up': ffn2_w_up[l],
            'ffn2_w_down': ffn2_w_down[l], 'ln3_g': ln3_g[l], 'ln3_b': ln3_b[l],
        }
        mk, mv = memory_kv(mem_prompt, w_mem_kv[l])
        yp, (c_new, r_new, k_new, v_new) = encoder_layer(yp, pos_p, l, lw, rel_bias, None, None, None, None,
                                                         None, mk, mv)
        conv_p.append(c_new)
        ret_p.append(r_new)
        dk_p.append(k_new)
        dv_p.append(v_new)
        mk_p.append(mk)
        mv_p.append(mv)
        ys, (c_new, r_new, k_new, v_new) = encoder_layer(ys, pos_s, l, lw, rel_bias, state_conv[l], state_ret[l],
                                                         cache_diff_k[l], cache_diff_v[l], past_pos,
                                                         cache_mem_k[l], cache_mem_v[l])
        conv_s.append(c_new)
        ret_s.append(r_new)
        dk_s.append(k_new)
        dv_s.append(v_new)
    return (yp, ys, jnp.stack(conv_p), jnp.stack(ret_p), jnp.stack(dk_p), jnp.stack(dv_p), jnp.stack(mk_p),
            jnp.stack(mv_p), jnp.stack(conv_s), jnp.stack(ret_s), jnp.stack(dk_s), jnp.stack(dv_s))
```

```python
import functools
import math

import jax
import jax.numpy as jnp
from jax import lax
from jax.experimental import pallas as pl
from jax.experimental.pallas import tpu as pltpu

F32 = jnp.float32
BF16 = jnp.bfloat16

CHUNK = 64
CONV_WIDTH = 3
RET_HEADS = 4
RET_DK = 128
DIFF_HEADS = 4
DIFF_D = 64
DIFF_DV = 128
MEM_HEADS = 4
MEM_HD = 128
BRANCH_W = 512
N_BRANCH = 4
REL_BUCKETS = 32
REL_MAX_DIST = 128
LN_EPS = 1e-5
ROPE_BASE = 10000.0
NEG_INF = -1e30
HEAD_W = 128

COL_CB, COL_CC, COL_CH, COL_RQ, COL_RK, COL_RV, COL_RG, COL_DQ, COL_DK, COL_DV, COL_MQ = range(11)
HEADS_PER_BRANCH = BRANCH_W // HEAD_W

V7X_VMEM_BYTES = 64 * 1024 * 1024
MIB = 1024 * 1024


def _params(semantics, vmem_mib):
    assert vmem_mib * MIB < V7X_VMEM_BYTES
    return pltpu.CompilerParams(dimension_semantics=semantics, vmem_limit_bytes=vmem_mib * MIB)


def _layer_norm(y, g, b):
    mu = jnp.mean(y, -1, keepdims=True)
    d = y - mu
    var = jnp.mean(d * d, -1, keepdims=True)
    return d * lax.rsqrt(var + LN_EPS) * g + b


def _dot(a, b):
    return jnp.dot(a, b, preferred_element_type=F32)


def _dot_nt(a, b):
    return lax.dot_general(a, b, (((1,), (1,)), ((), ())), preferred_element_type=F32)


def _dot_tn(a, b):
    return lax.dot_general(a, b, (((0,), (0,)), ((), ())), preferred_element_type=F32)


def _ffn_ln_kernel(x_ref, wa_ref, wb_ref, wd_ref, g_ref, b_ref, o32_ref, o16_ref, xb_ref, acc_ref,
                   *, alpha):
    j = pl.program_id(1)

    @pl.when(j == 0)
    def _():
        xb_ref[...] = x_ref[...].astype(BF16)

    xb = xb_ref[...]
    a = _dot(xb, wa_ref[...])
    b = _dot(xb, wb_ref[...])
    h = (a * jax.nn.sigmoid(a) * b).astype(BF16)
    d = _dot(h, wd_ref[...])

    @pl.when(j == 0)
    def _():
        acc_ref[...] = d

    @pl.when(j > 0)
    def _():
        acc_ref[...] += d

    @pl.when(j == pl.num_programs(1) - 1)
    def _():
        y = _layer_norm(alpha * x_ref[...] + 0.5 * acc_ref[...], g_ref[...], b_ref[...])
        o32_ref[...] = y
        o16_ref[...] = y.astype(BF16)


def _ffn_ln(x, w_up, w_down, g, b, *, alpha, tm, tf):
    m, d = x.shape
    d_ff = w_down.shape[0]
    nj = d_ff // tf
    assert m % tm == 0 and d_ff % tf == 0
    row = lambda i, j: (i, 0)
    return pl.pallas_call(
        functools.partial(_ffn_ln_kernel, alpha=alpha),
        grid=(m // tm, nj),
        in_specs=[
            pl.BlockSpec((tm, d), row),
            pl.BlockSpec((d, tf), lambda i, j: (0, j)),
            pl.BlockSpec((d, tf), lambda i, j: (0, j + nj)),
            pl.BlockSpec((tf, d), lambda i, j: (j, 0)),
            pl.BlockSpec((1, d), lambda i, j: (0, 0)),
            pl.BlockSpec((1, d), lambda i, j: (0, 0)),
        ],
        out_specs=[pl.BlockSpec((tm, d), row), pl.BlockSpec((tm, d), row)],
        out_shape=[jax.ShapeDtypeStruct((m, d), F32), jax.ShapeDtypeStruct((m, d), BF16)],
        scratch_shapes=[pltpu.VMEM((tm, d), BF16), pltpu.VMEM((tm, d), F32)],
        compiler_params=_params(("parallel", "arbitrary"), 56),
        name="ffn_ln",
    )(x, w_up, w_up, w_down, g.reshape(1, d), b.reshape(1, d))


def _matmul_kernel(x_ref, w_ref, o_ref):
    o_ref[...] = _dot(x_ref[...], w_ref[...])


def _matmul(x, w, *, tm, tn):
    m, k = x.shape
    n = w.shape[1]
    assert m % tm == 0 and n % tn == 0
    return pl.pallas_call(
        _matmul_kernel,
        grid=(m // tm, n // tn),
        in_specs=[pl.BlockSpec((tm, k), lambda i, j: (i, 0)), pl.BlockSpec((k, tn), lambda i, j: (0, j))],
        out_specs=pl.BlockSpec((tm, tn), lambda i, j: (i, j)),
        out_shape=jax.ShapeDtypeStruct((m, n), F32),
        compiler_params=_params(("parallel", "parallel"), 48),
        name="matmul",
    )(x, w)


def _conv_kernel(cb_ref, cc_ref, ch_ref, w_ref, prev_ref, y_ref, state_ref, carry_ref):
    l = pl.program_id(1)

    @pl.when(l == 0)
    def _():
        carry_ref[...] = prev_ref[...]

    u = cc_ref[...] * ch_ref[...]
    row = lax.broadcasted_iota(jnp.int32, u.shape, 0)
    c0 = carry_ref[0:1, :]
    c1 = carry_ref[1:2, :]
    u1 = jnp.where(row == 0, c1, pltpu.roll(u, 1, 0))
    u2 = jnp.where(row == 0, c0, jnp.where(row == 1, c1, pltpu.roll(u, 2, 0)))
    z = w_ref[0:1, :] * u2 + w_ref[1:2, :] * u1 + w_ref[2:3, :] * u
    y_ref[...] = (cb_ref[...] * z).astype(BF16)
    tl = u.shape[0]
    tail = u[tl - 8:, :]
    carry_ref[...] = tail[6:8, :]

    @pl.when(l == pl.num_programs(1) - 1)
    def _():
        state_ref[...] = tail[6:8, :]


def _conv_branch(proj, conv_w, prev, *, bsz, length, tl):
    m = proj.shape[0]
    nl = length // tl
    assert length % tl == 0 and tl >= 8
    col = lambda c: pl.BlockSpec((tl, BRANCH_W), lambda b, l: (b * nl + l, c))
    return pl.pallas_call(
        _conv_kernel,
        grid=(bsz, nl),
        in_specs=[
            col(COL_CB), col(COL_CC), col(COL_CH),
            pl.BlockSpec((CONV_WIDTH, BRANCH_W), lambda b, l: (0, 0)),
            pl.BlockSpec((None, CONV_WIDTH - 1, BRANCH_W), lambda b, l: (b, 0, 0)),
        ],
        out_specs=[
            pl.BlockSpec((tl, BRANCH_W), lambda b, l: (b * nl + l, 0)),
            pl.BlockSpec((None, CONV_WIDTH - 1, BRANCH_W), lambda b, l: (b, 0, 0)),
        ],
        out_shape=[
            jax.ShapeDtypeStruct((m, BRANCH_W), BF16),
            jax.ShapeDtypeStruct((bsz, CONV_WIDTH - 1, BRANCH_W), F32),
        ],
        scratch_shapes=[pltpu.VMEM((CONV_WIDTH - 1, BRANCH_W), F32)],
        compiler_params=_params(("parallel", "arbitrary"), 32),
        name="conv_branch",
    )(proj, proj, proj, conv_w, prev)


def _ret_log_gamma(h):
    return math.log1p(-(2.0 ** (-5.0 - h)))


def _retention_kernel(q_ref, k_ref, v_ref, g_ref, cos_ref, sin_ref, s0_ref, gn_ref, y_ref, sfin_ref, s_ref):
    c = pl.program_id(1)

    @pl.when(c == 0)
    def _():
        s_ref[...] = s0_ref[...]

    chunk = q_ref.shape[0]
    cosf = cos_ref[...]
    sinf = sin_ref[...]
    ri = lax.broadcasted_iota(jnp.int32, (chunk, chunk), 0)
    ci = lax.broadcasted_iota(jnp.int32, (chunk, chunk), 1)
    rel = (ri - ci).astype(F32)
    idx = lax.broadcasted_iota(jnp.int32, (chunk, 1), 0).astype(F32)
    for h in range(RET_HEADS):
        log_g = _ret_log_gamma(h)
        cols = slice(h * RET_DK, (h + 1) * RET_DK)
        q = q_ref[:, cols]
        k = k_ref[:, cols]
        q = q * cosf + pltpu.roll(q, RET_DK // 2, 1) * sinf
        k = (k * cosf + pltpu.roll(k, RET_DK // 2, 1) * sinf) * (RET_DK ** -0.5)
        vb = v_ref[:, cols].astype(BF16)
        intra = jnp.where(rel >= 0.0, jnp.exp(jnp.maximum(rel, 0.0) * log_g), 0.0)
        q_dec = jnp.exp((idx + 1.0) * log_g)
        k_dec = jnp.exp((chunk - 1.0 - idx) * log_g)
        c_dec = math.exp(chunk * log_g)
        qb = q.astype(BF16)
        s_prev = s_ref[h]
        att = _dot_nt(qb, k.astype(BF16)) * intra
        o = _dot(att.astype(BF16), vb) + _dot(qb, s_prev.astype(BF16)) * q_dec
        s_ref[h] = s_prev * c_dec + _dot_tn((k * k_dec).astype(BF16), vb)
        mu = jnp.mean(o, -1, keepdims=True)
        d = o - mu
        var = jnp.mean(d * d, -1, keepdims=True)
        ro = d * lax.rsqrt(var + LN_EPS) * gn_ref[:, cols]
        gate = g_ref[:, cols]
        y_ref[:, cols] = (gate * jax.nn.sigmoid(gate) * ro.astype(F32)).astype(BF16)

    @pl.when(c == pl.num_programs(1) - 1)
    def _():
        sfin_ref[...] = s_ref[...]


def _retention_branch(proj, cosf, sinf, s0, gn_g, *, bsz, length, chunk):
    m = proj.shape[0]
    nc = length // chunk
    assert length % chunk == 0
    col = lambda c: pl.BlockSpec((chunk, BRANCH_W), lambda b, i: (b * nc + i, c))
    state = pl.BlockSpec((None, RET_HEADS, RET_DK, RET_DK), lambda b, i: (b, 0, 0, 0))
    return pl.pallas_call(
        _retention_kernel,
        grid=(bsz, nc),
        in_specs=[
            col(COL_RQ), col(COL_RK), col(COL_RV), col(COL_RG),
            pl.BlockSpec((chunk, RET_DK), lambda b, i: (i, 0)),
            pl.BlockSpec((chunk, RET_DK), lambda b, i: (i, 0)),
            state,
            pl.BlockSpec((1, BRANCH_W), lambda b, i: (0, 0)),
        ],
        out_specs=[pl.BlockSpec((chunk, BRANCH_W), lambda b, i: (b * nc + i, 0)), state],
        out_shape=[
            jax.ShapeDtypeStruct((m, BRANCH_W), BF16),
            jax.ShapeDtypeStruct((bsz, RET_HEADS, RET_DK, RET_DK), F32),
        ],
        scratch_shapes=[pltpu.VMEM((RET_HEADS, RET_DK, RET_DK), F32)],
        compiler_params=_params(("parallel", "arbitrary"), 32),
        name="retention_branch",
    )(proj, proj, proj, proj, cosf, sinf, s0, gn_g.reshape(1, BRANCH_W))


def _split_q(q):
    lane = lax.broadcasted_iota(jnp.int32, q.shape, 1)
    q = q * (DIFF_D ** -0.5)
    return (jnp.where(lane < DIFF_D, q, 0.0).astype(BF16), jnp.where(lane >= DIFF_D, q, 0.0).astype(BF16))


def _diff_lambda(lam_ref, lam_init):
    e0 = jnp.exp(jnp.sum(lam_ref[0:1, :] * lam_ref[1:2, :], -1, keepdims=True))
    e1 = jnp.exp(jnp.sum(lam_ref[2:3, :] * lam_ref[3:4, :], -1, keepdims=True))
    return e0 - e1 + lam_init


def _diff_finish(o0, o1, lam, g, lam_init):
    o = o0 - lam * o1
    o = o * lax.rsqrt(jnp.mean(o * o, -1, keepdims=True) + LN_EPS) * g
    return (o * (1.0 - lam_init)).astype(BF16)


def _diff_tile(qs, kt, vt, bias, m_ref, l_ref, acc_ref):
    for c, qc in enumerate(qs):
        s = _dot_nt(qc, kt) + bias
        m_prev = m_ref[c]
        m_new = jnp.maximum(m_prev, jnp.max(s, -1, keepdims=True))
        p = jnp.exp(s - m_new)
        a = jnp.exp(m_prev - m_new)
        l_ref[c] = a * l_ref[c] + jnp.sum(p, -1, keepdims=True)
        acc_ref[c] = a * acc_ref[c] + _dot(p.astype(BF16), vt)
        m_ref[c] = m_new


def _diff_prompt_kernel(q_ref, k_ref, v_ref, bias_ref, lam_ref, g_ref, o_ref, m_ref, l_ref, acc_ref,
                        *, lam_init):
    i = pl.program_id(2)
    t = q_ref.shape[0]
    qs = _split_q(q_ref[...])
    m_ref[...] = jnp.full(m_ref.shape, NEG_INF, F32)
    l_ref[...] = jnp.zeros(l_ref.shape, F32)
    acc_ref[...] = jnp.zeros(acc_ref.shape, F32)

    def tile(j, bias):
        rows = pl.ds(pl.multiple_of(j * t, t), t)
        _diff_tile(qs, k_ref[rows, :].astype(BF16), v_ref[rows, :].astype(BF16), bias, m_ref, l_ref, acc_ref)

    def far_body(j, carry):
        tile(j, bias_ref[2])
        return carry

    lax.fori_loop(0, jnp.maximum(i - 1, 0), far_body, 0)

    @pl.when(i >= 1)
    def _():
        tile(i - 1, bias_ref[1])

    tile(i, bias_ref[0])
    lam = _diff_lambda(lam_ref, lam_init)
    o_ref[...] = _diff_finish(acc_ref[0] / l_ref[0], acc_ref[1] / l_ref[1], lam, g_ref[...], lam_init)


def _diff_prompt(proj, bias_tiles, diff_lambda, subln_g, *, bsz, length, t, lam_init):
    m = proj.shape[0]
    nq = length // t
    assert length % t == 0 and t % CHUNK == 0
    kv = lambda c: pl.BlockSpec((length, HEAD_W), lambda b, h, i: (b, c * HEADS_PER_BRANCH + h))
    return pl.pallas_call(
        functools.partial(_diff_prompt_kernel, lam_init=lam_init),
        grid=(bsz, DIFF_HEADS, nq),
        in_specs=[
            pl.BlockSpec((t, HEAD_W), lambda b, h, i: (b * nq + i, COL_DQ * HEADS_PER_BRANCH + h)),
            kv(COL_DK), kv(COL_DV),
            pl.BlockSpec((None, 3, t, t), lambda b, h, i: (h, 0, 0, 0)),
            pl.BlockSpec((4, DIFF_D), lambda b, h, i: (0, 0)),
            pl.BlockSpec((1, DIFF_DV), lambda b, h, i: (0, 0)),
        ],
        out_specs=pl.BlockSpec((t, HEAD_W), lambda b, h, i: (b * nq + i, h)),
        out_shape=jax.ShapeDtypeStruct((m, BRANCH_W), BF16),
        scratch_shapes=[
            pltpu.VMEM((2, t, 1), F32), pltpu.VMEM((2, t, 1), F32), pltpu.VMEM((2, t, DIFF_DV), F32),
        ],
        compiler_params=_params(("parallel", "parallel", "arbitrary"), 32),
        name="diff_attention_prompt",
    )(proj, proj, proj, bias_tiles, diff_lambda, subln_g.reshape(1, DIFF_DV))


def _diff_sample_kernel(q_ref, kn_ref, vn_ref, kp_ref, vp_ref, bp_ref, bn_ref, lam_ref, g_ref, o_ref,
                        *, lam_init):
    qs = _split_q(q_ref[...])
    kp = kp_ref[...].astype(BF16)
    vp = vp_ref[...].astype(BF16)
    kn = kn_ref[...].astype(BF16)
    vn = vn_ref[...].astype(BF16)
    outs = []
    for qc in qs:
        sp = _dot_nt(qc, kp) + bp_ref[...]
        sn = _dot_nt(qc, kn) + bn_ref[...]
        mx = jnp.maximum(jnp.max(sp, -1, keepdims=True), jnp.max(sn, -1, keepdims=True))
        pp = jnp.exp(sp - mx)
        pn = jnp.exp(sn - mx)
        den = jnp.sum(pp, -1, keepdims=True) + jnp.sum(pn, -1, keepdims=True)
        outs.append((_dot(pp.astype(BF16), vp) + _dot(pn.astype(BF16), vn)) / den)
    lam = _diff_lambda(lam_ref, lam_init)
    o_ref[...] = _diff_finish(outs[0], outs[1], lam, g_ref[...], lam_init)


def _diff_sample(proj, k_past, v_past, bias_past, bias_new, diff_lambda, subln_g, *, bsz, length, lam_init):
    m = proj.shape[0]
    past = k_past.shape[1]
    new = lambda c: pl.BlockSpec((length, HEAD_W), lambda b, h: (b, c * HEADS_PER_BRANCH + h))
    old = pl.BlockSpec((None, past, HEAD_W), lambda b, h: (b, 0, h))
    return pl.pallas_call(
        functools.partial(_diff_sample_kernel, lam_init=lam_init),
        grid=(bsz, DIFF_HEADS),
        in_specs=[
            new(COL_DQ), new(COL_DK), new(COL_DV), old, old,
            pl.BlockSpec((None, length, past), lambda b, h: (h, 0, 0)),
            pl.BlockSpec((None, length, length), lambda b, h: (h, 0, 0)),
            pl.BlockSpec((4, DIFF_D), lambda b, h: (0, 0)),
            pl.BlockSpec((1, DIFF_DV), lambda b, h: (0, 0)),
        ],
        out_specs=pl.BlockSpec((length, HEAD_W), lambda b, h: (b, h)),
        out_shape=jax.ShapeDtypeStruct((m, BRANCH_W), BF16),
        compiler_params=_params(("parallel", "parallel"), 32),
        name="diff_attention_sample",
    )(proj, proj, proj, k_past, v_past, bias_past, bias_new, diff_lambda, subln_g.reshape(1, DIFF_DV))


def _mem_attn_kernel(q_ref, k_ref, v_ref, o_ref):
    s = _dot_nt(q_ref[...].astype(BF16), k_ref[...].astype(BF16)) * (MEM_HD ** -0.5)
    p = jnp.exp(s - jnp.max(s, -1, keepdims=True))
    den = jnp.sum(p, -1, keepdims=True)
    o_ref[...] = (_dot(p.astype(BF16), v_ref[...].astype(BF16)) / den).astype(BF16)


def _mem_attention(proj, mem_k, mem_v, *, bsz, length, tq):
    m = proj.shape[0]
    nq = length // tq
    assert length % tq == 0
    tokens = mem_k.shape[1]
    kv = pl.BlockSpec((None, tokens, HEAD_W), lambda b, h, i: (b, 0, h))
    return pl.pallas_call(
        _mem_attn_kernel,
        grid=(bsz, MEM_HEADS, nq),
        in_specs=[pl.BlockSpec((tq, HEAD_W), lambda b, h, i: (b * nq + i, COL_MQ * HEADS_PER_BRANCH + h)), kv, kv],
        out_specs=pl.BlockSpec((tq, HEAD_W), lambda b, h, i: (b * nq + i, h)),
        out_shape=jax.ShapeDtypeStruct((m, BRANCH_W), BF16),
        compiler_params=_params(("parallel", "parallel", "parallel"), 32),
        name="mem_attention",
    )(proj, mem_k, mem_v)


def _merge_kernel(h_ref, ya_ref, yb_ref, yc_ref, yd_ref, wg_ref, wb_ref, bg_ref, o_ref):
    h = h_ref[...]
    acc = None
    for i, y_ref in enumerate((ya_ref, yb_ref, yc_ref, yd_ref)):
        gate = jax.nn.sigmoid(_dot(h, wg_ref[i]) + bg_ref[i])
        term = gate * _dot(y_ref[...], wb_ref[i])
        acc = term if acc is None else acc + term
    o_ref[...] = acc.astype(BF16)


def _merge(h16, ys, w_gate, w_branch, b_gate, *, tm, tn):
    m, d = h16.shape
    assert m % tm == 0 and d % tn == 0
    y_spec = pl.BlockSpec((tm, BRANCH_W), lambda i, j: (i, 0))
    return pl.pallas_call(
        _merge_kernel,
        grid=(m // tm, d // tn),
        in_specs=[
            pl.BlockSpec((tm, d), lambda i, j: (i, 0)), y_spec, y_spec, y_spec, y_spec,
            pl.BlockSpec((N_BRANCH, d, tn), lambda i, j: (0, 0, j)),
            pl.BlockSpec((N_BRANCH, BRANCH_W, tn), lambda i, j: (0, 0, j)),
            pl.BlockSpec((N_BRANCH, 1, tn), lambda i, j: (0, 0, j)),
        ],
        out_specs=pl.BlockSpec((tm, tn), lambda i, j: (i, j)),
        out_shape=jax.ShapeDtypeStruct((m, d), BF16),
        compiler_params=_params(("parallel", "parallel"), 48),
        name="gated_merge",
    )(h16, *ys, w_gate, w_branch, b_gate.reshape(N_BRANCH, 1, d))


def _proj_ln_kernel(m_ref, x_ref, w_ref, g_ref, b_ref, o_ref, *, alpha):
    y = alpha * x_ref[...] + _dot(m_ref[...], w_ref[...])
    o_ref[...] = _layer_norm(y, g_ref[...], b_ref[...])


def _proj_ln(merged, x, w_o, g, b, *, alpha, tm):
    m, d = x.shape
    assert m % tm == 0
    row = pl.BlockSpec((tm, d), lambda i: (i, 0))
    vec = pl.BlockSpec((1, d), lambda i: (0, 0))
    return pl.pallas_call(
        functools.partial(_proj_ln_kernel, alpha=alpha),
        grid=(m // tm,),
        in_specs=[row, row, pl.BlockSpec((d, d), lambda i: (0, 0)), vec, vec],
        out_specs=row,
        out_shape=jax.ShapeDtypeStruct((m, d), F32),
        compiler_params=_params(("parallel",), 48),
        name="proj_ln",
    )(merged, x, w_o, g.reshape(1, d), b.reshape(1, d))


def _rope_tables(pos):
    half = RET_DK // 2
    inv = ROPE_BASE ** (-jnp.arange(half, dtype=F32) / half)
    ang = pos.astype(F32)[:, None] * inv[None, :]
    cos = jnp.cos(ang)
    sin = jnp.sin(ang)
    return jnp.concatenate([cos, cos], -1), jnp.concatenate([-sin, sin], -1)


def _t5_bucket(rel):
    nb = REL_BUCKETS // 2
    max_exact = nb // 2
    n = jnp.abs(rel)
    nf = jnp.maximum(n, 1).astype(F32)
    large = max_exact + (jnp.log(nf / max_exact) / math.log(REL_MAX_DIST / max_exact)
                         * (nb - max_exact)).astype(jnp.int32)
    large = jnp.minimum(large, nb - 1)
    return jnp.where(rel > 0, nb, 0) + jnp.where(n < max_exact, n, large)


def _masked_bias(q_pos, k_pos, rel_bias):
    bias = jnp.transpose(rel_bias[_t5_bucket(k_pos[None, :] - q_pos[:, None])], (2, 0, 1)).astype(F32)
    allowed = (k_pos[None, :] // CHUNK) <= (q_pos[:, None] // CHUNK)
    return jnp.where(allowed[None], bias, NEG_INF)


def _prompt_bias_tiles(rel_bias, t):
    assert t % CHUNK == 0 and t + 1 >= REL_MAX_DIST
    base = jnp.arange(t, dtype=jnp.int32)
    tiles = [_masked_bias(base + d * t, base, rel_bias) for d in range(3)]
    return jnp.stack(tiles, axis=1)


def _encoder_layer(x, lw, *, bsz, length, layer_idx, alpha, cfg, rope, diff_fn, conv_prev, ret_prev, mem_k, mem_v):
    x1, x1b = _ffn_ln(x, lw['ffn1_w_up'], lw['ffn1_w_down'], lw['ln1_g'], lw['ln1_b'],
                      alpha=alpha, tm=cfg['tm_ffn'], tf=cfg['tf'])
    proj = _matmul(x1b, lw['w_in'], tm=cfg['tm_in'], tn=cfg['tn_in'])
    y_a, conv_new = _conv_branch(proj, lw['conv_w'], conv_prev, bsz=bsz, length=length, tl=cfg['t_conv'])
    y_b, ret_new = _retention_branch(proj, rope[0], rope[1], ret_prev, lw['ret_gn_g'],
                                     bsz=bsz, length=length, chunk=cfg['ret_chunk'])
    lam_init = 0.8 - 0.6 * math.exp(-0.3 * layer_idx)
    y_c = diff_fn(proj, lw['diff_lambda'], lw['diff_subln_g'], lam_init)
    y_d = _mem_attention(proj, mem_k, mem_v, bsz=bsz, length=length, tq=cfg['tq_mem'])
    merged = _merge(x1b, (y_a, y_b, y_c, y_d), lw['w_gate'], lw['w_branch'], lw['b_gate'],
                    tm=cfg['tm_merge'], tn=cfg['tn_merge'])
    x2 = _proj_ln(merged, x1, lw['w_o'], lw['ln2_g'], lw['ln2_b'], alpha=alpha, tm=cfg['tm_proj'])
    x3, _ = _ffn_ln(x2, lw['ffn2_w_up'], lw['ffn2_w_down'], lw['ln3_g'], lw['ln3_b'],
                    alpha=alpha, tm=cfg['tm_ffn'], tf=cfg['tf'])
    m = bsz * length
    dk = proj[:, COL_DK * BRANCH_W:(COL_DK + 1) * BRANCH_W].reshape(bsz, length, DIFF_HEADS, 2 * DIFF_D)
    dv = proj[:, COL_DV * BRANCH_W:(COL_DV + 1) * BRANCH_W].reshape(bsz, length, DIFF_HEADS, DIFF_DV)
    del m
    return x3, (conv_new, ret_new, dk, dv)


PROMPT_CFG = dict(tm_ffn=512, tf=512, tm_in=1024, tn_in=512, t_conv=512, ret_chunk=256, t_diff=256,
                  tq_mem=1024, tm_merge=1024, tn_merge=256, tm_proj=512)
SAMPLE_CFG = dict(tm_ffn=256, tf=512, tm_in=256, tn_in=512, t_conv=16, ret_chunk=16,
                  tq_mem=16, tm_merge=256, tn_merge=256, tm_proj=256)


def kernel(x_prompt, x_sample, state_conv, state_ret, cache_diff_k, cache_diff_v, cache_mem_k, cache_mem_v,
           mem_prompt, ffn1_w_up, ffn1_w_down, ln1_g, ln1_b, w_in, conv_w, ret_gn_g, diff_lambda,
           diff_subln_g, w_mem_kv, w_branch, w_gate, b_gate, w_o, ln2_g, ln2_b, ffn2_w_up, ffn2_w_down,
           ln3_g, ln3_b, rel_bias):
    bp, lp, d = x_prompt.shape
    bs, ls, _ = x_sample.shape
    depth = w_in.shape[0]
    past = cache_diff_k.shape[2]
    mem_tokens = mem_prompt.shape[1]
    alpha = (2 * depth) ** 0.25

    pos_p = jnp.arange(lp, dtype=jnp.int32)
    pos_s = past + jnp.arange(ls, dtype=jnp.int32)
    past_pos = jnp.arange(past, dtype=jnp.int32)
    rope_p = _rope_tables(pos_p)
    rope_s = _rope_tables(pos_s)
    bias_p = _prompt_bias_tiles(rel_bias, PROMPT_CFG['t_diff'])
    bias_s_past = _masked_bias(pos_s, past_pos, rel_bias)
    bias_s_new = _masked_bias(pos_s, pos_s, rel_bias)

    yp = x_prompt.reshape(bp * lp, d)
    ys = x_sample.reshape(bs * ls, d)
    mem16 = mem_prompt.reshape(bp * mem_tokens, d).astype(BF16)
    zero_conv = jnp.zeros((bp, CONV_WIDTH - 1, BRANCH_W), F32)
    zero_ret = jnp.zeros((bp, RET_HEADS, RET_DK, RET_DK), F32)

    outs_p = [[] for _ in range(6)]
    outs_s = [[] for _ in range(4)]
    for l in range(depth):
        lw = {
            'ffn1_w_up': ffn1_w_up[l].astype(BF16), 'ffn1_w_down': ffn1_w_down[l].astype(BF16),
            'ln1_g': ln1_g[l], 'ln1_b': ln1_b[l], 'w_in': w_in[l].astype(BF16), 'conv_w': conv_w[l],
            'ret_gn_g': ret_gn_g[l], 'diff_lambda': diff_lambda[l], 'diff_subln_g': diff_subln_g[l],
            'w_branch': w_branch[l].astype(BF16), 'w_gate': w_gate[l].astype(BF16), 'b_gate': b_gate[l],
            'w_o': w_o[l].astype(BF16), 'ln2_g': ln2_g[l], 'ln2_b': ln2_b[l],
            'ffn2_w_up': ffn2_w_up[l].astype(BF16), 'ffn2_w_down': ffn2_w_down[l].astype(BF16),
            'ln3_g': ln3_g[l], 'ln3_b': ln3_b[l],
        }
        mkv = _matmul(mem16, w_mem_kv[l].astype(BF16), tm=bp * mem_tokens, tn=512)
        half = MEM_HEADS * MEM_HD
        mk = mkv[:, :half].reshape(bp, mem_tokens, half)
        mv = mkv[:, half:].reshape(bp, mem_tokens, half)
        diff_p = lambda proj, lam_p, g, lam_init: _diff_prompt(
            proj, bias_p, lam_p, g, bsz=bp, length=lp, t=PROMPT_CFG['t_diff'], lam_init=lam_init)
        yp, (c_new, r_new, k_new, v_new) = _encoder_layer(
            yp, lw, bsz=bp, length=lp, layer_idx=l, alpha=alpha, cfg=PROMPT_CFG, rope=rope_p, diff_fn=diff_p,
            conv_prev=zero_conv, ret_prev=zero_ret, mem_k=mk, mem_v=mv)
        for acc, val in zip(outs_p, (c_new, r_new, k_new, v_new,
                                     mk.reshape(bp, mem_tokens, MEM_HEADS, MEM_HD),
                                     mv.reshape(bp, mem_tokens, MEM_HEADS, MEM_HD))):
            acc.append(val)
        kpast = cache_diff_k[l].reshape(bs, past, DIFF_HEADS * 2 * DIFF_D)
        vpast = cache_diff_v[l].reshape(bs, past, DIFF_HEADS * DIFF_DV)
        diff_s = lambda proj, lam_p, g, lam_init: _diff_sample(
            proj, kpast, vpast, bias_s_past, bias_s_new, lam_p, g, bsz=bs, length=ls, lam_init=lam_init)
        ys, (c_new, r_new, k_new, v_new) = _encoder_layer(
            ys, lw, bsz=bs, length=ls, layer_idx=l, alpha=alpha, cfg=SAMPLE_CFG, rope=rope_s, diff_fn=diff_s,
            conv_prev=state_conv[l], ret_prev=state_ret[l],
            mem_k=cache_mem_k[l].reshape(bs, mem_tokens, half), mem_v=cache_mem_v[l].reshape(bs, mem_tokens, half))
        for acc, val in zip(outs_s, (c_new, r_new, k_new, v_new)):
            acc.append(val)

    return (yp.reshape(bp, lp, d), ys.reshape(bs, ls, d),
            *(jnp.stack(v) for v in outs_p), *(jnp.stack(v) for v in outs_s))
```

```python
import functools
import math

import jax
import jax.numpy as jnp
from jax import lax
from jax.experimental import pallas as pl
from jax.experimental.pallas import tpu as pltpu

F32 = jnp.float32
BF16 = jnp.bfloat16

CHUNK = 64
CONV_WIDTH = 3
RET_HEADS = 4
RET_DK = 128
DIFF_HEADS = 4
DIFF_D = 64
DIFF_DV = 128
MEM_HEADS = 4
MEM_HD = 128
BRANCH_W = 512
N_BRANCH = 4
REL_BUCKETS = 32
REL_MAX_DIST = 128
LN_EPS = 1e-5
ROPE_BASE = 10000.0
NEG_INF = -1e30
HEAD_W = 128
LOG2E = math.log2(math.e)
LN_ROW_CHUNK = 128
FFN_UP_CHUNK = 256
FFN_DOWN_CHUNK = 512

COL_CB, COL_CC, COL_CH, COL_RQ, COL_RK, COL_RV, COL_RG, COL_DQ, COL_DK, COL_DV, COL_MQ = range(11)
HEADS_PER_BRANCH = BRANCH_W // HEAD_W

V7X_VMEM_BYTES = 64 * 1024 * 1024
MIB = 1024 * 1024


def _params(semantics, vmem_mib):
    assert vmem_mib * MIB < V7X_VMEM_BYTES
    return pltpu.CompilerParams(dimension_semantics=semantics, vmem_limit_bytes=vmem_mib * MIB)


def _layer_norm(y, g, b):
    mu = jnp.mean(y, -1, keepdims=True)
    d = y - mu
    var = jnp.mean(d * d, -1, keepdims=True)
    return d * lax.rsqrt(var + LN_EPS) * g + b


def _dot(a, b):
    return jnp.dot(a, b, preferred_element_type=F32)


def _dot_nt(a, b):
    return lax.dot_general(a, b, (((1,), (1,)), ((), ())), preferred_element_type=F32)


def _dot_tn(a, b):
    return lax.dot_general(a, b, (((0,), (0,)), ((), ())), preferred_element_type=F32)


def _ffn_ln_kernel(x_ref, wa_ref, wb_ref, wd_ref, g_ref, b_ref, o_ref, xb_ref, h_ref, *, alpha):
    j = pl.program_id(1)

    @pl.when(j == 0)
    def _():
        xb_ref[...] = x_ref[...].astype(BF16)
        o_ref[...] = jnp.zeros(o_ref.shape, F32)

    xb = xb_ref[...]
    for c in range(0, h_ref.shape[1], FFN_UP_CHUNK):
        cols = slice(c, c + FFN_UP_CHUNK)
        a = _dot(xb, wa_ref[:, cols])
        b = _dot(xb, wb_ref[:, cols])
        h_ref[:, cols] = (a * jax.nn.sigmoid(a) * b).astype(BF16)
    h = h_ref[...]
    for c in range(0, o_ref.shape[1], FFN_DOWN_CHUNK):
        cols = slice(c, c + FFN_DOWN_CHUNK)
        o_ref[:, cols] += _dot(h, wd_ref[:, cols])

    @pl.when(j == pl.num_programs(1) - 1)
    def _():
        for r in range(0, o_ref.shape[0], LN_ROW_CHUNK):
            rows = slice(r, r + LN_ROW_CHUNK)
            o_ref[rows, :] = _layer_norm(alpha * x_ref[rows, :] + 0.5 * o_ref[rows, :], g_ref[...], b_ref[...])


def _ffn_ln(x, w_up, w_down, g, b, layer, *, alpha, tm, tf):
    m, d = x.shape
    d_ff = w_down.shape[1]
    nj = d_ff // tf
    assert m % tm == 0 and d_ff % tf == 0
    row = lambda i, j: (i, 0)
    vec = pl.BlockSpec((None, 1, d), lambda i, j: (layer, 0, 0))
    return pl.pallas_call(
        functools.partial(_ffn_ln_kernel, alpha=alpha),
        grid=(m // tm, nj),
        in_specs=[
            pl.BlockSpec((tm, d), row),
            pl.BlockSpec((None, d, tf), lambda i, j: (layer, 0, j)),
            pl.BlockSpec((None, d, tf), lambda i, j: (layer, 0, j + nj)),
            pl.BlockSpec((None, tf, d), lambda i, j: (layer, j, 0)),
            vec, vec,
        ],
        out_specs=pl.BlockSpec((tm, d), row),
        out_shape=jax.ShapeDtypeStruct((m, d), F32),
        scratch_shapes=[pltpu.VMEM((tm, d), BF16), pltpu.VMEM((tm, tf), BF16)],
        compiler_params=_params(("parallel", "arbitrary"), 60),
        name="ffn_ln",
    )(x, w_up, w_up, w_down, g, b)


def _in_proj_kernel(x_ref, w_ref, o_ref, xb_ref):
    @pl.when(pl.program_id(1) == 0)
    def _():
        xb_ref[...] = x_ref[...].astype(BF16)

    o_ref[...] = _dot(xb_ref[...], w_ref[...])


def _in_proj(x, w, layer, *, tm, tn):
    m, k = x.shape
    n = w.shape[2]
    assert m % tm == 0 and n % tn == 0
    return pl.pallas_call(
        _in_proj_kernel,
        grid=(m // tm, n // tn),
        in_specs=[pl.BlockSpec((tm, k), lambda i, j: (i, 0)),
                  pl.BlockSpec((None, k, tn), lambda i, j: (layer, 0, j))],
        out_specs=[pl.BlockSpec((tm, tn), lambda i, j: (i, j)), pl.BlockSpec((tm, k), lambda i, j: (i, 0))],
        out_shape=[jax.ShapeDtypeStruct((m, n), F32), jax.ShapeDtypeStruct((m, k), BF16)],
        compiler_params=_params(("parallel", "arbitrary"), 56),
        name="in_proj",
    )(x, w)


def _matmul_kernel(x_ref, w_ref, o_ref):
    o_ref[...] = _dot(x_ref[...], w_ref[...])


def _matmul(x, w, layer, *, tm, tn):
    m, k = x.shape
    n = w.shape[2]
    assert m % tm == 0 and n % tn == 0
    return pl.pallas_call(
        _matmul_kernel,
        grid=(m // tm, n // tn),
        in_specs=[pl.BlockSpec((tm, k), lambda i, j: (i, 0)),
                  pl.BlockSpec((None, k, tn), lambda i, j: (layer, 0, j))],
        out_specs=pl.BlockSpec((tm, tn), lambda i, j: (i, j)),
        out_shape=jax.ShapeDtypeStruct((m, n), F32),
        compiler_params=_params(("parallel", "parallel"), 48),
        name="matmul",
    )(x, w)


def _conv_kernel(cb_ref, cc_ref, ch_ref, w_ref, prev_ref, y_ref, state_ref, carry_ref):
    l = pl.program_id(1)

    @pl.when(l == 0)
    def _():
        carry_ref[...] = prev_ref[...]

    u = cc_ref[...] * ch_ref[...]
    row = lax.broadcasted_iota(jnp.int32, u.shape, 0)
    c0 = carry_ref[0:1, :]
    c1 = carry_ref[1:2, :]
    u1 = jnp.where(row == 0, c1, pltpu.roll(u, 1, 0))
    u2 = jnp.where(row == 0, c0, jnp.where(row == 1, c1, pltpu.roll(u, 2, 0)))
    z = w_ref[0:1, :] * u2 + w_ref[1:2, :] * u1 + w_ref[2:3, :] * u
    y_ref[...] = (cb_ref[...] * z).astype(BF16)
    tl = u.shape[0]
    tail = u[tl - 8:, :]
    carry_ref[...] = tail[6:8, :]

    @pl.when(l == pl.num_programs(1) - 1)
    def _():
        state_ref[...] = tail[6:8, :]


def _conv_branch(proj, conv_w, prev, layer, *, bsz, length, tl):
    m = proj.shape[0]
    nl = length // tl
    assert length % tl == 0 and tl >= 8
    col = lambda c: pl.BlockSpec((tl, BRANCH_W), lambda b, l: (b * nl + l, c))
    return pl.pallas_call(
        _conv_kernel,
        grid=(bsz, nl),
        in_specs=[
            col(COL_CB), col(COL_CC), col(COL_CH),
            pl.BlockSpec((None, CONV_WIDTH, BRANCH_W), lambda b, l: (layer, 0, 0)),
            pl.BlockSpec((None, CONV_WIDTH - 1, BRANCH_W), lambda b, l: (b, 0, 0)),
        ],
        out_specs=[
            pl.BlockSpec((tl, BRANCH_W), lambda b, l: (b * nl + l, 0)),
            pl.BlockSpec((None, CONV_WIDTH - 1, BRANCH_W), lambda b, l: (b, 0, 0)),
        ],
        out_shape=[
            jax.ShapeDtypeStruct((m, BRANCH_W), BF16),
            jax.ShapeDtypeStruct((bsz, CONV_WIDTH - 1, BRANCH_W), F32),
        ],
        scratch_shapes=[pltpu.VMEM((CONV_WIDTH - 1, BRANCH_W), F32)],
        compiler_params=_params(("parallel", "arbitrary"), 32),
        name="conv_branch",
    )(proj, proj, proj, conv_w, prev)


def _ret_log_gamma(h):
    return math.log1p(-(2.0 ** (-5.0 - h)))


def _retention_kernel(q_ref, k_ref, v_ref, g_ref, cos_ref, sin_ref, s0_ref, gn_ref, y_ref, sfin_ref, s_ref):
    c = pl.program_id(1)

    @pl.when(c == 0)
    def _():
        s_ref[...] = s0_ref[...]

    chunk = q_ref.shape[0]
    cosf = cos_ref[...]
    sinf = sin_ref[...]
    ri = lax.broadcasted_iota(jnp.int32, (chunk, chunk), 0)
    ci = lax.broadcasted_iota(jnp.int32, (chunk, chunk), 1)
    rel = (ri - ci).astype(F32)
    idx = lax.broadcasted_iota(jnp.int32, (chunk, 1), 0).astype(F32)
    for h in range(RET_HEADS):
        log_g = _ret_log_gamma(h)
        cols = slice(h * RET_DK, (h + 1) * RET_DK)
        q = q_ref[:, cols]
        k = k_ref[:, cols]
        q = q * cosf + pltpu.roll(q, RET_DK // 2, 1) * sinf
        k = (k * cosf + pltpu.roll(k, RET_DK // 2, 1) * sinf) * (RET_DK ** -0.5)
        vb = v_ref[:, cols].astype(BF16)
        intra = jnp.where(rel >= 0.0, jnp.exp(jnp.maximum(rel, 0.0) * log_g), 0.0)
        q_dec = jnp.exp((idx + 1.0) * log_g)
        k_dec = jnp.exp((chunk - 1.0 - idx) * log_g)
        c_dec = math.exp(chunk * log_g)
        qb = q.astype(BF16)
        s_prev = s_ref[h]
        att = _dot_nt(qb, k.astype(BF16)) * intra
        o = _dot(att.astype(BF16), vb) + _dot(qb, s_prev.astype(BF16)) * q_dec
        s_ref[h] = s_prev * c_dec + _dot_tn((k * k_dec).astype(BF16), vb)
        mu = jnp.mean(o, -1, keepdims=True)
        d = o - mu
        var = jnp.mean(d * d, -1, keepdims=True)
        ro = d * lax.rsqrt(var + LN_EPS) * gn_ref[:, cols]
        gate = g_ref[:, cols]
        y_ref[:, cols] = (gate * jax.nn.sigmoid(gate) * ro).astype(BF16)

    @pl.when(c == pl.num_programs(1) - 1)
    def _():
        sfin_ref[...] = s_ref[...]


def _retention_branch(proj, cosf, sinf, s0, gn_g, layer, *, bsz, length, chunk):
    m = proj.shape[0]
    nc = length // chunk
    assert length % chunk == 0
    col = lambda c: pl.BlockSpec((chunk, BRANCH_W), lambda b, i: (b * nc + i, c))
    state = pl.BlockSpec((None, RET_HEADS, RET_DK, RET_DK), lambda b, i: (b, 0, 0, 0))
    return pl.pallas_call(
        _retention_kernel,
        grid=(bsz, nc),
        in_specs=[
            col(COL_RQ), col(COL_RK), col(COL_RV), col(COL_RG),
            pl.BlockSpec((chunk, RET_DK), lambda b, i: (i, 0)),
            pl.BlockSpec((chunk, RET_DK), lambda b, i: (i, 0)),
            state,
            pl.BlockSpec((None, 1, BRANCH_W), lambda b, i: (layer, 0, 0)),
        ],
        out_specs=[pl.BlockSpec((chunk, BRANCH_W), lambda b, i: (b * nc + i, 0)), state],
        out_shape=[
            jax.ShapeDtypeStruct((m, BRANCH_W), BF16),
            jax.ShapeDtypeStruct((bsz, RET_HEADS, RET_DK, RET_DK), F32),
        ],
        scratch_shapes=[pltpu.VMEM((RET_HEADS, RET_DK, RET_DK), F32)],
        compiler_params=_params(("parallel", "arbitrary"), 32),
        name="retention_branch",
    )(proj, proj, proj, proj, cosf, sinf, s0, gn_g)


def _diff_lambda(lam_ref, lam_init):
    e0 = jnp.exp(jnp.sum(lam_ref[0:1, :] * lam_ref[1:2, :], -1, keepdims=True))
    e1 = jnp.exp(jnp.sum(lam_ref[2:3, :] * lam_ref[3:4, :], -1, keepdims=True))
    return e0 - e1 + lam_init


def _diff_prompt_kernel(q_ref, k_ref, v_ref, bias_ref, lam_ref, g_ref, o_ref,
                        k16_ref, vt_ref, m_ref, l_ref, acc_ref, *, lam_init):
    i = pl.program_id(2)
    t = q_ref.shape[0]
    nt = k_ref.shape[0] // t

    @pl.when(i == 0)
    def _():
        k16_ref[...] = k_ref[...].astype(BF16)
        for j in range(nt):
            vt_ref[j] = jnp.transpose(v_ref[j * t:(j + 1) * t, :]).astype(BF16)

    qt = jnp.transpose(q_ref[...] * (DIFF_D ** -0.5 * LOG2E))
    feat = lax.broadcasted_iota(jnp.int32, qt.shape, 0)
    q2t = jnp.concatenate([jnp.where(feat < DIFF_D, qt, 0.0), jnp.where(feat >= DIFF_D, qt, 0.0)],
                          axis=1).astype(BF16)
    m_ref[...] = jnp.full(m_ref.shape, NEG_INF, F32)
    l_ref[...] = jnp.zeros(l_ref.shape, F32)
    acc_ref[...] = jnp.zeros(acc_ref.shape, F32)

    def tile(j, kind):
        kt = k16_ref[pl.ds(pl.multiple_of(j * t, t), t), :]
        s = _dot(kt, q2t) + bias_ref[kind]
        m_prev = m_ref[...]
        m_new = jnp.maximum(m_prev, jnp.max(s, 0, keepdims=True))
        p = jnp.exp2(s - m_new)
        a = jnp.exp2(m_prev - m_new)
        l_ref[...] = a * l_ref[...] + jnp.sum(p, 0, keepdims=True)
        acc_ref[...] = a * acc_ref[...] + _dot(vt_ref[j], p.astype(BF16))
        m_ref[...] = m_new

    def far_body(j, carry):
        tile(j, 2)
        return carry

    lax.fori_loop(0, jnp.maximum(i - 1, 0), far_body, 0)

    @pl.when(i >= 1)
    def _():
        tile(i - 1, 1)

    tile(i, 0)
    lam = _diff_lambda(lam_ref, lam_init)
    o = acc_ref[...] / l_ref[...]
    o = o[:, :t] - lam * o[:, t:]
    o = o * lax.rsqrt(jnp.mean(o * o, 0, keepdims=True) + LN_EPS) * g_ref[...]
    o_ref[...] = jnp.transpose(o * (1.0 - lam_init)).astype(BF16)


def _diff_prompt(proj, bias_tiles, diff_lambda, subln_g, layer, *, bsz, length, t, lam_init):
    m = proj.shape[0]
    nq = length // t
    assert length % t == 0 and t % CHUNK == 0
    kv = lambda c: pl.BlockSpec((length, HEAD_W), lambda b, h, i: (b, c * HEADS_PER_BRANCH + h))
    return pl.pallas_call(
        functools.partial(_diff_prompt_kernel, lam_init=lam_init),
        grid=(bsz, DIFF_HEADS, nq),
        in_specs=[
            pl.BlockSpec((t, HEAD_W), lambda b, h, i: (b * nq + i, COL_DQ * HEADS_PER_BRANCH + h)),
            kv(COL_DK), kv(COL_DV),
            pl.BlockSpec((None, 3, t, 2 * t), lambda b, h, i: (h, 0, 0, 0)),
            pl.BlockSpec((None, 4, DIFF_D), lambda b, h, i: (layer, 0, 0)),
            pl.BlockSpec((None, DIFF_DV, 1), lambda b, h, i: (layer, 0, 0)),
        ],
        out_specs=pl.BlockSpec((t, HEAD_W), lambda b, h, i: (b * nq + i, h)),
        out_shape=jax.ShapeDtypeStruct((m, BRANCH_W), BF16),
        scratch_shapes=[
            pltpu.VMEM((length, HEAD_W), BF16), pltpu.VMEM((nq, DIFF_DV, t), BF16),
            pltpu.VMEM((1, 2 * t), F32), pltpu.VMEM((1, 2 * t), F32), pltpu.VMEM((DIFF_DV, 2 * t), F32),
        ],
        compiler_params=_params(("parallel", "parallel", "arbitrary"), 40),
        name="diff_attention_prompt",
    )(proj, proj, proj, bias_tiles, diff_lambda, subln_g)


def _diff_sample_kernel(q_ref, kn_ref, vn_ref, kp_ref, vp_ref, bp_ref, bn_ref, lam_ref, g_ref, o_ref,
                        *, lam_init):
    q = q_ref[...] * (DIFF_D ** -0.5 * LOG2E)
    lane = lax.broadcasted_iota(jnp.int32, q.shape, 1)
    qs = (jnp.where(lane < DIFF_D, q, 0.0).astype(BF16), jnp.where(lane >= DIFF_D, q, 0.0).astype(BF16))
    kp = kp_ref[...].astype(BF16)
    vp = vp_ref[...].astype(BF16)
    kn = kn_ref[...].astype(BF16)
    vn = vn_ref[...].astype(BF16)
    outs = []
    for qc in qs:
        sp = _dot_nt(qc, kp) + bp_ref[...]
        sn = _dot_nt(qc, kn) + bn_ref[...]
        mx = jnp.maximum(jnp.max(sp, -1, keepdims=True), jnp.max(sn, -1, keepdims=True))
        pp = jnp.exp2(sp - mx)
        pn = jnp.exp2(sn - mx)
        den = jnp.sum(pp, -1, keepdims=True) + jnp.sum(pn, -1, keepdims=True)
        outs.append((_dot(pp.astype(BF16), vp) + _dot(pn.astype(BF16), vn)) / den)
    lam = _diff_lambda(lam_ref, lam_init)
    o = outs[0] - lam * outs[1]
    o = o * lax.rsqrt(jnp.mean(o * o, -1, keepdims=True) + LN_EPS) * g_ref[...]
    o_ref[...] = (o * (1.0 - lam_init)).astype(BF16)


def _diff_sample(proj, k_past, v_past, bias_past, bias_new, diff_lambda, subln_g, layer,
                 *, bsz, length, lam_init):
    m = proj.shape[0]
    past = k_past.shape[2]
    new = lambda c: pl.BlockSpec((length, HEAD_W), lambda b, h: (b, c * HEADS_PER_BRANCH + h))
    old = pl.BlockSpec((None, None, past, HEAD_W), lambda b, h: (layer, b, 0, h))
    return pl.pallas_call(
        functools.partial(_diff_sample_kernel, lam_init=lam_init),
        grid=(bsz, DIFF_HEADS),
        in_specs=[
            new(COL_DQ), new(COL_DK), new(COL_DV), old, old,
            pl.BlockSpec((None, length, past), lambda b, h: (h, 0, 0)),
            pl.BlockSpec((None, length, length), lambda b, h: (h, 0, 0)),
            pl.BlockSpec((None, 4, DIFF_D), lambda b, h: (layer, 0, 0)),
            pl.BlockSpec((None, 1, DIFF_DV), lambda b, h: (layer, 0, 0)),
        ],
        out_specs=pl.BlockSpec((length, HEAD_W), lambda b, h: (b, h)),
        out_shape=jax.ShapeDtypeStruct((m, BRANCH_W), BF16),
        compiler_params=_params(("parallel", "parallel"), 32),
        name="diff_attention_sample",
    )(proj, proj, proj, k_past, v_past, bias_past, bias_new, diff_lambda, subln_g)


def _mem_attn_kernel(q_ref, k_ref, v_ref, o_ref):
    s = _dot_nt(q_ref[...].astype(BF16), k_ref[...].astype(BF16)) * (MEM_HD ** -0.5)
    p = jnp.exp(s - jnp.max(s, -1, keepdims=True))
    den = jnp.sum(p, -1, keepdims=True)
    o_ref[...] = (_dot(p.astype(BF16), v_ref[...].astype(BF16)) / den).astype(BF16)


def _mem_attention(proj, mem_k, mem_v, *, k_index, v_index, bsz, length, tq):
    m = proj.shape[0]
    nq = length // tq
    assert length % tq == 0
    tokens = mem_k.shape[-2]
    block = (None,) * (mem_k.ndim - 2) + (tokens, HEAD_W)
    return pl.pallas_call(
        _mem_attn_kernel,
        grid=(bsz, MEM_HEADS, nq),
        in_specs=[pl.BlockSpec((tq, HEAD_W), lambda b, h, i: (b * nq + i, COL_MQ * HEADS_PER_BRANCH + h)),
                  pl.BlockSpec(block, lambda b, h, i: k_index(b, h)),
                  pl.BlockSpec(block, lambda b, h, i: v_index(b, h))],
        out_specs=pl.BlockSpec((tq, HEAD_W), lambda b, h, i: (b * nq + i, h)),
        out_shape=jax.ShapeDtypeStruct((m, BRANCH_W), BF16),
        compiler_params=_params(("parallel", "parallel", "parallel"), 32),
        name="mem_attention",
    )(proj, mem_k, mem_v)


def _merge_kernel(h_ref, ya_ref, yb_ref, yc_ref, yd_ref, wg_ref, wb_ref, bg_ref, o_ref):
    h = h_ref[...]
    acc = None
    for i, y_ref in enumerate((ya_ref, yb_ref, yc_ref, yd_ref)):
        gate = jax.nn.sigmoid(_dot(h, wg_ref[i]) + bg_ref[i])
        term = gate * _dot(y_ref[...], wb_ref[i])
        acc = term if acc is None else acc + term
    o_ref[...] = acc.astype(BF16)


def _merge(h16, ys, w_gate, w_branch, b_gate, layer, *, tm, tn):
    m, d = h16.shape
    assert m % tm == 0 and d % tn == 0
    y_spec = pl.BlockSpec((tm, BRANCH_W), lambda i, j: (i, 0))
    return pl.pallas_call(
        _merge_kernel,
        grid=(m // tm, d // tn),
        in_specs=[
            pl.BlockSpec((tm, d), lambda i, j: (i, 0)), y_spec, y_spec, y_spec, y_spec,
            pl.BlockSpec((None, N_BRANCH, d, tn), lambda i, j: (layer, 0, 0, j)),
            pl.BlockSpec((None, N_BRANCH, BRANCH_W, tn), lambda i, j: (layer, 0, 0, j)),
            pl.BlockSpec((None, N_BRANCH, 1, tn), lambda i, j: (layer, 0, 0, j)),
        ],
        out_specs=pl.BlockSpec((tm, tn), lambda i, j: (i, j)),
        out_shape=jax.ShapeDtypeStruct((m, d), BF16),
        compiler_params=_params(("parallel", "parallel"), 48),
        name="gated_merge",
    )(h16, *ys, w_gate, w_branch, b_gate)


def _proj_ln_kernel(m_ref, x_ref, w_ref, g_ref, b_ref, o_ref, *, alpha):
    y = alpha * x_ref[...] + _dot(m_ref[...], w_ref[...])
    o_ref[...] = _layer_norm(y, g_ref[...], b_ref[...])


def _proj_ln(merged, x, w_o, g, b, layer, *, alpha, tm):
    m, d = x.shape
    assert m % tm == 0
    row = pl.BlockSpec((tm, d), lambda i: (i, 0))
    vec = pl.BlockSpec((None, 1, d), lambda i: (layer, 0, 0))
    return pl.pallas_call(
        functools.partial(_proj_ln_kernel, alpha=alpha),
        grid=(m // tm,),
        in_specs=[row, row, pl.BlockSpec((None, d, d), lambda i: (layer, 0, 0)), vec, vec],
        out_specs=row,
        out_shape=jax.ShapeDtypeStruct((m, d), F32),
        compiler_params=_params(("parallel",), 48),
        name="proj_ln",
    )(merged, x, w_o, g, b)


def _rope_tables(pos):
    half = RET_DK // 2
    inv = ROPE_BASE ** (-jnp.arange(half, dtype=F32) / half)
    ang = pos.astype(F32)[:, None] * inv[None, :]
    cos = jnp.cos(ang)
    sin = jnp.sin(ang)
    return jnp.concatenate([cos, cos], -1), jnp.concatenate([-sin, sin], -1)


def _t5_bucket(rel):
    nb = REL_BUCKETS // 2
    max_exact = nb // 2
    n = jnp.abs(rel)
    nf = jnp.maximum(n, 1).astype(F32)
    large = max_exact + (jnp.log(nf / max_exact) / math.log(REL_MAX_DIST / max_exact)
                         * (nb - max_exact)).astype(jnp.int32)
    large = jnp.minimum(large, nb - 1)
    return jnp.where(rel > 0, nb, 0) + jnp.where(n < max_exact, n, large)


def _masked_bias(q_pos, k_pos, rel_bias):
    bucket = _t5_bucket(k_pos[None, :] - q_pos[:, None])
    onehot = bucket[None, None] == jnp.arange(REL_BUCKETS, dtype=bucket.dtype)[None, :, None, None]
    bias = jnp.sum(jnp.where(onehot, rel_bias.astype(F32).T[:, :, None, None], 0.0), axis=1)
    allowed = (k_pos[None, :] // CHUNK) <= (q_pos[:, None] // CHUNK)
    return jnp.where(allowed[None], bias * LOG2E, NEG_INF)


def _prompt_bias_tiles(rel_bias, t):
    assert t % CHUNK == 0 and t + 1 >= REL_MAX_DIST
    base = jnp.arange(t, dtype=jnp.int32)
    tiles = jnp.stack([_masked_bias(base + d * t, base, rel_bias) for d in range(3)], axis=1)
    tiles = jnp.swapaxes(tiles, -1, -2)
    return jnp.concatenate([tiles, tiles], axis=-1)


def _encoder_layer(x, w, layer, *, bsz, length, alpha, cfg, rope, diff_fn, conv_prev, ret_prev, mem_fn):
    x1 = _ffn_ln(x, w['ffn1_w_up'], w['ffn1_w_down'], w['ln1_g'], w['ln1_b'], layer,
                 alpha=alpha, tm=cfg['tm_ffn'], tf=cfg['tf'])
    proj, x1b = _in_proj(x1, w['w_in'], layer, tm=cfg['tm_in'], tn=cfg['tn_in'])
    y_a, conv_new = _conv_branch(proj, w['conv_w'], conv_prev, layer, bsz=bsz, length=length, tl=cfg['t_conv'])
    y_b, ret_new = _retention_branch(proj, rope[0], rope[1], ret_prev, w['ret_gn_g'], layer,
                                     bsz=bsz, length=length, chunk=cfg['ret_chunk'])
    lam_init = 0.8 - 0.6 * math.exp(-0.3 * layer)
    y_c = diff_fn(proj, lam_init)
    y_d = mem_fn(proj)
    merged = _merge(x1b, (y_a, y_b, y_c, y_d), w['w_gate'], w['w_branch'], w['b_gate'], layer,
                    tm=cfg['tm_merge'], tn=cfg['tn_merge'])
    x2 = _proj_ln(merged, x1, w['w_o'], w['ln2_g'], w['ln2_b'], layer, alpha=alpha, tm=cfg['tm_proj'])
    x3 = _ffn_ln(x2, w['ffn2_w_up'], w['ffn2_w_down'], w['ln3_g'], w['ln3_b'], layer,
                 alpha=alpha, tm=cfg['tm_ffn'], tf=cfg['tf'])
    dk = proj[:, COL_DK * BRANCH_W:(COL_DK + 1) * BRANCH_W].reshape(bsz, length, DIFF_HEADS, 2 * DIFF_D)
    dv = proj[:, COL_DV * BRANCH_W:(COL_DV + 1) * BRANCH_W].reshape(bsz, length, DIFF_HEADS, DIFF_DV)
    return x3, (conv_new, ret_new, dk, dv)


PROMPT_CFG = dict(tm_ffn=1024, tf=512, tm_in=1024, tn_in=512, t_conv=512, ret_chunk=256, t_diff=256,
                  tq_mem=1024, tm_merge=1024, tn_merge=256, tm_proj=512)
SAMPLE_CFG = dict(tm_ffn=256, tf=512, tm_in=256, tn_in=512, t_conv=16, ret_chunk=16,
                  tq_mem=16, tm_merge=256, tn_merge=256, tm_proj=256)


def kernel(x_prompt, x_sample, state_conv, state_ret, cache_diff_k, cache_diff_v, cache_mem_k, cache_mem_v,
           mem_prompt, ffn1_w_up, ffn1_w_down, ln1_g, ln1_b, w_in, conv_w, ret_gn_g, diff_lambda,
           diff_subln_g, w_mem_kv, w_branch, w_gate, b_gate, w_o, ln2_g, ln2_b, ffn2_w_up, ffn2_w_down,
           ln3_g, ln3_b, rel_bias):
    bp, lp, d = x_prompt.shape
    bs, ls, _ = x_sample.shape
    depth = w_in.shape[0]
    past = cache_diff_k.shape[2]
    mem_tokens = mem_prompt.shape[1]
    alpha = (2 * depth) ** 0.25
    half = MEM_HEADS * MEM_HD

    pos_p = jnp.arange(lp, dtype=jnp.int32)
    pos_s = past + jnp.arange(ls, dtype=jnp.int32)
    past_pos = jnp.arange(past, dtype=jnp.int32)
    rope_p = _rope_tables(pos_p)
    rope_s = _rope_tables(pos_s)
    bias_p = _prompt_bias_tiles(rel_bias, PROMPT_CFG['t_diff'])
    bias_s_past = _masked_bias(pos_s, past_pos, rel_bias)
    bias_s_new = _masked_bias(pos_s, pos_s, rel_bias)

    vec = lambda v: v.reshape(depth, 1, v.shape[-1])
    w = {
        'ffn1_w_up': ffn1_w_up.astype(BF16), 'ffn1_w_down': ffn1_w_down.astype(BF16),
        'ln1_g': vec(ln1_g), 'ln1_b': vec(ln1_b), 'w_in': w_in.astype(BF16), 'conv_w': conv_w,
        'ret_gn_g': vec(ret_gn_g), 'w_branch': w_branch.astype(BF16), 'w_gate': w_gate.astype(BF16),
        'b_gate': b_gate.reshape(depth, N_BRANCH, 1, d), 'w_o': w_o.astype(BF16),
        'ln2_g': vec(ln2_g), 'ln2_b': vec(ln2_b),
        'ffn2_w_up': ffn2_w_up.astype(BF16), 'ffn2_w_down': ffn2_w_down.astype(BF16),
        'ln3_g': vec(ln3_g), 'ln3_b': vec(ln3_b),
    }
    w_mem16 = w_mem_kv.astype(BF16)
    subln_col = diff_subln_g.reshape(depth, DIFF_DV, 1)
    subln_row = diff_subln_g.reshape(depth, 1, DIFF_DV)
    kpast = cache_diff_k.reshape(depth, bs, past, DIFF_HEADS * 2 * DIFF_D)
    vpast = cache_diff_v.reshape(depth, bs, past, DIFF_HEADS * DIFF_DV)
    mem_k_s = cache_mem_k.reshape(depth, bs, mem_tokens, half)
    mem_v_s = cache_mem_v.reshape(depth, bs, mem_tokens, half)

    yp = x_prompt.reshape(bp * lp, d)
    ys = x_sample.reshape(bs * ls, d)
    mem16 = mem_prompt.reshape(bp * mem_tokens, d).astype(BF16)
    zero_conv = jnp.zeros((bp, CONV_WIDTH - 1, BRANCH_W), F32)
    zero_ret = jnp.zeros((bp, RET_HEADS, RET_DK, RET_DK), F32)

    outs_p = [[] for _ in range(6)]
    outs_s = [[] for _ in range(4)]
    for l in range(depth):
        mkv = _matmul(mem16, w_mem16, l, tm=bp * mem_tokens, tn=512).reshape(bp, mem_tokens, 2 * half)
        diff_p = lambda proj, lam_init: _diff_prompt(
            proj, bias_p, diff_lambda, subln_col, l, bsz=bp, length=lp, t=PROMPT_CFG['t_diff'], lam_init=lam_init)
        mem_p = lambda proj: _mem_attention(
            proj, mkv, mkv, k_index=lambda b, h: (b, 0, h), v_index=lambda b, h: (b, 0, MEM_HEADS + h),
            bsz=bp, length=lp, tq=PROMPT_CFG['tq_mem'])
        yp, (c_new, r_new, k_new, v_new) = _encoder_layer(
            yp, w, l, bsz=bp, length=lp, alpha=alpha, cfg=PROMPT_CFG, rope=rope_p, diff_fn=diff_p,
            conv_prev=zero_conv, ret_prev=zero_ret, mem_fn=mem_p)
        for acc, val in zip(outs_p, (c_new, r_new, k_new, v_new,
                                     mkv[:, :, :half].reshape(bp, mem_tokens, MEM_HEADS, MEM_HD),
                                     mkv[:, :, half:].reshape(bp, mem_tokens, MEM_HEADS, MEM_HD))):
            acc.append(val)
        diff_s = lambda proj, lam_init: _diff_sample(
            proj, kpast, vpast, bias_s_past, bias_s_new, diff_lambda, subln_row, l,
            bsz=bs, length=ls, lam_init=lam_init)
        mem_s = lambda proj: _mem_attention(
            proj, mem_k_s, mem_v_s, k_index=lambda b, h: (l, b, 0, h), v_index=lambda b, h: (l, b, 0, h),
            bsz=bs, length=ls, tq=SAMPLE_CFG['tq_mem'])
        ys, (c_new, r_new, k_new, v_new) = _encoder_layer(
            ys, w, l, bsz=bs, length=ls, alpha=alpha, cfg=SAMPLE_CFG, rope=rope_s, diff_fn=diff_s,
            conv_prev=state_conv[l], ret_prev=state_ret[l], mem_fn=mem_s)
        for acc, val in zip(outs_s, (c_new, r_new, k_new, v_new)):
            acc.append(val)

    return (yp.reshape(bp, lp, d), ys.reshape(bs, ls, d),
            *(jnp.stack(v) for v in outs_p), *(jnp.stack(v) for v in outs_s))
```

```python
import functools
import math

import jax
import jax.numpy as jnp
from jax import lax
from jax.experimental import pallas as pl
from jax.experimental.pallas import tpu as pltpu

F32 = jnp.float32
BF16 = jnp.bfloat16

CHUNK = 64
CONV_WIDTH = 3
RET_HEADS = 4
RET_DK = 128
DIFF_HEADS = 4
DIFF_D = 64
DIFF_DV = 128
MEM_HEADS = 4
MEM_HD = 128
BRANCH_W = 512
N_BRANCH = 4
REL_BUCKETS = 32
REL_MAX_DIST = 128
LN_EPS = 1e-5
ROPE_BASE = 10000.0
NEG_INF = -1e30
HEAD_W = 128
LOG2E = math.log2(math.e)
LN_ROW_CHUNK = 128
FFN_UP_CHUNK = 256
FFN_DOWN_CHUNK = 512

COL_CB, COL_CC, COL_CH, COL_RQ, COL_RK, COL_RV, COL_RG, COL_DQ, COL_DK, COL_DV, COL_MQ = range(11)
HEADS_PER_BRANCH = BRANCH_W // HEAD_W

V7X_VMEM_BYTES = 64 * 1024 * 1024
MIB = 1024 * 1024


def _params(semantics, vmem_mib):
    assert vmem_mib * MIB < V7X_VMEM_BYTES
    return pltpu.CompilerParams(dimension_semantics=semantics, vmem_limit_bytes=vmem_mib * MIB)


def _layer_norm(y, g, b):
    mu = jnp.mean(y, -1, keepdims=True)
    d = y - mu
    var = jnp.mean(d * d, -1, keepdims=True)
    return d * lax.rsqrt(var + LN_EPS) * g + b


def _dot(a, b):
    return jnp.dot(a, b, preferred_element_type=F32)


def _dot_nt(a, b):
    return lax.dot_general(a, b, (((1,), (1,)), ((), ())), preferred_element_type=F32)


def _dot_tn(a, b):
    return lax.dot_general(a, b, (((0,), (0,)), ((), ())), preferred_element_type=F32)


def _ffn_ln_kernel(x_ref, wa_ref, wb_ref, wd_ref, g_ref, b_ref, o_ref, xb_ref, h_ref, *, alpha):
    j = pl.program_id(1)

    @pl.when(j == 0)
    def _():
        xb_ref[...] = x_ref[...].astype(BF16)
        o_ref[...] = jnp.zeros(o_ref.shape, F32)

    xb = xb_ref[...]
    for c in range(0, h_ref.shape[1], FFN_UP_CHUNK):
        cols = slice(c, c + FFN_UP_CHUNK)
        a = _dot(xb, wa_ref[:, cols])
        b = _dot(xb, wb_ref[:, cols])
        h_ref[:, cols] = (a * jax.nn.sigmoid(a) * b).astype(BF16)
    h = h_ref[...]
    for c in range(0, o_ref.shape[1], FFN_DOWN_CHUNK):
        cols = slice(c, c + FFN_DOWN_CHUNK)
        o_ref[:, cols] += _dot(h, wd_ref[:, cols])

    @pl.when(j == pl.num_programs(1) - 1)
    def _():
        for r in range(0, o_ref.shape[0], LN_ROW_CHUNK):
            rows = slice(r, r + LN_ROW_CHUNK)
            o_ref[rows, :] = _layer_norm(alpha * x_ref[rows, :] + 0.5 * o_ref[rows, :], g_ref[...], b_ref[...])


def _ffn_ln(x, w_up, w_down, g, b, layer, *, alpha, tm, tf):
    m, d = x.shape
    d_ff = w_down.shape[1]
    nj = d_ff // tf
    assert m % tm == 0 and d_ff % tf == 0
    row = lambda i, j: (i, 0)
    vec = pl.BlockSpec((None, 1, d), lambda i, j: (layer, 0, 0))
    return pl.pallas_call(
        functools.partial(_ffn_ln_kernel, alpha=alpha),
        grid=(m // tm, nj),
        in_specs=[
            pl.BlockSpec((tm, d), row),
            pl.BlockSpec((None, d, tf), lambda i, j: (layer, 0, j)),
            pl.BlockSpec((None, d, tf), lambda i, j: (layer, 0, j + nj)),
            pl.BlockSpec((None, tf, d), lambda i, j: (layer, j, 0)),
            vec, vec,
        ],
        out_specs=pl.BlockSpec((tm, d), row),
        out_shape=jax.ShapeDtypeStruct((m, d), F32),
        scratch_shapes=[pltpu.VMEM((tm, d), BF16), pltpu.VMEM((tm, tf), BF16)],
        compiler_params=_params(("parallel", "arbitrary"), 60),
        name="ffn_ln",
    )(x, w_up, w_up, w_down, g, b)


def _in_proj_kernel(x_ref, w_ref, o_ref, xb_ref):
    @pl.when(pl.program_id(1) == 0)
    def _():
        xb_ref[...] = x_ref[...].astype(BF16)

    o_ref[...] = _dot(xb_ref[...], w_ref[...])


def _in_proj(x, w, layer, *, tm, tn):
    m, k = x.shape
    n = w.shape[2]
    assert m % tm == 0 and n % tn == 0
    return pl.pallas_call(
        _in_proj_kernel,
        grid=(m // tm, n // tn),
        in_specs=[pl.BlockSpec((tm, k), lambda i, j: (i, 0)),
                  pl.BlockSpec((None, k, tn), lambda i, j: (layer, 0, j))],
        out_specs=[pl.BlockSpec((tm, tn), lambda i, j: (i, j)), pl.BlockSpec((tm, k), lambda i, j: (i, 0))],
        out_shape=[jax.ShapeDtypeStruct((m, n), F32), jax.ShapeDtypeStruct((m, k), BF16)],
        compiler_params=_params(("parallel", "arbitrary"), 56),
        name="in_proj",
    )(x, w)


def _matmul_kernel(x_ref, w_ref, o_ref):
    o_ref[...] = _dot(x_ref[...], w_ref[...])


def _matmul(x, w, layer, *, tm, tn):
    m, k = x.shape
    n = w.shape[2]
    assert m % tm == 0 and n % tn == 0
    return pl.pallas_call(
        _matmul_kernel,
        grid=(m // tm, n // tn),
        in_specs=[pl.BlockSpec((tm, k), lambda i, j: (i, 0)),
                  pl.BlockSpec((None, k, tn), lambda i, j: (layer, 0, j))],
        out_specs=pl.BlockSpec((tm, tn), lambda i, j: (i, j)),
        out_shape=jax.ShapeDtypeStruct((m, n), F32),
        compiler_params=_params(("parallel", "parallel"), 48),
        name="matmul",
    )(x, w)


def _conv_kernel(cb_ref, cc_ref, ch_ref, w_ref, prev_ref, y_ref, state_ref, carry_ref):
    l = pl.program_id(1)

    @pl.when(l == 0)
    def _():
        carry_ref[...] = prev_ref[...]

    u = cc_ref[...] * ch_ref[...]
    row = lax.broadcasted_iota(jnp.int32, u.shape, 0)
    c0 = carry_ref[0:1, :]
    c1 = carry_ref[1:2, :]
    u1 = jnp.where(row == 0, c1, pltpu.roll(u, 1, 0))
    u2 = jnp.where(row == 0, c0, jnp.where(row == 1, c1, pltpu.roll(u, 2, 0)))
    z = w_ref[0:1, :] * u2 + w_ref[1:2, :] * u1 + w_ref[2:3, :] * u
    y_ref[...] = (cb_ref[...] * z).astype(BF16)
    tl = u.shape[0]
    tail = u[tl - 8:, :]
    carry_ref[...] = tail[6:8, :]

    @pl.when(l == pl.num_programs(1) - 1)
    def _():
        state_ref[...] = tail[6:8, :]


def _conv_branch(proj, conv_w, prev, layer, *, bsz, length, tl):
    m = proj.shape[0]
    nl = length // tl
    assert length % tl == 0 and tl >= 8
    col = lambda c: pl.BlockSpec((tl, BRANCH_W), lambda b, l: (b * nl + l, c))
    return pl.pallas_call(
        _conv_kernel,
        grid=(bsz, nl),
        in_specs=[
            col(COL_CB), col(COL_CC), col(COL_CH),
            pl.BlockSpec((None, CONV_WIDTH, BRANCH_W), lambda b, l: (layer, 0, 0)),
            pl.BlockSpec((None, CONV_WIDTH - 1, BRANCH_W), lambda b, l: (b, 0, 0)),
        ],
        out_specs=[
            pl.BlockSpec((tl, BRANCH_W), lambda b, l: (b * nl + l, 0)),
            pl.BlockSpec((None, CONV_WIDTH - 1, BRANCH_W), lambda b, l: (b, 0, 0)),
        ],
        out_shape=[
            jax.ShapeDtypeStruct((m, BRANCH_W), BF16),
            jax.ShapeDtypeStruct((bsz, CONV_WIDTH - 1, BRANCH_W), F32),
        ],
        scratch_shapes=[pltpu.VMEM((CONV_WIDTH - 1, BRANCH_W), F32)],
        compiler_params=_params(("parallel", "arbitrary"), 32),
        name="conv_branch",
    )(proj, proj, proj, conv_w, prev)


def _ret_log_gamma(h):
    return math.log1p(-(2.0 ** (-5.0 - h)))


def _retention_kernel(q_ref, k_ref, v_ref, g_ref, cos_ref, sin_ref, s0_ref, gn_ref, y_ref, sfin_ref, s_ref):
    c = pl.program_id(1)

    @pl.when(c == 0)
    def _():
        s_ref[...] = s0_ref[...]

    chunk = q_ref.shape[0]
    cosf = cos_ref[...]
    sinf = sin_ref[...]
    ri = lax.broadcasted_iota(jnp.int32, (chunk, chunk), 0)
    ci = lax.broadcasted_iota(jnp.int32, (chunk, chunk), 1)
    rel = (ri - ci).astype(F32)
    idx = lax.broadcasted_iota(jnp.int32, (chunk, 1), 0).astype(F32)
    for h in range(RET_HEADS):
        log_g = _ret_log_gamma(h)
        cols = slice(h * RET_DK, (h + 1) * RET_DK)
        q = q_ref[:, cols]
        k = k_ref[:, cols]
        q = q * cosf + pltpu.roll(q, RET_DK // 2, 1) * sinf
        k = (k * cosf + pltpu.roll(k, RET_DK // 2, 1) * sinf) * (RET_DK ** -0.5)
        vb = v_ref[:, cols].astype(BF16)
        intra = jnp.where(rel >= 0.0, jnp.exp(jnp.maximum(rel, 0.0) * log_g), 0.0)
        q_dec = jnp.exp((idx + 1.0) * log_g)
        k_dec = jnp.exp((chunk - 1.0 - idx) * log_g)
        c_dec = math.exp(chunk * log_g)
        qb = q.astype(BF16)
        s_prev = s_ref[h]
        att = _dot_nt(qb, k.astype(BF16)) * intra
        o = _dot(att.astype(BF16), vb) + _dot(qb, s_prev.astype(BF16)) * q_dec
        s_ref[h] = s_prev * c_dec + _dot_tn((k * k_dec).astype(BF16), vb)
        mu = jnp.mean(o, -1, keepdims=True)
        d = o - mu
        var = jnp.mean(d * d, -1, keepdims=True)
        ro = d * lax.rsqrt(var + LN_EPS) * gn_ref[:, cols]
        gate = g_ref[:, cols]
        y_ref[:, cols] = (gate * jax.nn.sigmoid(gate) * ro).astype(BF16)

    @pl.when(c == pl.num_programs(1) - 1)
    def _():
        sfin_ref[...] = s_ref[...]


def _retention_branch(proj, cosf, sinf, s0, gn_g, layer, *, bsz, length, chunk):
    m = proj.shape[0]
    nc = length // chunk
    assert length % chunk == 0
    col = lambda c: pl.BlockSpec((chunk, BRANCH_W), lambda b, i: (b * nc + i, c))
    state = pl.BlockSpec((None, RET_HEADS, RET_DK, RET_DK), lambda b, i: (b, 0, 0, 0))
    return pl.pallas_call(
        _retention_kernel,
        grid=(bsz, nc),
        in_specs=[
            col(COL_RQ), col(COL_RK), col(COL_RV), col(COL_RG),
            pl.BlockSpec((chunk, RET_DK), lambda b, i: (i, 0)),
            pl.BlockSpec((chunk, RET_DK), lambda b, i: (i, 0)),
            state,
            pl.BlockSpec((None, 1, BRANCH_W), lambda b, i: (layer, 0, 0)),
        ],
        out_specs=[pl.BlockSpec((chunk, BRANCH_W), lambda b, i: (b * nc + i, 0)), state],
        out_shape=[
            jax.ShapeDtypeStruct((m, BRANCH_W), BF16),
            jax.ShapeDtypeStruct((bsz, RET_HEADS, RET_DK, RET_DK), F32),
        ],
        scratch_shapes=[pltpu.VMEM((RET_HEADS, RET_DK, RET_DK), F32)],
        compiler_params=_params(("parallel", "arbitrary"), 32),
        name="retention_branch",
    )(proj, proj, proj, proj, cosf, sinf, s0, gn_g)


def _diff_lambda(lam_ref, lam_init):
    e0 = jnp.exp(jnp.sum(lam_ref[0:1, :] * lam_ref[1:2, :], -1, keepdims=True))
    e1 = jnp.exp(jnp.sum(lam_ref[2:3, :] * lam_ref[3:4, :], -1, keepdims=True))
    return e0 - e1 + lam_init


def _diff_prompt_kernel(q_ref, k_ref, v_ref, bias_ref, lam_ref, g_ref, o_ref, k16_ref, vt_ref,
                        m0_ref, m1_ref, l0_ref, l1_ref, acc0_ref, acc1_ref, *, lam_init):
    i = pl.program_id(2)
    t = q_ref.shape[0]
    nt = k_ref.shape[0] // t
    m_refs, l_refs, acc_refs = (m0_ref, m1_ref), (l0_ref, l1_ref), (acc0_ref, acc1_ref)

    @pl.when(i == 0)
    def _():
        k16_ref[...] = k_ref[...].astype(BF16)
        for j in range(nt):
            vt_ref[j] = jnp.transpose(v_ref[j * t:(j + 1) * t, :]).astype(BF16)

    qt = jnp.transpose(q_ref[...] * (DIFF_D ** -0.5 * LOG2E))
    feat = lax.broadcasted_iota(jnp.int32, qt.shape, 0)
    qts = (jnp.where(feat < DIFF_D, qt, 0.0).astype(BF16), jnp.where(feat >= DIFF_D, qt, 0.0).astype(BF16))
    for c in range(2):
        m_refs[c][...] = jnp.full(m_refs[c].shape, NEG_INF, F32)
        l_refs[c][...] = jnp.zeros(l_refs[c].shape, F32)
        acc_refs[c][...] = jnp.zeros(acc_refs[c].shape, F32)

    def tile(j, kind):
        kt = k16_ref[pl.ds(pl.multiple_of(j * t, t), t), :]
        vt = vt_ref[j]
        bias = bias_ref[kind]
        s = [_dot(kt, qts[c]) + bias for c in range(2)]
        m_prev = [m_refs[c][...] for c in range(2)]
        m_new = [jnp.maximum(m_prev[c], jnp.max(s[c], 0, keepdims=True)) for c in range(2)]
        p = [jnp.exp2(s[c] - m_new[c]) for c in range(2)]
        a = [jnp.exp2(m_prev[c] - m_new[c]) for c in range(2)]
        for c in range(2):
            l_refs[c][...] = a[c] * l_refs[c][...] + jnp.sum(p[c], 0, keepdims=True)
            acc_refs[c][...] = a[c] * acc_refs[c][...] + _dot(vt, p[c].astype(BF16))
            m_refs[c][...] = m_new[c]

    def far_body(j, carry):
        tile(j, 2)
        return carry

    lax.fori_loop(0, jnp.maximum(i - 1, 0), far_body, 0)

    @pl.when(i >= 1)
    def _():
        tile(i - 1, 1)

    tile(i, 0)
    lam = _diff_lambda(lam_ref, lam_init)
    o = acc0_ref[...] / l0_ref[...] - lam * (acc1_ref[...] / l1_ref[...])
    o = o * lax.rsqrt(jnp.mean(o * o, 0, keepdims=True) + LN_EPS) * g_ref[...]
    o_ref[...] = jnp.transpose(o * (1.0 - lam_init)).astype(BF16)


def _diff_prompt(proj, bias_tiles, diff_lambda, subln_g, layer, *, bsz, length, t, lam_init):
    m = proj.shape[0]
    nq = length // t
    assert length % t == 0 and t % CHUNK == 0
    kv = lambda c: pl.BlockSpec((length, HEAD_W), lambda b, h, i: (b, c * HEADS_PER_BRANCH + h))
    stat = pltpu.VMEM((1, t), F32)
    acc = pltpu.VMEM((DIFF_DV, t), F32)
    return pl.pallas_call(
        functools.partial(_diff_prompt_kernel, lam_init=lam_init),
        grid=(bsz, DIFF_HEADS, nq),
        in_specs=[
            pl.BlockSpec((t, HEAD_W), lambda b, h, i: (b * nq + i, COL_DQ * HEADS_PER_BRANCH + h)),
            kv(COL_DK), kv(COL_DV),
            pl.BlockSpec((None, 3, t, t), lambda b, h, i: (h, 0, 0, 0)),
            pl.BlockSpec((None, 4, DIFF_D), lambda b, h, i: (layer, 0, 0)),
            pl.BlockSpec((None, DIFF_DV, 1), lambda b, h, i: (layer, 0, 0)),
        ],
        out_specs=pl.BlockSpec((t, HEAD_W), lambda b, h, i: (b * nq + i, h)),
        out_shape=jax.ShapeDtypeStruct((m, BRANCH_W), BF16),
        scratch_shapes=[pltpu.VMEM((length, HEAD_W), BF16), pltpu.VMEM((nq, DIFF_DV, t), BF16),
                        stat, stat, stat, stat, acc, acc],
        compiler_params=_params(("parallel", "parallel", "arbitrary"), 40),
        name="diff_attention_prompt",
    )(proj, proj, proj, bias_tiles, diff_lambda, subln_g)


def _diff_sample_kernel(q_ref, kn_ref, vn_ref, kp_ref, vp_ref, bp_ref, bn_ref, lam_ref, g_ref, o_ref,
                        *, lam_init):
    q = q_ref[...] * (DIFF_D ** -0.5 * LOG2E)
    lane = lax.broadcasted_iota(jnp.int32, q.shape, 1)
    qs = (jnp.where(lane < DIFF_D, q, 0.0).astype(BF16), jnp.where(lane >= DIFF_D, q, 0.0).astype(BF16))
    kp = kp_ref[...].astype(BF16)
    vp = vp_ref[...].astype(BF16)
    kn = kn_ref[...].astype(BF16)
    vn = vn_ref[...].astype(BF16)
    outs = []
    for qc in qs:
        sp = _dot_nt(qc, kp) + bp_ref[...]
        sn = _dot_nt(qc, kn) + bn_ref[...]
        mx = jnp.maximum(jnp.max(sp, -1, keepdims=True), jnp.max(sn, -1, keepdims=True))
        pp = jnp.exp2(sp - mx)
        pn = jnp.exp2(sn - mx)
        den = jnp.sum(pp, -1, keepdims=True) + jnp.sum(pn, -1, keepdims=True)
        outs.append((_dot(pp.astype(BF16), vp) + _dot(pn.astype(BF16), vn)) / den)
    lam = _diff_lambda(lam_ref, lam_init)
    o = outs[0] - lam * outs[1]
    o = o * lax.rsqrt(jnp.mean(o * o, -1, keepdims=True) + LN_EPS) * g_ref[...]
    o_ref[...] = (o * (1.0 - lam_init)).astype(BF16)


def _diff_sample(proj, k_past, v_past, bias_past, bias_new, diff_lambda, subln_g, layer,
                 *, bsz, length, lam_init):
    m = proj.shape[0]
    past = k_past.shape[2]
    new = lambda c: pl.BlockSpec((length, HEAD_W), lambda b, h: (b, c * HEADS_PER_BRANCH + h))
    old = pl.BlockSpec((None, None, past, HEAD_W), lambda b, h: (layer, b, 0, h))
    return pl.pallas_call(
        functools.partial(_diff_sample_kernel, lam_init=lam_init),
        grid=(bsz, DIFF_HEADS),
        in_specs=[
            new(COL_DQ), new(COL_DK), new(COL_DV), old, old,
            pl.BlockSpec((None, length, past), lambda b, h: (h, 0, 0)),
            pl.BlockSpec((None, length, length), lambda b, h: (h, 0, 0)),
            pl.BlockSpec((None, 4, DIFF_D), lambda b, h: (layer, 0, 0)),
            pl.BlockSpec((None, 1, DIFF_DV), lambda b, h: (layer, 0, 0)),
        ],
        out_specs=pl.BlockSpec((length, HEAD_W), lambda b, h: (b, h)),
        out_shape=jax.ShapeDtypeStruct((m, BRANCH_W), BF16),
        compiler_params=_params(("parallel", "parallel"), 32),
        name="diff_attention_sample",
    )(proj, proj, proj, k_past, v_past, bias_past, bias_new, diff_lambda, subln_g)


def _mem_attn_kernel(q_ref, k_ref, v_ref, o_ref):
    s = _dot_nt(q_ref[...].astype(BF16), k_ref[...].astype(BF16)) * (MEM_HD ** -0.5)
    p = jnp.exp(s - jnp.max(s, -1, keepdims=True))
    den = jnp.sum(p, -1, keepdims=True)
    o_ref[...] = (_dot(p.astype(BF16), v_ref[...].astype(BF16)) / den).astype(BF16)


def _mem_attention(proj, mem_k, mem_v, *, k_index, v_index, bsz, length, tq):
    m = proj.shape[0]
    nq = length // tq
    assert length % tq == 0
    tokens = mem_k.shape[-2]
    block = (None,) * (mem_k.ndim - 2) + (tokens, HEAD_W)
    return pl.pallas_call(
        _mem_attn_kernel,
        grid=(bsz, MEM_HEADS, nq),
        in_specs=[pl.BlockSpec((tq, HEAD_W), lambda b, h, i: (b * nq + i, COL_MQ * HEADS_PER_BRANCH + h)),
                  pl.BlockSpec(block, lambda b, h, i: k_index(b, h)),
                  pl.BlockSpec(block, lambda b, h, i: v_index(b, h))],
        out_specs=pl.BlockSpec((tq, HEAD_W), lambda b, h, i: (b * nq + i, h)),
        out_shape=jax.ShapeDtypeStruct((m, BRANCH_W), BF16),
        compiler_params=_params(("parallel", "parallel", "parallel"), 32),
        name="mem_attention",
    )(proj, mem_k, mem_v)


def _merge_kernel(h_ref, ya_ref, yb_ref, yc_ref, yd_ref, wg_ref, wb_ref, bg_ref, o_ref):
    h = h_ref[...]
    acc = None
    for i, y_ref in enumerate((ya_ref, yb_ref, yc_ref, yd_ref)):
        gate = jax.nn.sigmoid(_dot(h, wg_ref[i]) + bg_ref[i])
        term = gate * _dot(y_ref[...], wb_ref[i])
        acc = term if acc is None else acc + term
    o_ref[...] = acc.astype(BF16)


def _merge(h16, ys, w_gate, w_branch, b_gate, layer, *, tm, tn):
    m, d = h16.shape
    assert m % tm == 0 and d % tn == 0
    y_spec = pl.BlockSpec((tm, BRANCH_W), lambda i, j: (i, 0))
    return pl.pallas_call(
        _merge_kernel,
        grid=(m // tm, d // tn),
        in_specs=[
            pl.BlockSpec((tm, d), lambda i, j: (i, 0)), y_spec, y_spec, y_spec, y_spec,
            pl.BlockSpec((None, N_BRANCH, d, tn), lambda i, j: (layer, 0, 0, j)),
            pl.BlockSpec((None, N_BRANCH, BRANCH_W, tn), lambda i, j: (layer, 0, 0, j)),
            pl.BlockSpec((None, N_BRANCH, 1, tn), lambda i, j: (layer, 0, 0, j)),
        ],
        out_specs=pl.BlockSpec((tm, tn), lambda i, j: (i, j)),
        out_shape=jax.ShapeDtypeStruct((m, d), BF16),
        compiler_params=_params(("parallel", "parallel"), 48),
        name="gated_merge",
    )(h16, *ys, w_gate, w_branch, b_gate)


def _proj_ln_kernel(m_ref, x_ref, w_ref, g_ref, b_ref, o_ref, *, alpha):
    y = alpha * x_ref[...] + _dot(m_ref[...], w_ref[...])
    o_ref[...] = _layer_norm(y, g_ref[...], b_ref[...])


def _proj_ln(merged, x, w_o, g, b, layer, *, alpha, tm):
    m, d = x.shape
    assert m % tm == 0
    row = pl.BlockSpec((tm, d), lambda i: (i, 0))
    vec = pl.BlockSpec((None, 1, d), lambda i: (layer, 0, 0))
    return pl.pallas_call(
        functools.partial(_proj_ln_kernel, alpha=alpha),
        grid=(m // tm,),
        in_specs=[row, row, pl.BlockSpec((None, d, d), lambda i: (layer, 0, 0)), vec, vec],
        out_specs=row,
        out_shape=jax.ShapeDtypeStruct((m, d), F32),
        compiler_params=_params(("parallel",), 48),
        name="proj_ln",
    )(merged, x, w_o, g, b)


def _rope_tables(pos):
    half = RET_DK // 2
    inv = ROPE_BASE ** (-jnp.arange(half, dtype=F32) / half)
    ang = pos.astype(F32)[:, None] * inv[None, :]
    cos = jnp.cos(ang)
    sin = jnp.sin(ang)
    return jnp.concatenate([cos, cos], -1), jnp.concatenate([-sin, sin], -1)


def _t5_bucket(rel):
    nb = REL_BUCKETS // 2
    max_exact = nb // 2
    n = jnp.abs(rel)
    nf = jnp.maximum(n, 1).astype(F32)
    large = max_exact + (jnp.log(nf / max_exact) / math.log(REL_MAX_DIST / max_exact)
                         * (nb - max_exact)).astype(jnp.int32)
    large = jnp.minimum(large, nb - 1)
    return jnp.where(rel > 0, nb, 0) + jnp.where(n < max_exact, n, large)


def _masked_bias(q_pos, k_pos, rel_bias):
    bucket = _t5_bucket(k_pos[None, :] - q_pos[:, None])
    onehot = bucket[None, None] == jnp.arange(REL_BUCKETS, dtype=bucket.dtype)[None, :, None, None]
    bias = jnp.sum(jnp.where(onehot, rel_bias.astype(F32).T[:, :, None, None], 0.0), axis=1)
    allowed = (k_pos[None, :] // CHUNK) <= (q_pos[:, None] // CHUNK)
    return jnp.where(allowed[None], bias * LOG2E, NEG_INF)


def _prompt_bias_tiles(rel_bias, t):
    assert t % CHUNK == 0 and t + 1 >= REL_MAX_DIST
    base = jnp.arange(t, dtype=jnp.int32)
    tiles = jnp.stack([_masked_bias(base + d * t, base, rel_bias) for d in range(3)], axis=1)
    return jnp.swapaxes(tiles, -1, -2)


def _encoder_layer(x, w, layer, *, bsz, length, alpha, cfg, rope, diff_fn, conv_prev, ret_prev, mem_fn):
    x1 = _ffn_ln(x, w['ffn1_w_up'], w['ffn1_w_down'], w['ln1_g'], w['ln1_b'], layer,
                 alpha=alpha, tm=cfg['tm_ffn'], tf=cfg['tf'])
    proj, x1b = _in_proj(x1, w['w_in'], layer, tm=cfg['tm_in'], tn=cfg['tn_in'])
    y_a, conv_new = _conv_branch(proj, w['conv_w'], conv_prev, layer, bsz=bsz, length=length, tl=cfg['t_conv'])
    y_b, ret_new = _retention_branch(proj, rope[0], rope[1], ret_prev, w['ret_gn_g'], layer,
                                     bsz=bsz, length=length, chunk=cfg['ret_chunk'])
    lam_init = 0.8 - 0.6 * math.exp(-0.3 * layer)
    y_c = diff_fn(proj, lam_init)
    y_d = mem_fn(proj)
    merged = _merge(x1b, (y_a, y_b, y_c, y_d), w['w_gate'], w['w_branch'], w['b_gate'], layer,
                    tm=cfg['tm_merge'], tn=cfg['tn_merge'])
    x2 = _proj_ln(merged, x1, w['w_o'], w['ln2_g'], w['ln2_b'], layer, alpha=alpha, tm=cfg['tm_proj'])
    x3 = _ffn_ln(x2, w['ffn2_w_up'], w['ffn2_w_down'], w['ln3_g'], w['ln3_b'], layer,
                 alpha=alpha, tm=cfg['tm_ffn'], tf=cfg['tf'])
    dk = proj[:, COL_DK * BRANCH_W:(COL_DK + 1) * BRANCH_W].reshape(bsz, length, DIFF_HEADS, 2 * DIFF_D)
    dv = proj[:, COL_DV * BRANCH_W:(COL_DV + 1) * BRANCH_W].reshape(bsz, length, DIFF_HEADS, DIFF_DV)
    return x3, (conv_new, ret_new, dk, dv)


PROMPT_CFG = dict(tm_ffn=1024, tf=512, tm_in=1024, tn_in=1408, t_conv=512, ret_chunk=256, t_diff=512,
                  tq_mem=1024, tm_merge=1024, tn_merge=256, tm_proj=512)
SAMPLE_CFG = dict(tm_ffn=256, tf=512, tm_in=256, tn_in=512, t_conv=16, ret_chunk=16,
                  tq_mem=16, tm_merge=256, tn_merge=256, tm_proj=256)


def kernel(x_prompt, x_sample, state_conv, state_ret, cache_diff_k, cache_diff_v, cache_mem_k, cache_mem_v,
           mem_prompt, ffn1_w_up, ffn1_w_down, ln1_g, ln1_b, w_in, conv_w, ret_gn_g, diff_lambda,
           diff_subln_g, w_mem_kv, w_branch, w_gate, b_gate, w_o, ln2_g, ln2_b, ffn2_w_up, ffn2_w_down,
           ln3_g, ln3_b, rel_bias):
    bp, lp, d = x_prompt.shape
    bs, ls, _ = x_sample.shape
    depth = w_in.shape[0]
    past = cache_diff_k.shape[2]
    mem_tokens = mem_prompt.shape[1]
    alpha = (2 * depth) ** 0.25
    half = MEM_HEADS * MEM_HD

    pos_p = jnp.arange(lp, dtype=jnp.int32)
    pos_s = past + jnp.arange(ls, dtype=jnp.int32)
    past_pos = jnp.arange(past, dtype=jnp.int32)
    rope_p = _rope_tables(pos_p)
    rope_s = _rope_tables(pos_s)
    bias_p = _prompt_bias_tiles(rel_bias, PROMPT_CFG['t_diff'])
    bias_s_past = _masked_bias(pos_s, past_pos, rel_bias)
    bias_s_new = _masked_bias(pos_s, pos_s, rel_bias)

    vec = lambda v: v.reshape(depth, 1, v.shape[-1])
    w = {
        'ffn1_w_up': ffn1_w_up.astype(BF16), 'ffn1_w_down': ffn1_w_down.astype(BF16),
        'ln1_g': vec(ln1_g), 'ln1_b': vec(ln1_b), 'w_in': w_in.astype(BF16), 'conv_w': conv_w,
        'ret_gn_g': vec(ret_gn_g), 'w_branch': w_branch.astype(BF16), 'w_gate': w_gate.astype(BF16),
        'b_gate': b_gate.reshape(depth, N_BRANCH, 1, d), 'w_o': w_o.astype(BF16),
        'ln2_g': vec(ln2_g), 'ln2_b': vec(ln2_b),
        'ffn2_w_up': ffn2_w_up.astype(BF16), 'ffn2_w_down': ffn2_w_down.astype(BF16),
        'ln3_g': vec(ln3_g), 'ln3_b': vec(ln3_b),
    }
    w_mem16 = w_mem_kv.astype(BF16)
    subln_col = diff_subln_g.reshape(depth, DIFF_DV, 1)
    subln_row = diff_subln_g.reshape(depth, 1, DIFF_DV)
    kpast = cache_diff_k.reshape(depth, bs, past, DIFF_HEADS * 2 * DIFF_D)
    vpast = cache_diff_v.reshape(depth, bs, past, DIFF_HEADS * DIFF_DV)
    mem_k_s = cache_mem_k.reshape(depth, bs, mem_tokens, half)
    mem_v_s = cache_mem_v.reshape(depth, bs, mem_tokens, half)

    yp = x_prompt.reshape(bp * lp, d)
    ys = x_sample.reshape(bs * ls, d)
    mem16 = mem_prompt.reshape(bp * mem_tokens, d).astype(BF16)
    zero_conv = jnp.zeros((bp, CONV_WIDTH - 1, BRANCH_W), F32)
    zero_ret = jnp.zeros((bp, RET_HEADS, RET_DK, RET_DK), F32)

    outs_p = [[] for _ in range(6)]
    outs_s = [[] for _ in range(4)]
    for l in range(depth):
        mkv = _matmul(mem16, w_mem16, l, tm=bp * mem_tokens, tn=512).reshape(bp, mem_tokens, 2 * half)
        diff_p = lambda proj, lam_init: _diff_prompt(
            proj, bias_p, diff_lambda, subln_col, l, bsz=bp, length=lp, t=PROMPT_CFG['t_diff'], lam_init=lam_init)
        mem_p = lambda proj: _mem_attention(
            proj, mkv, mkv, k_index=lambda b, h: (b, 0, h), v_index=lambda b, h: (b, 0, MEM_HEADS + h),
            bsz=bp, length=lp, tq=PROMPT_CFG['tq_mem'])
        yp, (c_new, r_new, k_new, v_new) = _encoder_layer(
            yp, w, l, bsz=bp, length=lp, alpha=alpha, cfg=PROMPT_CFG, rope=rope_p, diff_fn=diff_p,
            conv_prev=zero_conv, ret_prev=zero_ret, mem_fn=mem_p)
        for acc, val in zip(outs_p, (c_new, r_new, k_new, v_new,
                                     mkv[:, :, :half].reshape(bp, mem_tokens, MEM_HEADS, MEM_HD),
                                     mkv[:, :, half:].reshape(bp, mem_tokens, MEM_HEADS, MEM_HD))):
            acc.append(val)
        diff_s = lambda proj, lam_init: _diff_sample(
            proj, kpast, vpast, bias_s_past, bias_s_new, diff_lambda, subln_row, l,
            bsz=bs, length=ls, lam_init=lam_init)
        mem_s = lambda proj: _mem_attention(
            proj, mem_k_s, mem_v_s, k_index=lambda b, h: (l, b, 0, h), v_index=lambda b, h: (l, b, 0, h),
            bsz=bs, length=ls, tq=SAMPLE_CFG['tq_mem'])
        ys, (c_new, r_new, k_new, v_new) = _encoder_layer(
            ys, w, l, bsz=bs, length=ls, alpha=alpha, cfg=SAMPLE_CFG, rope=rope_s, diff_fn=diff_s,
            conv_prev=state_conv[l], ret_prev=state_ret[l], mem_fn=mem_s)
        for acc, val in zip(outs_s, (c_new, r_new, k_new, v_new)):
            acc.append(val)

    return (yp.reshape(bp, lp, d), ys.reshape(bs, ls, d),
            *(jnp.stack(v) for v in outs_p), *(jnp.stack(v) for v in outs_s))
```

```python
import functools
import math

import jax
import jax.numpy as jnp
from jax import lax
from jax.experimental import pallas as pl
from jax.experimental.pallas import tpu as pltpu

F32 = jnp.float32
BF16 = jnp.bfloat16

CHUNK = 64
CONV_WIDTH = 3
RET_HEADS = 4
RET_DK = 128
DIFF_HEADS = 4
DIFF_D = 64
DIFF_DV = 128
MEM_HEADS = 4
MEM_HD = 128
BRANCH_W = 512
N_BRANCH = 4
REL_BUCKETS = 32
REL_MAX_DIST = 128
LN_EPS = 1e-5
ROPE_BASE = 10000.0
NEG_INF = -1e30
HEAD_W = 128
LOG2E = math.log2(math.e)
LN_ROW_CHUNK = 128
FFN_UP_CHUNK = 256
FFN_DOWN_CHUNK = 512

COL_CB, COL_CC, COL_CH, COL_RQ, COL_RK, COL_RV, COL_RG, COL_DQ, COL_DK, COL_DV, COL_MQ = range(11)
HEADS_PER_BRANCH = BRANCH_W // HEAD_W

V7X_VMEM_BYTES = 64 * 1024 * 1024
MIB = 1024 * 1024


def _params(semantics, vmem_mib):
    assert vmem_mib * MIB < V7X_VMEM_BYTES
    return pltpu.CompilerParams(dimension_semantics=semantics, vmem_limit_bytes=vmem_mib * MIB)


def _layer_norm(y, g, b):
    mu = jnp.mean(y, -1, keepdims=True)
    d = y - mu
    var = jnp.mean(d * d, -1, keepdims=True)
    return d * lax.rsqrt(var + LN_EPS) * g + b


def _dot(a, b):
    return jnp.dot(a, b, preferred_element_type=F32)


def _dot_nt(a, b):
    return lax.dot_general(a, b, (((1,), (1,)), ((), ())), preferred_element_type=F32)


def _dot_tn(a, b):
    return lax.dot_general(a, b, (((0,), (0,)), ((), ())), preferred_element_type=F32)


def _ffn_ln_kernel(x_ref, wa_ref, wb_ref, wd_ref, g_ref, b_ref, o_ref, xb_ref, h_ref, *, alpha):
    j = pl.program_id(1)

    @pl.when(j == 0)
    def _():
        xb_ref[...] = x_ref[...].astype(BF16)
        o_ref[...] = jnp.zeros(o_ref.shape, F32)

    xb = xb_ref[...]
    for c in range(0, h_ref.shape[1], FFN_UP_CHUNK):
        cols = slice(c, c + FFN_UP_CHUNK)
        a = _dot(xb, wa_ref[:, cols])
        b = _dot(xb, wb_ref[:, cols])
        h_ref[:, cols] = (a * jax.nn.sigmoid(a) * b).astype(BF16)
    h = h_ref[...]
    for c in range(0, o_ref.shape[1], FFN_DOWN_CHUNK):
        cols = slice(c, c + FFN_DOWN_CHUNK)
        o_ref[:, cols] += _dot(h, wd_ref[:, cols])

    @pl.when(j == pl.num_programs(1) - 1)
    def _():
        for r in range(0, o_ref.shape[0], LN_ROW_CHUNK):
            rows = slice(r, r + LN_ROW_CHUNK)
            o_ref[rows, :] = _layer_norm(alpha * x_ref[rows, :] + 0.5 * o_ref[rows, :], g_ref[...], b_ref[...])


def _ffn_ln(x, w_up, w_down, g, b, layer, *, alpha, tm, tf):
    m, d = x.shape
    d_ff = w_down.shape[1]
    nj = d_ff // tf
    assert m % tm == 0 and d_ff % tf == 0
    row = lambda i, j: (i, 0)
    vec = pl.BlockSpec((None, 1, d), lambda i, j: (layer, 0, 0))
    return pl.pallas_call(
        functools.partial(_ffn_ln_kernel, alpha=alpha),
        grid=(m // tm, nj),
        in_specs=[
            pl.BlockSpec((tm, d), row),
            pl.BlockSpec((None, d, tf), lambda i, j: (layer, 0, j)),
            pl.BlockSpec((None, d, tf), lambda i, j: (layer, 0, j + nj)),
            pl.BlockSpec((None, tf, d), lambda i, j: (layer, j, 0)),
            vec, vec,
        ],
        out_specs=pl.BlockSpec((tm, d), row),
        out_shape=jax.ShapeDtypeStruct((m, d), F32),
        scratch_shapes=[pltpu.VMEM((tm, d), BF16), pltpu.VMEM((tm, tf), BF16)],
        compiler_params=_params(("parallel", "arbitrary"), 60),
        name="ffn_ln",
    )(x, w_up, w_up, w_down, g, b)


def _in_proj_kernel(x_ref, w_ref, o_ref, xb_ref):
    @pl.when(pl.program_id(1) == 0)
    def _():
        xb_ref[...] = x_ref[...].astype(BF16)

    o_ref[...] = _dot(xb_ref[...], w_ref[...])


def _in_proj(x, w, layer, *, tm, tn):
    m, k = x.shape
    n = w.shape[2]
    assert m % tm == 0 and n % tn == 0
    return pl.pallas_call(
        _in_proj_kernel,
        grid=(m // tm, n // tn),
        in_specs=[pl.BlockSpec((tm, k), lambda i, j: (i, 0)),
                  pl.BlockSpec((None, k, tn), lambda i, j: (layer, 0, j))],
        out_specs=[pl.BlockSpec((tm, tn), lambda i, j: (i, j)), pl.BlockSpec((tm, k), lambda i, j: (i, 0))],
        out_shape=[jax.ShapeDtypeStruct((m, n), F32), jax.ShapeDtypeStruct((m, k), BF16)],
        compiler_params=_params(("parallel", "arbitrary"), 56),
        name="in_proj",
    )(x, w)


def _matmul_kernel(x_ref, w_ref, o_ref):
    o_ref[...] = _dot(x_ref[...], w_ref[...])


def _matmul(x, w, layer, *, tm, tn):
    m, k = x.shape
    n = w.shape[2]
    assert m % tm == 0 and n % tn == 0
    return pl.pallas_call(
        _matmul_kernel,
        grid=(m // tm, n // tn),
        in_specs=[pl.BlockSpec((tm, k), lambda i, j: (i, 0)),
                  pl.BlockSpec((None, k, tn), lambda i, j: (layer, 0, j))],
        out_specs=pl.BlockSpec((tm, tn), lambda i, j: (i, j)),
        out_shape=jax.ShapeDtypeStruct((m, n), F32),
        compiler_params=_params(("parallel", "parallel"), 48),
        name="matmul",
    )(x, w)


def _conv_kernel(cb_ref, cc_ref, ch_ref, w_ref, prev_ref, y_ref, state_ref, carry_ref):
    l = pl.program_id(1)

    @pl.when(l == 0)
    def _():
        carry_ref[...] = prev_ref[...]

    u = cc_ref[...] * ch_ref[...]
    row = lax.broadcasted_iota(jnp.int32, u.shape, 0)
    c0 = carry_ref[0:1, :]
    c1 = carry_ref[1:2, :]
    u1 = jnp.where(row == 0, c1, pltpu.roll(u, 1, 0))
    u2 = jnp.where(row == 0, c0, jnp.where(row == 1, c1, pltpu.roll(u, 2, 0)))
    z = w_ref[0:1, :] * u2 + w_ref[1:2, :] * u1 + w_ref[2:3, :] * u
    y_ref[...] = (cb_ref[...] * z).astype(BF16)
    tl = u.shape[0]
    tail = u[tl - 8:, :]
    carry_ref[...] = tail[6:8, :]

    @pl.when(l == pl.num_programs(1) - 1)
    def _():
        state_ref[...] = tail[6:8, :]


def _conv_branch(proj, conv_w, prev, layer, *, bsz, length, tl):
    m = proj.shape[0]
    nl = length // tl
    assert length % tl == 0 and tl >= 8
    col = lambda c: pl.BlockSpec((tl, BRANCH_W), lambda b, l: (b * nl + l, c))
    return pl.pallas_call(
        _conv_kernel,
        grid=(bsz, nl),
        in_specs=[
            col(COL_CB), col(COL_CC), col(COL_CH),
            pl.BlockSpec((None, CONV_WIDTH, BRANCH_W), lambda b, l: (layer, 0, 0)),
            pl.BlockSpec((None, CONV_WIDTH - 1, BRANCH_W), lambda b, l: (b, 0, 0)),
        ],
        out_specs=[
            pl.BlockSpec((tl, BRANCH_W), lambda b, l: (b * nl + l, 0)),
            pl.BlockSpec((None, CONV_WIDTH - 1, BRANCH_W), lambda b, l: (b, 0, 0)),
        ],
        out_shape=[
            jax.ShapeDtypeStruct((m, BRANCH_W), BF16),
            jax.ShapeDtypeStruct((bsz, CONV_WIDTH - 1, BRANCH_W), F32),
        ],
        scratch_shapes=[pltpu.VMEM((CONV_WIDTH - 1, BRANCH_W), F32)],
        compiler_params=_params(("parallel", "arbitrary"), 32),
        name="conv_branch",
    )(proj, proj, proj, conv_w, prev)


def _ret_log_gamma(h):
    return math.log1p(-(2.0 ** (-5.0 - h)))


def _retention_kernel(q_ref, k_ref, v_ref, g_ref, cos_ref, sin_ref, s0_ref, gn_ref, y_ref, sfin_ref, s_ref):
    c = pl.program_id(1)

    @pl.when(c == 0)
    def _():
        s_ref[...] = s0_ref[...]

    chunk = q_ref.shape[0]
    cosf = cos_ref[...]
    sinf = sin_ref[...]
    ri = lax.broadcasted_iota(jnp.int32, (chunk, chunk), 0)
    ci = lax.broadcasted_iota(jnp.int32, (chunk, chunk), 1)
    rel = (ri - ci).astype(F32)
    idx = lax.broadcasted_iota(jnp.int32, (chunk, 1), 0).astype(F32)
    for h in range(RET_HEADS):
        log_g = _ret_log_gamma(h)
        cols = slice(h * RET_DK, (h + 1) * RET_DK)
        q = q_ref[:, cols]
        k = k_ref[:, cols]
        q = q * cosf + pltpu.roll(q, RET_DK // 2, 1) * sinf
        k = (k * cosf + pltpu.roll(k, RET_DK // 2, 1) * sinf) * (RET_DK ** -0.5)
        vb = v_ref[:, cols].astype(BF16)
        intra = jnp.where(rel >= 0.0, jnp.exp(jnp.maximum(rel, 0.0) * log_g), 0.0)
        q_dec = jnp.exp((idx + 1.0) * log_g)
        k_dec = jnp.exp((chunk - 1.0 - idx) * log_g)
        c_dec = math.exp(chunk * log_g)
        qb = q.astype(BF16)
        s_prev = s_ref[h]
        att = _dot_nt(qb, k.astype(BF16)) * intra
        o = _dot(att.astype(BF16), vb) + _dot(qb, s_prev.astype(BF16)) * q_dec
        s_ref[h] = s_prev * c_dec + _dot_tn((k * k_dec).astype(BF16), vb)
        mu = jnp.mean(o, -1, keepdims=True)
        d = o - mu
        var = jnp.mean(d * d, -1, keepdims=True)
        ro = d * lax.rsqrt(var + LN_EPS) * gn_ref[:, cols]
        gate = g_ref[:, cols]
        y_ref[:, cols] = (gate * jax.nn.sigmoid(gate) * ro).astype(BF16)

    @pl.when(c == pl.num_programs(1) - 1)
    def _():
        sfin_ref[...] = s_ref[...]


def _retention_branch(proj, cosf, sinf, s0, gn_g, layer, *, bsz, length, chunk):
    m = proj.shape[0]
    nc = length // chunk
    assert length % chunk == 0
    col = lambda c: pl.BlockSpec((chunk, BRANCH_W), lambda b, i: (b * nc + i, c))
    state = pl.BlockSpec((None, RET_HEADS, RET_DK, RET_DK), lambda b, i: (b, 0, 0, 0))
    return pl.pallas_call(
        _retention_kernel,
        grid=(bsz, nc),
        in_specs=[
            col(COL_RQ), col(COL_RK), col(COL_RV), col(COL_RG),
            pl.BlockSpec((chunk, RET_DK), lambda b, i: (i, 0)),
            pl.BlockSpec((chunk, RET_DK), lambda b, i: (i, 0)),
            state,
            pl.BlockSpec((None, 1, BRANCH_W), lambda b, i: (layer, 0, 0)),
        ],
        out_specs=[pl.BlockSpec((chunk, BRANCH_W), lambda b, i: (b * nc + i, 0)), state],
        out_shape=[
            jax.ShapeDtypeStruct((m, BRANCH_W), BF16),
            jax.ShapeDtypeStruct((bsz, RET_HEADS, RET_DK, RET_DK), F32),
        ],
        scratch_shapes=[pltpu.VMEM((RET_HEADS, RET_DK, RET_DK), F32)],
        compiler_params=_params(("parallel", "arbitrary"), 32),
        name="retention_branch",
    )(proj, proj, proj, proj, cosf, sinf, s0, gn_g)


def _diff_lambda(lam_ref, lam_init):
    e0 = jnp.exp(jnp.sum(lam_ref[0:1, :] * lam_ref[1:2, :], -1, keepdims=True))
    e1 = jnp.exp(jnp.sum(lam_ref[2:3, :] * lam_ref[3:4, :], -1, keepdims=True))
    return e0 - e1 + lam_init


def _diff_prompt_kernel(q_ref, k_ref, v_ref, bias_ref, lam_ref, g_ref, o_ref, k16_ref, vt_ref,
                        m0_ref, m1_ref, l0_ref, l1_ref, acc0_ref, acc1_ref, *, lam_init):
    i = pl.program_id(2)
    t = q_ref.shape[0]
    nt = k_ref.shape[0] // t
    m_refs, l_refs, acc_refs = (m0_ref, m1_ref), (l0_ref, l1_ref), (acc0_ref, acc1_ref)

    @pl.when(i == 0)
    def _():
        k16_ref[...] = k_ref[...].astype(BF16)
        for j in range(nt):
            vt_ref[j] = jnp.transpose(v_ref[j * t:(j + 1) * t, :]).astype(BF16)

    qt = jnp.transpose(q_ref[...] * (DIFF_D ** -0.5 * LOG2E))
    feat = lax.broadcasted_iota(jnp.int32, qt.shape, 0)
    qts = (jnp.where(feat < DIFF_D, qt, 0.0).astype(BF16), jnp.where(feat >= DIFF_D, qt, 0.0).astype(BF16))
    for c in range(2):
        m_refs[c][...] = jnp.full(m_refs[c].shape, NEG_INF, F32)
        l_refs[c][...] = jnp.zeros(l_refs[c].shape, F32)
        acc_refs[c][...] = jnp.zeros(acc_refs[c].shape, F32)

    def tile(j, kind):
        kt = k16_ref[pl.ds(pl.multiple_of(j * t, t), t), :]
        vt = vt_ref[j]
        bias = bias_ref[kind]
        s = [_dot(kt, qts[c]) + bias for c in range(2)]
        m_prev = [m_refs[c][...] for c in range(2)]
        m_new = [jnp.maximum(m_prev[c], jnp.max(s[c], 0, keepdims=True)) for c in range(2)]
        p = [jnp.exp2(s[c] - m_new[c]) for c in range(2)]
        a = [jnp.exp2(m_prev[c] - m_new[c]) for c in range(2)]
        for c in range(2):
            l_refs[c][...] = a[c] * l_refs[c][...] + jnp.sum(p[c], 0, keepdims=True)
            acc_refs[c][...] = a[c] * acc_refs[c][...] + _dot(vt, p[c].astype(BF16))
            m_refs[c][...] = m_new[c]

    def far_body(j, carry):
        tile(j, 2)
        return carry

    lax.fori_loop(0, jnp.maximum(i - 1, 0), far_body, 0)

    @pl.when(i >= 1)
    def _():
        tile(i - 1, 1)

    tile(i, 0)
    lam = _diff_lambda(lam_ref, lam_init)
    o = acc0_ref[...] / l0_ref[...] - lam * (acc1_ref[...] / l1_ref[...])
    o = o * lax.rsqrt(jnp.mean(o * o, 0, keepdims=True) + LN_EPS) * g_ref[...]
    o_ref[...] = jnp.transpose(o * (1.0 - lam_init)).astype(BF16)


def _diff_prompt(proj, bias_tiles, diff_lambda, subln_g, layer, *, bsz, length, t, lam_init):
    m = proj.shape[0]
    nq = length // t
    assert length % t == 0 and t % CHUNK == 0
    kv = lambda c: pl.BlockSpec((length, HEAD_W), lambda b, h, i: (b, c * HEADS_PER_BRANCH + h))
    stat = pltpu.VMEM((1, t), F32)
    acc = pltpu.VMEM((DIFF_DV, t), F32)
    return pl.pallas_call(
        functools.partial(_diff_prompt_kernel, lam_init=lam_init),
        grid=(bsz, DIFF_HEADS, nq),
        in_specs=[
            pl.BlockSpec((t, HEAD_W), lambda b, h, i: (b * nq + i, COL_DQ * HEADS_PER_BRANCH + h)),
            kv(COL_DK), kv(COL_DV),
            pl.BlockSpec((None, 3, t, t), lambda b, h, i: (h, 0, 0, 0)),
            pl.BlockSpec((None, 4, DIFF_D), lambda b, h, i: (layer, 0, 0)),
            pl.BlockSpec((None, DIFF_DV, 1), lambda b, h, i: (layer, 0, 0)),
        ],
        out_specs=pl.BlockSpec((t, HEAD_W), lambda b, h, i: (b * nq + i, h)),
        out_shape=jax.ShapeDtypeStruct((m, BRANCH_W), BF16),
        scratch_shapes=[pltpu.VMEM((length, HEAD_W), BF16), pltpu.VMEM((nq, DIFF_DV, t), BF16),
                        stat, stat, stat, stat, acc, acc],
        compiler_params=_params(("parallel", "parallel", "arbitrary"), 40),
        name="diff_attention_prompt",
    )(proj, proj, proj, bias_tiles, diff_lambda, subln_g)


def _head_rows(ref, h, n):
    return ref[pl.ds(h, n, stride=HEADS_PER_BRANCH), :]


def _sample_attn_kernel(q_ref, kn_ref, vn_ref, mq_ref, kp_ref, vp_ref, mk_ref, mv_ref, bp_ref, bn_ref,
                        lam_ref, g_ref, yc_ref, yd_ref, *, lam_init):
    past = kp_ref.shape[0] // HEADS_PER_BRANCH
    tokens = mk_ref.shape[0] // HEADS_PER_BRANCH
    lam = _diff_lambda(lam_ref, lam_init)
    for h in range(HEADS_PER_BRANCH):
        cols = slice(h * HEAD_W, (h + 1) * HEAD_W)
        q = q_ref[:, cols] * (DIFF_D ** -0.5 * LOG2E)
        lane = lax.broadcasted_iota(jnp.int32, q.shape, 1)
        qs = (jnp.where(lane < DIFF_D, q, 0.0).astype(BF16), jnp.where(lane >= DIFF_D, q, 0.0).astype(BF16))
        kp = _head_rows(kp_ref, h, past).astype(BF16)
        vp = _head_rows(vp_ref, h, past).astype(BF16)
        kn = kn_ref[:, cols].astype(BF16)
        vn = vn_ref[:, cols].astype(BF16)
        outs = []
        for qc in qs:
            sp = _dot_nt(qc, kp) + bp_ref[h]
            sn = _dot_nt(qc, kn) + bn_ref[h]
            mx = jnp.maximum(jnp.max(sp, -1, keepdims=True), jnp.max(sn, -1, keepdims=True))
            pp = jnp.exp2(sp - mx)
            pn = jnp.exp2(sn - mx)
            den = jnp.sum(pp, -1, keepdims=True) + jnp.sum(pn, -1, keepdims=True)
            outs.append((_dot(pp.astype(BF16), vp) + _dot(pn.astype(BF16), vn)) / den)
        o = outs[0] - lam * outs[1]
        o = o * lax.rsqrt(jnp.mean(o * o, -1, keepdims=True) + LN_EPS) * g_ref[...]
        yc_ref[:, cols] = (o * (1.0 - lam_init)).astype(BF16)

        mk = _head_rows(mk_ref, h, tokens).astype(BF16)
        mv = _head_rows(mv_ref, h, tokens).astype(BF16)
        s = _dot_nt(mq_ref[:, cols].astype(BF16), mk) * (MEM_HD ** -0.5)
        p = jnp.exp(s - jnp.max(s, -1, keepdims=True))
        den = jnp.sum(p, -1, keepdims=True)
        yd_ref[:, cols] = (_dot(p.astype(BF16), mv) / den).astype(BF16)


def _sample_attention(proj, k_past, v_past, mem_k, mem_v, bias_past, bias_new, diff_lambda, subln_g, layer,
                      *, bsz, length, lam_init):
    m = proj.shape[0]
    col = lambda c: pl.BlockSpec((length, BRANCH_W), lambda b: (b, c))
    cache = lambda a: pl.BlockSpec((None, None) + a.shape[2:], lambda b: (layer, b, 0, 0))
    whole = lambda a: pl.BlockSpec(a.shape, lambda b: (0,) * a.ndim)
    out = pl.BlockSpec((length, BRANCH_W), lambda b: (b, 0))
    return pl.pallas_call(
        functools.partial(_sample_attn_kernel, lam_init=lam_init),
        grid=(bsz,),
        in_specs=[
            col(COL_DQ), col(COL_DK), col(COL_DV), col(COL_MQ),
            cache(k_past), cache(v_past), cache(mem_k), cache(mem_v),
            whole(bias_past), whole(bias_new),
            pl.BlockSpec((None, 4, DIFF_D), lambda b: (layer, 0, 0)),
            pl.BlockSpec((None, 1, DIFF_DV), lambda b: (layer, 0, 0)),
        ],
        out_specs=[out, out],
        out_shape=[jax.ShapeDtypeStruct((m, BRANCH_W), BF16), jax.ShapeDtypeStruct((m, BRANCH_W), BF16)],
        compiler_params=_params(("parallel",), 40),
        name="sample_attention",
    )(proj, proj, proj, proj, k_past, v_past, mem_k, mem_v, bias_past, bias_new, diff_lambda, subln_g)


def _mem_attn_kernel(q_ref, k_ref, v_ref, o_ref):
    s = _dot_nt(q_ref[...].astype(BF16), k_ref[...].astype(BF16)) * (MEM_HD ** -0.5)
    p = jnp.exp(s - jnp.max(s, -1, keepdims=True))
    den = jnp.sum(p, -1, keepdims=True)
    o_ref[...] = (_dot(p.astype(BF16), v_ref[...].astype(BF16)) / den).astype(BF16)


def _mem_attention(proj, mem_k, mem_v, *, k_index, v_index, bsz, length, tq):
    m = proj.shape[0]
    nq = length // tq
    assert length % tq == 0
    tokens = mem_k.shape[-2]
    block = (None,) * (mem_k.ndim - 2) + (tokens, HEAD_W)
    return pl.pallas_call(
        _mem_attn_kernel,
        grid=(bsz, MEM_HEADS, nq),
        in_specs=[pl.BlockSpec((tq, HEAD_W), lambda b, h, i: (b * nq + i, COL_MQ * HEADS_PER_BRANCH + h)),
                  pl.BlockSpec(block, lambda b, h, i: k_index(b, h)),
                  pl.BlockSpec(block, lambda b, h, i: v_index(b, h))],
        out_specs=pl.BlockSpec((tq, HEAD_W), lambda b, h, i: (b * nq + i, h)),
        out_shape=jax.ShapeDtypeStruct((m, BRANCH_W), BF16),
        compiler_params=_params(("parallel", "parallel", "parallel"), 32),
        name="mem_attention",
    )(proj, mem_k, mem_v)


def _merge_kernel(h_ref, ya_ref, yb_ref, yc_ref, yd_ref, wg_ref, wb_ref, bg_ref, o_ref):
    h = h_ref[...]
    acc = None
    for i, y_ref in enumerate((ya_ref, yb_ref, yc_ref, yd_ref)):
        gate = jax.nn.sigmoid(_dot(h, wg_ref[i]) + bg_ref[i])
        term = gate * _dot(y_ref[...], wb_ref[i])
        acc = term if acc is None else acc + term
    o_ref[...] = acc.astype(BF16)


def _merge(h16, ys, w_gate, w_branch, b_gate, layer, *, tm, tn):
    m, d = h16.shape
    assert m % tm == 0 and d % tn == 0
    y_spec = pl.BlockSpec((tm, BRANCH_W), lambda i, j: (i, 0))
    return pl.pallas_call(
        _merge_kernel,
        grid=(m // tm, d // tn),
        in_specs=[
            pl.BlockSpec((tm, d), lambda i, j: (i, 0)), y_spec, y_spec, y_spec, y_spec,
            pl.BlockSpec((None, N_BRANCH, d, tn), lambda i, j: (layer, 0, 0, j)),
            pl.BlockSpec((None, N_BRANCH, BRANCH_W, tn), lambda i, j: (layer, 0, 0, j)),
            pl.BlockSpec((None, N_BRANCH, 1, tn), lambda i, j: (layer, 0, 0, j)),
        ],
        out_specs=pl.BlockSpec((tm, tn), lambda i, j: (i, j)),
        out_shape=jax.ShapeDtypeStruct((m, d), BF16),
        compiler_params=_params(("parallel", "parallel"), 48),
        name="gated_merge",
    )(h16, *ys, w_gate, w_branch, b_gate)


def _proj_ln_kernel(m_ref, x_ref, w_ref, g_ref, b_ref, o_ref, *, alpha):
    y = alpha * x_ref[...] + _dot(m_ref[...], w_ref[...])
    o_ref[...] = _layer_norm(y, g_ref[...], b_ref[...])


def _proj_ln(merged, x, w_o, g, b, layer, *, alpha, tm):
    m, d = x.shape
    assert m % tm == 0
    row = pl.BlockSpec((tm, d), lambda i: (i, 0))
    vec = pl.BlockSpec((None, 1, d), lambda i: (layer, 0, 0))
    return pl.pallas_call(
        functools.partial(_proj_ln_kernel, alpha=alpha),
        grid=(m // tm,),
        in_specs=[row, row, pl.BlockSpec((None, d, d), lambda i: (layer, 0, 0)), vec, vec],
        out_specs=row,
        out_shape=jax.ShapeDtypeStruct((m, d), F32),
        compiler_params=_params(("parallel",), 48),
        name="proj_ln",
    )(merged, x, w_o, g, b)


def _rope_tables(pos):
    half = RET_DK // 2
    inv = ROPE_BASE ** (-jnp.arange(half, dtype=F32) / half)
    ang = pos.astype(F32)[:, None] * inv[None, :]
    cos = jnp.cos(ang)
    sin = jnp.sin(ang)
    return jnp.concatenate([cos, cos], -1), jnp.concatenate([-sin, sin], -1)


def _t5_bucket(rel):
    nb = REL_BUCKETS // 2
    max_exact = nb // 2
    n = jnp.abs(rel)
    nf = jnp.maximum(n, 1).astype(F32)
    large = max_exact + (jnp.log(nf / max_exact) / math.log(REL_MAX_DIST / max_exact)
                         * (nb - max_exact)).astype(jnp.int32)
    large = jnp.minimum(large, nb - 1)
    return jnp.where(rel > 0, nb, 0) + jnp.where(n < max_exact, n, large)


def _masked_bias(q_pos, k_pos, rel_bias):
    bucket = _t5_bucket(k_pos[None, :] - q_pos[:, None])
    onehot = bucket[None, None] == jnp.arange(REL_BUCKETS, dtype=bucket.dtype)[None, :, None, None]
    bias = jnp.sum(jnp.where(onehot, rel_bias.astype(F32).T[:, :, None, None], 0.0), axis=1)
    allowed = (k_pos[None, :] // CHUNK) <= (q_pos[:, None] // CHUNK)
    return jnp.where(allowed[None], bias * LOG2E, NEG_INF)


def _prompt_bias_tiles(rel_bias, t):
    assert t % CHUNK == 0 and t + 1 >= REL_MAX_DIST
    base = jnp.arange(t, dtype=jnp.int32)
    tiles = jnp.stack([_masked_bias(base + d * t, base, rel_bias) for d in range(3)], axis=1)
    return jnp.swapaxes(tiles, -1, -2)


def _kv_export_kernel(k_ref, v_ref, *rest):
    ok_ref, ov_ref = rest[-2:]
    n = k_ref.shape[0]
    for h in range(HEADS_PER_BRANCH):
        cols = slice(h * HEAD_W, (h + 1) * HEAD_W)
        rows = pl.ds(h, n, stride=HEADS_PER_BRANCH)
        ok_ref[rows, :] = k_ref[:, cols]
        ov_ref[rows, :] = v_ref[:, cols]


def _kv_export(proj, prev, layer, depth, *, tm):
    m = proj.shape[0]
    assert m % tm == 0
    col = lambda c: pl.BlockSpec((tm, BRANCH_W), lambda i: (i, c))
    out = pl.BlockSpec((None, tm * HEADS_PER_BRANCH, HEAD_W), lambda i: (layer, i, 0))
    shape = jax.ShapeDtypeStruct((depth, m * HEADS_PER_BRANCH, HEAD_W), F32)
    keep = [pl.BlockSpec(memory_space=pl.ANY)] * len(prev)
    return pl.pallas_call(
        _kv_export_kernel,
        grid=(m // tm,),
        in_specs=[col(COL_DK), col(COL_DV)] + keep,
        out_specs=[out, out],
        out_shape=[shape, shape],
        input_output_aliases={2 + n: n for n in range(len(prev))},
        compiler_params=_params(("parallel",), 32),
        name="kv_export",
    )(proj, proj, *prev)


def _encoder_layer(x, w, layer, *, bsz, length, alpha, cfg, rope, attn_fn, conv_prev, ret_prev):
    x1 = _ffn_ln(x, w['ffn1_w_up'], w['ffn1_w_down'], w['ln1_g'], w['ln1_b'], layer,
                 alpha=alpha, tm=cfg['tm_ffn'], tf=cfg['tf'])
    proj, x1b = _in_proj(x1, w['w_in'], layer, tm=cfg['tm_in'], tn=cfg['tn_in'])
    y_a, conv_new = _conv_branch(proj, w['conv_w'], conv_prev, layer, bsz=bsz, length=length, tl=cfg['t_conv'])
    y_b, ret_new = _retention_branch(proj, rope[0], rope[1], ret_prev, w['ret_gn_g'], layer,
                                     bsz=bsz, length=length, chunk=cfg['ret_chunk'])
    y_c, y_d = attn_fn(proj, 0.8 - 0.6 * math.exp(-0.3 * layer))
    merged = _merge(x1b, (y_a, y_b, y_c, y_d), w['w_gate'], w['w_branch'], w['b_gate'], layer,
                    tm=cfg['tm_merge'], tn=cfg['tn_merge'])
    x2 = _proj_ln(merged, x1, w['w_o'], w['ln2_g'], w['ln2_b'], layer, alpha=alpha, tm=cfg['tm_proj'])
    x3 = _ffn_ln(x2, w['ffn2_w_up'], w['ffn2_w_down'], w['ln3_g'], w['ln3_b'], layer,
                 alpha=alpha, tm=cfg['tm_ffn'], tf=cfg['tf'])
    return x3, proj, conv_new, ret_new


PROMPT_CFG = dict(tm_ffn=1024, tf=512, tm_in=1024, tn_in=1408, t_conv=512, ret_chunk=256, t_diff=512,
                  tq_mem=1024, tm_merge=1024, tn_merge=256, tm_proj=512, tm_export=1024)
SAMPLE_CFG = dict(tm_ffn=256, tf=512, tm_in=256, tn_in=512, t_conv=16, ret_chunk=16,
                  tm_merge=256, tn_merge=256, tm_proj=256, tm_export=256)


def kernel(x_prompt, x_sample, state_conv, state_ret, cache_diff_k, cache_diff_v, cache_mem_k, cache_mem_v,
           mem_prompt, ffn1_w_up, ffn1_w_down, ln1_g, ln1_b, w_in, conv_w, ret_gn_g, diff_lambda,
           diff_subln_g, w_mem_kv, w_branch, w_gate, b_gate, w_o, ln2_g, ln2_b, ffn2_w_up, ffn2_w_down,
           ln3_g, ln3_b, rel_bias):
    bp, lp, d = x_prompt.shape
    bs, ls, _ = x_sample.shape
    depth = w_in.shape[0]
    past = cache_diff_k.shape[2]
    mem_tokens = mem_prompt.shape[1]
    alpha = (2 * depth) ** 0.25
    half = MEM_HEADS * MEM_HD

    pos_p = jnp.arange(lp, dtype=jnp.int32)
    pos_s = past + jnp.arange(ls, dtype=jnp.int32)
    past_pos = jnp.arange(past, dtype=jnp.int32)
    rope_p = _rope_tables(pos_p)
    rope_s = _rope_tables(pos_s)
    bias_p = _prompt_bias_tiles(rel_bias, PROMPT_CFG['t_diff'])
    bias_s_past = _masked_bias(pos_s, past_pos, rel_bias)
    bias_s_new = _masked_bias(pos_s, pos_s, rel_bias)

    vec = lambda v: v.reshape(depth, 1, v.shape[-1])
    w = {
        'ffn1_w_up': ffn1_w_up.astype(BF16), 'ffn1_w_down': ffn1_w_down.astype(BF16),
        'ln1_g': vec(ln1_g), 'ln1_b': vec(ln1_b), 'w_in': w_in.astype(BF16), 'conv_w': conv_w,
        'ret_gn_g': vec(ret_gn_g), 'w_branch': w_branch.astype(BF16), 'w_gate': w_gate.astype(BF16),
        'b_gate': b_gate.reshape(depth, N_BRANCH, 1, d), 'w_o': w_o.astype(BF16),
        'ln2_g': vec(ln2_g), 'ln2_b': vec(ln2_b),
        'ffn2_w_up': ffn2_w_up.astype(BF16), 'ffn2_w_down': ffn2_w_down.astype(BF16),
        'ln3_g': vec(ln3_g), 'ln3_b': vec(ln3_b),
    }
    w_mem16 = w_mem_kv.astype(BF16)
    subln_col = diff_subln_g.reshape(depth, DIFF_DV, 1)
    subln_row = diff_subln_g.reshape(depth, 1, DIFF_DV)
    head_major = lambda c: c.reshape(depth, bs, c.shape[2] * c.shape[3], c.shape[4])
    kpast, vpast = head_major(cache_diff_k), head_major(cache_diff_v)
    mem_k_s, mem_v_s = head_major(cache_mem_k), head_major(cache_mem_v)

    yp = x_prompt.reshape(bp * lp, d)
    ys = x_sample.reshape(bs * ls, d)
    mem16 = mem_prompt.reshape(bp * mem_tokens, d).astype(BF16)
    zero_conv = jnp.zeros((bp, CONV_WIDTH - 1, BRANCH_W), F32)
    zero_ret = jnp.zeros((bp, RET_HEADS, RET_DK, RET_DK), F32)

    conv_p, ret_p, mk_p, mv_p, conv_s, ret_s = [], [], [], [], [], []
    kv_p, kv_s = [], []
    for l in range(depth):
        mkv = _matmul(mem16, w_mem16, l, tm=bp * mem_tokens, tn=512).reshape(bp, mem_tokens, 2 * half)

        def attn_p(proj, lam_init):
            y_c = _diff_prompt(proj, bias_p, diff_lambda, subln_col, l, bsz=bp, length=lp,
                               t=PROMPT_CFG['t_diff'], lam_init=lam_init)
            y_d = _mem_attention(proj, mkv, mkv, k_index=lambda b, h: (b, 0, h),
                                 v_index=lambda b, h: (b, 0, MEM_HEADS + h),
                                 bsz=bp, length=lp, tq=PROMPT_CFG['tq_mem'])
            return y_c, y_d

        yp, proj, c_new, r_new = _encoder_layer(
            yp, w, l, bsz=bp, length=lp, alpha=alpha, cfg=PROMPT_CFG, rope=rope_p, attn_fn=attn_p,
            conv_prev=zero_conv, ret_prev=zero_ret)
        kv_p = _kv_export(proj, kv_p, l, depth, tm=PROMPT_CFG['tm_export'])
        conv_p.append(c_new)
        ret_p.append(r_new)
        mk_p.append(mkv[:, :, :half].reshape(bp, mem_tokens, MEM_HEADS, MEM_HD))
        mv_p.append(mkv[:, :, half:].reshape(bp, mem_tokens, MEM_HEADS, MEM_HD))

        def attn_s(proj, lam_init):
            return _sample_attention(proj, kpast, vpast, mem_k_s, mem_v_s, bias_s_past, bias_s_new,
                                     diff_lambda, subln_row, l, bsz=bs, length=ls, lam_init=lam_init)

        ys, proj, c_new, r_new = _encoder_layer(
            ys, w, l, bsz=bs, length=ls, alpha=alpha, cfg=SAMPLE_CFG, rope=rope_s, attn_fn=attn_s,
            conv_prev=state_conv[l], ret_prev=state_ret[l])
        kv_s = _kv_export(proj, kv_s, l, depth, tm=SAMPLE_CFG['tm_export'])
        conv_s.append(c_new)
        ret_s.append(r_new)

    kv_shape = lambda b, n: (depth, b, n, DIFF_HEADS, DIFF_DV)
    return (yp.reshape(bp, lp, d), ys.reshape(bs, ls, d), jnp.stack(conv_p), jnp.stack(ret_p),
            kv_p[0].reshape(kv_shape(bp, lp)), kv_p[1].reshape(kv_shape(bp, lp)),
            jnp.stack(mk_p), jnp.stack(mv_p), jnp.stack(conv_s), jnp.stack(ret_s),
            kv_s[0].reshape(kv_shape(bs, ls)), kv_s[1].reshape(kv_shape(bs, ls)))
```

```python
import functools
import math

import jax
import jax.numpy as jnp
from jax import lax
from jax.experimental import pallas as pl
from jax.experimental.pallas import tpu as pltpu

F32 = jnp.float32
BF16 = jnp.bfloat16

CHUNK = 64
CONV_WIDTH = 3
RET_HEADS = 4
RET_DK = 128
DIFF_HEADS = 4
DIFF_D = 64
DIFF_DV = 128
MEM_HEADS = 4
MEM_HD = 128
BRANCH_W = 512
N_BRANCH = 4
REL_BUCKETS = 32
REL_MAX_DIST = 128
LN_EPS = 1e-5
ROPE_BASE = 10000.0
NEG_INF = -1e30
HEAD_W = 128
LOG2E = math.log2(math.e)
LN_ROW_CHUNK = 128
PROJ_ROW_CHUNK = 256
FFN_UP_CHUNK = 256
FFN_DOWN_CHUNK = 512

COL_CB, COL_CC, COL_CH, COL_RQ, COL_RK, COL_RV, COL_RG, COL_DQ, COL_DK, COL_DV, COL_MQ = range(11)
HEADS_PER_BRANCH = BRANCH_W // HEAD_W

V7X_VMEM_BYTES = 64 * 1024 * 1024
MIB = 1024 * 1024


def _params(semantics, vmem_mib):
    assert vmem_mib * MIB < V7X_VMEM_BYTES
    return pltpu.CompilerParams(dimension_semantics=semantics, vmem_limit_bytes=vmem_mib * MIB)


def _layer_norm(y, g, b):
    mu = jnp.mean(y, -1, keepdims=True)
    d = y - mu
    var = jnp.mean(d * d, -1, keepdims=True)
    return d * lax.rsqrt(var + LN_EPS) * g + b


def _dot(a, b):
    return jnp.dot(a, b, preferred_element_type=F32)


def _dot_nt(a, b):
    return lax.dot_general(a, b, (((1,), (1,)), ((), ())), preferred_element_type=F32)


def _dot_tn(a, b):
    return lax.dot_general(a, b, (((0,), (0,)), ((), ())), preferred_element_type=F32)


def _ffn_ln_kernel(x_ref, wa_ref, wb_ref, wd_ref, g_ref, b_ref, o_ref, xb_ref, h_ref, *, alpha):
    j = pl.program_id(1)

    @pl.when(j == 0)
    def _():
        xb_ref[...] = x_ref[...].astype(BF16)
        o_ref[...] = jnp.zeros(o_ref.shape, F32)

    xb = xb_ref[...]
    for c in range(0, h_ref.shape[1], FFN_UP_CHUNK):
        cols = slice(c, c + FFN_UP_CHUNK)
        a = _dot(xb, wa_ref[:, cols])
        b = _dot(xb, wb_ref[:, cols])
        h_ref[:, cols] = (a * jax.nn.sigmoid(a) * b).astype(BF16)
    h = h_ref[...]
    for c in range(0, o_ref.shape[1], FFN_DOWN_CHUNK):
        cols = slice(c, c + FFN_DOWN_CHUNK)
        o_ref[:, cols] += _dot(h, wd_ref[:, cols])

    @pl.when(j == pl.num_programs(1) - 1)
    def _():
        for r in range(0, o_ref.shape[0], LN_ROW_CHUNK):
            rows = slice(r, r + LN_ROW_CHUNK)
            o_ref[rows, :] = _layer_norm(alpha * x_ref[rows, :] + 0.5 * o_ref[rows, :], g_ref[...], b_ref[...])


def _ffn_ln(x, w_up, w_down, g, b, layer, *, alpha, tm, tf):
    m, d = x.shape
    d_ff = w_down.shape[1]
    nj = d_ff // tf
    assert m % tm == 0 and d_ff % tf == 0
    row = lambda i, j: (i, 0)
    vec = pl.BlockSpec((None, 1, d), lambda i, j: (layer, 0, 0))
    return pl.pallas_call(
        functools.partial(_ffn_ln_kernel, alpha=alpha),
        grid=(m // tm, nj),
        in_specs=[
            pl.BlockSpec((tm, d), row),
            pl.BlockSpec((None, d, tf), lambda i, j: (layer, 0, j)),
            pl.BlockSpec((None, d, tf), lambda i, j: (layer, 0, j + nj)),
            pl.BlockSpec((None, tf, d), lambda i, j: (layer, j, 0)),
            vec, vec,
        ],
        out_specs=pl.BlockSpec((tm, d), row),
        out_shape=jax.ShapeDtypeStruct((m, d), F32),
        scratch_shapes=[pltpu.VMEM((tm, d), BF16), pltpu.VMEM((tm, tf), BF16)],
        compiler_params=_params(("parallel", "arbitrary"), 60),
        name="ffn_ln",
    )(x, w_up, w_up, w_down, g, b)


def _in_proj_kernel(x_ref, w_ref, o_ref, xb_ref):
    @pl.when(pl.program_id(1) == 0)
    def _():
        xb_ref[...] = x_ref[...].astype(BF16)

    o_ref[...] = _dot(xb_ref[...], w_ref[...])


def _in_proj(x, w, layer, *, tm, tn):
    m, k = x.shape
    n = w.shape[2]
    assert m % tm == 0 and n % tn == 0
    return pl.pallas_call(
        _in_proj_kernel,
        grid=(m // tm, n // tn),
        in_specs=[pl.BlockSpec((tm, k), lambda i, j: (i, 0)),
                  pl.BlockSpec((None, k, tn), lambda i, j: (layer, 0, j))],
        out_specs=[pl.BlockSpec((tm, tn), lambda i, j: (i, j)), pl.BlockSpec((tm, k), lambda i, j: (i, 0))],
        out_shape=[jax.ShapeDtypeStruct((m, n), F32), jax.ShapeDtypeStruct((m, k), BF16)],
        compiler_params=_params(("parallel", "arbitrary"), 56),
        name="in_proj",
    )(x, w)


def _matmul_kernel(x_ref, w_ref, o_ref):
    o_ref[...] = _dot(x_ref[...], w_ref[...])


def _matmul(x, w, layer, *, tm, tn):
    m, k = x.shape
    n = w.shape[2]
    assert m % tm == 0 and n % tn == 0
    return pl.pallas_call(
        _matmul_kernel,
        grid=(m // tm, n // tn),
        in_specs=[pl.BlockSpec((tm, k), lambda i, j: (i, 0)),
                  pl.BlockSpec((None, k, tn), lambda i, j: (layer, 0, j))],
        out_specs=pl.BlockSpec((tm, tn), lambda i, j: (i, j)),
        out_shape=jax.ShapeDtypeStruct((m, n), F32),
        compiler_params=_params(("parallel", "parallel"), 48),
        name="matmul",
    )(x, w)


def _conv_kernel(cb_ref, cc_ref, ch_ref, w_ref, prev_ref, y_ref, state_ref, carry_ref):
    l = pl.program_id(1)

    @pl.when(l == 0)
    def _():
        carry_ref[...] = prev_ref[...]

    u = cc_ref[...] * ch_ref[...]
    row = lax.broadcasted_iota(jnp.int32, u.shape, 0)
    c0 = carry_ref[0:1, :]
    c1 = carry_ref[1:2, :]
    u1 = jnp.where(row == 0, c1, pltpu.roll(u, 1, 0))
    u2 = jnp.where(row == 0, c0, jnp.where(row == 1, c1, pltpu.roll(u, 2, 0)))
    z = w_ref[0:1, :] * u2 + w_ref[1:2, :] * u1 + w_ref[2:3, :] * u
    y_ref[...] = (cb_ref[...] * z).astype(BF16)
    tl = u.shape[0]
    tail = u[tl - 8:, :]
    carry_ref[...] = tail[6:8, :]

    @pl.when(l == pl.num_programs(1) - 1)
    def _():
        state_ref[...] = tail[6:8, :]


def _conv_branch(proj, conv_w, prev, layer, *, bsz, length, tl):
    m = proj.shape[0]
    nl = length // tl
    assert length % tl == 0 and tl >= 8
    col = lambda c: pl.BlockSpec((tl, BRANCH_W), lambda b, l: (b * nl + l, c))
    return pl.pallas_call(
        _conv_kernel,
        grid=(bsz, nl),
        in_specs=[
            col(COL_CB), col(COL_CC), col(COL_CH),
            pl.BlockSpec((None, CONV_WIDTH, BRANCH_W), lambda b, l: (layer, 0, 0)),
            pl.BlockSpec((None, CONV_WIDTH - 1, BRANCH_W), lambda b, l: (b, 0, 0)),
        ],
        out_specs=[
            pl.BlockSpec((tl, BRANCH_W), lambda b, l: (b * nl + l, 0)),
            pl.BlockSpec((None, CONV_WIDTH - 1, BRANCH_W), lambda b, l: (b, 0, 0)),
        ],
        out_shape=[
            jax.ShapeDtypeStruct((m, BRANCH_W), BF16),
            jax.ShapeDtypeStruct((bsz, CONV_WIDTH - 1, BRANCH_W), F32),
        ],
        scratch_shapes=[pltpu.VMEM((CONV_WIDTH - 1, BRANCH_W), F32)],
        compiler_params=_params(("parallel", "arbitrary"), 32),
        name="conv_branch",
    )(proj, proj, proj, conv_w, prev)


def _ret_log_gamma(h):
    return math.log1p(-(2.0 ** (-5.0 - h)))


def _retention_kernel(q_ref, k_ref, v_ref, g_ref, cos_ref, sin_ref, s0_ref, gn_ref, y_ref, sfin_ref, s_ref):
    c = pl.program_id(1)

    @pl.when(c == 0)
    def _():
        s_ref[...] = s0_ref[...]

    chunk = q_ref.shape[0]
    cosf = cos_ref[...]
    sinf = sin_ref[...]
    ri = lax.broadcasted_iota(jnp.int32, (chunk, chunk), 0)
    ci = lax.broadcasted_iota(jnp.int32, (chunk, chunk), 1)
    rel = (ri - ci).astype(F32)
    idx = lax.broadcasted_iota(jnp.int32, (chunk, 1), 0).astype(F32)
    for h in range(RET_HEADS):
        log_g = _ret_log_gamma(h)
        cols = slice(h * RET_DK, (h + 1) * RET_DK)
        q = q_ref[:, cols]
        k = k_ref[:, cols]
        q = q * cosf + pltpu.roll(q, RET_DK // 2, 1) * sinf
        k = (k * cosf + pltpu.roll(k, RET_DK // 2, 1) * sinf) * (RET_DK ** -0.5)
        vb = v_ref[:, cols].astype(BF16)
        intra = jnp.where(rel >= 0.0, jnp.exp(jnp.maximum(rel, 0.0) * log_g), 0.0)
        q_dec = jnp.exp((idx + 1.0) * log_g)
        k_dec = jnp.exp((chunk - 1.0 - idx) * log_g)
        c_dec = math.exp(chunk * log_g)
        qb = q.astype(BF16)
        s_prev = s_ref[h]
        att = _dot_nt(qb, k.astype(BF16)) * intra
        o = _dot(att.astype(BF16), vb) + _dot(qb, s_prev.astype(BF16)) * q_dec
        s_ref[h] = s_prev * c_dec + _dot_tn((k * k_dec).astype(BF16), vb)
        mu = jnp.mean(o, -1, keepdims=True)
        d = o - mu
        var = jnp.mean(d * d, -1, keepdims=True)
        ro = d * lax.rsqrt(var + LN_EPS) * gn_ref[:, cols]
        gate = g_ref[:, cols]
        y_ref[:, cols] = (gate * jax.nn.sigmoid(gate) * ro).astype(BF16)

    @pl.when(c == pl.num_programs(1) - 1)
    def _():
        sfin_ref[...] = s_ref[...]


def _retention_branch(proj, cosf, sinf, s0, gn_g, layer, *, bsz, length, chunk):
    m = proj.shape[0]
    nc = length // chunk
    assert length % chunk == 0
    col = lambda c: pl.BlockSpec((chunk, BRANCH_W), lambda b, i: (b * nc + i, c))
    state = pl.BlockSpec((None, RET_HEADS, RET_DK, RET_DK), lambda b, i: (b, 0, 0, 0))
    return pl.pallas_call(
        _retention_kernel,
        grid=(bsz, nc),
        in_specs=[
            col(COL_RQ), col(COL_RK), col(COL_RV), col(COL_RG),
            pl.BlockSpec((chunk, RET_DK), lambda b, i: (i, 0)),
            pl.BlockSpec((chunk, RET_DK), lambda b, i: (i, 0)),
            state,
            pl.BlockSpec((None, 1, BRANCH_W), lambda b, i: (layer, 0, 0)),
        ],
        out_specs=[pl.BlockSpec((chunk, BRANCH_W), lambda b, i: (b * nc + i, 0)), state],
        out_shape=[
            jax.ShapeDtypeStruct((m, BRANCH_W), BF16),
            jax.ShapeDtypeStruct((bsz, RET_HEADS, RET_DK, RET_DK), F32),
        ],
        scratch_shapes=[pltpu.VMEM((RET_HEADS, RET_DK, RET_DK), F32)],
        compiler_params=_params(("parallel", "arbitrary"), 32),
        name="retention_branch",
    )(proj, proj, proj, proj, cosf, sinf, s0, gn_g)


def _diff_lambda(lam_ref, lam_init):
    e0 = jnp.exp(jnp.sum(lam_ref[0:1, :] * lam_ref[1:2, :], -1, keepdims=True))
    e1 = jnp.exp(jnp.sum(lam_ref[2:3, :] * lam_ref[3:4, :], -1, keepdims=True))
    return e0 - e1 + lam_init


def _diff_prompt_kernel(q_ref, k_ref, v_ref, bias_ref, lam_ref, g_ref, o_ref, k16_ref, vt_ref,
                        sa_ref, sb_ref, m0_ref, m1_ref, l0_ref, l1_ref, acc0_ref, acc1_ref, *, lam_init):
    i = pl.program_id(2)
    t = q_ref.shape[0]
    nt = k_ref.shape[0] // t
    m_refs, l_refs, acc_refs = (m0_ref, m1_ref), (l0_ref, l1_ref), (acc0_ref, acc1_ref)

    @pl.when(i == 0)
    def _():
        k16_ref[...] = k_ref[...].astype(BF16)
        for j in range(nt):
            vt_ref[j] = jnp.transpose(v_ref[j * t:(j + 1) * t, :]).astype(BF16)

    qt = jnp.transpose(q_ref[...] * (DIFF_D ** -0.5 * LOG2E))
    feat = lax.broadcasted_iota(jnp.int32, qt.shape, 0)
    qts = (jnp.where(feat < DIFF_D, qt, 0.0).astype(BF16), jnp.where(feat >= DIFF_D, qt, 0.0).astype(BF16))
    for c in range(2):
        m_refs[c][...] = jnp.full(m_refs[c].shape, NEG_INF, F32)
        l_refs[c][...] = jnp.zeros(l_refs[c].shape, F32)
        acc_refs[c][...] = jnp.zeros(acc_refs[c].shape, F32)

    def scores(j, dst_ref):
        jc = jnp.minimum(j, i)
        kt = k16_ref[pl.ds(pl.multiple_of(jc * t, t), t), :]
        bias = bias_ref[jnp.where(jc == i, 0, jnp.where(jc == i - 1, 1, 2))]
        for c in range(2):
            dst_ref[c] = _dot(kt, qts[c]) + bias

    def consume(j, src_ref):
        vt = vt_ref[j]
        s = [src_ref[c] for c in range(2)]
        m_prev = [m_refs[c][...] for c in range(2)]
        m_new = [jnp.maximum(m_prev[c], jnp.max(s[c], 0, keepdims=True)) for c in range(2)]
        p = [jnp.exp2(s[c] - m_new[c]) for c in range(2)]
        a = [jnp.exp2(m_prev[c] - m_new[c]) for c in range(2)]
        for c in range(2):
            l_refs[c][...] = a[c] * l_refs[c][...] + jnp.sum(p[c], 0, keepdims=True)
            acc_refs[c][...] = a[c] * acc_refs[c][...] + _dot(vt, p[c].astype(BF16))
            m_refs[c][...] = m_new[c]

    scores(0, sa_ref)

    def pair(p, carry):
        j = 2 * p
        scores(j + 1, sb_ref)
        consume(j, sa_ref)
        scores(j + 2, sa_ref)
        consume(j + 1, sb_ref)
        return carry

    n_tiles = i + 1
    lax.fori_loop(0, n_tiles // 2, pair, 0)

    @pl.when(n_tiles % 2 == 1)
    def _():
        consume(i, sa_ref)

    lam = _diff_lambda(lam_ref, lam_init)
    o = acc0_ref[...] / l0_ref[...] - lam * (acc1_ref[...] / l1_ref[...])
    o = o * lax.rsqrt(jnp.mean(o * o, 0, keepdims=True) + LN_EPS) * g_ref[...]
    o_ref[...] = jnp.transpose(o * (1.0 - lam_init)).astype(BF16)


def _diff_prompt(proj, bias_tiles, diff_lambda, subln_g, layer, *, bsz, length, t, lam_init):
    m = proj.shape[0]
    nq = length // t
    assert length % t == 0 and t % CHUNK == 0
    kv = lambda c: pl.BlockSpec((length, HEAD_W), lambda b, h, i: (b, c * HEADS_PER_BRANCH + h))
    stat = pltpu.VMEM((1, t), F32)
    acc = pltpu.VMEM((DIFF_DV, t), F32)
    return pl.pallas_call(
        functools.partial(_diff_prompt_kernel, lam_init=lam_init),
        grid=(bsz, DIFF_HEADS, nq),
        in_specs=[
            pl.BlockSpec((t, HEAD_W), lambda b, h, i: (b * nq + i, COL_DQ * HEADS_PER_BRANCH + h)),
            kv(COL_DK), kv(COL_DV),
            pl.BlockSpec((None, 3, t, t), lambda b, h, i: (h, 0, 0, 0)),
            pl.BlockSpec((None, 4, DIFF_D), lambda b, h, i: (layer, 0, 0)),
            pl.BlockSpec((None, DIFF_DV, 1), lambda b, h, i: (layer, 0, 0)),
        ],
        out_specs=pl.BlockSpec((t, HEAD_W), lambda b, h, i: (b * nq + i, h)),
        out_shape=jax.ShapeDtypeStruct((m, BRANCH_W), BF16),
        scratch_shapes=[pltpu.VMEM((length, HEAD_W), BF16), pltpu.VMEM((nq, DIFF_DV, t), BF16),
                        pltpu.VMEM((2, t, t), F32), pltpu.VMEM((2, t, t), F32),
                        stat, stat, stat, stat, acc, acc],
        compiler_params=_params(("parallel", "parallel", "arbitrary"), 40),
        name="diff_attention_prompt",
    )(proj, proj, proj, bias_tiles, diff_lambda, subln_g)


def _head_rows(ref, h, n):
    return ref[pl.ds(h, n, stride=HEADS_PER_BRANCH), :]


def _sample_attn_kernel(q_ref, kn_ref, vn_ref, mq_ref, kp_ref, vp_ref, mk_ref, mv_ref, bp_ref, bn_ref,
                        lam_ref, g_ref, yc_ref, yd_ref, *, lam_init):
    past = kp_ref.shape[0] // HEADS_PER_BRANCH
    tokens = mk_ref.shape[0] // HEADS_PER_BRANCH
    lam = _diff_lambda(lam_ref, lam_init)
    for h in range(HEADS_PER_BRANCH):
        cols = slice(h * HEAD_W, (h + 1) * HEAD_W)
        q = q_ref[:, cols] * (DIFF_D ** -0.5 * LOG2E)
        lane = lax.broadcasted_iota(jnp.int32, q.shape, 1)
        qs = (jnp.where(lane < DIFF_D, q, 0.0).astype(BF16), jnp.where(lane >= DIFF_D, q, 0.0).astype(BF16))
        kp = _head_rows(kp_ref, h, past).astype(BF16)
        vp = _head_rows(vp_ref, h, past).astype(BF16)
        kn = kn_ref[:, cols].astype(BF16)
        vn = vn_ref[:, cols].astype(BF16)
        outs = []
        for qc in qs:
            sp = _dot_nt(qc, kp) + bp_ref[h]
            sn = _dot_nt(qc, kn) + bn_ref[h]
            mx = jnp.maximum(jnp.max(sp, -1, keepdims=True), jnp.max(sn, -1, keepdims=True))
            pp = jnp.exp2(sp - mx)
            pn = jnp.exp2(sn - mx)
            den = jnp.sum(pp, -1, keepdims=True) + jnp.sum(pn, -1, keepdims=True)
            outs.append((_dot(pp.astype(BF16), vp) + _dot(pn.astype(BF16), vn)) / den)
        o = outs[0] - lam * outs[1]
        o = o * lax.rsqrt(jnp.mean(o * o, -1, keepdims=True) + LN_EPS) * g_ref[...]
        yc_ref[:, cols] = (o * (1.0 - lam_init)).astype(BF16)

        mk = _head_rows(mk_ref, h, tokens).astype(BF16)
        mv = _head_rows(mv_ref, h, tokens).astype(BF16)
        s = _dot_nt(mq_ref[:, cols].astype(BF16), mk) * (MEM_HD ** -0.5)
        p = jnp.exp(s - jnp.max(s, -1, keepdims=True))
        den = jnp.sum(p, -1, keepdims=True)
        yd_ref[:, cols] = (_dot(p.astype(BF16), mv) / den).astype(BF16)


def _sample_attention(proj, k_past, v_past, mem_k, mem_v, bias_past, bias_new, diff_lambda, subln_g, layer,
                      *, bsz, length, lam_init):
    m = proj.shape[0]
    col = lambda c: pl.BlockSpec((length, BRANCH_W), lambda b: (b, c))
    cache = lambda a: pl.BlockSpec((None, None) + a.shape[2:], lambda b: (layer, b, 0, 0))
    whole = lambda a: pl.BlockSpec(a.shape, lambda b: (0,) * a.ndim)
    out = pl.BlockSpec((length, BRANCH_W), lambda b: (b, 0))
    return pl.pallas_call(
        functools.partial(_sample_attn_kernel, lam_init=lam_init),
        grid=(bsz,),
        in_specs=[
            col(COL_DQ), col(COL_DK), col(COL_DV), col(COL_MQ),
            cache(k_past), cache(v_past), cache(mem_k), cache(mem_v),
            whole(bias_past), whole(bias_new),
            pl.BlockSpec((None, 4, DIFF_D), lambda b: (layer, 0, 0)),
            pl.BlockSpec((None, 1, DIFF_DV), lambda b: (layer, 0, 0)),
        ],
        out_specs=[out, out],
        out_shape=[jax.ShapeDtypeStruct((m, BRANCH_W), BF16), jax.ShapeDtypeStruct((m, BRANCH_W), BF16)],
        compiler_params=_params(("parallel",), 40),
        name="sample_attention",
    )(proj, proj, proj, proj, k_past, v_past, mem_k, mem_v, bias_past, bias_new, diff_lambda, subln_g)


def _mem_attn_kernel(q_ref, k_ref, v_ref, o_ref):
    s = _dot_nt(q_ref[...].astype(BF16), k_ref[...].astype(BF16)) * (MEM_HD ** -0.5)
    p = jnp.exp(s - jnp.max(s, -1, keepdims=True))
    den = jnp.sum(p, -1, keepdims=True)
    o_ref[...] = (_dot(p.astype(BF16), v_ref[...].astype(BF16)) / den).astype(BF16)


def _mem_attention(proj, mem_k, mem_v, *, k_index, v_index, bsz, length, tq):
    m = proj.shape[0]
    nq = length // tq
    assert length % tq == 0
    tokens = mem_k.shape[-2]
    block = (None,) * (mem_k.ndim - 2) + (tokens, HEAD_W)
    return pl.pallas_call(
        _mem_attn_kernel,
        grid=(bsz, MEM_HEADS, nq),
        in_specs=[pl.BlockSpec((tq, HEAD_W), lambda b, h, i: (b * nq + i, COL_MQ * HEADS_PER_BRANCH + h)),
                  pl.BlockSpec(block, lambda b, h, i: k_index(b, h)),
                  pl.BlockSpec(block, lambda b, h, i: v_index(b, h))],
        out_specs=pl.BlockSpec((tq, HEAD_W), lambda b, h, i: (b * nq + i, h)),
        out_shape=jax.ShapeDtypeStruct((m, BRANCH_W), BF16),
        compiler_params=_params(("parallel", "parallel", "parallel"), 32),
        name="mem_attention",
    )(proj, mem_k, mem_v)


def _merge_kernel(h_ref, ya_ref, yb_ref, yc_ref, yd_ref, wg_ref, wb_ref, bg_ref, o_ref):
    h = h_ref[...]
    acc = None
    for i, y_ref in enumerate((ya_ref, yb_ref, yc_ref, yd_ref)):
        gate = jax.nn.sigmoid(_dot(h, wg_ref[i]) + bg_ref[i])
        term = gate * _dot(y_ref[...], wb_ref[i])
        acc = term if acc is None else acc + term
    o_ref[...] = acc.astype(BF16)


def _merge(h16, ys, w_gate, w_branch, b_gate, layer, *, tm, tn):
    m, d = h16.shape
    assert m % tm == 0 and d % tn == 0
    y_spec = pl.BlockSpec((tm, BRANCH_W), lambda i, j: (i, 0))
    return pl.pallas_call(
        _merge_kernel,
        grid=(m // tm, d // tn),
        in_specs=[
            pl.BlockSpec((tm, d), lambda i, j: (i, 0)), y_spec, y_spec, y_spec, y_spec,
            pl.BlockSpec((None, N_BRANCH, d, tn), lambda i, j: (layer, 0, 0, j)),
            pl.BlockSpec((None, N_BRANCH, BRANCH_W, tn), lambda i, j: (layer, 0, 0, j)),
            pl.BlockSpec((None, N_BRANCH, 1, tn), lambda i, j: (layer, 0, 0, j)),
        ],
        out_specs=pl.BlockSpec((tm, tn), lambda i, j: (i, j)),
        out_shape=jax.ShapeDtypeStruct((m, d), BF16),
        compiler_params=_params(("parallel", "parallel"), 48),
        name="gated_merge",
    )(h16, *ys, w_gate, w_branch, b_gate)


def _proj_ln_kernel(m_ref, x_ref, w_ref, g_ref, b_ref, o_ref, *, alpha):
    for r in range(0, o_ref.shape[0], PROJ_ROW_CHUNK):
        rows = slice(r, r + PROJ_ROW_CHUNK)
        y = alpha * x_ref[rows, :] + _dot(m_ref[rows, :], w_ref[...])
        o_ref[rows, :] = _layer_norm(y, g_ref[...], b_ref[...])


def _proj_ln(merged, x, w_o, g, b, layer, *, alpha, tm):
    m, d = x.shape
    assert m % tm == 0
    row = pl.BlockSpec((tm, d), lambda i: (i, 0))
    vec = pl.BlockSpec((None, 1, d), lambda i: (layer, 0, 0))
    return pl.pallas_call(
        functools.partial(_proj_ln_kernel, alpha=alpha),
        grid=(m // tm,),
        in_specs=[row, row, pl.BlockSpec((None, d, d), lambda i: (layer, 0, 0)), vec, vec],
        out_specs=row,
        out_shape=jax.ShapeDtypeStruct((m, d), F32),
        compiler_params=_params(("parallel",), 48),
        name="proj_ln",
    )(merged, x, w_o, g, b)


def _rope_tables(pos):
    half = RET_DK // 2
    inv = ROPE_BASE ** (-jnp.arange(half, dtype=F32) / half)
    ang = pos.astype(F32)[:, None] * inv[None, :]
    cos = jnp.cos(ang)
    sin = jnp.sin(ang)
    return jnp.concatenate([cos, cos], -1), jnp.concatenate([-sin, sin], -1)


def _t5_bucket(rel):
    nb = REL_BUCKETS // 2
    max_exact = nb // 2
    n = jnp.abs(rel)
    nf = jnp.maximum(n, 1).astype(F32)
    large = max_exact + (jnp.log(nf / max_exact) / math.log(REL_MAX_DIST / max_exact)
                         * (nb - max_exact)).astype(jnp.int32)
    large = jnp.minimum(large, nb - 1)
    return jnp.where(rel > 0, nb, 0) + jnp.where(n < max_exact, n, large)


def _masked_bias(q_pos, k_pos, rel_bias):
    bucket = _t5_bucket(k_pos[None, :] - q_pos[:, None])
    onehot = bucket[None, None] == jnp.arange(REL_BUCKETS, dtype=bucket.dtype)[None, :, None, None]
    bias = jnp.sum(jnp.where(onehot, rel_bias.astype(F32).T[:, :, None, None], 0.0), axis=1)
    allowed = (k_pos[None, :] // CHUNK) <= (q_pos[:, None] // CHUNK)
    return jnp.where(allowed[None], bias * LOG2E, NEG_INF)


def _prompt_bias_tiles(rel_bias, t):
    assert t % CHUNK == 0 and t + 1 >= REL_MAX_DIST
    base = jnp.arange(t, dtype=jnp.int32)
    tiles = jnp.stack([_masked_bias(base + d * t, base, rel_bias) for d in range(3)], axis=1)
    return jnp.swapaxes(tiles, -1, -2)


def _kv_export_kernel(k_ref, v_ref, *rest):
    ok_ref, ov_ref = rest[-2:]
    n = k_ref.shape[0]
    for h in range(HEADS_PER_BRANCH):
        cols = slice(h * HEAD_W, (h + 1) * HEAD_W)
        rows = pl.ds(h, n, stride=HEADS_PER_BRANCH)
        ok_ref[rows, :] = k_ref[:, cols]
        ov_ref[rows, :] = v_ref[:, cols]


def _kv_export(proj, prev, layer, depth, *, tm):
    m = proj.shape[0]
    assert m % tm == 0
    col = lambda c: pl.BlockSpec((tm, BRANCH_W), lambda i: (i, c))
    out = pl.BlockSpec((None, tm * HEADS_PER_BRANCH, HEAD_W), lambda i: (layer, i, 0))
    shape = jax.ShapeDtypeStruct((depth, m * HEADS_PER_BRANCH, HEAD_W), F32)
    keep = [pl.BlockSpec(memory_space=pl.ANY)] * len(prev)
    return pl.pallas_call(
        _kv_export_kernel,
        grid=(m // tm,),
        in_specs=[col(COL_DK), col(COL_DV)] + keep,
        out_specs=[out, out],
        out_shape=[shape, shape],
        input_output_aliases={2 + n: n for n in range(len(prev))},
        compiler_params=_params(("parallel",), 32),
        name="kv_export",
    )(proj, proj, *prev)


def _encoder_layer(x, w, layer, *, bsz, length, alpha, cfg, rope, attn_fn, conv_prev, ret_prev):
    x1 = _ffn_ln(x, w['ffn1_w_up'], w['ffn1_w_down'], w['ln1_g'], w['ln1_b'], layer,
                 alpha=alpha, tm=cfg['tm_ffn'], tf=cfg['tf'])
    proj, x1b = _in_proj(x1, w['w_in'], layer, tm=cfg['tm_in'], tn=cfg['tn_in'])
    y_a, conv_new = _conv_branch(proj, w['conv_w'], conv_prev, layer, bsz=bsz, length=length, tl=cfg['t_conv'])
    y_b, ret_new = _retention_branch(proj, rope[0], rope[1], ret_prev, w['ret_gn_g'], layer,
                                     bsz=bsz, length=length, chunk=cfg['ret_chunk'])
    y_c, y_d = attn_fn(proj, 0.8 - 0.6 * math.exp(-0.3 * layer))
    merged = _merge(x1b, (y_a, y_b, y_c, y_d), w['w_gate'], w['w_branch'], w['b_gate'], layer,
                    tm=cfg['tm_merge'], tn=cfg['tn_merge'])
    x2 = _proj_ln(merged, x1, w['w_o'], w['ln2_g'], w['ln2_b'], layer, alpha=alpha, tm=cfg['tm_proj'])
    x3 = _ffn_ln(x2, w['ffn2_w_up'], w['ffn2_w_down'], w['ln3_g'], w['ln3_b'], layer,
                 alpha=alpha, tm=cfg['tm_ffn'], tf=cfg['tf'])
    return x3, proj, conv_new, ret_new


PROMPT_CFG = dict(tm_ffn=1024, tf=512, tm_in=1024, tn_in=1408, t_conv=512, ret_chunk=256, t_diff=512,
                  tq_mem=1024, tm_merge=1024, tn_merge=256, tm_proj=512, tm_export=1024)
SAMPLE_CFG = dict(tm_ffn=256, tf=512, tm_in=256, tn_in=512, t_conv=16, ret_chunk=16,
                  tm_merge=256, tn_merge=256, tm_proj=256, tm_export=256)


def kernel(x_prompt, x_sample, state_conv, state_ret, cache_diff_k, cache_diff_v, cache_mem_k, cache_mem_v,
           mem_prompt, ffn1_w_up, ffn1_w_down, ln1_g, ln1_b, w_in, conv_w, ret_gn_g, diff_lambda,
           diff_subln_g, w_mem_kv, w_branch, w_gate, b_gate, w_o, ln2_g, ln2_b, ffn2_w_up, ffn2_w_down,
           ln3_g, ln3_b, rel_bias):
    bp, lp, d = x_prompt.shape
    bs, ls, _ = x_sample.shape
    depth = w_in.shape[0]
    past = cache_diff_k.shape[2]
    mem_tokens = mem_prompt.shape[1]
    alpha = (2 * depth) ** 0.25
    half = MEM_HEADS * MEM_HD

    pos_p = jnp.arange(lp, dtype=jnp.int32)
    pos_s = past + jnp.arange(ls, dtype=jnp.int32)
    past_pos = jnp.arange(past, dtype=jnp.int32)
    rope_p = _rope_tables(pos_p)
    rope_s = _rope_tables(pos_s)
    bias_p = _prompt_bias_tiles(rel_bias, PROMPT_CFG['t_diff'])
    bias_s_past = _masked_bias(pos_s, past_pos, rel_bias)
    bias_s_new = _masked_bias(pos_s, pos_s, rel_bias)

    vec = lambda v: v.reshape(depth, 1, v.shape[-1])
    w = {
        'ffn1_w_up': ffn1_w_up.astype(BF16), 'ffn1_w_down': ffn1_w_down.astype(BF16),
        'ln1_g': vec(ln1_g), 'ln1_b': vec(ln1_b), 'w_in': w_in.astype(BF16), 'conv_w': conv_w,
        'ret_gn_g': vec(ret_gn_g), 'w_branch': w_branch.astype(BF16), 'w_gate': w_gate.astype(BF16),
        'b_gate': b_gate.reshape(depth, N_BRANCH, 1, d), 'w_o': w_o.astype(BF16),
        'ln2_g': vec(ln2_g), 'ln2_b': vec(ln2_b),
        'ffn2_w_up': ffn2_w_up.astype(BF16), 'ffn2_w_down': ffn2_w_down.astype(BF16),
        'ln3_g': vec(ln3_g), 'ln3_b': vec(ln3_b),
    }
    w_mem16 = w_mem_kv.astype(BF16)
    subln_col = diff_subln_g.reshape(depth, DIFF_DV, 1)
    subln_row = diff_subln_g.reshape(depth, 1, DIFF_DV)
    head_major = lambda c: c.reshape(depth, bs, c.shape[2] * c.shape[3], c.shape[4])
    kpast, vpast = head_major(cache_diff_k), head_major(cache_diff_v)
    mem_k_s, mem_v_s = head_major(cache_mem_k), head_major(cache_mem_v)

    yp = x_prompt.reshape(bp * lp, d)
    ys = x_sample.reshape(bs * ls, d)
    mem16 = mem_prompt.reshape(bp * mem_tokens, d).astype(BF16)
    zero_conv = jnp.zeros((bp, CONV_WIDTH - 1, BRANCH_W), F32)
    zero_ret = jnp.zeros((bp, RET_HEADS, RET_DK, RET_DK), F32)

    conv_p, ret_p, mk_p, mv_p, conv_s, ret_s = [], [], [], [], [], []
    kv_p, kv_s = [], []
    for l in range(depth):
        mkv = _matmul(mem16, w_mem16, l, tm=bp * mem_tokens, tn=512).reshape(bp, mem_tokens, 2 * half)

        def attn_p(proj, lam_init):
            y_c = _diff_prompt(proj, bias_p, diff_lambda, subln_col, l, bsz=bp, length=lp,
                               t=PROMPT_CFG['t_diff'], lam_init=lam_init)
            y_d = _mem_attention(proj, mkv, mkv, k_index=lambda b, h: (b, 0, h),
                                 v_index=lambda b, h: (b, 0, MEM_HEADS + h),
                                 bsz=bp, length=lp, tq=PROMPT_CFG['tq_mem'])
            return y_c, y_d

        yp, proj, c_new, r_new = _encoder_layer(
            yp, w, l, bsz=bp, length=lp, alpha=alpha, cfg=PROMPT_CFG, rope=rope_p, attn_fn=attn_p,
            conv_prev=zero_conv, ret_prev=zero_ret)
        kv_p = _kv_export(proj, kv_p, l, depth, tm=PROMPT_CFG['tm_export'])
        conv_p.append(c_new)
        ret_p.append(r_new)
        mk_p.append(mkv[:, :, :half].reshape(bp, mem_tokens, MEM_HEADS, MEM_HD))
        mv_p.append(mkv[:, :, half:].reshape(bp, mem_tokens, MEM_HEADS, MEM_HD))

        def attn_s(proj, lam_init):
            return _sample_attention(proj, kpast, vpast, mem_k_s, mem_v_s, bias_s_past, bias_s_new,
                                     diff_lambda, subln_row, l, bsz=bs, length=ls, lam_init=lam_init)

        ys, proj, c_new, r_new = _encoder_layer(
            ys, w, l, bsz=bs, length=ls, alpha=alpha, cfg=SAMPLE_CFG, rope=rope_s, attn_fn=attn_s,
            conv_prev=state_conv[l], ret_prev=state_ret[l])
        kv_s = _kv_export(proj, kv_s, l, depth, tm=SAMPLE_CFG['tm_export'])
        conv_s.append(c_new)
        ret_s.append(r_new)

    kv_shape = lambda b, n: (depth, b, n, DIFF_HEADS, DIFF_DV)
    return (yp.reshape(bp, lp, d), ys.reshape(bs, ls, d), jnp.stack(conv_p), jnp.stack(ret_p),
            kv_p[0].reshape(kv_shape(bp, lp)), kv_p[1].reshape(kv_shape(bp, lp)),
            jnp.stack(mk_p), jnp.stack(mv_p), jnp.stack(conv_s), jnp.stack(ret_s),
            kv_s[0].reshape(kv_shape(bs, ls)), kv_s[1].reshape(kv_shape(bs, ls)))
```

```python
import functools
import math

import jax
import jax.numpy as jnp
from jax import lax
from jax.experimental import pallas as pl
from jax.experimental.pallas import tpu as pltpu

F32 = jnp.float32
BF16 = jnp.bfloat16

CHUNK = 64
CONV_WIDTH = 3
RET_HEADS = 4
RET_DK = 128
DIFF_HEADS = 4
DIFF_D = 64
DIFF_DV = 128
MEM_HEADS = 4
MEM_HD = 128
BRANCH_W = 512
N_BRANCH = 4
REL_BUCKETS = 32
REL_MAX_DIST = 128
LN_EPS = 1e-5
ROPE_BASE = 10000.0
NEG_INF = -1e30
HEAD_W = 128
LOG2E = math.log2(math.e)
LN_ROW_CHUNK = 128
PROJ_ROW_CHUNK = 256
BIAS_BASE = 256
FFN_UP_CHUNK = 256
FFN_DOWN_CHUNK = 512

COL_CB, COL_CC, COL_CH, COL_RQ, COL_RK, COL_RV, COL_RG, COL_DQ, COL_DK, COL_DV, COL_MQ = range(11)
HEADS_PER_BRANCH = BRANCH_W // HEAD_W

V7X_VMEM_BYTES = 64 * 1024 * 1024
MIB = 1024 * 1024


def _params(semantics, vmem_mib):
    assert vmem_mib * MIB < V7X_VMEM_BYTES
    return pltpu.CompilerParams(dimension_semantics=semantics, vmem_limit_bytes=vmem_mib * MIB)


def _layer_norm(y, g, b):
    mu = jnp.mean(y, -1, keepdims=True)
    d = y - mu
    var = jnp.mean(d * d, -1, keepdims=True)
    return d * lax.rsqrt(var + LN_EPS) * g + b


def _dot(a, b):
    return jnp.dot(a, b, preferred_element_type=F32)


def _dot_nt(a, b):
    return lax.dot_general(a, b, (((1,), (1,)), ((), ())), preferred_element_type=F32)


def _dot_tn(a, b):
    return lax.dot_general(a, b, (((0,), (0,)), ((), ())), preferred_element_type=F32)


def _ffn_ln_kernel(x_ref, wa_ref, wb_ref, wd_ref, g_ref, b_ref, o_ref, xb_ref, h_ref, *, alpha):
    j = pl.program_id(1)

    @pl.when(j == 0)
    def _():
        xb_ref[...] = x_ref[...].astype(BF16)
        o_ref[...] = jnp.zeros(o_ref.shape, F32)

    xb = xb_ref[...]
    for c in range(0, h_ref.shape[1], FFN_UP_CHUNK):
        cols = slice(c, c + FFN_UP_CHUNK)
        a = _dot(xb, wa_ref[:, cols])
        b = _dot(xb, wb_ref[:, cols])
        h_ref[:, cols] = (a * jax.nn.sigmoid(a) * b).astype(BF16)
    h = h_ref[...]
    for c in range(0, o_ref.shape[1], FFN_DOWN_CHUNK):
        cols = slice(c, c + FFN_DOWN_CHUNK)
        o_ref[:, cols] += _dot(h, wd_ref[:, cols])

    @pl.when(j == pl.num_programs(1) - 1)
    def _():
        for r in range(0, o_ref.shape[0], LN_ROW_CHUNK):
            rows = slice(r, r + LN_ROW_CHUNK)
            o_ref[rows, :] = _layer_norm(alpha * x_ref[rows, :] + 0.5 * o_ref[rows, :], g_ref[...], b_ref[...])


def _ffn_ln(x, w_up, w_down, g, b, layer, *, alpha, tm, tf):
    m, d = x.shape
    d_ff = w_down.shape[1]
    nj = d_ff // tf
    assert m % tm == 0 and d_ff % tf == 0 and nj >= 2
    row = lambda i, j: (i, 0)
    vec = pl.BlockSpec((None, 1, d), lambda i, j: (layer, 0, 0))
    return pl.pallas_call(
        functools.partial(_ffn_ln_kernel, alpha=alpha),
        grid=(m // tm, nj),
        in_specs=[
            pl.BlockSpec((tm, d), row),
            pl.BlockSpec((None, d, tf), lambda i, j: (layer, 0, j)),
            pl.BlockSpec((None, d, tf), lambda i, j: (layer, 0, j + nj)),
            pl.BlockSpec((None, tf, d), lambda i, j: (layer, j, 0)),
            vec, vec,
        ],
        out_specs=pl.BlockSpec((tm, d), row),
        out_shape=jax.ShapeDtypeStruct((m, d), F32),
        scratch_shapes=[pltpu.VMEM((tm, d), BF16), pltpu.VMEM((tm, tf), BF16)],
        compiler_params=_params(("parallel", "arbitrary"), 60),
        name="ffn_ln",
    )(x, w_up, w_up, w_down, g, b)


def _in_proj_kernel(x_ref, w_ref, o_ref, xb_ref):
    @pl.when(pl.program_id(1) == 0)
    def _():
        xb_ref[...] = x_ref[...].astype(BF16)

    o_ref[...] = _dot(xb_ref[...], w_ref[...])


def _in_proj(x, w, layer, *, tm, tn):
    m, k = x.shape
    n = w.shape[2]
    assert m % tm == 0 and n % tn == 0
    return pl.pallas_call(
        _in_proj_kernel,
        grid=(m // tm, n // tn),
        in_specs=[pl.BlockSpec((tm, k), lambda i, j: (i, 0)),
                  pl.BlockSpec((None, k, tn), lambda i, j: (layer, 0, j))],
        out_specs=[pl.BlockSpec((tm, tn), lambda i, j: (i, j)), pl.BlockSpec((tm, k), lambda i, j: (i, 0))],
        out_shape=[jax.ShapeDtypeStruct((m, n), F32), jax.ShapeDtypeStruct((m, k), BF16)],
        compiler_params=_params(("parallel", "arbitrary"), 56),
        name="in_proj",
    )(x, w)


def _matmul_kernel(x_ref, w_ref, o_ref):
    o_ref[...] = _dot(x_ref[...], w_ref[...])


def _matmul(x, w, layer, *, tm, tn):
    m, k = x.shape
    n = w.shape[2]
    assert m % tm == 0 and n % tn == 0
    return pl.pallas_call(
        _matmul_kernel,
        grid=(m // tm, n // tn),
        in_specs=[pl.BlockSpec((tm, k), lambda i, j: (i, 0)),
                  pl.BlockSpec((None, k, tn), lambda i, j: (layer, 0, j))],
        out_specs=pl.BlockSpec((tm, tn), lambda i, j: (i, j)),
        out_shape=jax.ShapeDtypeStruct((m, n), F32),
        compiler_params=_params(("parallel", "parallel"), 48),
        name="matmul",
    )(x, w)


def _conv_body(cb_ref, cc_ref, ch_ref, w_ref, carry_ref, y_ref):
    u = cc_ref[...] * ch_ref[...]
    row = lax.broadcasted_iota(jnp.int32, u.shape, 0)
    c0 = carry_ref[0:1, :]
    c1 = carry_ref[1:2, :]
    u1 = jnp.where(row == 0, c1, pltpu.roll(u, 1, 0))
    u2 = jnp.where(row == 0, c0, jnp.where(row == 1, c1, pltpu.roll(u, 2, 0)))
    z = w_ref[0:1, :] * u2 + w_ref[1:2, :] * u1 + w_ref[2:3, :] * u
    y_ref[...] = (cb_ref[...] * z).astype(BF16)
    tl = u.shape[0]
    tail = u[tl - 8:, :][6:8, :]
    carry_ref[...] = tail
    return tail


def _conv_kernel(cb_ref, cc_ref, ch_ref, w_ref, prev_ref, y_ref, state_ref, carry_ref):
    l = pl.program_id(1)

    @pl.when(l == 0)
    def _():
        carry_ref[...] = prev_ref[...]

    tail = _conv_body(cb_ref, cc_ref, ch_ref, w_ref, carry_ref, y_ref)

    @pl.when(l == pl.num_programs(1) - 1)
    def _():
        state_ref[...] = tail


def _conv_branch(proj, conv_w, prev, layer, *, bsz, length, tl):
    m = proj.shape[0]
    nl = length // tl
    assert length % tl == 0 and tl >= 8
    col = lambda c: pl.BlockSpec((tl, BRANCH_W), lambda b, l: (b * nl + l, c))
    return pl.pallas_call(
        _conv_kernel,
        grid=(bsz, nl),
        in_specs=[
            col(COL_CB), col(COL_CC), col(COL_CH),
            pl.BlockSpec((None, CONV_WIDTH, BRANCH_W), lambda b, l: (layer, 0, 0)),
            pl.BlockSpec((None, CONV_WIDTH - 1, BRANCH_W), lambda b, l: (b, 0, 0)),
        ],
        out_specs=[
            pl.BlockSpec((tl, BRANCH_W), lambda b, l: (b * nl + l, 0)),
            pl.BlockSpec((None, CONV_WIDTH - 1, BRANCH_W), lambda b, l: (b, 0, 0)),
        ],
        out_shape=[
            jax.ShapeDtypeStruct((m, BRANCH_W), BF16),
            jax.ShapeDtypeStruct((bsz, CONV_WIDTH - 1, BRANCH_W), F32),
        ],
        scratch_shapes=[pltpu.VMEM((CONV_WIDTH - 1, BRANCH_W), F32)],
        compiler_params=_params(("parallel", "arbitrary"), 32),
        name="conv_branch",
    )(proj, proj, proj, conv_w, prev)


def _ret_log_gamma(h):
    return math.log1p(-(2.0 ** (-5.0 - h)))


def _ret_fill_decay(intra_ref):
    chunk = intra_ref.shape[1]
    ri = lax.broadcasted_iota(jnp.int32, (chunk, chunk), 0)
    ci = lax.broadcasted_iota(jnp.int32, (chunk, chunk), 1)
    rel = (ri - ci).astype(F32)
    for h in range(RET_HEADS):
        intra_ref[h] = jnp.where(rel >= 0.0, jnp.exp(jnp.maximum(rel, 0.0) * _ret_log_gamma(h)), 0.0)


def _retention_body(q_ref, k_ref, v_ref, g_ref, cos_ref, sin_ref, gn_ref, y_ref, s_ref, intra_ref):
    chunk = q_ref.shape[0]
    cosf = cos_ref[...]
    sinf = sin_ref[...]
    idx = lax.broadcasted_iota(jnp.int32, (chunk, 1), 0).astype(F32)
    for h in range(RET_HEADS):
        log_g = _ret_log_gamma(h)
        cols = slice(h * RET_DK, (h + 1) * RET_DK)
        q = q_ref[:, cols]
        k = k_ref[:, cols]
        q = q * cosf + pltpu.roll(q, RET_DK // 2, 1) * sinf
        k = (k * cosf + pltpu.roll(k, RET_DK // 2, 1) * sinf) * (RET_DK ** -0.5)
        vb = v_ref[:, cols].astype(BF16)
        q_dec = jnp.exp((idx + 1.0) * log_g)
        k_dec = jnp.exp((chunk - 1.0 - idx) * log_g)
        c_dec = math.exp(chunk * log_g)
        qb = q.astype(BF16)
        s_prev = s_ref[h]
        att = _dot_nt(qb, k.astype(BF16)) * intra_ref[h]
        o = _dot(att.astype(BF16), vb) + _dot(qb, s_prev.astype(BF16)) * q_dec
        s_ref[h] = s_prev * c_dec + _dot_tn((k * k_dec).astype(BF16), vb)
        mu = jnp.mean(o, -1, keepdims=True)
        d = o - mu
        var = jnp.mean(d * d, -1, keepdims=True)
        ro = d * lax.rsqrt(var + LN_EPS) * gn_ref[:, cols]
        gate = g_ref[:, cols]
        y_ref[:, cols] = (gate * jax.nn.sigmoid(gate) * ro).astype(BF16)


def _retention_kernel(q_ref, k_ref, v_ref, g_ref, cos_ref, sin_ref, s0_ref, gn_ref, y_ref, sfin_ref,
                      s_ref, intra_ref):
    c = pl.program_id(1)

    @pl.when(jnp.logical_and(pl.program_id(0) == 0, c == 0))
    def _():
        _ret_fill_decay(intra_ref)

    @pl.when(c == 0)
    def _():
        s_ref[...] = s0_ref[...]

    _retention_body(q_ref, k_ref, v_ref, g_ref, cos_ref, sin_ref, gn_ref, y_ref, s_ref, intra_ref)

    @pl.when(c == pl.num_programs(1) - 1)
    def _():
        sfin_ref[...] = s_ref[...]


def _retention_branch(proj, cosf, sinf, s0, gn_g, layer, *, bsz, length, chunk):
    m = proj.shape[0]
    nc = length // chunk
    assert length % chunk == 0
    col = lambda c: pl.BlockSpec((chunk, BRANCH_W), lambda b, i: (b * nc + i, c))
    state = pl.BlockSpec((None, RET_HEADS, RET_DK, RET_DK), lambda b, i: (b, 0, 0, 0))
    return pl.pallas_call(
        _retention_kernel,
        grid=(bsz, nc),
        in_specs=[
            col(COL_RQ), col(COL_RK), col(COL_RV), col(COL_RG),
            pl.BlockSpec((chunk, RET_DK), lambda b, i: (i, 0)),
            pl.BlockSpec((chunk, RET_DK), lambda b, i: (i, 0)),
            state,
            pl.BlockSpec((None, 1, BRANCH_W), lambda b, i: (layer, 0, 0)),
        ],
        out_specs=[pl.BlockSpec((chunk, BRANCH_W), lambda b, i: (b * nc + i, 0)), state],
        out_shape=[
            jax.ShapeDtypeStruct((m, BRANCH_W), BF16),
            jax.ShapeDtypeStruct((bsz, RET_HEADS, RET_DK, RET_DK), F32),
        ],
        scratch_shapes=[pltpu.VMEM((RET_HEADS, RET_DK, RET_DK), F32),
                        pltpu.VMEM((RET_HEADS, chunk, chunk), F32)],
        compiler_params=_params(("arbitrary", "arbitrary"), 32),
        name="retention_branch",
    )(proj, proj, proj, proj, cosf, sinf, s0, gn_g)


def _mem_head(q, mk, mv):
    s = _dot_nt(q.astype(BF16), mk.astype(BF16)) * (MEM_HD ** -0.5)
    p = jnp.exp(s - jnp.max(s, -1, keepdims=True))
    den = jnp.sum(p, -1, keepdims=True)
    return _dot(p.astype(BF16), mv.astype(BF16)) / den


def _local_mixers_kernel(cb_ref, cc_ref, ch_ref, rq_ref, rk_ref, rv_ref, rg_ref, mq_ref, w_ref, prev_ref,
                         cos_ref, sin_ref, s0_ref, gn_ref, mk_ref, mv_ref,
                         ya_ref, yb_ref, yd_ref, conv_state_ref, sfin_ref,
                         carry_ref, s_ref, intra_ref):
    c = pl.program_id(1)

    @pl.when(jnp.logical_and(pl.program_id(0) == 0, c == 0))
    def _():
        _ret_fill_decay(intra_ref)

    @pl.when(c == 0)
    def _():
        carry_ref[...] = prev_ref[...]
        s_ref[...] = s0_ref[...]

    tail = _conv_body(cb_ref, cc_ref, ch_ref, w_ref, carry_ref, ya_ref)
    _retention_body(rq_ref, rk_ref, rv_ref, rg_ref, cos_ref, sin_ref, gn_ref, yb_ref, s_ref, intra_ref)
    for h in range(MEM_HEADS):
        cols = slice(h * MEM_HD, (h + 1) * MEM_HD)
        yd_ref[:, cols] = _mem_head(mq_ref[:, cols], mk_ref[:, cols], mv_ref[:, cols]).astype(BF16)

    @pl.when(c == pl.num_programs(1) - 1)
    def _():
        conv_state_ref[...] = tail
        sfin_ref[...] = s_ref[...]


def _local_mixers(proj, conv_w, conv_prev, cosf, sinf, s0, gn_g, mkv, layer, *, bsz, length, chunk):
    m = proj.shape[0]
    nc = length // chunk
    assert length % chunk == 0 and chunk >= 8
    tokens = mkv.shape[1]
    col = lambda c: pl.BlockSpec((chunk, BRANCH_W), lambda b, i: (b * nc + i, c))
    out = pl.BlockSpec((chunk, BRANCH_W), lambda b, i: (b * nc + i, 0))
    state = pl.BlockSpec((None, RET_HEADS, RET_DK, RET_DK), lambda b, i: (b, 0, 0, 0))
    conv_state = pl.BlockSpec((None, CONV_WIDTH - 1, BRANCH_W), lambda b, i: (b, 0, 0))
    rope = pl.BlockSpec((chunk, RET_DK), lambda b, i: (i, 0))
    y_shape = jax.ShapeDtypeStruct((m, BRANCH_W), BF16)
    return pl.pallas_call(
        _local_mixers_kernel,
        grid=(bsz, nc),
        in_specs=[
            col(COL_CB), col(COL_CC), col(COL_CH), col(COL_RQ), col(COL_RK), col(COL_RV), col(COL_RG), col(COL_MQ),
            pl.BlockSpec((None, CONV_WIDTH, BRANCH_W), lambda b, i: (layer, 0, 0)),
            conv_state, rope, rope, state,
            pl.BlockSpec((None, 1, BRANCH_W), lambda b, i: (layer, 0, 0)),
            pl.BlockSpec((None, tokens, BRANCH_W), lambda b, i: (b, 0, 0)),
            pl.BlockSpec((None, tokens, BRANCH_W), lambda b, i: (b, 0, 1)),
        ],
        out_specs=[out, out, out, conv_state, state],
        out_shape=[
            y_shape, y_shape, y_shape,
            jax.ShapeDtypeStruct((bsz, CONV_WIDTH - 1, BRANCH_W), F32),
            jax.ShapeDtypeStruct((bsz, RET_HEADS, RET_DK, RET_DK), F32),
        ],
        scratch_shapes=[pltpu.VMEM((CONV_WIDTH - 1, BRANCH_W), F32),
                        pltpu.VMEM((RET_HEADS, RET_DK, RET_DK), F32),
                        pltpu.VMEM((RET_HEADS, chunk, chunk), F32)],
        compiler_params=_params(("arbitrary", "arbitrary"), 40),
        name="local_mixers",
    )(proj, proj, proj, proj, proj, proj, proj, proj, conv_w, conv_prev, cosf, sinf, s0, gn_g, mkv, mkv)


def _diff_lambda(lam_ref, lam_init):
    e0 = jnp.exp(jnp.sum(lam_ref[0:1, :] * lam_ref[1:2, :], -1, keepdims=True))
    e1 = jnp.exp(jnp.sum(lam_ref[2:3, :] * lam_ref[3:4, :], -1, keepdims=True))
    return e0 - e1 + lam_init


def _diff_prompt_kernel(q_ref, k_ref, v_ref, bias_ref, lam_ref, g_ref, o_ref, k16_ref, vt_ref,
                        sa_ref, sb_ref, m0_ref, m1_ref, l0_ref, l1_ref, acc0_ref, acc1_ref, *, lam_init):
    i = pl.program_id(2)
    t = q_ref.shape[0]
    nt = k_ref.shape[0] // t
    m_refs, l_refs, acc_refs = (m0_ref, m1_ref), (l0_ref, l1_ref), (acc0_ref, acc1_ref)

    @pl.when(i == 0)
    def _():
        k16_ref[...] = k_ref[...].astype(BF16)
        for j in range(nt):
            vt_ref[j] = jnp.transpose(v_ref[j * t:(j + 1) * t, :]).astype(BF16)

    qt = jnp.transpose(q_ref[...] * (DIFF_D ** -0.5 * LOG2E))
    feat = lax.broadcasted_iota(jnp.int32, qt.shape, 0)
    qts = (jnp.where(feat < DIFF_D, qt, 0.0).astype(BF16), jnp.where(feat >= DIFF_D, qt, 0.0).astype(BF16))
    for c in range(2):
        m_refs[c][...] = jnp.full(m_refs[c].shape, NEG_INF, F32)
        l_refs[c][...] = jnp.zeros(l_refs[c].shape, F32)
        acc_refs[c][...] = jnp.zeros(acc_refs[c].shape, F32)

    def scores(j, dst_ref):
        jc = jnp.minimum(j, i)
        kt = k16_ref[pl.ds(pl.multiple_of(jc * t, t), t), :]
        bias = bias_ref[jnp.where(jc == i, 0, jnp.where(jc == i - 1, 1, 2))]
        for c in range(2):
            dst_ref[c] = _dot(kt, qts[c]) + bias

    def consume(j, src_ref):
        vt = vt_ref[j]
        s = [src_ref[c] for c in range(2)]
        m_prev = [m_refs[c][...] for c in range(2)]
        m_new = [jnp.maximum(m_prev[c], jnp.max(s[c], 0, keepdims=True)) for c in range(2)]
        p = [jnp.exp2(s[c] - m_new[c]) for c in range(2)]
        a = [jnp.exp2(m_prev[c] - m_new[c]) for c in range(2)]
        for c in range(2):
            l_refs[c][...] = a[c] * l_refs[c][...] + jnp.sum(p[c], 0, keepdims=True)
            acc_refs[c][...] = a[c] * acc_refs[c][...] + _dot(vt, p[c].astype(BF16))
            m_refs[c][...] = m_new[c]

    scores(0, sa_ref)

    def pair(p, carry):
        j = 2 * p
        scores(j + 1, sb_ref)
        consume(j, sa_ref)
        scores(j + 2, sa_ref)
        consume(j + 1, sb_ref)
        return carry

    n_tiles = i + 1
    lax.fori_loop(0, n_tiles // 2, pair, 0)

    @pl.when(n_tiles % 2 == 1)
    def _():
        consume(i, sa_ref)

    lam = _diff_lambda(lam_ref, lam_init)
    o = acc0_ref[...] / l0_ref[...] - lam * (acc1_ref[...] / l1_ref[...])
    o = o * lax.rsqrt(jnp.mean(o * o, 0, keepdims=True) + LN_EPS) * g_ref[...]
    o_ref[...] = jnp.transpose(o * (1.0 - lam_init)).astype(BF16)


def _diff_prompt(proj, bias_tiles, diff_lambda, subln_g, layer, *, bsz, length, t, lam_init):
    m = proj.shape[0]
    nq = length // t
    assert length % t == 0 and t % CHUNK == 0
    kv = lambda c: pl.BlockSpec((length, HEAD_W), lambda b, h, i: (b, c * HEADS_PER_BRANCH + h))
    stat = pltpu.VMEM((1, t), F32)
    acc = pltpu.VMEM((DIFF_DV, t), F32)
    return pl.pallas_call(
        functools.partial(_diff_prompt_kernel, lam_init=lam_init),
        grid=(bsz, DIFF_HEADS, nq),
        in_specs=[
            pl.BlockSpec((t, HEAD_W), lambda b, h, i: (b * nq + i, COL_DQ * HEADS_PER_BRANCH + h)),
            kv(COL_DK), kv(COL_DV),
            pl.BlockSpec((None, 3, t, t), lambda b, h, i: (h, 0, 0, 0)),
            pl.BlockSpec((None, 4, DIFF_D), lambda b, h, i: (layer, 0, 0)),
            pl.BlockSpec((None, DIFF_DV, 1), lambda b, h, i: (layer, 0, 0)),
        ],
        out_specs=pl.BlockSpec((t, HEAD_W), lambda b, h, i: (b * nq + i, h)),
        out_shape=jax.ShapeDtypeStruct((m, BRANCH_W), BF16),
        scratch_shapes=[pltpu.VMEM((length, HEAD_W), BF16), pltpu.VMEM((nq, DIFF_DV, t), BF16),
                        pltpu.VMEM((2, t, t), F32), pltpu.VMEM((2, t, t), F32),
                        stat, stat, stat, stat, acc, acc],
        compiler_params=_params(("parallel", "parallel", "arbitrary"), 40),
        name="diff_attention_prompt",
    )(proj, proj, proj, bias_tiles, diff_lambda, subln_g)


def _head_rows(ref, h, n):
    return ref[pl.ds(h, n, stride=HEADS_PER_BRANCH), :]


def _sample_attn_kernel(q_ref, kn_ref, vn_ref, mq_ref, kp_ref, vp_ref, mk_ref, mv_ref, bp_ref, bn_ref,
                        lam_ref, g_ref, yc_ref, yd_ref, *, lam_init):
    past = kp_ref.shape[0] // HEADS_PER_BRANCH
    tokens = mk_ref.shape[0] // HEADS_PER_BRANCH
    lam = _diff_lambda(lam_ref, lam_init)
    for h in range(HEADS_PER_BRANCH):
        cols = slice(h * HEAD_W, (h + 1) * HEAD_W)
        q = q_ref[:, cols] * (DIFF_D ** -0.5 * LOG2E)
        lane = lax.broadcasted_iota(jnp.int32, q.shape, 1)
        qs = (jnp.where(lane < DIFF_D, q, 0.0).astype(BF16), jnp.where(lane >= DIFF_D, q, 0.0).astype(BF16))
        kp = _head_rows(kp_ref, h, past).astype(BF16)
        vp = _head_rows(vp_ref, h, past).astype(BF16)
        kn = kn_ref[:, cols].astype(BF16)
        vn = vn_ref[:, cols].astype(BF16)
        outs = []
        for qc in qs:
            sp = _dot_nt(qc, kp) + bp_ref[h]
            sn = _dot_nt(qc, kn) + bn_ref[h]
            mx = jnp.maximum(jnp.max(sp, -1, keepdims=True), jnp.max(sn, -1, keepdims=True))
            pp = jnp.exp2(sp - mx)
            pn = jnp.exp2(sn - mx)
            den = jnp.sum(pp, -1, keepdims=True) + jnp.sum(pn, -1, keepdims=True)
            outs.append((_dot(pp.astype(BF16), vp) + _dot(pn.astype(BF16), vn)) / den)
        o = outs[0] - lam * outs[1]
        o = o * lax.rsqrt(jnp.mean(o * o, -1, keepdims=True) + LN_EPS) * g_ref[...]
        yc_ref[:, cols] = (o * (1.0 - lam_init)).astype(BF16)

        mk = _head_rows(mk_ref, h, tokens).astype(BF16)
        mv = _head_rows(mv_ref, h, tokens).astype(BF16)
        s = _dot_nt(mq_ref[:, cols].astype(BF16), mk) * (MEM_HD ** -0.5)
        p = jnp.exp(s - jnp.max(s, -1, keepdims=True))
        den = jnp.sum(p, -1, keepdims=True)
        yd_ref[:, cols] = (_dot(p.astype(BF16), mv) / den).astype(BF16)


def _sample_attention(proj, k_past, v_past, mem_k, mem_v, bias_past, bias_new, diff_lambda, subln_g, layer,
                      *, bsz, length, lam_init):
    m = proj.shape[0]
    col = lambda c: pl.BlockSpec((length, BRANCH_W), lambda b: (b, c))
    cache = lambda a: pl.BlockSpec((None, None) + a.shape[2:], lambda b: (layer, b, 0, 0))
    whole = lambda a: pl.BlockSpec(a.shape, lambda b: (0,) * a.ndim)
    out = pl.BlockSpec((length, BRANCH_W), lambda b: (b, 0))
    return pl.pallas_call(
        functools.partial(_sample_attn_kernel, lam_init=lam_init),
        grid=(bsz,),
        in_specs=[
            col(COL_DQ), col(COL_DK), col(COL_DV), col(COL_MQ),
            cache(k_past), cache(v_past), cache(mem_k), cache(mem_v),
            whole(bias_past), whole(bias_new),
            pl.BlockSpec((None, 4, DIFF_D), lambda b: (layer, 0, 0)),
            pl.BlockSpec((None, 1, DIFF_DV), lambda b: (layer, 0, 0)),
        ],
        out_specs=[out, out],
        out_shape=[jax.ShapeDtypeStruct((m, BRANCH_W), BF16), jax.ShapeDtypeStruct((m, BRANCH_W), BF16)],
        compiler_params=_params(("parallel",), 40),
        name="sample_attention",
    )(proj, proj, proj, proj, k_past, v_past, mem_k, mem_v, bias_past, bias_new, diff_lambda, subln_g)


def _merge_kernel(h_ref, ya_ref, yb_ref, yc_ref, yd_ref, wg_ref, wb_ref, bg_ref, o_ref):
    h = h_ref[...]
    acc = None
    for i, y_ref in enumerate((ya_ref, yb_ref, yc_ref, yd_ref)):
        gate = jax.nn.sigmoid(_dot(h, wg_ref[i]) + bg_ref[i])
        term = gate * _dot(y_ref[...], wb_ref[i])
        acc = term if acc is None else acc + term
    o_ref[...] = acc.astype(BF16)


def _merge(h16, ys, w_gate, w_branch, b_gate, layer, *, tm, tn):
    m, d = h16.shape
    assert m % tm == 0 and d % tn == 0
    y_spec = pl.BlockSpec((tm, BRANCH_W), lambda i, j: (i, 0))
    return pl.pallas_call(
        _merge_kernel,
        grid=(m // tm, d // tn),
        in_specs=[
            pl.BlockSpec((tm, d), lambda i, j: (i, 0)), y_spec, y_spec, y_spec, y_spec,
            pl.BlockSpec((None, N_BRANCH, d, tn), lambda i, j: (layer, 0, 0, j)),
            pl.BlockSpec((None, N_BRANCH, BRANCH_W, tn), lambda i, j: (layer, 0, 0, j)),
            pl.BlockSpec((None, N_BRANCH, 1, tn), lambda i, j: (layer, 0, 0, j)),
        ],
        out_specs=pl.BlockSpec((tm, tn), lambda i, j: (i, j)),
        out_shape=jax.ShapeDtypeStruct((m, d), BF16),
        compiler_params=_params(("parallel", "parallel"), 48),
        name="gated_merge",
    )(h16, *ys, w_gate, w_branch, b_gate)


def _proj_ln_kernel(m_ref, x_ref, w_ref, g_ref, b_ref, o_ref, *, alpha):
    for r in range(0, o_ref.shape[0], PROJ_ROW_CHUNK):
        rows = slice(r, r + PROJ_ROW_CHUNK)
        y = alpha * x_ref[rows, :] + _dot(m_ref[rows, :], w_ref[...])
        o_ref[rows, :] = _layer_norm(y, g_ref[...], b_ref[...])


def _proj_ln(merged, x, w_o, g, b, layer, *, alpha, tm):
    m, d = x.shape
    assert m % tm == 0
    row = pl.BlockSpec((tm, d), lambda i: (i, 0))
    vec = pl.BlockSpec((None, 1, d), lambda i: (layer, 0, 0))
    return pl.pallas_call(
        functools.partial(_proj_ln_kernel, alpha=alpha),
        grid=(m // tm,),
        in_specs=[row, row, pl.BlockSpec((None, d, d), lambda i: (layer, 0, 0)), vec, vec],
        out_specs=row,
        out_shape=jax.ShapeDtypeStruct((m, d), F32),
        compiler_params=_params(("parallel",), 48),
        name="proj_ln",
    )(merged, x, w_o, g, b)


def _rope_tables(pos):
    half = RET_DK // 2
    inv = ROPE_BASE ** (-jnp.arange(half, dtype=F32) / half)
    ang = pos.astype(F32)[:, None] * inv[None, :]
    cos = jnp.cos(ang)
    sin = jnp.sin(ang)
    return jnp.concatenate([cos, cos], -1), jnp.concatenate([-sin, sin], -1)


def _t5_bucket(rel):
    nb = REL_BUCKETS // 2
    max_exact = nb // 2
    n = jnp.abs(rel)
    nf = jnp.maximum(n, 1).astype(F32)
    large = max_exact + (jnp.log(nf / max_exact) / math.log(REL_MAX_DIST / max_exact)
                         * (nb - max_exact)).astype(jnp.int32)
    large = jnp.minimum(large, nb - 1)
    return jnp.where(rel > 0, nb, 0) + jnp.where(n < max_exact, n, large)


def _masked_bias(q_pos, k_pos, rel_bias):
    bucket = _t5_bucket(k_pos[None, :] - q_pos[:, None])
    onehot = bucket[None, None] == jnp.arange(REL_BUCKETS, dtype=bucket.dtype)[None, :, None, None]
    bias = jnp.sum(jnp.where(onehot, rel_bias.astype(F32).T[:, :, None, None], 0.0), axis=1)
    allowed = (k_pos[None, :] // CHUNK) <= (q_pos[:, None] // CHUNK)
    return jnp.where(allowed[None], bias * LOG2E, NEG_INF)


def _prompt_bias_tiles(rel_bias, t):
    base = BIAS_BASE
    assert base % CHUNK == 0 and base + 1 >= REL_MAX_DIST and t % base == 0
    n = t // base
    pos = jnp.arange(base, dtype=jnp.int32)
    diag, sub, far = (jnp.swapaxes(_masked_bias(pos + d * base, pos, rel_bias), -1, -2) for d in range(3))
    masked = jnp.full_like(diag, NEG_INF)
    pick = lambda delta: masked if delta > 0 else (diag, sub, far)[min(-delta, 2)]
    tile = lambda d: jnp.concatenate(
        [jnp.concatenate([pick(kk - qq - d * n) for qq in range(n)], axis=-1) for kk in range(n)], axis=-2)
    return jnp.stack([tile(d) for d in range(3)], axis=1)


def _kv_export_kernel(k_ref, v_ref, *rest):
    ok_ref, ov_ref = rest[-2:]
    n = k_ref.shape[0]
    for h in range(HEADS_PER_BRANCH):
        cols = slice(h * HEAD_W, (h + 1) * HEAD_W)
        rows = pl.ds(h, n, stride=HEADS_PER_BRANCH)
        ok_ref[rows, :] = k_ref[:, cols]
        ov_ref[rows, :] = v_ref[:, cols]


def _kv_export(proj, prev, layer, depth, *, tm):
    m = proj.shape[0]
    assert m % tm == 0
    col = lambda c: pl.BlockSpec((tm, BRANCH_W), lambda i: (i, c))
    out = pl.BlockSpec((None, tm * HEADS_PER_BRANCH, HEAD_W), lambda i: (layer, i, 0))
    shape = jax.ShapeDtypeStruct((depth, m * HEADS_PER_BRANCH, HEAD_W), F32)
    keep = [pl.BlockSpec(memory_space=pl.ANY)] * len(prev)
    return pl.pallas_call(
        _kv_export_kernel,
        grid=(m // tm,),
        in_specs=[col(COL_DK), col(COL_DV)] + keep,
        out_specs=[out, out],
        out_shape=[shape, shape],
        input_output_aliases={2 + n: n for n in range(len(prev))},
        compiler_params=_params(("parallel",), 32),
        name="kv_export",
    )(proj, proj, *prev)


def _encoder_layer(x, w, layer, *, alpha, cfg, mixers_fn):
    x1 = _ffn_ln(x, w['ffn1_w_up'], w['ffn1_w_down'], w['ln1_g'], w['ln1_b'], layer,
                 alpha=alpha, tm=cfg['tm_ffn'], tf=cfg['tf'])
    proj, x1b = _in_proj(x1, w['w_in'], layer, tm=cfg['tm_in'], tn=cfg['tn_in'])
    y_a, y_b, y_c, y_d, conv_new, ret_new = mixers_fn(proj, 0.8 - 0.6 * math.exp(-0.3 * layer))
    merged = _merge(x1b, (y_a, y_b, y_c, y_d), w['w_gate'], w['w_branch'], w['b_gate'], layer,
                    tm=cfg['tm_merge'], tn=cfg['tn_merge'])
    x2 = _proj_ln(merged, x1, w['w_o'], w['ln2_g'], w['ln2_b'], layer, alpha=alpha, tm=cfg['tm_proj'])
    x3 = _ffn_ln(x2, w['ffn2_w_up'], w['ffn2_w_down'], w['ln3_g'], w['ln3_b'], layer,
                 alpha=alpha, tm=cfg['tm_ffn'], tf=cfg['tf'])
    return x3, proj, conv_new, ret_new


PROMPT_CFG = dict(tm_ffn=1024, tf=512, tm_in=1024, tn_in=1408, t_conv=512, ret_chunk=256, t_diff=512,
                  tq_mem=1024, tm_merge=1024, tn_merge=256, tm_proj=512, tm_export=1024)
SAMPLE_CFG = dict(tm_ffn=256, tf=512, tm_in=256, tn_in=512, t_conv=16, ret_chunk=16,
                  tm_merge=256, tn_merge=256, tm_proj=256, tm_export=256)


def kernel(x_prompt, x_sample, state_conv, state_ret, cache_diff_k, cache_diff_v, cache_mem_k, cache_mem_v,
           mem_prompt, ffn1_w_up, ffn1_w_down, ln1_g, ln1_b, w_in, conv_w, ret_gn_g, diff_lambda,
           diff_subln_g, w_mem_kv, w_branch, w_gate, b_gate, w_o, ln2_g, ln2_b, ffn2_w_up, ffn2_w_down,
           ln3_g, ln3_b, rel_bias):
    bp, lp, d = x_prompt.shape
    bs, ls, _ = x_sample.shape
    depth = w_in.shape[0]
    past = cache_diff_k.shape[2]
    mem_tokens = mem_prompt.shape[1]
    alpha = (2 * depth) ** 0.25
    half = MEM_HEADS * MEM_HD

    pos_p = jnp.arange(lp, dtype=jnp.int32)
    pos_s = past + jnp.arange(ls, dtype=jnp.int32)
    past_pos = jnp.arange(past, dtype=jnp.int32)
    rope_p = _rope_tables(pos_p)
    rope_s = _rope_tables(pos_s)
    bias_p = _prompt_bias_tiles(rel_bias, PROMPT_CFG['t_diff'])
    bias_s_past = _masked_bias(pos_s, past_pos, rel_bias)
    bias_s_new = _masked_bias(pos_s, pos_s, rel_bias)

    vec = lambda v: v.reshape(depth, 1, v.shape[-1])
    w = {
        'ffn1_w_up': ffn1_w_up.astype(BF16), 'ffn1_w_down': ffn1_w_down.astype(BF16),
        'ln1_g': vec(ln1_g), 'ln1_b': vec(ln1_b), 'w_in': w_in.astype(BF16), 'conv_w': conv_w,
        'ret_gn_g': vec(ret_gn_g), 'w_branch': w_branch.astype(BF16), 'w_gate': w_gate.astype(BF16),
        'b_gate': b_gate.reshape(depth, N_BRANCH, 1, d), 'w_o': w_o.astype(BF16),
        'ln2_g': vec(ln2_g), 'ln2_b': vec(ln2_b),
        'ffn2_w_up': ffn2_w_up.astype(BF16), 'ffn2_w_down': ffn2_w_down.astype(BF16),
        'ln3_g': vec(ln3_g), 'ln3_b': vec(ln3_b),
    }
    w_mem16 = w_mem_kv.astype(BF16)
    subln_col = diff_subln_g.reshape(depth, DIFF_DV, 1)
    subln_row = diff_subln_g.reshape(depth, 1, DIFF_DV)
    head_major = lambda c: c.reshape(depth, bs, c.shape[2] * c.shape[3], c.shape[4])
    kpast, vpast = head_major(cache_diff_k), head_major(cache_diff_v)
    mem_k_s, mem_v_s = head_major(cache_mem_k), head_major(cache_mem_v)

    yp = x_prompt.reshape(bp * lp, d)
    ys = x_sample.reshape(bs * ls, d)
    mem16 = mem_prompt.reshape(bp * mem_tokens, d).astype(BF16)
    zero_conv = jnp.zeros((bp, CONV_WIDTH - 1, BRANCH_W), F32)
    zero_ret = jnp.zeros((bp, RET_HEADS, RET_DK, RET_DK), F32)

    conv_p, ret_p, mk_p, mv_p, conv_s, ret_s = [], [], [], [], [], []
    kv_p, kv_s = [], []
    for l in range(depth):
        mkv = _matmul(mem16, w_mem16, l, tm=bp * mem_tokens, tn=512).reshape(bp, mem_tokens, 2 * half)

        def mixers_p(proj, lam_init):
            y_a, y_b, y_d, c_new, r_new = _local_mixers(
                proj, conv_w, zero_conv, rope_p[0], rope_p[1], zero_ret, w['ret_gn_g'], mkv, l,
                bsz=bp, length=lp, chunk=PROMPT_CFG['ret_chunk'])
            y_c = _diff_prompt(proj, bias_p, diff_lambda, subln_col, l, bsz=bp, length=lp,
                               t=PROMPT_CFG['t_diff'], lam_init=lam_init)
            return y_a, y_b, y_c, y_d, c_new, r_new

        yp, proj, c_new, r_new = _encoder_layer(yp, w, l, alpha=alpha, cfg=PROMPT_CFG, mixers_fn=mixers_p)
        kv_p = _kv_export(proj, kv_p, l, depth, tm=PROMPT_CFG['tm_export'])
        conv_p.append(c_new)
        ret_p.append(r_new)
        mk_p.append(mkv[:, :, :half].reshape(bp, mem_tokens, MEM_HEADS, MEM_HD))
        mv_p.append(mkv[:, :, half:].reshape(bp, mem_tokens, MEM_HEADS, MEM_HD))

        def mixers_s(proj, lam_init):
            y_a, c_new = _conv_branch(proj, conv_w, state_conv[l], l, bsz=bs, length=ls, tl=SAMPLE_CFG['t_conv'])
            y_b, r_new = _retention_branch(proj, rope_s[0], rope_s[1], state_ret[l], w['ret_gn_g'], l,
                                           bsz=bs, length=ls, chunk=SAMPLE_CFG['ret_chunk'])
            y_c, y_d = _sample_attention(proj, kpast, vpast, mem_k_s, mem_v_s, bias_s_past, bias_s_new,
                                         diff_lambda, subln_row, l, bsz=bs, length=ls, lam_init=lam_init)
            return y_a, y_b, y_c, y_d, c_new, r_new

        ys, proj, c_new, r_new = _encoder_layer(ys, w, l, alpha=alpha, cfg=SAMPLE_CFG, mixers_fn=mixers_s)
        kv_s = _kv_export(proj, kv_s, l, depth, tm=SAMPLE_CFG['tm_export'])
        conv_s.append(c_new)
        ret_s.append(r_new)

    kv_shape = lambda b, n: (depth, b, n, DIFF_HEADS, DIFF_DV)
    return (yp.reshape(bp, lp, d), ys.reshape(bs, ls, d), jnp.stack(conv_p), jnp.stack(ret_p),
            kv_p[0].reshape(kv_shape(bp, lp)), kv_p[1].reshape(kv_shape(bp, lp)),
            jnp.stack(mk_p), jnp.stack(mv_p), jnp.stack(conv_s), jnp.stack(ret_s),
            kv_s[0].reshape(kv_shape(bs, ls)), kv_s[1].reshape(kv_shape(bs, ls)))
```

```python
import functools
import math

import jax
import jax.numpy as jnp
from jax import lax
from jax.experimental import pallas as pl
from jax.experimental.pallas import tpu as pltpu

F32 = jnp.float32
BF16 = jnp.bfloat16

CHUNK = 64
CONV_WIDTH = 3
RET_HEADS = 4
RET_DK = 128
DIFF_HEADS = 4
DIFF_D = 64
DIFF_DV = 128
MEM_HEADS = 4
MEM_HD = 128
BRANCH_W = 512
N_BRANCH = 4
REL_BUCKETS = 32
REL_MAX_DIST = 128
LN_EPS = 1e-5
ROPE_BASE = 10000.0
NEG_INF = -1e30
HEAD_W = 128
LOG2E = math.log2(math.e)
LN_ROW_CHUNK = 128
PROJ_ROW_CHUNK = 256
BIAS_BASE = 256
FFN_UP_CHUNK = 256
FFN_DOWN_CHUNK = 512

COL_CB, COL_CC, COL_CH, COL_RQ, COL_RK, COL_RV, COL_RG, COL_DQ, COL_DK, COL_DV, COL_MQ = range(11)
HEADS_PER_BRANCH = BRANCH_W // HEAD_W

LANES = 128
BF16_SUBLANES = 16
V7X_VMEM_BYTES = 64 * 1024 * 1024
MIB = 1024 * 1024


def _params(semantics, vmem_mib):
    assert vmem_mib * MIB < V7X_VMEM_BYTES
    return pltpu.CompilerParams(dimension_semantics=semantics, vmem_limit_bytes=vmem_mib * MIB)


def _layer_norm(y, g, b):
    mu = jnp.mean(y, -1, keepdims=True)
    d = y - mu
    var = jnp.mean(d * d, -1, keepdims=True)
    return d * lax.rsqrt(var + LN_EPS) * g + b


def _dot(a, b):
    return jnp.dot(a, b, preferred_element_type=F32)


def _dot_nt(a, b):
    return lax.dot_general(a, b, (((1,), (1,)), ((), ())), preferred_element_type=F32)


def _dot_tn(a, b):
    return lax.dot_general(a, b, (((0,), (0,)), ((), ())), preferred_element_type=F32)


def _side_cast_specs(casts, grid):
    in_specs, out_specs, out_shapes = [], [], []
    for src, layer, swap in casts:
        _, r, c = src.shape
        gr, gc = (grid[1], grid[0]) if swap else grid
        assert r % gr == 0 and c % gc == 0
        blk = (None, r // gr, c // gc)
        assert blk[1] % BF16_SUBLANES == 0 and blk[2] % LANES == 0
        pick = (lambda i, j: (j, i)) if swap else (lambda i, j: (i, j))
        in_specs.append(pl.BlockSpec(blk, lambda i, j, layer=layer, pick=pick: (layer,) + pick(i, j)))
        out_specs.append(pl.BlockSpec(blk, lambda i, j, pick=pick: (0,) + pick(i, j)))
        out_shapes.append(jax.ShapeDtypeStruct((1, r, c), BF16))
    return in_specs, out_specs, out_shapes


def _side_cast(src_refs, dst_refs):
    for src_ref, dst_ref in zip(src_refs, dst_refs):
        dst_ref[...] = src_ref[...].astype(BF16)


def _ffn_ln_kernel(x_ref, wa_ref, wb_ref, wd_ref, g_ref, b_ref, *rest, alpha, n_cast):
    cast_src, o_ref, cast_dst = rest[:n_cast], rest[n_cast], rest[n_cast + 1:2 * n_cast + 1]
    xb_ref, h_ref = rest[2 * n_cast + 1:]
    j = pl.program_id(1)

    @pl.when(j == 0)
    def _():
        xb_ref[...] = x_ref[...].astype(BF16)
        o_ref[...] = jnp.zeros(o_ref.shape, F32)

    xb = xb_ref[...]
    for c in range(0, h_ref.shape[1], FFN_UP_CHUNK):
        cols = slice(c, c + FFN_UP_CHUNK)
        a = _dot(xb, wa_ref[:, cols])
        b = _dot(xb, wb_ref[:, cols])
        h_ref[:, cols] = (a * jax.nn.sigmoid(a) * b).astype(BF16)
    h = h_ref[...]
    for c in range(0, o_ref.shape[1], FFN_DOWN_CHUNK):
        cols = slice(c, c + FFN_DOWN_CHUNK)
        o_ref[:, cols] += _dot(h, wd_ref[:, cols])
    _side_cast(cast_src, cast_dst)

    @pl.when(j == pl.num_programs(1) - 1)
    def _():
        for r in range(0, o_ref.shape[0], LN_ROW_CHUNK):
            rows = slice(r, r + LN_ROW_CHUNK)
            o_ref[rows, :] = _layer_norm(alpha * x_ref[rows, :] + 0.5 * o_ref[rows, :], g_ref[...], b_ref[...])


def _ffn_ln(x, w_up, w_down, g, b, layer, *, alpha, tm, tf, casts=()):
    m, d = x.shape
    d_ff = w_down.shape[1]
    nj = d_ff // tf
    assert m % tm == 0 and d_ff % tf == 0
    grid = (m // tm, nj)
    row = lambda i, j: (i, 0)
    vec = pl.BlockSpec((None, 1, d), lambda i, j: (layer, 0, 0))
    cast_in, cast_out, cast_shapes = _side_cast_specs(casts, grid)
    out = pl.pallas_call(
        functools.partial(_ffn_ln_kernel, alpha=alpha, n_cast=len(casts)),
        grid=grid,
        in_specs=[
            pl.BlockSpec((tm, d), row),
            pl.BlockSpec((None, d, tf), lambda i, j: (0, 0, j)),
            pl.BlockSpec((None, d, tf), lambda i, j: (0, 0, j + nj)),
            pl.BlockSpec((None, tf, d), lambda i, j: (0, j, 0)),
            vec, vec,
        ] + cast_in,
        out_specs=[pl.BlockSpec((tm, d), row)] + cast_out,
        out_shape=[jax.ShapeDtypeStruct((m, d), F32)] + cast_shapes,
        scratch_shapes=[pltpu.VMEM((tm, d), BF16), pltpu.VMEM((tm, tf), BF16)],
        compiler_params=_params(("parallel", "arbitrary"), 60),
        name="ffn_ln",
    )(x, w_up, w_up, w_down, g, b, *(c[0] for c in casts))
    return out[0], out[1:]


def _in_proj_kernel(x_ref, w_ref, *rest, n_cast):
    cast_src, o_ref, xb_ref, cast_dst = rest[:n_cast], rest[n_cast], rest[n_cast + 1], rest[n_cast + 2:]

    @pl.when(pl.program_id(1) == 0)
    def _():
        xb_ref[...] = x_ref[...].astype(BF16)

    o_ref[...] = _dot(xb_ref[...], w_ref[...])
    _side_cast(cast_src, cast_dst)


def _in_proj(x, w, *, tm, tn, casts=()):
    m, k = x.shape
    n = w.shape[2]
    assert m % tm == 0 and n % tn == 0
    grid = (m // tm, n // tn)
    cast_in, cast_out, cast_shapes = _side_cast_specs(casts, grid)
    out = pl.pallas_call(
        functools.partial(_in_proj_kernel, n_cast=len(casts)),
        grid=grid,
        in_specs=[pl.BlockSpec((tm, k), lambda i, j: (i, 0)),
                  pl.BlockSpec((None, k, tn), lambda i, j: (0, 0, j))] + cast_in,
        out_specs=[pl.BlockSpec((tm, tn), lambda i, j: (i, j)), pl.BlockSpec((tm, k), lambda i, j: (i, 0))] + cast_out,
        out_shape=[jax.ShapeDtypeStruct((m, n), F32), jax.ShapeDtypeStruct((m, k), BF16)] + cast_shapes,
        compiler_params=_params(("parallel", "arbitrary"), 60),
        name="in_proj",
    )(x, w, *(c[0] for c in casts))
    return out[0], out[1], out[2:]


def _matmul_kernel(x_ref, w_ref, o_ref):
    o_ref[...] = _dot(x_ref[...], w_ref[...])


def _matmul(x, w, layer, *, tm, tn):
    m, k = x.shape
    n = w.shape[2]
    assert m % tm == 0 and n % tn == 0
    return pl.pallas_call(
        _matmul_kernel,
        grid=(m // tm, n // tn),
        in_specs=[pl.BlockSpec((tm, k), lambda i, j: (i, 0)),
                  pl.BlockSpec((None, k, tn), lambda i, j: (layer, 0, j))],
        out_specs=pl.BlockSpec((tm, tn), lambda i, j: (i, j)),
        out_shape=jax.ShapeDtypeStruct((m, n), F32),
        compiler_params=_params(("parallel", "parallel"), 48),
        name="matmul",
    )(x, w)


def _conv_body(cb_ref, cc_ref, ch_ref, w_ref, carry_ref, y_ref):
    u = cc_ref[...] * ch_ref[...]
    row = lax.broadcasted_iota(jnp.int32, u.shape, 0)
    c0 = carry_ref[0:1, :]
    c1 = carry_ref[1:2, :]
    u1 = jnp.where(row == 0, c1, pltpu.roll(u, 1, 0))
    u2 = jnp.where(row == 0, c0, jnp.where(row == 1, c1, pltpu.roll(u, 2, 0)))
    z = w_ref[0:1, :] * u2 + w_ref[1:2, :] * u1 + w_ref[2:3, :] * u
    y_ref[...] = (cb_ref[...] * z).astype(BF16)
    tl = u.shape[0]
    tail = u[tl - 8:, :][6:8, :]
    carry_ref[...] = tail
    return tail


def _conv_kernel(cb_ref, cc_ref, ch_ref, w_ref, prev_ref, y_ref, state_ref, carry_ref):
    l = pl.program_id(1)

    @pl.when(l == 0)
    def _():
        carry_ref[...] = prev_ref[...]

    tail = _conv_body(cb_ref, cc_ref, ch_ref, w_ref, carry_ref, y_ref)

    @pl.when(l == pl.num_programs(1) - 1)
    def _():
        state_ref[...] = tail


def _conv_branch(proj, conv_w, prev, layer, *, bsz, length, tl):
    m = proj.shape[0]
    nl = length // tl
    assert length % tl == 0 and tl >= 8
    col = lambda c: pl.BlockSpec((tl, BRANCH_W), lambda b, l: (b * nl + l, c))
    return pl.pallas_call(
        _conv_kernel,
        grid=(bsz, nl),
        in_specs=[
            col(COL_CB), col(COL_CC), col(COL_CH),
            pl.BlockSpec((None, CONV_WIDTH, BRANCH_W), lambda b, l: (layer, 0, 0)),
            pl.BlockSpec((None, CONV_WIDTH - 1, BRANCH_W), lambda b, l: (b, 0, 0)),
        ],
        out_specs=[
            pl.BlockSpec((tl, BRANCH_W), lambda b, l: (b * nl + l, 0)),
            pl.BlockSpec((None, CONV_WIDTH - 1, BRANCH_W), lambda b, l: (b, 0, 0)),
        ],
        out_shape=[
            jax.ShapeDtypeStruct((m, BRANCH_W), BF16),
            jax.ShapeDtypeStruct((bsz, CONV_WIDTH - 1, BRANCH_W), F32),
        ],
        scratch_shapes=[pltpu.VMEM((CONV_WIDTH - 1, BRANCH_W), F32)],
        compiler_params=_params(("parallel", "arbitrary"), 32),
        name="conv_branch",
    )(proj, proj, proj, conv_w, prev)


def _ret_log_gamma(h):
    return math.log1p(-(2.0 ** (-5.0 - h)))


def _ret_fill_decay(intra_ref):
    chunk = intra_ref.shape[1]
    ri = lax.broadcasted_iota(jnp.int32, (chunk, chunk), 0)
    ci = lax.broadcasted_iota(jnp.int32, (chunk, chunk), 1)
    rel = (ri - ci).astype(F32)
    for h in range(RET_HEADS):
        intra_ref[h] = jnp.where(rel >= 0.0, jnp.exp(jnp.maximum(rel, 0.0) * _ret_log_gamma(h)), 0.0)


def _retention_body(q_ref, k_ref, v_ref, g_ref, cos_ref, sin_ref, gn_ref, y_ref, s_ref, intra_ref):
    chunk = q_ref.shape[0]
    cosf = cos_ref[...]
    sinf = sin_ref[...]
    idx = lax.broadcasted_iota(jnp.int32, (chunk, 1), 0).astype(F32)
    for h in range(RET_HEADS):
        log_g = _ret_log_gamma(h)
        cols = slice(h * RET_DK, (h + 1) * RET_DK)
        q = q_ref[:, cols]
        k = k_ref[:, cols]
        q = q * cosf + pltpu.roll(q, RET_DK // 2, 1) * sinf
        k = (k * cosf + pltpu.roll(k, RET_DK // 2, 1) * sinf) * (RET_DK ** -0.5)
        vb = v_ref[:, cols].astype(BF16)
        q_dec = jnp.exp((idx + 1.0) * log_g)
        k_dec = jnp.exp((chunk - 1.0 - idx) * log_g)
        c_dec = math.exp(chunk * log_g)
        qb = q.astype(BF16)
        s_prev = s_ref[h]
        att = _dot_nt(qb, k.astype(BF16)) * intra_ref[h]
        o = _dot(att.astype(BF16), vb) + _dot(qb, s_prev.astype(BF16)) * q_dec
        s_ref[h] = s_prev * c_dec + _dot_tn((k * k_dec).astype(BF16), vb)
        mu = jnp.mean(o, -1, keepdims=True)
        d = o - mu
        var = jnp.mean(d * d, -1, keepdims=True)
        ro = d * lax.rsqrt(var + LN_EPS) * gn_ref[:, cols]
        gate = g_ref[:, cols]
        y_ref[:, cols] = (gate * jax.nn.sigmoid(gate) * ro).astype(BF16)


def _retention_kernel(q_ref, k_ref, v_ref, g_ref, cos_ref, sin_ref, s0_ref, gn_ref, y_ref, sfin_ref,
                      s_ref, intra_ref):
    c = pl.program_id(1)

    @pl.when(jnp.logical_and(pl.program_id(0) == 0, c == 0))
    def _():
        _ret_fill_decay(intra_ref)

    @pl.when(c == 0)
    def _():
        s_ref[...] = s0_ref[...]

    _retention_body(q_ref, k_ref, v_ref, g_ref, cos_ref, sin_ref, gn_ref, y_ref, s_ref, intra_ref)

    @pl.when(c == pl.num_programs(1) - 1)
    def _():
        sfin_ref[...] = s_ref[...]


def _retention_branch(proj, cosf, sinf, s0, gn_g, layer, *, bsz, length, chunk):
    m = proj.shape[0]
    nc = length // chunk
    assert length % chunk == 0
    col = lambda c: pl.BlockSpec((chunk, BRANCH_W), lambda b, i: (b * nc + i, c))
    state = pl.BlockSpec((None, RET_HEADS, RET_DK, RET_DK), lambda b, i: (b, 0, 0, 0))
    return pl.pallas_call(
        _retention_kernel,
        grid=(bsz, nc),
        in_specs=[
            col(COL_RQ), col(COL_RK), col(COL_RV), col(COL_RG),
            pl.BlockSpec((chunk, RET_DK), lambda b, i: (i, 0)),
            pl.BlockSpec((chunk, RET_DK), lambda b, i: (i, 0)),
            state,
            pl.BlockSpec((None, 1, BRANCH_W), lambda b, i: (layer, 0, 0)),
        ],
        out_specs=[pl.BlockSpec((chunk, BRANCH_W), lambda b, i: (b * nc + i, 0)), state],
        out_shape=[
            jax.ShapeDtypeStruct((m, BRANCH_W), BF16),
            jax.ShapeDtypeStruct((bsz, RET_HEADS, RET_DK, RET_DK), F32),
        ],
        scratch_shapes=[pltpu.VMEM((RET_HEADS, RET_DK, RET_DK), F32),
                        pltpu.VMEM((RET_HEADS, chunk, chunk), F32)],
        compiler_params=_params(("arbitrary", "arbitrary"), 32),
        name="retention_branch",
    )(proj, proj, proj, proj, cosf, sinf, s0, gn_g)


def _mem_head(q, mk, mv):
    s = _dot_nt(q.astype(BF16), mk.astype(BF16)) * (MEM_HD ** -0.5)
    p = jnp.exp(s - jnp.max(s, -1, keepdims=True))
    den = jnp.sum(p, -1, keepdims=True)
    return _dot(p.astype(BF16), mv.astype(BF16)) / den


def _local_mixers_kernel(cb_ref, cc_ref, ch_ref, rq_ref, rk_ref, rv_ref, rg_ref, mq_ref, w_ref, prev_ref,
                         cos_ref, sin_ref, s0_ref, gn_ref, mk_ref, mv_ref,
                         ya_ref, yb_ref, yd_ref, conv_state_ref, sfin_ref,
                         carry_ref, s_ref, intra_ref):
    c = pl.program_id(1)

    @pl.when(jnp.logical_and(pl.program_id(0) == 0, c == 0))
    def _():
        _ret_fill_decay(intra_ref)

    @pl.when(c == 0)
    def _():
        carry_ref[...] = prev_ref[...]
        s_ref[...] = s0_ref[...]

    tail = _conv_body(cb_ref, cc_ref, ch_ref, w_ref, carry_ref, ya_ref)
    _retention_body(rq_ref, rk_ref, rv_ref, rg_ref, cos_ref, sin_ref, gn_ref, yb_ref, s_ref, intra_ref)
    for h in range(MEM_HEADS):
        cols = slice(h * MEM_HD, (h + 1) * MEM_HD)
        yd_ref[:, cols] = _mem_head(mq_ref[:, cols], mk_ref[:, cols], mv_ref[:, cols]).astype(BF16)

    @pl.when(c == pl.num_programs(1) - 1)
    def _():
        conv_state_ref[...] = tail
        sfin_ref[...] = s_ref[...]


def _local_mixers(proj, conv_w, conv_prev, cosf, sinf, s0, gn_g, mkv, layer, *, bsz, length, chunk):
    m = proj.shape[0]
    nc = length // chunk
    assert length % chunk == 0 and chunk >= 8
    tokens = mkv.shape[1]
    col = lambda c: pl.BlockSpec((chunk, BRANCH_W), lambda b, i: (b * nc + i, c))
    out = pl.BlockSpec((chunk, BRANCH_W), lambda b, i: (b * nc + i, 0))
    state = pl.BlockSpec((None, RET_HEADS, RET_DK, RET_DK), lambda b, i: (b, 0, 0, 0))
    conv_state = pl.BlockSpec((None, CONV_WIDTH - 1, BRANCH_W), lambda b, i: (b, 0, 0))
    rope = pl.BlockSpec((chunk, RET_DK), lambda b, i: (i, 0))
    y_shape = jax.ShapeDtypeStruct((m, BRANCH_W), BF16)
    return pl.pallas_call(
        _local_mixers_kernel,
        grid=(bsz, nc),
        in_specs=[
            col(COL_CB), col(COL_CC), col(COL_CH), col(COL_RQ), col(COL_RK), col(COL_RV), col(COL_RG), col(COL_MQ),
            pl.BlockSpec((None, CONV_WIDTH, BRANCH_W), lambda b, i: (layer, 0, 0)),
            conv_state, rope, rope, state,
            pl.BlockSpec((None, 1, BRANCH_W), lambda b, i: (layer, 0, 0)),
            pl.BlockSpec((None, tokens, BRANCH_W), lambda b, i: (b, 0, 0)),
            pl.BlockSpec((None, tokens, BRANCH_W), lambda b, i: (b, 0, 1)),
        ],
        out_specs=[out, out, out, conv_state, state],
        out_shape=[
            y_shape, y_shape, y_shape,
            jax.ShapeDtypeStruct((bsz, CONV_WIDTH - 1, BRANCH_W), F32),
            jax.ShapeDtypeStruct((bsz, RET_HEADS, RET_DK, RET_DK), F32),
        ],
        scratch_shapes=[pltpu.VMEM((CONV_WIDTH - 1, BRANCH_W), F32),
                        pltpu.VMEM((RET_HEADS, RET_DK, RET_DK), F32),
                        pltpu.VMEM((RET_HEADS, chunk, chunk), F32)],
        compiler_params=_params(("arbitrary", "arbitrary"), 40),
        name="local_mixers",
    )(proj, proj, proj, proj, proj, proj, proj, proj, conv_w, conv_prev, cosf, sinf, s0, gn_g, mkv, mkv)


def _diff_lambda(lam_ref, lam_init):
    e0 = jnp.exp(jnp.sum(lam_ref[0:1, :] * lam_ref[1:2, :], -1, keepdims=True))
    e1 = jnp.exp(jnp.sum(lam_ref[2:3, :] * lam_ref[3:4, :], -1, keepdims=True))
    return e0 - e1 + lam_init


def _diff_prompt_kernel(q_ref, k_ref, v_ref, bias_ref, lam_ref, g_ref, o_ref, k16_ref, vt_ref,
                        sa_ref, sb_ref, m0_ref, m1_ref, l0_ref, l1_ref, acc0_ref, acc1_ref, *, lam_init):
    i = pl.program_id(2)
    t = q_ref.shape[0]
    nt = k_ref.shape[0] // t
    m_refs, l_refs, acc_refs = (m0_ref, m1_ref), (l0_ref, l1_ref), (acc0_ref, acc1_ref)

    @pl.when(i == 0)
    def _():
        k16_ref[...] = k_ref[...].astype(BF16)
        for j in range(nt):
            vt_ref[j] = jnp.transpose(v_ref[j * t:(j + 1) * t, :]).astype(BF16)

    qt = jnp.transpose(q_ref[...] * (DIFF_D ** -0.5 * LOG2E))
    feat = lax.broadcasted_iota(jnp.int32, qt.shape, 0)
    qts = (jnp.where(feat < DIFF_D, qt, 0.0).astype(BF16), jnp.where(feat >= DIFF_D, qt, 0.0).astype(BF16))
    for c in range(2):
        m_refs[c][...] = jnp.full(m_refs[c].shape, NEG_INF, F32)
        l_refs[c][...] = jnp.zeros(l_refs[c].shape, F32)
        acc_refs[c][...] = jnp.zeros(acc_refs[c].shape, F32)

    def scores(j, dst_ref):
        jc = jnp.minimum(j, i)
        kt = k16_ref[pl.ds(pl.multiple_of(jc * t, t), t), :]
        bias = bias_ref[jnp.where(jc == i, 0, jnp.where(jc == i - 1, 1, 2))]
        for c in range(2):
            dst_ref[c] = _dot(kt, qts[c]) + bias

    def consume(j, src_ref):
        vt = vt_ref[j]
        s = [src_ref[c] for c in range(2)]
        m_prev = [m_refs[c][...] for c in range(2)]
        m_new = [jnp.maximum(m_prev[c], jnp.max(s[c], 0, keepdims=True)) for c in range(2)]
        p = [jnp.exp2(s[c] - m_new[c]) for c in range(2)]
        a = [jnp.exp2(m_prev[c] - m_new[c]) for c in range(2)]
        for c in range(2):
            l_refs[c][...] = a[c] * l_refs[c][...] + jnp.sum(p[c], 0, keepdims=True)
            acc_refs[c][...] = a[c] * acc_refs[c][...] + _dot(vt, p[c].astype(BF16))
            m_refs[c][...] = m_new[c]

    scores(0, sa_ref)

    def pair(p, carry):
        j = 2 * p
        scores(j + 1, sb_ref)
        consume(j, sa_ref)
        scores(j + 2, sa_ref)
        consume(j + 1, sb_ref)
        return carry

    n_tiles = i + 1
    lax.fori_loop(0, n_tiles // 2, pair, 0)

    @pl.when(n_tiles % 2 == 1)
    def _():
        consume(i, sa_ref)

    lam = _diff_lambda(lam_ref, lam_init)
    o = acc0_ref[...] / l0_ref[...] - lam * (acc1_ref[...] / l1_ref[...])
    o = o * lax.rsqrt(jnp.mean(o * o, 0, keepdims=True) + LN_EPS) * g_ref[...]
    o_ref[...] = jnp.transpose(o * (1.0 - lam_init)).astype(BF16)


def _diff_prompt(proj, bias_tiles, diff_lambda, subln_g, layer, *, bsz, length, t, lam_init):
    m = proj.shape[0]
    nq = length // t
    assert length % t == 0 and t % CHUNK == 0
    kv = lambda c: pl.BlockSpec((length, HEAD_W), lambda b, h, i: (b, c * HEADS_PER_BRANCH + h))
    stat = pltpu.VMEM((1, t), F32)
    acc = pltpu.VMEM((DIFF_DV, t), F32)
    return pl.pallas_call(
        functools.partial(_diff_prompt_kernel, lam_init=lam_init),
        grid=(bsz, DIFF_HEADS, nq),
        in_specs=[
            pl.BlockSpec((t, HEAD_W), lambda b, h, i: (b * nq + i, COL_DQ * HEADS_PER_BRANCH + h)),
            kv(COL_DK), kv(COL_DV),
            pl.BlockSpec((None, 3, t, t), lambda b, h, i: (h, 0, 0, 0)),
            pl.BlockSpec((None, 4, DIFF_D), lambda b, h, i: (layer, 0, 0)),
            pl.BlockSpec((None, DIFF_DV, 1), lambda b, h, i: (layer, 0, 0)),
        ],
        out_specs=pl.BlockSpec((t, HEAD_W), lambda b, h, i: (b * nq + i, h)),
        out_shape=jax.ShapeDtypeStruct((m, BRANCH_W), BF16),
        scratch_shapes=[pltpu.VMEM((length, HEAD_W), BF16), pltpu.VMEM((nq, DIFF_DV, t), BF16),
                        pltpu.VMEM((2, t, t), F32), pltpu.VMEM((2, t, t), F32),
                        stat, stat, stat, stat, acc, acc],
        compiler_params=_params(("parallel", "parallel", "arbitrary"), 40),
        name="diff_attention_prompt",
    )(proj, proj, proj, bias_tiles, diff_lambda, subln_g)


def _head_rows(ref, h, n):
    return ref[pl.ds(h, n, stride=HEADS_PER_BRANCH), :]


def _sample_attn_kernel(q_ref, kn_ref, vn_ref, mq_ref, kp_ref, vp_ref, mk_ref, mv_ref, bp_ref, bn_ref,
                        lam_ref, g_ref, yc_ref, yd_ref, *, lam_init):
    past = kp_ref.shape[0] // HEADS_PER_BRANCH
    tokens = mk_ref.shape[0] // HEADS_PER_BRANCH
    lam = _diff_lambda(lam_ref, lam_init)
    for h in range(HEADS_PER_BRANCH):
        cols = slice(h * HEAD_W, (h + 1) * HEAD_W)
        q = q_ref[:, cols] * (DIFF_D ** -0.5 * LOG2E)
        lane = lax.broadcasted_iota(jnp.int32, q.shape, 1)
        qs = (jnp.where(lane < DIFF_D, q, 0.0).astype(BF16), jnp.where(lane >= DIFF_D, q, 0.0).astype(BF16))
        kp = _head_rows(kp_ref, h, past).astype(BF16)
        vp = _head_rows(vp_ref, h, past).astype(BF16)
        kn = kn_ref[:, cols].astype(BF16)
        vn = vn_ref[:, cols].astype(BF16)
        outs = []
        for qc in qs:
            sp = _dot_nt(qc, kp) + bp_ref[h]
            sn = _dot_nt(qc, kn) + bn_ref[h]
            mx = jnp.maximum(jnp.max(sp, -1, keepdims=True), jnp.max(sn, -1, keepdims=True))
            pp = jnp.exp2(sp - mx)
            pn = jnp.exp2(sn - mx)
            den = jnp.sum(pp, -1, keepdims=True) + jnp.sum(pn, -1, keepdims=True)
            outs.append((_dot(pp.astype(BF16), vp) + _dot(pn.astype(BF16), vn)) / den)
        o = outs[0] - lam * outs[1]
        o = o * lax.rsqrt(jnp.mean(o * o, -1, keepdims=True) + LN_EPS) * g_ref[...]
        yc_ref[:, cols] = (o * (1.0 - lam_init)).astype(BF16)

        mk = _head_rows(mk_ref, h, tokens).astype(BF16)
        mv = _head_rows(mv_ref, h, tokens).astype(BF16)
        s = _dot_nt(mq_ref[:, cols].astype(BF16), mk) * (MEM_HD ** -0.5)
        p = jnp.exp(s - jnp.max(s, -1, keepdims=True))
        den = jnp.sum(p, -1, keepdims=True)
        yd_ref[:, cols] = (_dot(p.astype(BF16), mv) / den).astype(BF16)


def _sample_attention(proj, k_past, v_past, mem_k, mem_v, bias_past, bias_new, diff_lambda, subln_g, layer,
                      *, bsz, length, lam_init):
    m = proj.shape[0]
    col = lambda c: pl.BlockSpec((length, BRANCH_W), lambda b: (b, c))
    cache = lambda a: pl.BlockSpec((None, None) + a.shape[2:], lambda b: (layer, b, 0, 0))
    whole = lambda a: pl.BlockSpec(a.shape, lambda b: (0,) * a.ndim)
    out = pl.BlockSpec((length, BRANCH_W), lambda b: (b, 0))
    return pl.pallas_call(
        functools.partial(_sample_attn_kernel, lam_init=lam_init),
        grid=(bsz,),
        in_specs=[
            col(COL_DQ), col(COL_DK), col(COL_DV), col(COL_MQ),
            cache(k_past), cache(v_past), cache(mem_k), cache(mem_v),
            whole(bias_past), whole(bias_new),
            pl.BlockSpec((None, 4, DIFF_D), lambda b: (layer, 0, 0)),
            pl.BlockSpec((None, 1, DIFF_DV), lambda b: (layer, 0, 0)),
        ],
        out_specs=[out, out],
        out_shape=[jax.ShapeDtypeStruct((m, BRANCH_W), BF16), jax.ShapeDtypeStruct((m, BRANCH_W), BF16)],
        compiler_params=_params(("parallel",), 40),
        name="sample_attention",
    )(proj, proj, proj, proj, k_past, v_past, mem_k, mem_v, bias_past, bias_new, diff_lambda, subln_g)


def _merge_kernel(h_ref, ya_ref, yb_ref, yc_ref, yd_ref, wg_ref, wb_ref, bg_ref, o_ref):
    h = h_ref[...]
    acc = None
    for i, y_ref in enumerate((ya_ref, yb_ref, yc_ref, yd_ref)):
        gate = jax.nn.sigmoid(_dot(h, wg_ref[i]) + bg_ref[i])
        term = gate * _dot(y_ref[...], wb_ref[i])
        acc = term if acc is None else acc + term
    o_ref[...] = acc.astype(BF16)


def _merge(h16, ys, w_gate, w_branch, b_gate, layer, *, tm, tn):
    m, d = h16.shape
    assert m % tm == 0 and d % tn == 0
    y_spec = pl.BlockSpec((tm, BRANCH_W), lambda i, j: (i, 0))
    return pl.pallas_call(
        _merge_kernel,
        grid=(m // tm, d // tn),
        in_specs=[
            pl.BlockSpec((tm, d), lambda i, j: (i, 0)), y_spec, y_spec, y_spec, y_spec,
            pl.BlockSpec((None, N_BRANCH, d, tn), lambda i, j: (0, 0, 0, j)),
            pl.BlockSpec((None, N_BRANCH, BRANCH_W, tn), lambda i, j: (0, 0, 0, j)),
            pl.BlockSpec((None, N_BRANCH, 1, tn), lambda i, j: (layer, 0, 0, j)),
        ],
        out_specs=pl.BlockSpec((tm, tn), lambda i, j: (i, j)),
        out_shape=jax.ShapeDtypeStruct((m, d), BF16),
        compiler_params=_params(("parallel", "parallel"), 48),
        name="gated_merge",
    )(h16, *ys, w_gate, w_branch, b_gate)


def _proj_ln_kernel(m_ref, x_ref, w_ref, g_ref, b_ref, o_ref, *, alpha):
    for r in range(0, o_ref.shape[0], PROJ_ROW_CHUNK):
        rows = slice(r, r + PROJ_ROW_CHUNK)
        y = alpha * x_ref[rows, :] + _dot(m_ref[rows, :], w_ref[...])
        o_ref[rows, :] = _layer_norm(y, g_ref[...], b_ref[...])


def _proj_ln(merged, x, w_o, g, b, layer, *, alpha, tm):
    m, d = x.shape
    assert m % tm == 0
    row = pl.BlockSpec((tm, d), lambda i: (i, 0))
    vec = pl.BlockSpec((None, 1, d), lambda i: (layer, 0, 0))
    return pl.pallas_call(
        functools.partial(_proj_ln_kernel, alpha=alpha),
        grid=(m // tm,),
        in_specs=[row, row, pl.BlockSpec((None, d, d), lambda i: (0, 0, 0)), vec, vec],
        out_specs=row,
        out_shape=jax.ShapeDtypeStruct((m, d), F32),
        compiler_params=_params(("parallel",), 48),
        name="proj_ln",
    )(merged, x, w_o, g, b)


def _rope_tables(pos):
    half = RET_DK // 2
    inv = ROPE_BASE ** (-jnp.arange(half, dtype=F32) / half)
    ang = pos.astype(F32)[:, None] * inv[None, :]
    cos = jnp.cos(ang)
    sin = jnp.sin(ang)
    return jnp.concatenate([cos, cos], -1), jnp.concatenate([-sin, sin], -1)


def _t5_bucket(rel):
    nb = REL_BUCKETS // 2
    max_exact = nb // 2
    n = jnp.abs(rel)
    nf = jnp.maximum(n, 1).astype(F32)
    large = max_exact + (jnp.log(nf / max_exact) / math.log(REL_MAX_DIST / max_exact)
                         * (nb - max_exact)).astype(jnp.int32)
    large = jnp.minimum(large, nb - 1)
    return jnp.where(rel > 0, nb, 0) + jnp.where(n < max_exact, n, large)


def _masked_bias(q_pos, k_pos, rel_bias):
    bucket = _t5_bucket(k_pos[None, :] - q_pos[:, None])
    onehot = bucket[None, None] == jnp.arange(REL_BUCKETS, dtype=bucket.dtype)[None, :, None, None]
    bias = jnp.sum(jnp.where(onehot, rel_bias.astype(F32).T[:, :, None, None], 0.0), axis=1)
    allowed = (k_pos[None, :] // CHUNK) <= (q_pos[:, None] // CHUNK)
    return jnp.where(allowed[None], bias * LOG2E, NEG_INF)


def _prompt_bias_tiles(rel_bias, t):
    base = BIAS_BASE
    assert base % CHUNK == 0 and base + 1 >= REL_MAX_DIST and t % base == 0
    n = t // base
    pos = jnp.arange(base, dtype=jnp.int32)
    diag, sub, far = (jnp.swapaxes(_masked_bias(pos + d * base, pos, rel_bias), -1, -2) for d in range(3))
    masked = jnp.full_like(diag, NEG_INF)
    pick = lambda delta: masked if delta > 0 else (diag, sub, far)[min(-delta, 2)]
    tile = lambda d: jnp.concatenate(
        [jnp.concatenate([pick(kk - qq - d * n) for qq in range(n)], axis=-1) for kk in range(n)], axis=-2)
    return jnp.stack([tile(d) for d in range(3)], axis=1)


def _kv_export_kernel(k_ref, v_ref, *rest):
    ok_ref, ov_ref = rest[-2:]
    n = k_ref.shape[0]
    for h in range(HEADS_PER_BRANCH):
        cols = slice(h * HEAD_W, (h + 1) * HEAD_W)
        rows = pl.ds(h, n, stride=HEADS_PER_BRANCH)
        ok_ref[rows, :] = k_ref[:, cols]
        ov_ref[rows, :] = v_ref[:, cols]


def _kv_export(proj, prev, layer, depth, *, tm):
    m = proj.shape[0]
    assert m % tm == 0
    col = lambda c: pl.BlockSpec((tm, BRANCH_W), lambda i: (i, c))
    out = pl.BlockSpec((None, tm * HEADS_PER_BRANCH, HEAD_W), lambda i: (layer, i, 0))
    shape = jax.ShapeDtypeStruct((depth, m * HEADS_PER_BRANCH, HEAD_W), F32)
    keep = [pl.BlockSpec(memory_space=pl.ANY)] * len(prev)
    return pl.pallas_call(
        _kv_export_kernel,
        grid=(m // tm,),
        in_specs=[col(COL_DK), col(COL_DV)] + keep,
        out_specs=[out, out],
        out_shape=[shape, shape],
        input_output_aliases={2 + n: n for n in range(len(prev))},
        compiler_params=_params(("parallel",), 32),
        name="kv_export",
    )(proj, proj, *prev)


CAST_IN_FFN1 = (('w_in', False), ('ffn2_w_up', False), ('ffn2_w_down', True))
CAST_IN_PROJ = (('w_gate', False), ('w_branch', False), ('w_o', False))
CAST_IN_FFN2 = (('ffn1_w_up', False), ('ffn1_w_down', True))


def _encoder_layer(x, w16, p, layer, *, alpha, cfg, mixers_fn, w32=None, w16_next=None):
    d = x.shape[1]
    casts = lambda names, l: tuple((w32[n], l, swap) for n, swap in names) if w32 is not None else ()
    x1, done = _ffn_ln(x, w16['ffn1_w_up'], w16['ffn1_w_down'], p['ln1_g'], p['ln1_b'], layer,
                       alpha=alpha, tm=cfg['tm_ffn'], tf=cfg['tf'], casts=casts(CAST_IN_FFN1, layer))
    w16.update(zip((n for n, _ in CAST_IN_FFN1), done))
    proj, x1b, done = _in_proj(x1, w16['w_in'], tm=cfg['tm_in'], tn=cfg['tn_in'], casts=casts(CAST_IN_PROJ, layer))
    w16.update(zip((n for n, _ in CAST_IN_PROJ), done))
    y_a, y_b, y_c, y_d, conv_new, ret_new = mixers_fn(proj, 0.8 - 0.6 * math.exp(-0.3 * layer))
    merged = _merge(x1b, (y_a, y_b, y_c, y_d), w16['w_gate'].reshape(1, N_BRANCH, d, d),
                    w16['w_branch'].reshape(1, N_BRANCH, BRANCH_W, d), p['b_gate'], layer,
                    tm=cfg['tm_merge'], tn=cfg['tn_merge'])
    x2 = _proj_ln(merged, x1, w16['w_o'], p['ln2_g'], p['ln2_b'], layer, alpha=alpha, tm=cfg['tm_proj'])
    x3, done = _ffn_ln(x2, w16['ffn2_w_up'], w16['ffn2_w_down'], p['ln3_g'], p['ln3_b'], layer,
                       alpha=alpha, tm=cfg['tm_ffn'], tf=cfg['tf'],
                       casts=casts(CAST_IN_FFN2, layer + 1) if w16_next is not None else ())
    if w16_next is not None:
        w16_next.update(zip((n for n, _ in CAST_IN_FFN2), done))
    return x3, proj, conv_new, ret_new


PROMPT_CFG = dict(tm_ffn=1024, tf=512, tm_in=1024, tn_in=1408, t_conv=512, ret_chunk=256, t_diff=512,
                  tq_mem=1024, tm_merge=1024, tn_merge=256, tm_proj=512, tm_export=1024)
SAMPLE_CFG = dict(tm_ffn=256, tf=512, tm_in=256, tn_in=512, t_conv=16, ret_chunk=16,
                  tm_merge=256, tn_merge=256, tm_proj=256, tm_export=256)


def kernel(x_prompt, x_sample, state_conv, state_ret, cache_diff_k, cache_diff_v, cache_mem_k, cache_mem_v,
           mem_prompt, ffn1_w_up, ffn1_w_down, ln1_g, ln1_b, w_in, conv_w, ret_gn_g, diff_lambda,
           diff_subln_g, w_mem_kv, w_branch, w_gate, b_gate, w_o, ln2_g, ln2_b, ffn2_w_up, ffn2_w_down,
           ln3_g, ln3_b, rel_bias):
    bp, lp, d = x_prompt.shape
    bs, ls, _ = x_sample.shape
    depth = w_in.shape[0]
    past = cache_diff_k.shape[2]
    mem_tokens = mem_prompt.shape[1]
    alpha = (2 * depth) ** 0.25
    half = MEM_HEADS * MEM_HD

    pos_p = jnp.arange(lp, dtype=jnp.int32)
    pos_s = past + jnp.arange(ls, dtype=jnp.int32)
    past_pos = jnp.arange(past, dtype=jnp.int32)
    rope_p = _rope_tables(pos_p)
    rope_s = _rope_tables(pos_s)
    bias_p = _prompt_bias_tiles(rel_bias, PROMPT_CFG['t_diff'])
    bias_s_past = _masked_bias(pos_s, past_pos, rel_bias)
    bias_s_new = _masked_bias(pos_s, pos_s, rel_bias)

    vec = lambda v: v.reshape(depth, 1, v.shape[-1])
    p = {'ln1_g': vec(ln1_g), 'ln1_b': vec(ln1_b), 'ln2_g': vec(ln2_g), 'ln2_b': vec(ln2_b),
         'ln3_g': vec(ln3_g), 'ln3_b': vec(ln3_b), 'b_gate': b_gate.reshape(depth, N_BRANCH, 1, d)}
    gn_g = vec(ret_gn_g)
    w32 = {'ffn1_w_up': ffn1_w_up, 'ffn1_w_down': ffn1_w_down, 'w_in': w_in,
           'w_gate': w_gate.reshape(depth, N_BRANCH * d, d), 'w_branch': w_branch.reshape(depth, N_BRANCH * BRANCH_W, d),
           'w_o': w_o, 'ffn2_w_up': ffn2_w_up, 'ffn2_w_down': ffn2_w_down}
    w16 = [dict() for _ in range(depth)]
    w16[0].update({n: w32[n][:1].astype(BF16) for n, _ in CAST_IN_FFN2})
    w_mem16 = w_mem_kv.astype(BF16)
    subln_col = diff_subln_g.reshape(depth, DIFF_DV, 1)
    subln_row = diff_subln_g.reshape(depth, 1, DIFF_DV)
    head_major = lambda c: c.reshape(depth, bs, c.shape[2] * c.shape[3], c.shape[4])
    kpast, vpast = head_major(cache_diff_k), head_major(cache_diff_v)
    mem_k_s, mem_v_s = head_major(cache_mem_k), head_major(cache_mem_v)

    yp = x_prompt.reshape(bp * lp, d)
    ys = x_sample.reshape(bs * ls, d)
    mem16 = mem_prompt.reshape(bp * mem_tokens, d).astype(BF16)
    zero_conv = jnp.zeros((bp, CONV_WIDTH - 1, BRANCH_W), F32)
    zero_ret = jnp.zeros((bp, RET_HEADS, RET_DK, RET_DK), F32)

    conv_p, ret_p, mk_p, mv_p, conv_s, ret_s = [], [], [], [], [], []
    kv_p, kv_s = [], []
    for l in range(depth):
        mkv = _matmul(mem16, w_mem16, l, tm=bp * mem_tokens, tn=512).reshape(bp, mem_tokens, 2 * half)

        def mixers_p(proj, lam_init):
            y_a, y_b, y_d, c_new, r_new = _local_mixers(
                proj, conv_w, zero_conv, rope_p[0], rope_p[1], zero_ret, gn_g, mkv, l,
                bsz=bp, length=lp, chunk=PROMPT_CFG['ret_chunk'])
            y_c = _diff_prompt(proj, bias_p, diff_lambda, subln_col, l, bsz=bp, length=lp,
                               t=PROMPT_CFG['t_diff'], lam_init=lam_init)
            return y_a, y_b, y_c, y_d, c_new, r_new

        yp, proj, c_new, r_new = _encoder_layer(
            yp, w16[l], p, l, alpha=alpha, cfg=PROMPT_CFG, mixers_fn=mixers_p,
            w32=w32, w16_next=w16[l + 1] if l + 1 < depth else None)
        kv_p = _kv_export(proj, kv_p, l, depth, tm=PROMPT_CFG['tm_export'])
        conv_p.append(c_new)
        ret_p.append(r_new)
        mk_p.append(mkv[:, :, :half].reshape(bp, mem_tokens, MEM_HEADS, MEM_HD))
        mv_p.append(mkv[:, :, half:].reshape(bp, mem_tokens, MEM_HEADS, MEM_HD))

        def mixers_s(proj, lam_init):
            y_a, c_new = _conv_branch(proj, conv_w, state_conv[l], l, bsz=bs, length=ls, tl=SAMPLE_CFG['t_conv'])
            y_b, r_new = _retention_branch(proj, rope_s[0], rope_s[1], state_ret[l], gn_g, l,
                                           bsz=bs, length=ls, chunk=SAMPLE_CFG['ret_chunk'])
            y_c, y_d = _sample_attention(proj, kpast, vpast, mem_k_s, mem_v_s, bias_s_past, bias_s_new,
                                         diff_lambda, subln_row, l, bsz=bs, length=ls, lam_init=lam_init)
            return y_a, y_b, y_c, y_d, c_new, r_new

        ys, proj, c_new, r_new = _encoder_layer(ys, w16[l], p, l, alpha=alpha, cfg=SAMPLE_CFG, mixers_fn=mixers_s)
        kv_s = _kv_export(proj, kv_s, l, depth, tm=SAMPLE_CFG['tm_export'])
        conv_s.append(c_new)
        ret_s.append(r_new)

    kv_shape = lambda b, n: (depth, b, n, DIFF_HEADS, DIFF_DV)
    return (yp.reshape(bp, lp, d), ys.reshape(bs, ls, d), jnp.stack(conv_p), jnp.stack(ret_p),
            kv_p[0].reshape(kv_shape(bp, lp)), kv_p[1].reshape(kv_shape(bp, lp)),
            jnp.stack(mk_p), jnp.stack(mv_p), jnp.stack(conv_s), jnp.stack(ret_s),
            kv_s[0].reshape(kv_shape(bs, ls)), kv_s[1].reshape(kv_shape(bs, ls)))
```

```python
import functools
import math

import jax
import jax.numpy as jnp
from jax import lax
from jax.experimental import pallas as pl
from jax.experimental.pallas import tpu as pltpu

F32 = jnp.float32
BF16 = jnp.bfloat16

CHUNK = 64
CONV_WIDTH = 3
RET_HEADS = 4
RET_DK = 128
DIFF_HEADS = 4
DIFF_D = 64
DIFF_DV = 128
MEM_HEADS = 4
MEM_HD = 128
BRANCH_W = 512
N_BRANCH = 4
REL_BUCKETS = 32
REL_MAX_DIST = 128
LN_EPS = 1e-5
ROPE_BASE = 10000.0
NEG_INF = -1e30
HEAD_W = 128
LOG2E = math.log2(math.e)
LN_ROW_CHUNK = 128
PROJ_ROW_CHUNK = 256
BIAS_BASE = 256
FFN_UP_CHUNK = 256
FFN_DOWN_CHUNK = 512

COL_CB, COL_CC, COL_CH, COL_RQ, COL_RK, COL_RV, COL_RG, COL_DQ, COL_DK, COL_DV, COL_MQ = range(11)
HEADS_PER_BRANCH = BRANCH_W // HEAD_W

LANES = 128
BF16_SUBLANES = 16
V7X_VMEM_BYTES = 64 * 1024 * 1024
MIB = 1024 * 1024


def _params(semantics, vmem_mib):
    assert vmem_mib * MIB < V7X_VMEM_BYTES
    return pltpu.CompilerParams(dimension_semantics=semantics, vmem_limit_bytes=vmem_mib * MIB)


def _layer_norm(y, g, b):
    mu = jnp.mean(y, -1, keepdims=True)
    d = y - mu
    var = jnp.mean(d * d, -1, keepdims=True)
    return d * lax.rsqrt(var + LN_EPS) * g + b


def _dot(a, b):
    return jnp.dot(a, b, preferred_element_type=F32)


def _dot_nt(a, b):
    return lax.dot_general(a, b, (((1,), (1,)), ((), ())), preferred_element_type=F32)


def _dot_tn(a, b):
    return lax.dot_general(a, b, (((0,), (0,)), ((), ())), preferred_element_type=F32)


def _side_cast_specs(casts, grid):
    in_specs, out_specs, out_shapes = [], [], []
    for src, layer, swap in casts:
        _, r, c = src.shape
        gr, gc = (grid[1], grid[0]) if swap else grid
        assert r % gr == 0 and c % gc == 0
        blk = (None, r // gr, c // gc)
        assert blk[1] % BF16_SUBLANES == 0 and blk[2] % LANES == 0
        pick = (lambda i, j: (j, i)) if swap else (lambda i, j: (i, j))
        in_specs.append(pl.BlockSpec(blk, lambda i, j, layer=layer, pick=pick: (layer,) + pick(i, j)))
        out_specs.append(pl.BlockSpec(blk, lambda i, j, pick=pick: (0,) + pick(i, j)))
        out_shapes.append(jax.ShapeDtypeStruct((1, r, c), BF16))
    return in_specs, out_specs, out_shapes


def _side_cast(src_refs, dst_refs):
    for src_ref, dst_ref in zip(src_refs, dst_refs):
        dst_ref[...] = src_ref[...].astype(BF16)


def _ffn_ln_kernel(x_ref, wa_ref, wb_ref, wd_ref, g_ref, b_ref, *rest, alpha, n_cast):
    cast_src, o_ref, cast_dst = rest[:n_cast], rest[n_cast], rest[n_cast + 1:2 * n_cast + 1]
    xb_ref, h_ref = rest[2 * n_cast + 1:]
    j = pl.program_id(1)

    @pl.when(j == 0)
    def _():
        xb_ref[...] = x_ref[...].astype(BF16)
        o_ref[...] = jnp.zeros(o_ref.shape, F32)

    xb = xb_ref[...]
    for c in range(0, h_ref.shape[1], FFN_UP_CHUNK):
        cols = slice(c, c + FFN_UP_CHUNK)
        a = _dot(xb, wa_ref[:, cols])
        b = _dot(xb, wb_ref[:, cols])
        h_ref[:, cols] = (a * jax.nn.sigmoid(a) * b).astype(BF16)
    h = h_ref[...]
    for c in range(0, o_ref.shape[1], FFN_DOWN_CHUNK):
        cols = slice(c, c + FFN_DOWN_CHUNK)
        o_ref[:, cols] += _dot(h, wd_ref[:, cols])
    _side_cast(cast_src, cast_dst)

    @pl.when(j == pl.num_programs(1) - 1)
    def _():
        for r in range(0, o_ref.shape[0], LN_ROW_CHUNK):
            rows = slice(r, r + LN_ROW_CHUNK)
            o_ref[rows, :] = _layer_norm(alpha * x_ref[rows, :] + 0.5 * o_ref[rows, :], g_ref[...], b_ref[...])


def _ffn_ln(x, w_up, w_down, g, b, layer, *, alpha, tm, tf, casts=()):
    m, d = x.shape
    d_ff = w_down.shape[1]
    nj = d_ff // tf
    assert m % tm == 0 and d_ff % tf == 0
    grid = (m // tm, nj)
    row = lambda i, j: (i, 0)
    vec = pl.BlockSpec((None, 1, d), lambda i, j: (layer, 0, 0))
    cast_in, cast_out, cast_shapes = _side_cast_specs(casts, grid)
    out = pl.pallas_call(
        functools.partial(_ffn_ln_kernel, alpha=alpha, n_cast=len(casts)),
        grid=grid,
        in_specs=[
            pl.BlockSpec((tm, d), row),
            pl.BlockSpec((None, d, tf), lambda i, j: (0, 0, j)),
            pl.BlockSpec((None, d, tf), lambda i, j: (0, 0, j + nj)),
            pl.BlockSpec((None, tf, d), lambda i, j: (0, j, 0)),
            vec, vec,
        ] + cast_in,
        out_specs=[pl.BlockSpec((tm, d), row)] + cast_out,
        out_shape=[jax.ShapeDtypeStruct((m, d), F32)] + cast_shapes,
        scratch_shapes=[pltpu.VMEM((tm, d), BF16), pltpu.VMEM((tm, tf), BF16)],
        compiler_params=_params(("parallel", "arbitrary"), 60),
        name="ffn_ln",
    )(x, w_up, w_up, w_down, g, b, *(c[0] for c in casts))
    return out[0], out[1:]


def _in_proj_kernel(x_ref, w_ref, *rest, n_cast):
    cast_src, o_ref, xb_ref, cast_dst = rest[:n_cast], rest[n_cast], rest[n_cast + 1], rest[n_cast + 2:]

    @pl.when(pl.program_id(1) == 0)
    def _():
        xb_ref[...] = x_ref[...].astype(BF16)

    o_ref[...] = _dot(xb_ref[...], w_ref[...])
    _side_cast(cast_src, cast_dst)


def _in_proj(x, w, *, tm, tn, casts=()):
    m, k = x.shape
    n = w.shape[2]
    assert m % tm == 0 and n % tn == 0
    grid = (m // tm, n // tn)
    cast_in, cast_out, cast_shapes = _side_cast_specs(casts, grid)
    out = pl.pallas_call(
        functools.partial(_in_proj_kernel, n_cast=len(casts)),
        grid=grid,
        in_specs=[pl.BlockSpec((tm, k), lambda i, j: (i, 0)),
                  pl.BlockSpec((None, k, tn), lambda i, j: (0, 0, j))] + cast_in,
        out_specs=[pl.BlockSpec((tm, tn), lambda i, j: (i, j)), pl.BlockSpec((tm, k), lambda i, j: (i, 0))] + cast_out,
        out_shape=[jax.ShapeDtypeStruct((m, n), F32), jax.ShapeDtypeStruct((m, k), BF16)] + cast_shapes,
        compiler_params=_params(("parallel", "arbitrary"), 60),
        name="in_proj",
    )(x, w, *(c[0] for c in casts))
    return out[0], out[1], out[2:]


def _matmul_kernel(x_ref, w_ref, o_ref):
    o_ref[...] = _dot(x_ref[...], w_ref[...])


def _matmul(x, w, layer, *, tm, tn):
    m, k = x.shape
    n = w.shape[2]
    assert m % tm == 0 and n % tn == 0
    return pl.pallas_call(
        _matmul_kernel,
        grid=(m // tm, n // tn),
        in_specs=[pl.BlockSpec((tm, k), lambda i, j: (i, 0)),
                  pl.BlockSpec((None, k, tn), lambda i, j: (layer, 0, j))],
        out_specs=pl.BlockSpec((tm, tn), lambda i, j: (i, j)),
        out_shape=jax.ShapeDtypeStruct((m, n), F32),
        compiler_params=_params(("parallel", "parallel"), 48),
        name="matmul",
    )(x, w)


def _conv_body(cb_ref, cc_ref, ch_ref, w_ref, carry_ref, y_ref):
    u = cc_ref[...] * ch_ref[...]
    row = lax.broadcasted_iota(jnp.int32, u.shape, 0)
    c0 = carry_ref[0:1, :]
    c1 = carry_ref[1:2, :]
    u1 = jnp.where(row == 0, c1, pltpu.roll(u, 1, 0))
    u2 = jnp.where(row == 0, c0, jnp.where(row == 1, c1, pltpu.roll(u, 2, 0)))
    z = w_ref[0:1, :] * u2 + w_ref[1:2, :] * u1 + w_ref[2:3, :] * u
    y_ref[...] = (cb_ref[...] * z).astype(BF16)
    tl = u.shape[0]
    tail = u[tl - 8:, :][6:8, :]
    carry_ref[...] = tail
    return tail


def _conv_kernel(cb_ref, cc_ref, ch_ref, w_ref, prev_ref, y_ref, state_ref, carry_ref):
    l = pl.program_id(1)

    @pl.when(l == 0)
    def _():
        carry_ref[...] = prev_ref[...]

    tail = _conv_body(cb_ref, cc_ref, ch_ref, w_ref, carry_ref, y_ref)

    @pl.when(l == pl.num_programs(1) - 1)
    def _():
        state_ref[...] = tail


def _conv_branch(proj, conv_w, prev, layer, *, bsz, length, tl):
    m = proj.shape[0]
    nl = length // tl
    assert length % tl == 0 and tl >= 8
    col = lambda c: pl.BlockSpec((tl, BRANCH_W), lambda b, l: (b * nl + l, c))
    return pl.pallas_call(
        _conv_kernel,
        grid=(bsz, nl),
        in_specs=[
            col(COL_CB), col(COL_CC), col(COL_CH),
            pl.BlockSpec((None, CONV_WIDTH, BRANCH_W), lambda b, l: (layer, 0, 0)),
            pl.BlockSpec((None, CONV_WIDTH - 1, BRANCH_W), lambda b, l: (b, 0, 0)),
        ],
        out_specs=[
            pl.BlockSpec((tl, BRANCH_W), lambda b, l: (b * nl + l, 0)),
            pl.BlockSpec((None, CONV_WIDTH - 1, BRANCH_W), lambda b, l: (b, 0, 0)),
        ],
        out_shape=[
            jax.ShapeDtypeStruct((m, BRANCH_W), BF16),
            jax.ShapeDtypeStruct((bsz, CONV_WIDTH - 1, BRANCH_W), F32),
        ],
        scratch_shapes=[pltpu.VMEM((CONV_WIDTH - 1, BRANCH_W), F32)],
        compiler_params=_params(("parallel", "arbitrary"), 32),
        name="conv_branch",
    )(proj, proj, proj, conv_w, prev)


def _ret_log_gamma(h):
    return math.log1p(-(2.0 ** (-5.0 - h)))


def _ret_fill_decay(intra_ref):
    chunk = intra_ref.shape[1]
    ri = lax.broadcasted_iota(jnp.int32, (chunk, chunk), 0)
    ci = lax.broadcasted_iota(jnp.int32, (chunk, chunk), 1)
    rel = (ri - ci).astype(F32)
    for h in range(RET_HEADS):
        intra_ref[h] = jnp.where(rel >= 0.0, jnp.exp(jnp.maximum(rel, 0.0) * _ret_log_gamma(h)), 0.0)


def _retention_body(q_ref, k_ref, v_ref, g_ref, cos_ref, sin_ref, gn_ref, y_ref, s_ref, intra_ref):
    chunk = q_ref.shape[0]
    cosf = cos_ref[...]
    sinf = sin_ref[...]
    idx = lax.broadcasted_iota(jnp.int32, (chunk, 1), 0).astype(F32)
    for h in range(RET_HEADS):
        log_g = _ret_log_gamma(h)
        cols = slice(h * RET_DK, (h + 1) * RET_DK)
        q = q_ref[:, cols]
        k = k_ref[:, cols]
        q = q * cosf + pltpu.roll(q, RET_DK // 2, 1) * sinf
        k = (k * cosf + pltpu.roll(k, RET_DK // 2, 1) * sinf) * (RET_DK ** -0.5)
        vb = v_ref[:, cols].astype(BF16)
        q_dec = jnp.exp((idx + 1.0) * log_g)
        k_dec = jnp.exp((chunk - 1.0 - idx) * log_g)
        c_dec = math.exp(chunk * log_g)
        qb = q.astype(BF16)
        s_prev = s_ref[h]
        att = _dot_nt(qb, k.astype(BF16)) * intra_ref[h]
        o = _dot(att.astype(BF16), vb) + _dot(qb, s_prev.astype(BF16)) * q_dec
        s_ref[h] = s_prev * c_dec + _dot_tn((k * k_dec).astype(BF16), vb)
        mu = jnp.mean(o, -1, keepdims=True)
        d = o - mu
        var = jnp.mean(d * d, -1, keepdims=True)
        ro = d * lax.rsqrt(var + LN_EPS) * gn_ref[:, cols]
        gate = g_ref[:, cols]
        y_ref[:, cols] = (gate * jax.nn.sigmoid(gate) * ro).astype(BF16)


def _retention_kernel(q_ref, k_ref, v_ref, g_ref, cos_ref, sin_ref, s0_ref, gn_ref, y_ref, sfin_ref,
                      s_ref, intra_ref):
    c = pl.program_id(1)

    @pl.when(jnp.logical_and(pl.program_id(0) == 0, c == 0))
    def _():
        _ret_fill_decay(intra_ref)

    @pl.when(c == 0)
    def _():
        s_ref[...] = s0_ref[...]

    _retention_body(q_ref, k_ref, v_ref, g_ref, cos_ref, sin_ref, gn_ref, y_ref, s_ref, intra_ref)

    @pl.when(c == pl.num_programs(1) - 1)
    def _():
        sfin_ref[...] = s_ref[...]


def _retention_branch(proj, cosf, sinf, s0, gn_g, layer, *, bsz, length, chunk):
    m = proj.shape[0]
    nc = length // chunk
    assert length % chunk == 0
    col = lambda c: pl.BlockSpec((chunk, BRANCH_W), lambda b, i: (b * nc + i, c))
    state = pl.BlockSpec((None, RET_HEADS, RET_DK, RET_DK), lambda b, i: (b, 0, 0, 0))
    return pl.pallas_call(
        _retention_kernel,
        grid=(bsz, nc),
        in_specs=[
            col(COL_RQ), col(COL_RK), col(COL_RV), col(COL_RG),
            pl.BlockSpec((chunk, RET_DK), lambda b, i: (i, 0)),
            pl.BlockSpec((chunk, RET_DK), lambda b, i: (i, 0)),
            state,
            pl.BlockSpec((None, 1, BRANCH_W), lambda b, i: (layer, 0, 0)),
        ],
        out_specs=[pl.BlockSpec((chunk, BRANCH_W), lambda b, i: (b * nc + i, 0)), state],
        out_shape=[
            jax.ShapeDtypeStruct((m, BRANCH_W), BF16),
            jax.ShapeDtypeStruct((bsz, RET_HEADS, RET_DK, RET_DK), F32),
        ],
        scratch_shapes=[pltpu.VMEM((RET_HEADS, RET_DK, RET_DK), F32),
                        pltpu.VMEM((RET_HEADS, chunk, chunk), F32)],
        compiler_params=_params(("arbitrary", "arbitrary"), 32),
        name="retention_branch",
    )(proj, proj, proj, proj, cosf, sinf, s0, gn_g)


def _mem_head(q, mk, mv):
    s = _dot_nt(q.astype(BF16), mk.astype(BF16)) * (MEM_HD ** -0.5)
    p = jnp.exp(s - jnp.max(s, -1, keepdims=True))
    den = jnp.sum(p, -1, keepdims=True)
    return _dot(p.astype(BF16), mv.astype(BF16)) / den


def _local_mixers_kernel(cb_ref, cc_ref, ch_ref, rq_ref, rk_ref, rv_ref, rg_ref, mq_ref, w_ref, prev_ref,
                         cos_ref, sin_ref, s0_ref, gn_ref, mk_ref, mv_ref,
                         ya_ref, yb_ref, yd_ref, conv_state_ref, sfin_ref,
                         carry_ref, s_ref, intra_ref):
    c = pl.program_id(1)

    @pl.when(jnp.logical_and(pl.program_id(0) == 0, c == 0))
    def _():
        _ret_fill_decay(intra_ref)

    @pl.when(c == 0)
    def _():
        carry_ref[...] = prev_ref[...]
        s_ref[...] = s0_ref[...]

    tail = _conv_body(cb_ref, cc_ref, ch_ref, w_ref, carry_ref, ya_ref)
    _retention_body(rq_ref, rk_ref, rv_ref, rg_ref, cos_ref, sin_ref, gn_ref, yb_ref, s_ref, intra_ref)
    for h in range(MEM_HEADS):
        cols = slice(h * MEM_HD, (h + 1) * MEM_HD)
        yd_ref[:, cols] = _mem_head(mq_ref[:, cols], mk_ref[:, cols], mv_ref[:, cols]).astype(BF16)

    @pl.when(c == pl.num_programs(1) - 1)
    def _():
        conv_state_ref[...] = tail
        sfin_ref[...] = s_ref[...]


def _local_mixers(proj, conv_w, conv_prev, cosf, sinf, s0, gn_g, mkv, layer, *, bsz, length, chunk):
    m = proj.shape[0]
    nc = length // chunk
    assert length % chunk == 0 and chunk >= 8
    tokens = mkv.shape[1]
    col = lambda c: pl.BlockSpec((chunk, BRANCH_W), lambda b, i: (b * nc + i, c))
    out = pl.BlockSpec((chunk, BRANCH_W), lambda b, i: (b * nc + i, 0))
    state = pl.BlockSpec((None, RET_HEADS, RET_DK, RET_DK), lambda b, i: (b, 0, 0, 0))
    conv_state = pl.BlockSpec((None, CONV_WIDTH - 1, BRANCH_W), lambda b, i: (b, 0, 0))
    rope = pl.BlockSpec((chunk, RET_DK), lambda b, i: (i, 0))
    y_shape = jax.ShapeDtypeStruct((m, BRANCH_W), BF16)
    return pl.pallas_call(
        _local_mixers_kernel,
        grid=(bsz, nc),
        in_specs=[
            col(COL_CB), col(COL_CC), col(COL_CH), col(COL_RQ), col(COL_RK), col(COL_RV), col(COL_RG), col(COL_MQ),
            pl.BlockSpec((None, CONV_WIDTH, BRANCH_W), lambda b, i: (layer, 0, 0)),
            conv_state, rope, rope, state,
            pl.BlockSpec((None, 1, BRANCH_W), lambda b, i: (layer, 0, 0)),
            pl.BlockSpec((None, tokens, BRANCH_W), lambda b, i: (b, 0, 0)),
            pl.BlockSpec((None, tokens, BRANCH_W), lambda b, i: (b, 0, 1)),
        ],
        out_specs=[out, out, out, conv_state, state],
        out_shape=[
            y_shape, y_shape, y_shape,
            jax.ShapeDtypeStruct((bsz, CONV_WIDTH - 1, BRANCH_W), F32),
            jax.ShapeDtypeStruct((bsz, RET_HEADS, RET_DK, RET_DK), F32),
        ],
        scratch_shapes=[pltpu.VMEM((CONV_WIDTH - 1, BRANCH_W), F32),
                        pltpu.VMEM((RET_HEADS, RET_DK, RET_DK), F32),
                        pltpu.VMEM((RET_HEADS, chunk, chunk), F32)],
        compiler_params=_params(("arbitrary", "arbitrary"), 40),
        name="local_mixers",
    )(proj, proj, proj, proj, proj, proj, proj, proj, conv_w, conv_prev, cosf, sinf, s0, gn_g, mkv, mkv)


def _diff_lambda(lam_ref, lam_init):
    e0 = jnp.exp(jnp.sum(lam_ref[0:1, :] * lam_ref[1:2, :], -1, keepdims=True))
    e1 = jnp.exp(jnp.sum(lam_ref[2:3, :] * lam_ref[3:4, :], -1, keepdims=True))
    return e0 - e1 + lam_init


def _diff_prompt_kernel(q_ref, k_ref, v_ref, bias_ref, lam_ref, g_ref, o_ref, k16_ref, vt_ref,
                        sa_ref, sb_ref, m0_ref, m1_ref, l0_ref, l1_ref, acc0_ref, acc1_ref, *, lam_init):
    i = pl.program_id(2)
    t = q_ref.shape[0]
    nt = k_ref.shape[0] // t
    m_refs, l_refs, acc_refs = (m0_ref, m1_ref), (l0_ref, l1_ref), (acc0_ref, acc1_ref)

    @pl.when(i == 0)
    def _():
        k16_ref[...] = k_ref[...].astype(BF16)
        for j in range(nt):
            vt_ref[j] = jnp.transpose(v_ref[j * t:(j + 1) * t, :]).astype(BF16)

    qt = jnp.transpose(q_ref[...] * (DIFF_D ** -0.5 * LOG2E))
    feat = lax.broadcasted_iota(jnp.int32, qt.shape, 0)
    qts = (jnp.where(feat < DIFF_D, qt, 0.0).astype(BF16), jnp.where(feat >= DIFF_D, qt, 0.0).astype(BF16))
    for c in range(2):
        m_refs[c][...] = jnp.full(m_refs[c].shape, NEG_INF, F32)
        l_refs[c][...] = jnp.zeros(l_refs[c].shape, F32)
        acc_refs[c][...] = jnp.zeros(acc_refs[c].shape, F32)

    def scores(j, kind, dst_ref):
        kt = k16_ref[j * t:(j + 1) * t, :]
        bias = bias_ref[kind]
        for c in range(2):
            dst_ref[c] = _dot(kt, qts[c]) + bias

    def consume(j, src_ref):
        vt = vt_ref[j]
        s = [src_ref[c] for c in range(2)]
        m_prev = [m_refs[c][...] for c in range(2)]
        m_new = [jnp.maximum(m_prev[c], jnp.max(s[c], 0, keepdims=True)) for c in range(2)]
        p = [jnp.exp2(s[c] - m_new[c]) for c in range(2)]
        a = [jnp.exp2(m_prev[c] - m_new[c]) for c in range(2)]
        for c in range(2):
            l_refs[c][...] = a[c] * l_refs[c][...] + jnp.sum(p[c], 0, keepdims=True)
            acc_refs[c][...] = a[c] * acc_refs[c][...] + _dot(vt, p[c].astype(BF16))
            m_refs[c][...] = m_new[c]

    bufs = (sa_ref, sb_ref)
    for qi in range(nt):
        @pl.when(i == qi)
        def _(qi=qi):
            kind = lambda j: min(qi - j, 2)
            scores(0, kind(0), bufs[0])
            for j in range(qi + 1):
                if j < qi:
                    scores(j + 1, kind(j + 1), bufs[(j + 1) % 2])
                consume(j, bufs[j % 2])

    lam = _diff_lambda(lam_ref, lam_init)
    o = acc0_ref[...] / l0_ref[...] - lam * (acc1_ref[...] / l1_ref[...])
    o = o * lax.rsqrt(jnp.mean(o * o, 0, keepdims=True) + LN_EPS) * g_ref[...]
    o_ref[...] = jnp.transpose(o * (1.0 - lam_init)).astype(BF16)


def _diff_prompt(proj, bias_tiles, diff_lambda, subln_g, layer, *, bsz, length, t, lam_init):
    m = proj.shape[0]
    nq = length // t
    assert length % t == 0 and t % CHUNK == 0
    kv = lambda c: pl.BlockSpec((length, HEAD_W), lambda b, h, i: (b, c * HEADS_PER_BRANCH + h))
    stat = pltpu.VMEM((1, t), F32)
    acc = pltpu.VMEM((DIFF_DV, t), F32)
    return pl.pallas_call(
        functools.partial(_diff_prompt_kernel, lam_init=lam_init),
        grid=(bsz, DIFF_HEADS, nq),
        in_specs=[
            pl.BlockSpec((t, HEAD_W), lambda b, h, i: (b * nq + i, COL_DQ * HEADS_PER_BRANCH + h)),
            kv(COL_DK), kv(COL_DV),
            pl.BlockSpec((None, 3, t, t), lambda b, h, i: (h, 0, 0, 0)),
            pl.BlockSpec((None, 4, DIFF_D), lambda b, h, i: (layer, 0, 0)),
            pl.BlockSpec((None, DIFF_DV, 1), lambda b, h, i: (layer, 0, 0)),
        ],
        out_specs=pl.BlockSpec((t, HEAD_W), lambda b, h, i: (b * nq + i, h)),
        out_shape=jax.ShapeDtypeStruct((m, BRANCH_W), BF16),
        scratch_shapes=[pltpu.VMEM((length, HEAD_W), BF16), pltpu.VMEM((nq, DIFF_DV, t), BF16),
                        pltpu.VMEM((2, t, t), F32), pltpu.VMEM((2, t, t), F32),
                        stat, stat, stat, stat, acc, acc],
        compiler_params=_params(("parallel", "parallel", "arbitrary"), 40),
        name="diff_attention_prompt",
    )(proj, proj, proj, bias_tiles, diff_lambda, subln_g)


def _head_rows(ref, h, n):
    return ref[pl.ds(h, n, stride=HEADS_PER_BRANCH), :]


def _sample_attn_kernel(q_ref, kn_ref, vn_ref, mq_ref, kp_ref, vp_ref, mk_ref, mv_ref, bp_ref, bn_ref,
                        lam_ref, g_ref, yc_ref, yd_ref, *, lam_init):
    past = kp_ref.shape[0] // HEADS_PER_BRANCH
    tokens = mk_ref.shape[0] // HEADS_PER_BRANCH
    lam = _diff_lambda(lam_ref, lam_init)
    for h in range(HEADS_PER_BRANCH):
        cols = slice(h * HEAD_W, (h + 1) * HEAD_W)
        q = q_ref[:, cols] * (DIFF_D ** -0.5 * LOG2E)
        lane = lax.broadcasted_iota(jnp.int32, q.shape, 1)
        qs = (jnp.where(lane < DIFF_D, q, 0.0).astype(BF16), jnp.where(lane >= DIFF_D, q, 0.0).astype(BF16))
        kp = _head_rows(kp_ref, h, past).astype(BF16)
        vp = _head_rows(vp_ref, h, past).astype(BF16)
        kn = kn_ref[:, cols].astype(BF16)
        vn = vn_ref[:, cols].astype(BF16)
        outs = []
        for qc in qs:
            sp = _dot_nt(qc, kp) + bp_ref[h]
            sn = _dot_nt(qc, kn) + bn_ref[h]
            mx = jnp.maximum(jnp.max(sp, -1, keepdims=True), jnp.max(sn, -1, keepdims=True))
            pp = jnp.exp2(sp - mx)
            pn = jnp.exp2(sn - mx)
            den = jnp.sum(pp, -1, keepdims=True) + jnp.sum(pn, -1, keepdims=True)
            outs.append((_dot(pp.astype(BF16), vp) + _dot(pn.astype(BF16), vn)) / den)
        o = outs[0] - lam * outs[1]
        o = o * lax.rsqrt(jnp.mean(o * o, -1, keepdims=True) + LN_EPS) * g_ref[...]
        yc_ref[:, cols] = (o * (1.0 - lam_init)).astype(BF16)

        mk = _head_rows(mk_ref, h, tokens).astype(BF16)
        mv = _head_rows(mv_ref, h, tokens).astype(BF16)
        s = _dot_nt(mq_ref[:, cols].astype(BF16), mk) * (MEM_HD ** -0.5)
        p = jnp.exp(s - jnp.max(s, -1, keepdims=True))
        den = jnp.sum(p, -1, keepdims=True)
        yd_ref[:, cols] = (_dot(p.astype(BF16), mv) / den).astype(BF16)


def _sample_attention(proj, k_past, v_past, mem_k, mem_v, bias_past, bias_new, diff_lambda, subln_g, layer,
                      *, bsz, length, lam_init):
    m = proj.shape[0]
    col = lambda c: pl.BlockSpec((length, BRANCH_W), lambda b: (b, c))
    cache = lambda a: pl.BlockSpec((None, None) + a.shape[2:], lambda b: (layer, b, 0, 0))
    whole = lambda a: pl.BlockSpec(a.shape, lambda b: (0,) * a.ndim)
    out = pl.BlockSpec((length, BRANCH_W), lambda b: (b, 0))
    return pl.pallas_call(
        functools.partial(_sample_attn_kernel, lam_init=lam_init),
        grid=(bsz,),
        in_specs=[
            col(COL_DQ), col(COL_DK), col(COL_DV), col(COL_MQ),
            cache(k_past), cache(v_past), cache(mem_k), cache(mem_v),
            whole(bias_past), whole(bias_new),
            pl.BlockSpec((None, 4, DIFF_D), lambda b: (layer, 0, 0)),
            pl.BlockSpec((None, 1, DIFF_DV), lambda b: (layer, 0, 0)),
        ],
        out_specs=[out, out],
        out_shape=[jax.ShapeDtypeStruct((m, BRANCH_W), BF16), jax.ShapeDtypeStruct((m, BRANCH_W), BF16)],
        compiler_params=_params(("parallel",), 40),
        name="sample_attention",
    )(proj, proj, proj, proj, k_past, v_past, mem_k, mem_v, bias_past, bias_new, diff_lambda, subln_g)


def _merge_kernel(h_ref, ya_ref, yb_ref, yc_ref, yd_ref, wg_ref, wb_ref, bg_ref, o_ref):
    h = h_ref[...]
    acc = None
    for i, y_ref in enumerate((ya_ref, yb_ref, yc_ref, yd_ref)):
        gate = jax.nn.sigmoid(_dot(h, wg_ref[i]) + bg_ref[i])
        term = gate * _dot(y_ref[...], wb_ref[i])
        acc = term if acc is None else acc + term
    o_ref[...] = acc.astype(BF16)


def _merge(h16, ys, w_gate, w_branch, b_gate, layer, *, tm, tn):
    m, d = h16.shape
    assert m % tm == 0 and d % tn == 0
    y_spec = pl.BlockSpec((tm, BRANCH_W), lambda i, j: (i, 0))
    return pl.pallas_call(
        _merge_kernel,
        grid=(m // tm, d // tn),
        in_specs=[
            pl.BlockSpec((tm, d), lambda i, j: (i, 0)), y_spec, y_spec, y_spec, y_spec,
            pl.BlockSpec((None, N_BRANCH, d, tn), lambda i, j: (0, 0, 0, j)),
            pl.BlockSpec((None, N_BRANCH, BRANCH_W, tn), lambda i, j: (0, 0, 0, j)),
            pl.BlockSpec((None, N_BRANCH, 1, tn), lambda i, j: (layer, 0, 0, j)),
        ],
        out_specs=pl.BlockSpec((tm, tn), lambda i, j: (i, j)),
        out_shape=jax.ShapeDtypeStruct((m, d), BF16),
        compiler_params=_params(("parallel", "parallel"), 48),
        name="gated_merge",
    )(h16, *ys, w_gate, w_branch, b_gate)


def _proj_ln_kernel(m_ref, x_ref, w_ref, g_ref, b_ref, o_ref, *, alpha):
    for r in range(0, o_ref.shape[0], PROJ_ROW_CHUNK):
        rows = slice(r, r + PROJ_ROW_CHUNK)
        y = alpha * x_ref[rows, :] + _dot(m_ref[rows, :], w_ref[...])
        o_ref[rows, :] = _layer_norm(y, g_ref[...], b_ref[...])


def _proj_ln(merged, x, w_o, g, b, layer, *, alpha, tm):
    m, d = x.shape
    assert m % tm == 0
    row = pl.BlockSpec((tm, d), lambda i: (i, 0))
    vec = pl.BlockSpec((None, 1, d), lambda i: (layer, 0, 0))
    return pl.pallas_call(
        functools.partial(_proj_ln_kernel, alpha=alpha),
        grid=(m // tm,),
        in_specs=[row, row, pl.BlockSpec((None, d, d), lambda i: (0, 0, 0)), vec, vec],
        out_specs=row,
        out_shape=jax.ShapeDtypeStruct((m, d), F32),
        compiler_params=_params(("parallel",), 48),
        name="proj_ln",
    )(merged, x, w_o, g, b)


def _rope_tables(pos):
    half = RET_DK // 2
    inv = ROPE_BASE ** (-jnp.arange(half, dtype=F32) / half)
    ang = pos.astype(F32)[:, None] * inv[None, :]
    cos = jnp.cos(ang)
    sin = jnp.sin(ang)
    return jnp.concatenate([cos, cos], -1), jnp.concatenate([-sin, sin], -1)


def _t5_bucket(rel):
    nb = REL_BUCKETS // 2
    max_exact = nb // 2
    n = jnp.abs(rel)
    nf = jnp.maximum(n, 1).astype(F32)
    large = max_exact + (jnp.log(nf / max_exact) / math.log(REL_MAX_DIST / max_exact)
                         * (nb - max_exact)).astype(jnp.int32)
    large = jnp.minimum(large, nb - 1)
    return jnp.where(rel > 0, nb, 0) + jnp.where(n < max_exact, n, large)


def _masked_bias(q_pos, k_pos, rel_bias):
    bucket = _t5_bucket(k_pos[None, :] - q_pos[:, None])
    onehot = bucket[None, None] == jnp.arange(REL_BUCKETS, dtype=bucket.dtype)[None, :, None, None]
    bias = jnp.sum(jnp.where(onehot, rel_bias.astype(F32).T[:, :, None, None], 0.0), axis=1)
    allowed = (k_pos[None, :] // CHUNK) <= (q_pos[:, None] // CHUNK)
    return jnp.where(allowed[None], bias * LOG2E, NEG_INF)


def _prompt_bias_tiles(rel_bias, t):
    base = BIAS_BASE
    assert base % CHUNK == 0 and base + 1 >= REL_MAX_DIST and t % base == 0
    n = t // base
    pos = jnp.arange(base, dtype=jnp.int32)
    diag, sub, far = (jnp.swapaxes(_masked_bias(pos + d * base, pos, rel_bias), -1, -2) for d in range(3))
    masked = jnp.full_like(diag, NEG_INF)
    pick = lambda delta: masked if delta > 0 else (diag, sub, far)[min(-delta, 2)]
    tile = lambda d: jnp.concatenate(
        [jnp.concatenate([pick(kk - qq - d * n) for qq in range(n)], axis=-1) for kk in range(n)], axis=-2)
    return jnp.stack([tile(d) for d in range(3)], axis=1)


def _kv_export_kernel(k_ref, v_ref, *rest):
    ok_ref, ov_ref = rest[-2:]
    n = k_ref.shape[0]
    for h in range(HEADS_PER_BRANCH):
        cols = slice(h * HEAD_W, (h + 1) * HEAD_W)
        rows = pl.ds(h, n, stride=HEADS_PER_BRANCH)
        ok_ref[rows, :] = k_ref[:, cols]
        ov_ref[rows, :] = v_ref[:, cols]


def _kv_export(proj, prev, layer, depth, *, tm):
    m = proj.shape[0]
    assert m % tm == 0
    col = lambda c: pl.BlockSpec((tm, BRANCH_W), lambda i: (i, c))
    out = pl.BlockSpec((None, tm * HEADS_PER_BRANCH, HEAD_W), lambda i: (layer, i, 0))
    shape = jax.ShapeDtypeStruct((depth, m * HEADS_PER_BRANCH, HEAD_W), F32)
    keep = [pl.BlockSpec(memory_space=pl.ANY)] * len(prev)
    return pl.pallas_call(
        _kv_export_kernel,
        grid=(m // tm,),
        in_specs=[col(COL_DK), col(COL_DV)] + keep,
        out_specs=[out, out],
        out_shape=[shape, shape],
        input_output_aliases={2 + n: n for n in range(len(prev))},
        compiler_params=_params(("parallel",), 32),
        name="kv_export",
    )(proj, proj, *prev)


CAST_IN_FFN1 = (('w_in', False), ('ffn2_w_up', False), ('ffn2_w_down', True))
CAST_IN_PROJ = (('w_gate', False), ('w_branch', False), ('w_o', False))
CAST_IN_FFN2 = (('ffn1_w_up', False), ('ffn1_w_down', True))


def _encoder_layer(x, w16, p, layer, *, alpha, cfg, mixers_fn, w32=None, w16_next=None):
    d = x.shape[1]
    casts = lambda names, l: tuple((w32[n], l, swap) for n, swap in names) if w32 is not None else ()
    x1, done = _ffn_ln(x, w16['ffn1_w_up'], w16['ffn1_w_down'], p['ln1_g'], p['ln1_b'], layer,
                       alpha=alpha, tm=cfg['tm_ffn'], tf=cfg['tf'], casts=casts(CAST_IN_FFN1, layer))
    w16.update(zip((n for n, _ in CAST_IN_FFN1), done))
    proj, x1b, done = _in_proj(x1, w16['w_in'], tm=cfg['tm_in'], tn=cfg['tn_in'], casts=casts(CAST_IN_PROJ, layer))
    w16.update(zip((n for n, _ in CAST_IN_PROJ), done))
    y_a, y_b, y_c, y_d, conv_new, ret_new = mixers_fn(proj, 0.8 - 0.6 * math.exp(-0.3 * layer))
    merged = _merge(x1b, (y_a, y_b, y_c, y_d), w16['w_gate'].reshape(1, N_BRANCH, d, d),
                    w16['w_branch'].reshape(1, N_BRANCH, BRANCH_W, d), p['b_gate'], layer,
                    tm=cfg['tm_merge'], tn=cfg['tn_merge'])
    x2 = _proj_ln(merged, x1, w16['w_o'], p['ln2_g'], p['ln2_b'], layer, alpha=alpha, tm=cfg['tm_proj'])
    x3, done = _ffn_ln(x2, w16['ffn2_w_up'], w16['ffn2_w_down'], p['ln3_g'], p['ln3_b'], layer,
                       alpha=alpha, tm=cfg['tm_ffn'], tf=cfg['tf'],
                       casts=casts(CAST_IN_FFN2, layer + 1) if w16_next is not None else ())
    if w16_next is not None:
        w16_next.update(zip((n for n, _ in CAST_IN_FFN2), done))
    return x3, proj, conv_new, ret_new


PROMPT_CFG = dict(tm_ffn=1024, tf=512, tm_in=1024, tn_in=1408, t_conv=512, ret_chunk=256, t_diff=512,
                  tq_mem=1024, tm_merge=1024, tn_merge=256, tm_proj=512, tm_export=1024)
SAMPLE_CFG = dict(tm_ffn=256, tf=512, tm_in=256, tn_in=512, t_conv=16, ret_chunk=16,
                  tm_merge=256, tn_merge=256, tm_proj=256, tm_export=256)


def kernel(x_prompt, x_sample, state_conv, state_ret, cache_diff_k, cache_diff_v, cache_mem_k, cache_mem_v,
           mem_prompt, ffn1_w_up, ffn1_w_down, ln1_g, ln1_b, w_in, conv_w, ret_gn_g, diff_lambda,
           diff_subln_g, w_mem_kv, w_branch, w_gate, b_gate, w_o, ln2_g, ln2_b, ffn2_w_up, ffn2_w_down,
           ln3_g, ln3_b, rel_bias):
    bp, lp, d = x_prompt.shape
    bs, ls, _ = x_sample.shape
    depth = w_in.shape[0]
    past = cache_diff_k.shape[2]
    mem_tokens = mem_prompt.shape[1]
    alpha = (2 * depth) ** 0.25
    half = MEM_HEADS * MEM_HD

    pos_p = jnp.arange(lp, dtype=jnp.int32)
    pos_s = past + jnp.arange(ls, dtype=jnp.int32)
    past_pos = jnp.arange(past, dtype=jnp.int32)
    rope_p = _rope_tables(pos_p)
    rope_s = _rope_tables(pos_s)
    bias_p = _prompt_bias_tiles(rel_bias, PROMPT_CFG['t_diff'])
    bias_s_past = _masked_bias(pos_s, past_pos, rel_bias)
    bias_s_new = _masked_bias(pos_s, pos_s, rel_bias)

    vec = lambda v: v.reshape(depth, 1, v.shape[-1])
    p = {'ln1_g': vec(ln1_g), 'ln1_b': vec(ln1_b), 'ln2_g': vec(ln2_g), 'ln2_b': vec(ln2_b),
         'ln3_g': vec(ln3_g), 'ln3_b': vec(ln3_b), 'b_gate': b_gate.reshape(depth, N_BRANCH, 1, d)}
    gn_g = vec(ret_gn_g)
    w32 = {'ffn1_w_up': ffn1_w_up, 'ffn1_w_down': ffn1_w_down, 'w_in': w_in,
           'w_gate': w_gate.reshape(depth, N_BRANCH * d, d), 'w_branch': w_branch.reshape(depth, N_BRANCH * BRANCH_W, d),
           'w_o': w_o, 'ffn2_w_up': ffn2_w_up, 'ffn2_w_down': ffn2_w_down}
    w16 = [dict() for _ in range(depth)]
    w16[0].update({n: w32[n][:1].astype(BF16) for n, _ in CAST_IN_FFN2})
    w_mem16 = w_mem_kv.astype(BF16)
    subln_col = diff_subln_g.reshape(depth, DIFF_DV, 1)
    subln_row = diff_subln_g.reshape(depth, 1, DIFF_DV)
    head_major = lambda c: c.reshape(depth, bs, c.shape[2] * c.shape[3], c.shape[4])
    kpast, vpast = head_major(cache_diff_k), head_major(cache_diff_v)
    mem_k_s, mem_v_s = head_major(cache_mem_k), head_major(cache_mem_v)

    yp = x_prompt.reshape(bp * lp, d)
    ys = x_sample.reshape(bs * ls, d)
    mem16 = mem_prompt.reshape(bp * mem_tokens, d).astype(BF16)
    zero_conv = jnp.zeros((bp, CONV_WIDTH - 1, BRANCH_W), F32)
    zero_ret = jnp.zeros((bp, RET_HEADS, RET_DK, RET_DK), F32)

    conv_p, ret_p, mk_p, mv_p, conv_s, ret_s = [], [], [], [], [], []
    kv_p, kv_s = [], []
    for l in range(depth):
        mkv = _matmul(mem16, w_mem16, l, tm=bp * mem_tokens, tn=512).reshape(bp, mem_tokens, 2 * half)

        def mixers_p(proj, lam_init):
            y_a, y_b, y_d, c_new, r_new = _local_mixers(
                proj, conv_w, zero_conv, rope_p[0], rope_p[1], zero_ret, gn_g, mkv, l,
                bsz=bp, length=lp, chunk=PROMPT_CFG['ret_chunk'])
            y_c = _diff_prompt(proj, bias_p, diff_lambda, subln_col, l, bsz=bp, length=lp,
                               t=PROMPT_CFG['t_diff'], lam_init=lam_init)
            return y_a, y_b, y_c, y_d, c_new, r_new

        yp, proj, c_new, r_new = _encoder_layer(
            yp, w16[l], p, l, alpha=alpha, cfg=PROMPT_CFG, mixers_fn=mixers_p,
            w32=w32, w16_next=w16[l + 1] if l + 1 < depth else None)
        kv_p = _kv_export(proj, kv_p, l, depth, tm=PROMPT_CFG['tm_export'])
        conv_p.append(c_new)
        ret_p.append(r_new)
        mk_p.append(mkv[:, :, :half].reshape(bp, mem_tokens, MEM_HEADS, MEM_HD))
        mv_p.append(mkv[:, :, half:].reshape(bp, mem_tokens, MEM_HEADS, MEM_HD))

        def mixers_s(proj, lam_init):
            y_a, c_new = _conv_branch(proj, conv_w, state_conv[l], l, bsz=bs, length=ls, tl=SAMPLE_CFG['t_conv'])
            y_b, r_new = _retention_branch(proj, rope_s[0], rope_s[1], state_ret[l], gn_g, l,
                                           bsz=bs, length=ls, chunk=SAMPLE_CFG['ret_chunk'])
            y_c, y_d = _sample_attention(proj, kpast, vpast, mem_k_s, mem_v_s, bias_s_past, bias_s_new,
                                         diff_lambda, subln_row, l, bsz=bs, length=ls, lam_init=lam_init)
            return y_a, y_b, y_c, y_d, c_new, r_new

        ys, proj, c_new, r_new = _encoder_layer(ys, w16[l], p, l, alpha=alpha, cfg=SAMPLE_CFG, mixers_fn=mixers_s)
        kv_s = _kv_export(proj, kv_s, l, depth, tm=SAMPLE_CFG['tm_export'])
        conv_s.append(c_new)
        ret_s.append(r_new)

    kv_shape = lambda b, n: (depth, b, n, DIFF_HEADS, DIFF_DV)
    return (yp.reshape(bp, lp, d), ys.reshape(bs, ls, d), jnp.stack(conv_p), jnp.stack(ret_p),
            kv_p[0].reshape(kv_shape(bp, lp)), kv_p[1].reshape(kv_shape(bp, lp)),
            jnp.stack(mk_p), jnp.stack(mv_p), jnp.stack(conv_s), jnp.stack(ret_s),
            kv_s[0].reshape(kv_shape(bs, ls)), kv_s[1].reshape(kv_shape(bs, ls)))
```

```python
import functools
import math

import jax
import jax.numpy as jnp
from jax import lax
from jax.experimental import pallas as pl
from jax.experimental.pallas import tpu as pltpu

F32 = jnp.float32
BF16 = jnp.bfloat16

CHUNK = 64
CONV_WIDTH = 3
RET_HEADS = 4
RET_DK = 128
DIFF_HEADS = 4
DIFF_D = 64
DIFF_DV = 128
MEM_HEADS = 4
MEM_HD = 128
BRANCH_W = 512
N_BRANCH = 4
REL_BUCKETS = 32
REL_MAX_DIST = 128
LN_EPS = 1e-5
ROPE_BASE = 10000.0
NEG_INF = -1e30
HEAD_W = 128
LOG2E = math.log2(math.e)
LN_ROW_CHUNK = 128
PROJ_ROW_CHUNK = 256
BIAS_BASE = 256
FAR_TILE = 2
FFN_UP_CHUNK = 256
FFN_DOWN_CHUNK = 512

COL_CB, COL_CC, COL_CH, COL_RQ, COL_RK, COL_RV, COL_RG, COL_DQ, COL_DK, COL_DV, COL_MQ = range(11)
HEADS_PER_BRANCH = BRANCH_W // HEAD_W

LANES = 128
F32_SUBLANES = 8
BF16_SUBLANES = 16
V7X_VMEM_BYTES = 64 * 1024 * 1024
MIB = 1024 * 1024


def _params(semantics, vmem_mib):
    assert vmem_mib * MIB < V7X_VMEM_BYTES
    return pltpu.CompilerParams(dimension_semantics=semantics, vmem_limit_bytes=vmem_mib * MIB)


def _layer_norm(y, g, b):
    mu = jnp.mean(y, -1, keepdims=True)
    d = y - mu
    var = jnp.mean(d * d, -1, keepdims=True)
    return d * lax.rsqrt(var + LN_EPS) * g + b


def _dot(a, b):
    return jnp.dot(a, b, preferred_element_type=F32)


def _dot_nt(a, b):
    return lax.dot_general(a, b, (((1,), (1,)), ((), ())), preferred_element_type=F32)


def _dot_tn(a, b):
    return lax.dot_general(a, b, (((0,), (0,)), ((), ())), preferred_element_type=F32)


def _side_cast_specs(casts, grid):
    in_specs, out_specs, out_shapes = [], [], []
    for src, layer, swap in casts:
        _, r, c = src.shape
        gr, gc = (grid[1], grid[0]) if swap else grid
        assert r % gr == 0 and c % gc == 0
        blk = (None, r // gr, c // gc)
        assert blk[1] % BF16_SUBLANES == 0 and blk[2] % LANES == 0
        pick = (lambda i, j: (j, i)) if swap else (lambda i, j: (i, j))
        in_specs.append(pl.BlockSpec(blk, lambda i, j, layer=layer, pick=pick: (layer,) + pick(i, j)))
        out_specs.append(pl.BlockSpec(blk, lambda i, j, pick=pick: (0,) + pick(i, j)))
        out_shapes.append(jax.ShapeDtypeStruct((1, r, c), BF16))
    return in_specs, out_specs, out_shapes


def _side_cast(src_refs, dst_refs):
    for src_ref, dst_ref in zip(src_refs, dst_refs):
        dst_ref[...] = src_ref[...].astype(BF16)


def _ffn_ln_kernel(x_ref, wa_ref, wb_ref, wd_ref, g_ref, b_ref, *rest, alpha, n_cast):
    cast_src, o_ref, cast_dst = rest[:n_cast], rest[n_cast], rest[n_cast + 1:2 * n_cast + 1]
    xb_ref, h_ref = rest[2 * n_cast + 1:]
    j = pl.program_id(1)

    @pl.when(j == 0)
    def _():
        xb_ref[...] = x_ref[...].astype(BF16)
        o_ref[...] = jnp.zeros(o_ref.shape, F32)

    xb = xb_ref[...]
    for c in range(0, h_ref.shape[1], FFN_UP_CHUNK):
        cols = slice(c, c + FFN_UP_CHUNK)
        a = _dot(xb, wa_ref[:, cols])
        b = _dot(xb, wb_ref[:, cols])
        h_ref[:, cols] = (a * jax.nn.sigmoid(a) * b).astype(BF16)
    h = h_ref[...]
    for c in range(0, o_ref.shape[1], FFN_DOWN_CHUNK):
        cols = slice(c, c + FFN_DOWN_CHUNK)
        o_ref[:, cols] += _dot(h, wd_ref[:, cols])
    _side_cast(cast_src, cast_dst)

    @pl.when(j == pl.num_programs(1) - 1)
    def _():
        for r in range(0, o_ref.shape[0], LN_ROW_CHUNK):
            rows = slice(r, r + LN_ROW_CHUNK)
            o_ref[rows, :] = _layer_norm(alpha * x_ref[rows, :] + 0.5 * o_ref[rows, :], g_ref[...], b_ref[...])


def _ffn_ln(x, w_up, w_down, g, b, layer, *, alpha, tm, tf, casts=()):
    m, d = x.shape
    d_ff = w_down.shape[1]
    nj = d_ff // tf
    assert m % tm == 0 and d_ff % tf == 0
    grid = (m // tm, nj)
    row = lambda i, j: (i, 0)
    vec = pl.BlockSpec((None, 1, d), lambda i, j: (layer, 0, 0))
    cast_in, cast_out, cast_shapes = _side_cast_specs(casts, grid)
    out = pl.pallas_call(
        functools.partial(_ffn_ln_kernel, alpha=alpha, n_cast=len(casts)),
        grid=grid,
        in_specs=[
            pl.BlockSpec((tm, d), row),
            pl.BlockSpec((None, d, tf), lambda i, j: (0, 0, j)),
            pl.BlockSpec((None, d, tf), lambda i, j: (0, 0, j + nj)),
            pl.BlockSpec((None, tf, d), lambda i, j: (0, j, 0)),
            vec, vec,
        ] + cast_in,
        out_specs=[pl.BlockSpec((tm, d), row)] + cast_out,
        out_shape=[jax.ShapeDtypeStruct((m, d), F32)] + cast_shapes,
        scratch_shapes=[pltpu.VMEM((tm, d), BF16), pltpu.VMEM((tm, tf), BF16)],
        compiler_params=_params(("parallel", "arbitrary"), 60),
        name="ffn_ln",
    )(x, w_up, w_up, w_down, g, b, *(c[0] for c in casts))
    return out[0], out[1:]


def _in_proj_kernel(x_ref, w_ref, *rest, n_cast):
    cast_src, o_ref, xb_ref, cast_dst = rest[:n_cast], rest[n_cast], rest[n_cast + 1], rest[n_cast + 2:]

    @pl.when(pl.program_id(1) == 0)
    def _():
        xb_ref[...] = x_ref[...].astype(BF16)

    o_ref[...] = _dot(xb_ref[...], w_ref[...])
    _side_cast(cast_src, cast_dst)


def _in_proj(x, w, *, tm, tn, casts=()):
    m, k = x.shape
    n = w.shape[2]
    assert m % tm == 0 and n % tn == 0
    grid = (m // tm, n // tn)
    cast_in, cast_out, cast_shapes = _side_cast_specs(casts, grid)
    out = pl.pallas_call(
        functools.partial(_in_proj_kernel, n_cast=len(casts)),
        grid=grid,
        in_specs=[pl.BlockSpec((tm, k), lambda i, j: (i, 0)),
                  pl.BlockSpec((None, k, tn), lambda i, j: (0, 0, j))] + cast_in,
        out_specs=[pl.BlockSpec((tm, tn), lambda i, j: (i, j)), pl.BlockSpec((tm, k), lambda i, j: (i, 0))] + cast_out,
        out_shape=[jax.ShapeDtypeStruct((m, n), F32), jax.ShapeDtypeStruct((m, k), BF16)] + cast_shapes,
        compiler_params=_params(("parallel", "arbitrary"), 60),
        name="in_proj",
    )(x, w, *(c[0] for c in casts))
    return out[0], out[1], out[2:]


def _matmul_kernel(x_ref, w_ref, o_ref):
    o_ref[...] = _dot(x_ref[...], w_ref[...])


def _matmul(x, w, layer, *, tm, tn):
    m, k = x.shape
    n = w.shape[2]
    assert m % tm == 0 and n % tn == 0
    return pl.pallas_call(
        _matmul_kernel,
        grid=(m // tm, n // tn),
        in_specs=[pl.BlockSpec((tm, k), lambda i, j: (i, 0)),
                  pl.BlockSpec((None, k, tn), lambda i, j: (layer, 0, j))],
        out_specs=pl.BlockSpec((tm, tn), lambda i, j: (i, j)),
        out_shape=jax.ShapeDtypeStruct((m, n), F32),
        compiler_params=_params(("parallel", "parallel"), 48),
        name="matmul",
    )(x, w)


def _conv_body(cb_ref, cc_ref, ch_ref, w_ref, carry_ref, y_ref):
    u = cc_ref[...] * ch_ref[...]
    row = lax.broadcasted_iota(jnp.int32, u.shape, 0)
    c0 = carry_ref[0:1, :]
    c1 = carry_ref[1:2, :]
    u1 = jnp.where(row == 0, c1, pltpu.roll(u, 1, 0))
    u2 = jnp.where(row == 0, c0, jnp.where(row == 1, c1, pltpu.roll(u, 2, 0)))
    z = w_ref[0:1, :] * u2 + w_ref[1:2, :] * u1 + w_ref[2:3, :] * u
    y_ref[...] = (cb_ref[...] * z).astype(BF16)
    tail = u[u.shape[0] - F32_SUBLANES:, :][F32_SUBLANES - (CONV_WIDTH - 1):, :]
    carry_ref[...] = tail
    return tail


def _conv_kernel(cb_ref, cc_ref, ch_ref, w_ref, prev_ref, y_ref, state_ref, carry_ref):
    l = pl.program_id(1)

    @pl.when(l == 0)
    def _():
        carry_ref[...] = prev_ref[...]

    tail = _conv_body(cb_ref, cc_ref, ch_ref, w_ref, carry_ref, y_ref)

    @pl.when(l == pl.num_programs(1) - 1)
    def _():
        state_ref[...] = tail


def _conv_branch(proj, conv_w, prev, layer, *, bsz, length, tl):
    m = proj.shape[0]
    nl = length // tl
    assert length % tl == 0 and tl >= F32_SUBLANES
    col = lambda c: pl.BlockSpec((tl, BRANCH_W), lambda b, l: (b * nl + l, c))
    return pl.pallas_call(
        _conv_kernel,
        grid=(bsz, nl),
        in_specs=[
            col(COL_CB), col(COL_CC), col(COL_CH),
            pl.BlockSpec((None, CONV_WIDTH, BRANCH_W), lambda b, l: (layer, 0, 0)),
            pl.BlockSpec((None, CONV_WIDTH - 1, BRANCH_W), lambda b, l: (b, 0, 0)),
        ],
        out_specs=[
            pl.BlockSpec((tl, BRANCH_W), lambda b, l: (b * nl + l, 0)),
            pl.BlockSpec((None, CONV_WIDTH - 1, BRANCH_W), lambda b, l: (b, 0, 0)),
        ],
        out_shape=[
            jax.ShapeDtypeStruct((m, BRANCH_W), BF16),
            jax.ShapeDtypeStruct((bsz, CONV_WIDTH - 1, BRANCH_W), F32),
        ],
        scratch_shapes=[pltpu.VMEM((CONV_WIDTH - 1, BRANCH_W), F32)],
        compiler_params=_params(("parallel", "arbitrary"), 32),
        name="conv_branch",
    )(proj, proj, proj, conv_w, prev)


def _ret_log_gamma(h):
    return math.log1p(-(2.0 ** (-5.0 - h)))


def _ret_fill_decay(intra_ref):
    chunk = intra_ref.shape[1]
    ri = lax.broadcasted_iota(jnp.int32, (chunk, chunk), 0)
    ci = lax.broadcasted_iota(jnp.int32, (chunk, chunk), 1)
    rel = (ri - ci).astype(F32)
    for h in range(RET_HEADS):
        intra_ref[h] = jnp.where(rel >= 0.0, jnp.exp(jnp.maximum(rel, 0.0) * _ret_log_gamma(h)), 0.0)


def _retention_body(q_ref, k_ref, v_ref, g_ref, cos_ref, sin_ref, gn_ref, y_ref, s_ref, intra_ref, rows):
    chunk = intra_ref.shape[1]
    cosf = cos_ref[rows, :]
    sinf = sin_ref[rows, :]
    idx = lax.broadcasted_iota(jnp.int32, (chunk, 1), 0).astype(F32)
    for h in range(RET_HEADS):
        log_g = _ret_log_gamma(h)
        cols = slice(h * RET_DK, (h + 1) * RET_DK)
        q = q_ref[rows, cols]
        k = k_ref[rows, cols]
        q = q * cosf + pltpu.roll(q, RET_DK // 2, 1) * sinf
        k = (k * cosf + pltpu.roll(k, RET_DK // 2, 1) * sinf) * (RET_DK ** -0.5)
        vb = v_ref[rows, cols].astype(BF16)
        q_dec = jnp.exp((idx + 1.0) * log_g)
        k_dec = jnp.exp((chunk - 1.0 - idx) * log_g)
        c_dec = math.exp(chunk * log_g)
        qb = q.astype(BF16)
        s_prev = s_ref[h]
        att = _dot_nt(qb, k.astype(BF16)) * intra_ref[h]
        o = _dot(att.astype(BF16), vb) + _dot(qb, s_prev.astype(BF16)) * q_dec
        s_ref[h] = s_prev * c_dec + _dot_tn((k * k_dec).astype(BF16), vb)
        mu = jnp.mean(o, -1, keepdims=True)
        d = o - mu
        var = jnp.mean(d * d, -1, keepdims=True)
        ro = d * lax.rsqrt(var + LN_EPS) * gn_ref[:, cols]
        gate = g_ref[rows, cols]
        y_ref[rows, cols] = (gate * jax.nn.sigmoid(gate) * ro).astype(BF16)


def _retention_kernel(q_ref, k_ref, v_ref, g_ref, cos_ref, sin_ref, s0_ref, gn_ref, y_ref, sfin_ref,
                      s_ref, intra_ref):
    c = pl.program_id(1)

    @pl.when(jnp.logical_and(pl.program_id(0) == 0, c == 0))
    def _():
        _ret_fill_decay(intra_ref)

    @pl.when(c == 0)
    def _():
        s_ref[...] = s0_ref[...]

    _retention_body(q_ref, k_ref, v_ref, g_ref, cos_ref, sin_ref, gn_ref, y_ref, s_ref, intra_ref,
                    slice(0, q_ref.shape[0]))

    @pl.when(c == pl.num_programs(1) - 1)
    def _():
        sfin_ref[...] = s_ref[...]


def _retention_branch(proj, cosf, sinf, s0, gn_g, layer, *, bsz, length, chunk):
    m = proj.shape[0]
    nc = length // chunk
    assert length % chunk == 0
    col = lambda c: pl.BlockSpec((chunk, BRANCH_W), lambda b, i: (b * nc + i, c))
    state = pl.BlockSpec((None, RET_HEADS, RET_DK, RET_DK), lambda b, i: (b, 0, 0, 0))
    return pl.pallas_call(
        _retention_kernel,
        grid=(bsz, nc),
        in_specs=[
            col(COL_RQ), col(COL_RK), col(COL_RV), col(COL_RG),
            pl.BlockSpec((chunk, RET_DK), lambda b, i: (i, 0)),
            pl.BlockSpec((chunk, RET_DK), lambda b, i: (i, 0)),
            state,
            pl.BlockSpec((None, 1, BRANCH_W), lambda b, i: (layer, 0, 0)),
        ],
        out_specs=[pl.BlockSpec((chunk, BRANCH_W), lambda b, i: (b * nc + i, 0)), state],
        out_shape=[
            jax.ShapeDtypeStruct((m, BRANCH_W), BF16),
            jax.ShapeDtypeStruct((bsz, RET_HEADS, RET_DK, RET_DK), F32),
        ],
        scratch_shapes=[pltpu.VMEM((RET_HEADS, RET_DK, RET_DK), F32),
                        pltpu.VMEM((RET_HEADS, chunk, chunk), F32)],
        compiler_params=_params(("arbitrary", "arbitrary"), 32),
        name="retention_branch",
    )(proj, proj, proj, proj, cosf, sinf, s0, gn_g)


def _mem_head(q, mk, mv):
    s = _dot_nt(q.astype(BF16), mk.astype(BF16)) * (MEM_HD ** -0.5)
    p = jnp.exp(s - jnp.max(s, -1, keepdims=True))
    den = jnp.sum(p, -1, keepdims=True)
    return _dot(p.astype(BF16), mv.astype(BF16)) / den


def _local_mixers_kernel(cb_ref, cc_ref, ch_ref, rq_ref, rk_ref, rv_ref, rg_ref, mq_ref, w_ref, prev_ref,
                         cos_ref, sin_ref, s0_ref, gn_ref, mk_ref, mv_ref,
                         ya_ref, yb_ref, yd_ref, conv_state_ref, sfin_ref,
                         carry_ref, s_ref, intra_ref):
    c = pl.program_id(1)

    @pl.when(jnp.logical_and(pl.program_id(0) == 0, c == 0))
    def _():
        _ret_fill_decay(intra_ref)

    @pl.when(c == 0)
    def _():
        carry_ref[...] = prev_ref[...]
        s_ref[...] = s0_ref[...]

    tail = _conv_body(cb_ref, cc_ref, ch_ref, w_ref, carry_ref, ya_ref)
    chunk = intra_ref.shape[1]
    for r in range(0, rq_ref.shape[0], chunk):
        _retention_body(rq_ref, rk_ref, rv_ref, rg_ref, cos_ref, sin_ref, gn_ref, yb_ref, s_ref, intra_ref,
                        slice(r, r + chunk))
    for h in range(MEM_HEADS):
        cols = slice(h * MEM_HD, (h + 1) * MEM_HD)
        yd_ref[:, cols] = _mem_head(mq_ref[:, cols], mk_ref[:, cols], mv_ref[:, cols]).astype(BF16)

    @pl.when(c == pl.num_programs(1) - 1)
    def _():
        conv_state_ref[...] = tail
        sfin_ref[...] = s_ref[...]


def _local_mixers(proj, conv_w, conv_prev, cosf, sinf, s0, gn_g, mkv, layer, *, bsz, length, rows, chunk):
    m = proj.shape[0]
    nc = length // rows
    assert length % rows == 0 and rows % chunk == 0 and chunk >= F32_SUBLANES
    tokens = mkv.shape[1]
    col = lambda c: pl.BlockSpec((rows, BRANCH_W), lambda b, i: (b * nc + i, c))
    out = pl.BlockSpec((rows, BRANCH_W), lambda b, i: (b * nc + i, 0))
    state = pl.BlockSpec((None, RET_HEADS, RET_DK, RET_DK), lambda b, i: (b, 0, 0, 0))
    conv_state = pl.BlockSpec((None, CONV_WIDTH - 1, BRANCH_W), lambda b, i: (b, 0, 0))
    rope = pl.BlockSpec((rows, RET_DK), lambda b, i: (i, 0))
    y_shape = jax.ShapeDtypeStruct((m, BRANCH_W), BF16)
    return pl.pallas_call(
        _local_mixers_kernel,
        grid=(bsz, nc),
        in_specs=[
            col(COL_CB), col(COL_CC), col(COL_CH), col(COL_RQ), col(COL_RK), col(COL_RV), col(COL_RG), col(COL_MQ),
            pl.BlockSpec((None, CONV_WIDTH, BRANCH_W), lambda b, i: (layer, 0, 0)),
            conv_state, rope, rope, state,
            pl.BlockSpec((None, 1, BRANCH_W), lambda b, i: (layer, 0, 0)),
            pl.BlockSpec((None, tokens, BRANCH_W), lambda b, i: (b, 0, 0)),
            pl.BlockSpec((None, tokens, BRANCH_W), lambda b, i: (b, 0, 1)),
        ],
        out_specs=[out, out, out, conv_state, state],
        out_shape=[
            y_shape, y_shape, y_shape,
            jax.ShapeDtypeStruct((bsz, CONV_WIDTH - 1, BRANCH_W), F32),
            jax.ShapeDtypeStruct((bsz, RET_HEADS, RET_DK, RET_DK), F32),
        ],
        scratch_shapes=[pltpu.VMEM((CONV_WIDTH - 1, BRANCH_W), F32),
                        pltpu.VMEM((RET_HEADS, RET_DK, RET_DK), F32),
                        pltpu.VMEM((RET_HEADS, chunk, chunk), F32)],
        compiler_params=_params(("arbitrary", "arbitrary"), 40),
        name="local_mixers",
    )(proj, proj, proj, proj, proj, proj, proj, proj, conv_w, conv_prev, cosf, sinf, s0, gn_g, mkv, mkv)


def _diff_lambda(lam_ref, lam_init):
    e0 = jnp.exp(jnp.sum(lam_ref[0:1, :] * lam_ref[1:2, :], -1, keepdims=True))
    e1 = jnp.exp(jnp.sum(lam_ref[2:3, :] * lam_ref[3:4, :], -1, keepdims=True))
    return e0 - e1 + lam_init


def _diff_prompt_kernel(q_ref, k_ref, v_ref, bias_ref, lam_ref, g_ref, o_ref, k16_ref, vt_ref,
                        sa_ref, sb_ref, m0_ref, m1_ref, l0_ref, l1_ref, acc0_ref, acc1_ref, *, lam_init):
    i = pl.program_id(2)
    t = q_ref.shape[0]
    nt = k_ref.shape[0] // t
    m_refs, l_refs, acc_refs = (m0_ref, m1_ref), (l0_ref, l1_ref), (acc0_ref, acc1_ref)

    @pl.when(i == 0)
    def _():
        k16_ref[...] = k_ref[...].astype(BF16)
        for j in range(nt):
            vt_ref[j] = jnp.transpose(v_ref[j * t:(j + 1) * t, :]).astype(BF16)

    qt = jnp.transpose(q_ref[...] * (DIFF_D ** -0.5 * LOG2E))
    feat = lax.broadcasted_iota(jnp.int32, qt.shape, 0)
    qts = (jnp.where(feat < DIFF_D, qt, 0.0).astype(BF16), jnp.where(feat >= DIFF_D, qt, 0.0).astype(BF16))
    for c in range(2):
        m_refs[c][...] = jnp.full(m_refs[c].shape, NEG_INF, F32)
        l_refs[c][...] = jnp.zeros(l_refs[c].shape, F32)
        acc_refs[c][...] = jnp.zeros(acc_refs[c].shape, F32)

    def scores(j, kind, dst_ref):
        kt = k16_ref[j * t:(j + 1) * t, :]
        for c in range(2):
            s = _dot(kt, qts[c])
            dst_ref[c] = s if kind == FAR_TILE else s + bias_ref[kind]

    def consume(j, src_ref):
        vt = vt_ref[j]
        s = [src_ref[c] for c in range(2)]
        m_prev = [m_refs[c][...] for c in range(2)]
        m_new = [jnp.maximum(m_prev[c], jnp.max(s[c], 0, keepdims=True)) for c in range(2)]
        p = [jnp.exp2(s[c] - m_new[c]) for c in range(2)]
        a = [jnp.exp2(m_prev[c] - m_new[c]) for c in range(2)]
        for c in range(2):
            l_refs[c][...] = a[c] * l_refs[c][...] + jnp.sum(p[c], 0, keepdims=True)
            acc_refs[c][...] = a[c] * acc_refs[c][...] + _dot(vt, p[c].astype(BF16))
            m_refs[c][...] = m_new[c]

    bufs = (sa_ref, sb_ref)
    for qi in range(nt):
        @pl.when(i == qi)
        def _(qi=qi):
            kind = lambda j: min(qi - j, FAR_TILE)
            scores(0, kind(0), bufs[0])
            for j in range(qi + 1):
                if j < qi:
                    scores(j + 1, kind(j + 1), bufs[(j + 1) % 2])
                consume(j, bufs[j % 2])

    lam = _diff_lambda(lam_ref, lam_init)
    o = acc0_ref[...] / l0_ref[...] - lam * (acc1_ref[...] / l1_ref[...])
    o = o * lax.rsqrt(jnp.mean(o * o, 0, keepdims=True) + LN_EPS) * g_ref[...]
    o_ref[...] = jnp.transpose(o * (1.0 - lam_init)).astype(BF16)


def _diff_prompt(proj, bias_tiles, diff_lambda, subln_g, layer, *, bsz, length, t, lam_init):
    m = proj.shape[0]
    nq = length // t
    assert length % t == 0 and t % CHUNK == 0
    kv = lambda c: pl.BlockSpec((length, HEAD_W), lambda b, h, i: (b, c * HEADS_PER_BRANCH + h))
    stat = pltpu.VMEM((1, t), F32)
    acc = pltpu.VMEM((DIFF_DV, t), F32)
    return pl.pallas_call(
        functools.partial(_diff_prompt_kernel, lam_init=lam_init),
        grid=(bsz, DIFF_HEADS, nq),
        in_specs=[
            pl.BlockSpec((t, HEAD_W), lambda b, h, i: (b * nq + i, COL_DQ * HEADS_PER_BRANCH + h)),
            kv(COL_DK), kv(COL_DV),
            pl.BlockSpec((None, FAR_TILE, t, t), lambda b, h, i: (h, 0, 0, 0)),
            pl.BlockSpec((None, 4, DIFF_D), lambda b, h, i: (layer, 0, 0)),
            pl.BlockSpec((None, DIFF_DV, 1), lambda b, h, i: (layer, 0, 0)),
        ],
        out_specs=pl.BlockSpec((t, HEAD_W), lambda b, h, i: (b * nq + i, h)),
        out_shape=jax.ShapeDtypeStruct((m, BRANCH_W), BF16),
        scratch_shapes=[pltpu.VMEM((length, HEAD_W), BF16), pltpu.VMEM((nq, DIFF_DV, t), BF16),
                        pltpu.VMEM((2, t, t), F32), pltpu.VMEM((2, t, t), F32),
                        stat, stat, stat, stat, acc, acc],
        compiler_params=_params(("parallel", "parallel", "arbitrary"), 40),
        name="diff_attention_prompt",
    )(proj, proj, proj, bias_tiles, diff_lambda, subln_g)


def _head_rows(ref, h, n):
    return ref[pl.ds(h, n, stride=HEADS_PER_BRANCH), :]


def _sample_attn_kernel(q_ref, kn_ref, vn_ref, mq_ref, kp_ref, vp_ref, mk_ref, mv_ref, bp_ref, bn_ref,
                        lam_ref, g_ref, yc_ref, yd_ref, *, lam_init):
    past = kp_ref.shape[0] // HEADS_PER_BRANCH
    tokens = mk_ref.shape[0] // HEADS_PER_BRANCH
    lam = _diff_lambda(lam_ref, lam_init)
    for h in range(HEADS_PER_BRANCH):
        cols = slice(h * HEAD_W, (h + 1) * HEAD_W)
        q = q_ref[:, cols] * (DIFF_D ** -0.5 * LOG2E)
        lane = lax.broadcasted_iota(jnp.int32, q.shape, 1)
        qs = (jnp.where(lane < DIFF_D, q, 0.0).astype(BF16), jnp.where(lane >= DIFF_D, q, 0.0).astype(BF16))
        kp = _head_rows(kp_ref, h, past).astype(BF16)
        vp = _head_rows(vp_ref, h, past).astype(BF16)
        kn = kn_ref[:, cols].astype(BF16)
        vn = vn_ref[:, cols].astype(BF16)
        outs = []
        for qc in qs:
            sp = _dot_nt(qc, kp) + bp_ref[h]
            sn = _dot_nt(qc, kn) + bn_ref[h]
            mx = jnp.maximum(jnp.max(sp, -1, keepdims=True), jnp.max(sn, -1, keepdims=True))
            pp = jnp.exp2(sp - mx)
            pn = jnp.exp2(sn - mx)
            den = jnp.sum(pp, -1, keepdims=True) + jnp.sum(pn, -1, keepdims=True)
            outs.append((_dot(pp.astype(BF16), vp) + _dot(pn.astype(BF16), vn)) / den)
        o = outs[0] - lam * outs[1]
        o = o * lax.rsqrt(jnp.mean(o * o, -1, keepdims=True) + LN_EPS) * g_ref[...]
        yc_ref[:, cols] = (o * (1.0 - lam_init)).astype(BF16)

        mk = _head_rows(mk_ref, h, tokens).astype(BF16)
        mv = _head_rows(mv_ref, h, tokens).astype(BF16)
        s = _dot_nt(mq_ref[:, cols].astype(BF16), mk) * (MEM_HD ** -0.5)
        p = jnp.exp(s - jnp.max(s, -1, keepdims=True))
        den = jnp.sum(p, -1, keepdims=True)
        yd_ref[:, cols] = (_dot(p.astype(BF16), mv) / den).astype(BF16)


def _sample_attention(proj, k_past, v_past, mem_k, mem_v, bias_past, bias_new, diff_lambda, subln_g, layer,
                      *, bsz, length, lam_init):
    m = proj.shape[0]
    col = lambda c: pl.BlockSpec((length, BRANCH_W), lambda b: (b, c))
    cache = lambda a: pl.BlockSpec((None, None) + a.shape[2:], lambda b: (layer, b, 0, 0))
    whole = lambda a: pl.BlockSpec(a.shape, lambda b: (0,) * a.ndim)
    out = pl.BlockSpec((length, BRANCH_W), lambda b: (b, 0))
    return pl.pallas_call(
        functools.partial(_sample_attn_kernel, lam_init=lam_init),
        grid=(bsz,),
        in_specs=[
            col(COL_DQ), col(COL_DK), col(COL_DV), col(COL_MQ),
            cache(k_past), cache(v_past), cache(mem_k), cache(mem_v),
            whole(bias_past), whole(bias_new),
            pl.BlockSpec((None, 4, DIFF_D), lambda b: (layer, 0, 0)),
            pl.BlockSpec((None, 1, DIFF_DV), lambda b: (layer, 0, 0)),
        ],
        out_specs=[out, out],
        out_shape=[jax.ShapeDtypeStruct((m, BRANCH_W), BF16), jax.ShapeDtypeStruct((m, BRANCH_W), BF16)],
        compiler_params=_params(("parallel",), 40),
        name="sample_attention",
    )(proj, proj, proj, proj, k_past, v_past, mem_k, mem_v, bias_past, bias_new, diff_lambda, subln_g)


def _merge_kernel(h_ref, ya_ref, yb_ref, yc_ref, yd_ref, wg_ref, wb_ref, bg_ref, o_ref):
    h = h_ref[...]
    acc = None
    for i, y_ref in enumerate((ya_ref, yb_ref, yc_ref, yd_ref)):
        gate = jax.nn.sigmoid(_dot(h, wg_ref[i]) + bg_ref[i])
        term = gate * _dot(y_ref[...], wb_ref[i])
        acc = term if acc is None else acc + term
    o_ref[...] = acc.astype(BF16)


def _merge(h16, ys, w_gate, w_branch, b_gate, layer, *, tm, tn):
    m, d = h16.shape
    assert m % tm == 0 and d % tn == 0
    y_spec = pl.BlockSpec((tm, BRANCH_W), lambda i, j: (i, 0))
    return pl.pallas_call(
        _merge_kernel,
        grid=(m // tm, d // tn),
        in_specs=[
            pl.BlockSpec((tm, d), lambda i, j: (i, 0)), y_spec, y_spec, y_spec, y_spec,
            pl.BlockSpec((None, N_BRANCH, d, tn), lambda i, j: (0, 0, 0, j)),
            pl.BlockSpec((None, N_BRANCH, BRANCH_W, tn), lambda i, j: (0, 0, 0, j)),
            pl.BlockSpec((None, N_BRANCH, 1, tn), lambda i, j: (layer, 0, 0, j)),
        ],
        out_specs=pl.BlockSpec((tm, tn), lambda i, j: (i, j)),
        out_shape=jax.ShapeDtypeStruct((m, d), BF16),
        compiler_params=_params(("parallel", "parallel"), 56),
        name="gated_merge",
    )(h16, *ys, w_gate, w_branch, b_gate)


def _proj_ln_kernel(m_ref, x_ref, w_ref, g_ref, b_ref, o_ref, *, alpha):
    for r in range(0, o_ref.shape[0], PROJ_ROW_CHUNK):
        rows = slice(r, r + PROJ_ROW_CHUNK)
        y = alpha * x_ref[rows, :] + _dot(m_ref[rows, :], w_ref[...])
        o_ref[rows, :] = _layer_norm(y, g_ref[...], b_ref[...])


def _proj_ln(merged, x, w_o, g, b, layer, *, alpha, tm):
    m, d = x.shape
    assert m % tm == 0
    row = pl.BlockSpec((tm, d), lambda i: (i, 0))
    vec = pl.BlockSpec((None, 1, d), lambda i: (layer, 0, 0))
    return pl.pallas_call(
        functools.partial(_proj_ln_kernel, alpha=alpha),
        grid=(m // tm,),
        in_specs=[row, row, pl.BlockSpec((None, d, d), lambda i: (0, 0, 0)), vec, vec],
        out_specs=row,
        out_shape=jax.ShapeDtypeStruct((m, d), F32),
        compiler_params=_params(("parallel",), 48),
        name="proj_ln",
    )(merged, x, w_o, g, b)


def _rope_tables(pos):
    half = RET_DK // 2
    inv = ROPE_BASE ** (-jnp.arange(half, dtype=F32) / half)
    ang = pos.astype(F32)[:, None] * inv[None, :]
    cos = jnp.cos(ang)
    sin = jnp.sin(ang)
    return jnp.concatenate([cos, cos], -1), jnp.concatenate([-sin, sin], -1)


def _t5_bucket(rel):
    nb = REL_BUCKETS // 2
    max_exact = nb // 2
    n = jnp.abs(rel)
    nf = jnp.maximum(n, 1).astype(F32)
    large = max_exact + (jnp.log(nf / max_exact) / math.log(REL_MAX_DIST / max_exact)
                         * (nb - max_exact)).astype(jnp.int32)
    large = jnp.minimum(large, nb - 1)
    return jnp.where(rel > 0, nb, 0) + jnp.where(n < max_exact, n, large)


def _masked_bias(q_pos, k_pos, rel_bias):
    bucket = _t5_bucket(k_pos[None, :] - q_pos[:, None])
    onehot = bucket[None, None] == jnp.arange(REL_BUCKETS, dtype=bucket.dtype)[None, :, None, None]
    bias = jnp.sum(jnp.where(onehot, rel_bias.astype(F32).T[:, :, None, None], 0.0), axis=1)
    allowed = (k_pos[None, :] // CHUNK) <= (q_pos[:, None] // CHUNK)
    return jnp.where(allowed[None], bias * LOG2E, NEG_INF)


def _prompt_bias_tiles(rel_bias, t):
    base = BIAS_BASE
    assert base % CHUNK == 0 and base + 1 >= REL_MAX_DIST and t % base == 0
    n = t // base
    pos = jnp.arange(base, dtype=jnp.int32)
    diag, sub, far = (jnp.swapaxes(_masked_bias(pos + d * base, pos, rel_bias), -1, -2) for d in range(3))
    shift = far[:, :1, :1]
    masked = jnp.full_like(diag, NEG_INF)
    pick = lambda delta: masked if delta > 0 else (diag - shift, sub - shift, far - shift)[min(-delta, 2)]
    tile = lambda d: jnp.concatenate(
        [jnp.concatenate([pick(kk - qq - d * n) for qq in range(n)], axis=-1) for kk in range(n)], axis=-2)
    return jnp.stack([tile(d) for d in range(FAR_TILE)], axis=1)


def _kv_export_kernel(k_ref, v_ref, *rest):
    ok_ref, ov_ref = rest[-2:]
    n = k_ref.shape[0]
    for h in range(HEADS_PER_BRANCH):
        cols = slice(h * HEAD_W, (h + 1) * HEAD_W)
        rows = pl.ds(h, n, stride=HEADS_PER_BRANCH)
        ok_ref[rows, :] = k_ref[:, cols]
        ov_ref[rows, :] = v_ref[:, cols]


def _kv_export(proj, prev, layer, depth, *, tm):
    m = proj.shape[0]
    assert m % tm == 0
    col = lambda c: pl.BlockSpec((tm, BRANCH_W), lambda i: (i, c))
    out = pl.BlockSpec((None, tm * HEADS_PER_BRANCH, HEAD_W), lambda i: (layer, i, 0))
    shape = jax.ShapeDtypeStruct((depth, m * HEADS_PER_BRANCH, HEAD_W), F32)
    keep = [pl.BlockSpec(memory_space=pl.ANY)] * len(prev)
    return pl.pallas_call(
        _kv_export_kernel,
        grid=(m // tm,),
        in_specs=[col(COL_DK), col(COL_DV)] + keep,
        out_specs=[out, out],
        out_shape=[shape, shape],
        input_output_aliases={2 + n: n for n in range(len(prev))},
        compiler_params=_params(("parallel",), 32),
        name="kv_export",
    )(proj, proj, *prev)


CAST_IN_FFN1 = (('w_in', False), ('ffn2_w_up', False), ('ffn2_w_down', True))
CAST_IN_PROJ = (('w_gate', False), ('w_branch', False), ('w_o', False))
CAST_IN_FFN2 = (('ffn1_w_up', False), ('ffn1_w_down', True))


def _encoder_layer(x, w16, p, layer, *, alpha, cfg, mixers_fn, w32=None, w16_next=None):
    d = x.shape[1]
    casts = lambda names, l: tuple((w32[n], l, swap) for n, swap in names) if w32 is not None else ()
    x1, done = _ffn_ln(x, w16['ffn1_w_up'], w16['ffn1_w_down'], p['ln1_g'], p['ln1_b'], layer,
                       alpha=alpha, tm=cfg['tm_ffn'], tf=cfg['tf'], casts=casts(CAST_IN_FFN1, layer))
    w16.update(zip((n for n, _ in CAST_IN_FFN1), done))
    proj, x1b, done = _in_proj(x1, w16['w_in'], tm=cfg['tm_in'], tn=cfg['tn_in'], casts=casts(CAST_IN_PROJ, layer))
    w16.update(zip((n for n, _ in CAST_IN_PROJ), done))
    y_a, y_b, y_c, y_d, conv_new, ret_new = mixers_fn(proj, 0.8 - 0.6 * math.exp(-0.3 * layer))
    merged = _merge(x1b, (y_a, y_b, y_c, y_d), w16['w_gate'].reshape(1, N_BRANCH, d, d),
                    w16['w_branch'].reshape(1, N_BRANCH, BRANCH_W, d), p['b_gate'], layer,
                    tm=cfg['tm_merge'], tn=cfg['tn_merge'])
    x2 = _proj_ln(merged, x1, w16['w_o'], p['ln2_g'], p['ln2_b'], layer, alpha=alpha, tm=cfg['tm_proj'])
    x3, done = _ffn_ln(x2, w16['ffn2_w_up'], w16['ffn2_w_down'], p['ln3_g'], p['ln3_b'], layer,
                       alpha=alpha, tm=cfg['tm_ffn'], tf=cfg['tf'],
                       casts=casts(CAST_IN_FFN2, layer + 1) if w16_next is not None else ())
    if w16_next is not None:
        w16_next.update(zip((n for n, _ in CAST_IN_FFN2), done))
    return x3, proj, conv_new, ret_new


PROMPT_CFG = dict(tm_ffn=1024, tf=512, tm_in=1024, tn_in=1408, mix_rows=256, ret_chunk=256, t_diff=512,
                  tm_merge=1024, tn_merge=512, tm_proj=512, tm_export=1024)
SAMPLE_CFG = dict(tm_ffn=256, tf=512, tm_in=256, tn_in=512, t_conv=16, ret_chunk=16,
                  tm_merge=256, tn_merge=256, tm_proj=256, tm_export=256)


def kernel(x_prompt, x_sample, state_conv, state_ret, cache_diff_k, cache_diff_v, cache_mem_k, cache_mem_v,
           mem_prompt, ffn1_w_up, ffn1_w_down, ln1_g, ln1_b, w_in, conv_w, ret_gn_g, diff_lambda,
           diff_subln_g, w_mem_kv, w_branch, w_gate, b_gate, w_o, ln2_g, ln2_b, ffn2_w_up, ffn2_w_down,
           ln3_g, ln3_b, rel_bias):
    bp, lp, d = x_prompt.shape
    bs, ls, _ = x_sample.shape
    depth = w_in.shape[0]
    past = cache_diff_k.shape[2]
    mem_tokens = mem_prompt.shape[1]
    alpha = (2 * depth) ** 0.25
    half = MEM_HEADS * MEM_HD

    pos_p = jnp.arange(lp, dtype=jnp.int32)
    pos_s = past + jnp.arange(ls, dtype=jnp.int32)
    past_pos = jnp.arange(past, dtype=jnp.int32)
    rope_p = _rope_tables(pos_p)
    rope_s = _rope_tables(pos_s)
    bias_p = _prompt_bias_tiles(rel_bias, PROMPT_CFG['t_diff'])
    bias_s_past = _masked_bias(pos_s, past_pos, rel_bias)
    bias_s_new = _masked_bias(pos_s, pos_s, rel_bias)

    vec = lambda v: v.reshape(depth, 1, v.shape[-1])
    p = {'ln1_g': vec(ln1_g), 'ln1_b': vec(ln1_b), 'ln2_g': vec(ln2_g), 'ln2_b': vec(ln2_b),
         'ln3_g': vec(ln3_g), 'ln3_b': vec(ln3_b), 'b_gate': b_gate.reshape(depth, N_BRANCH, 1, d)}
    gn_g = vec(ret_gn_g)
    w32 = {'ffn1_w_up': ffn1_w_up, 'ffn1_w_down': ffn1_w_down, 'w_in': w_in,
           'w_gate': w_gate.reshape(depth, N_BRANCH * d, d), 'w_branch': w_branch.reshape(depth, N_BRANCH * BRANCH_W, d),
           'w_o': w_o, 'ffn2_w_up': ffn2_w_up, 'ffn2_w_down': ffn2_w_down}
    w16 = [dict() for _ in range(depth)]
    w16[0].update({n: w32[n][:1].astype(BF16) for n, _ in CAST_IN_FFN2})
    w_mem16 = w_mem_kv.astype(BF16)
    subln_col = diff_subln_g.reshape(depth, DIFF_DV, 1)
    subln_row = diff_subln_g.reshape(depth, 1, DIFF_DV)
    head_major = lambda c: c.reshape(depth, bs, c.shape[2] * c.shape[3], c.shape[4])
    kpast, vpast = head_major(cache_diff_k), head_major(cache_diff_v)
    mem_k_s, mem_v_s = head_major(cache_mem_k), head_major(cache_mem_v)

    yp = x_prompt.reshape(bp * lp, d)
    ys = x_sample.reshape(bs * ls, d)
    mem16 = mem_prompt.reshape(bp * mem_tokens, d).astype(BF16)
    zero_conv = jnp.zeros((bp, CONV_WIDTH - 1, BRANCH_W), F32)
    zero_ret = jnp.zeros((bp, RET_HEADS, RET_DK, RET_DK), F32)

    conv_p, ret_p, mk_p, mv_p, conv_s, ret_s = [], [], [], [], [], []
    kv_p, kv_s = [], []
    for l in range(depth):
        mkv = _matmul(mem16, w_mem16, l, tm=bp * mem_tokens, tn=512).reshape(bp, mem_tokens, 2 * half)

        def mixers_p(proj, lam_init):
            y_a, y_b, y_d, c_new, r_new = _local_mixers(
                proj, conv_w, zero_conv, rope_p[0], rope_p[1], zero_ret, gn_g, mkv, l,
                bsz=bp, length=lp, rows=PROMPT_CFG['mix_rows'], chunk=PROMPT_CFG['ret_chunk'])
            y_c = _diff_prompt(proj, bias_p, diff_lambda, subln_col, l, bsz=bp, length=lp,
                               t=PROMPT_CFG['t_diff'], lam_init=lam_init)
            return y_a, y_b, y_c, y_d, c_new, r_new

        yp, proj, c_new, r_new = _encoder_layer(
            yp, w16[l], p, l, alpha=alpha, cfg=PROMPT_CFG, mixers_fn=mixers_p,
            w32=w32, w16_next=w16[l + 1] if l + 1 < depth else None)
        kv_p = _kv_export(proj, kv_p, l, depth, tm=PROMPT_CFG['tm_export'])
        conv_p.append(c_new)
        ret_p.append(r_new)
        mk_p.append(mkv[:, :, :half].reshape(bp, mem_tokens, MEM_HEADS, MEM_HD))
        mv_p.append(mkv[:, :, half:].reshape(bp, mem_tokens, MEM_HEADS, MEM_HD))

        def mixers_s(proj, lam_init):
            y_a, c_new = _conv_branch(proj, conv_w, state_conv[l], l, bsz=bs, length=ls, tl=SAMPLE_CFG['t_conv'])
            y_b, r_new = _retention_branch(proj, rope_s[0], rope_s[1], state_ret[l], gn_g, l,
                                           bsz=bs, length=ls, chunk=SAMPLE_CFG['ret_chunk'])
            y_c, y_d = _sample_attention(proj, kpast, vpast, mem_k_s, mem_v_s, bias_s_past, bias_s_new,
                                         diff_lambda, subln_row, l, bsz=bs, length=ls, lam_init=lam_init)
            return y_a, y_b, y_c, y_d, c_new, r_new

        ys, proj, c_new, r_new = _encoder_layer(ys, w16[l], p, l, alpha=alpha, cfg=SAMPLE_CFG, mixers_fn=mixers_s)
        kv_s = _kv_export(proj, kv_s, l, depth, tm=SAMPLE_CFG['tm_export'])
        conv_s.append(c_new)
        ret_s.append(r_new)

    kv_shape = lambda b, n: (depth, b, n, DIFF_HEADS, DIFF_DV)
    return (yp.reshape(bp, lp, d), ys.reshape(bs, ls, d), jnp.stack(conv_p), jnp.stack(ret_p),
            kv_p[0].reshape(kv_shape(bp, lp)), kv_p[1].reshape(kv_shape(bp, lp)),
            jnp.stack(mk_p), jnp.stack(mv_p), jnp.stack(conv_s), jnp.stack(ret_s),
            kv_s[0].reshape(kv_shape(bs, ls)), kv_s[1].reshape(kv_shape(bs, ls)))
```

```python
import functools
import math

import jax
import jax.numpy as jnp
from jax import lax
from jax.experimental import pallas as pl
from jax.experimental.pallas import tpu as pltpu

F32 = jnp.float32
BF16 = jnp.bfloat16

CHUNK = 64
CONV_WIDTH = 3
RET_HEADS = 4
RET_DK = 128
DIFF_HEADS = 4
DIFF_D = 64
DIFF_DV = 128
MEM_HEADS = 4
MEM_HD = 128
BRANCH_W = 512
N_BRANCH = 4
REL_BUCKETS = 32
REL_MAX_DIST = 128
LN_EPS = 1e-5
ROPE_BASE = 10000.0
NEG_INF = -1e30
HEAD_W = 128
LOG2E = math.log2(math.e)
LN_ROW_CHUNK = 128
PROJ_ROW_CHUNK = 256
BIAS_BASE = 256
FAR_TILE = 2
FFN_UP_CHUNK = 256
FFN_DOWN_CHUNK = 512

COL_CB, COL_CC, COL_CH, COL_RQ, COL_RK, COL_RV, COL_RG, COL_DQ, COL_DK, COL_DV, COL_MQ = range(11)
HEADS_PER_BRANCH = BRANCH_W // HEAD_W

LANES = 128
F32_SUBLANES = 8
BF16_SUBLANES = 16
V7X_VMEM_BYTES = 64 * 1024 * 1024
MIB = 1024 * 1024


def _params(semantics, vmem_mib):
    assert vmem_mib * MIB < V7X_VMEM_BYTES
    return pltpu.CompilerParams(dimension_semantics=semantics, vmem_limit_bytes=vmem_mib * MIB)


def _layer_norm(y, g, b):
    mu = jnp.mean(y, -1, keepdims=True)
    d = y - mu
    var = jnp.mean(d * d, -1, keepdims=True)
    return d * lax.rsqrt(var + LN_EPS) * g + b


def _dot(a, b):
    return jnp.dot(a, b, preferred_element_type=F32)


def _dot_nt(a, b):
    return lax.dot_general(a, b, (((1,), (1,)), ((), ())), preferred_element_type=F32)


def _dot_tn(a, b):
    return lax.dot_general(a, b, (((0,), (0,)), ((), ())), preferred_element_type=F32)


def _side_cast_specs(casts, grid):
    in_specs, out_specs, out_shapes = [], [], []
    for src, layer, split in casts:
        _, r, c = src.shape
        if split == 'flat':
            gr, gc = math.prod(grid), 1

            def pick(*ids, grid=grid):
                flat = ids[0]
                for size, idx in zip(grid[1:], ids[1:]):
                    flat = flat * size + idx
                return (flat, 0)
        elif split == 'cr':
            gr, gc = grid[1], grid[0]
            pick = lambda i, j: (j, i)
        else:
            assert split == 'rc'
            gr, gc = grid
            pick = lambda i, j: (i, j)
        assert r % gr == 0 and c % gc == 0
        blk = (None, r // gr, c // gc)
        assert blk[1] % BF16_SUBLANES == 0 and blk[2] % LANES == 0
        in_specs.append(pl.BlockSpec(blk, lambda *ids, layer=layer, pick=pick: (layer,) + pick(*ids)))
        out_specs.append(pl.BlockSpec(blk, lambda *ids, pick=pick: (0,) + pick(*ids)))
        out_shapes.append(jax.ShapeDtypeStruct((1, r, c), BF16))
    return in_specs, out_specs, out_shapes


def _side_cast(src_refs, dst_refs):
    for src_ref, dst_ref in zip(src_refs, dst_refs):
        dst_ref[...] = src_ref[...].astype(BF16)


def _ffn_ln_kernel(x_ref, wa_ref, wb_ref, wd_ref, g_ref, b_ref, *rest, alpha, n_cast):
    cast_src, o_ref, cast_dst = rest[:n_cast], rest[n_cast], rest[n_cast + 1:2 * n_cast + 1]
    xb_ref, h_ref = rest[2 * n_cast + 1:]
    j = pl.program_id(1)

    @pl.when(j == 0)
    def _():
        xb_ref[...] = x_ref[...].astype(BF16)
        o_ref[...] = jnp.zeros(o_ref.shape, F32)

    xb = xb_ref[...]
    for c in range(0, h_ref.shape[1], FFN_UP_CHUNK):
        cols = slice(c, c + FFN_UP_CHUNK)
        a = _dot(xb, wa_ref[:, cols])
        b = _dot(xb, wb_ref[:, cols])
        h_ref[:, cols] = (a * jax.nn.sigmoid(a) * b).astype(BF16)
    h = h_ref[...]
    for c in range(0, o_ref.shape[1], FFN_DOWN_CHUNK):
        cols = slice(c, c + FFN_DOWN_CHUNK)
        o_ref[:, cols] += _dot(h, wd_ref[:, cols])
    _side_cast(cast_src, cast_dst)

    @pl.when(j == pl.num_programs(1) - 1)
    def _():
        for r in range(0, o_ref.shape[0], LN_ROW_CHUNK):
            rows = slice(r, r + LN_ROW_CHUNK)
            o_ref[rows, :] = _layer_norm(alpha * x_ref[rows, :] + 0.5 * o_ref[rows, :], g_ref[...], b_ref[...])


def _ffn_ln(x, w_up, w_down, g, b, layer, *, alpha, tm, tf, casts=()):
    m, d = x.shape
    d_ff = w_down.shape[1]
    nj = d_ff // tf
    assert m % tm == 0 and d_ff % tf == 0
    grid = (m // tm, nj)
    row = lambda i, j: (i, 0)
    vec = pl.BlockSpec((None, 1, d), lambda i, j: (layer, 0, 0))
    cast_in, cast_out, cast_shapes = _side_cast_specs(casts, grid)
    out = pl.pallas_call(
        functools.partial(_ffn_ln_kernel, alpha=alpha, n_cast=len(casts)),
        grid=grid,
        in_specs=[
            pl.BlockSpec((tm, d), row),
            pl.BlockSpec((None, d, tf), lambda i, j: (0, 0, j)),
            pl.BlockSpec((None, d, tf), lambda i, j: (0, 0, j + nj)),
            pl.BlockSpec((None, tf, d), lambda i, j: (0, j, 0)),
            vec, vec,
        ] + cast_in,
        out_specs=[pl.BlockSpec((tm, d), row)] + cast_out,
        out_shape=[jax.ShapeDtypeStruct((m, d), F32)] + cast_shapes,
        scratch_shapes=[pltpu.VMEM((tm, d), BF16), pltpu.VMEM((tm, tf), BF16)],
        compiler_params=_params(("parallel", "arbitrary"), 60),
        name="ffn_ln",
    )(x, w_up, w_up, w_down, g, b, *(c[0] for c in casts))
    return out[0], out[1:]


def _in_proj_kernel(x_ref, w_ref, *rest, n_cast):
    cast_src, o_ref, xb_ref, cast_dst = rest[:n_cast], rest[n_cast], rest[n_cast + 1], rest[n_cast + 2:]

    @pl.when(pl.program_id(1) == 0)
    def _():
        xb_ref[...] = x_ref[...].astype(BF16)

    o_ref[...] = _dot(xb_ref[...], w_ref[...])
    _side_cast(cast_src, cast_dst)


def _in_proj(x, w, *, tm, tn, casts=()):
    m, k = x.shape
    n = w.shape[2]
    assert m % tm == 0 and n % tn == 0
    grid = (m // tm, n // tn)
    cast_in, cast_out, cast_shapes = _side_cast_specs(casts, grid)
    out = pl.pallas_call(
        functools.partial(_in_proj_kernel, n_cast=len(casts)),
        grid=grid,
        in_specs=[pl.BlockSpec((tm, k), lambda i, j: (i, 0)),
                  pl.BlockSpec((None, k, tn), lambda i, j: (0, 0, j))] + cast_in,
        out_specs=[pl.BlockSpec((tm, tn), lambda i, j: (i, j)), pl.BlockSpec((tm, k), lambda i, j: (i, 0))] + cast_out,
        out_shape=[jax.ShapeDtypeStruct((m, n), F32), jax.ShapeDtypeStruct((m, k), BF16)] + cast_shapes,
        compiler_params=_params(("parallel", "arbitrary"), 60),
        name="in_proj",
    )(x, w, *(c[0] for c in casts))
    return out[0], out[1], out[2:]


def _matmul_kernel(x_ref, w_ref, o_ref):
    o_ref[...] = _dot(x_ref[...], w_ref[...])


def _matmul(x, w, layer, *, tm, tn):
    m, k = x.shape
    n = w.shape[2]
    assert m % tm == 0 and n % tn == 0
    return pl.pallas_call(
        _matmul_kernel,
        grid=(m // tm, n // tn),
        in_specs=[pl.BlockSpec((tm, k), lambda i, j: (i, 0)),
                  pl.BlockSpec((None, k, tn), lambda i, j: (layer, 0, j))],
        out_specs=pl.BlockSpec((tm, tn), lambda i, j: (i, j)),
        out_shape=jax.ShapeDtypeStruct((m, n), F32),
        compiler_params=_params(("parallel", "parallel"), 48),
        name="matmul",
    )(x, w)


def _conv_body(cb_ref, cc_ref, ch_ref, w_ref, carry_ref, y_ref):
    u = cc_ref[...] * ch_ref[...]
    row = lax.broadcasted_iota(jnp.int32, u.shape, 0)
    c0 = carry_ref[0:1, :]
    c1 = carry_ref[1:2, :]
    u1 = jnp.where(row == 0, c1, pltpu.roll(u, 1, 0))
    u2 = jnp.where(row == 0, c0, jnp.where(row == 1, c1, pltpu.roll(u, 2, 0)))
    z = w_ref[0:1, :] * u2 + w_ref[1:2, :] * u1 + w_ref[2:3, :] * u
    y_ref[...] = (cb_ref[...] * z).astype(BF16)
    tail = u[u.shape[0] - F32_SUBLANES:, :][F32_SUBLANES - (CONV_WIDTH - 1):, :]
    carry_ref[...] = tail
    return tail


def _conv_kernel(cb_ref, cc_ref, ch_ref, w_ref, prev_ref, y_ref, state_ref, carry_ref):
    l = pl.program_id(1)

    @pl.when(l == 0)
    def _():
        carry_ref[...] = prev_ref[...]

    tail = _conv_body(cb_ref, cc_ref, ch_ref, w_ref, carry_ref, y_ref)

    @pl.when(l == pl.num_programs(1) - 1)
    def _():
        state_ref[...] = tail


def _conv_branch(proj, conv_w, prev, layer, *, bsz, length, tl):
    m = proj.shape[0]
    nl = length // tl
    assert length % tl == 0 and tl >= F32_SUBLANES
    col = lambda c: pl.BlockSpec((tl, BRANCH_W), lambda b, l: (b * nl + l, c))
    return pl.pallas_call(
        _conv_kernel,
        grid=(bsz, nl),
        in_specs=[
            col(COL_CB), col(COL_CC), col(COL_CH),
            pl.BlockSpec((None, CONV_WIDTH, BRANCH_W), lambda b, l: (layer, 0, 0)),
            pl.BlockSpec((None, CONV_WIDTH - 1, BRANCH_W), lambda b, l: (b, 0, 0)),
        ],
        out_specs=[
            pl.BlockSpec((tl, BRANCH_W), lambda b, l: (b * nl + l, 0)),
            pl.BlockSpec((None, CONV_WIDTH - 1, BRANCH_W), lambda b, l: (b, 0, 0)),
        ],
        out_shape=[
            jax.ShapeDtypeStruct((m, BRANCH_W), BF16),
            jax.ShapeDtypeStruct((bsz, CONV_WIDTH - 1, BRANCH_W), F32),
        ],
        scratch_shapes=[pltpu.VMEM((CONV_WIDTH - 1, BRANCH_W), F32)],
        compiler_params=_params(("parallel", "arbitrary"), 32),
        name="conv_branch",
    )(proj, proj, proj, conv_w, prev)


def _ret_log_gamma(h):
    return math.log1p(-(2.0 ** (-5.0 - h)))


def _ret_fill_decay(intra_ref):
    chunk = intra_ref.shape[1]
    ri = lax.broadcasted_iota(jnp.int32, (chunk, chunk), 0)
    ci = lax.broadcasted_iota(jnp.int32, (chunk, chunk), 1)
    rel = (ri - ci).astype(F32)
    for h in range(RET_HEADS):
        intra_ref[h] = jnp.where(rel >= 0.0, jnp.exp(jnp.maximum(rel, 0.0) * _ret_log_gamma(h)), 0.0)


def _retention_body(q_ref, k_ref, v_ref, g_ref, cos_ref, sin_ref, gn_ref, y_ref, s_ref, intra_ref, rows):
    chunk = intra_ref.shape[1]
    cosf = cos_ref[rows, :]
    sinf = sin_ref[rows, :]
    idx = lax.broadcasted_iota(jnp.int32, (chunk, 1), 0).astype(F32)
    for h in range(RET_HEADS):
        log_g = _ret_log_gamma(h)
        cols = slice(h * RET_DK, (h + 1) * RET_DK)
        q = q_ref[rows, cols]
        k = k_ref[rows, cols]
        q = q * cosf + pltpu.roll(q, RET_DK // 2, 1) * sinf
        k = (k * cosf + pltpu.roll(k, RET_DK // 2, 1) * sinf) * (RET_DK ** -0.5)
        vb = v_ref[rows, cols].astype(BF16)
        q_dec = jnp.exp((idx + 1.0) * log_g)
        k_dec = jnp.exp((chunk - 1.0 - idx) * log_g)
        c_dec = math.exp(chunk * log_g)
        qb = q.astype(BF16)
        s_prev = s_ref[h]
        att = _dot_nt(qb, k.astype(BF16)) * intra_ref[h]
        o = _dot(att.astype(BF16), vb) + _dot(qb, s_prev.astype(BF16)) * q_dec
        s_ref[h] = s_prev * c_dec + _dot_tn((k * k_dec).astype(BF16), vb)
        mu = jnp.mean(o, -1, keepdims=True)
        d = o - mu
        var = jnp.mean(d * d, -1, keepdims=True)
        ro = d * lax.rsqrt(var + LN_EPS) * gn_ref[:, cols]
        gate = g_ref[rows, cols]
        y_ref[rows, cols] = (gate * jax.nn.sigmoid(gate) * ro).astype(BF16)


def _retention_kernel(q_ref, k_ref, v_ref, g_ref, cos_ref, sin_ref, s0_ref, gn_ref, y_ref, sfin_ref,
                      s_ref, intra_ref):
    c = pl.program_id(1)

    @pl.when(jnp.logical_and(pl.program_id(0) == 0, c == 0))
    def _():
        _ret_fill_decay(intra_ref)

    @pl.when(c == 0)
    def _():
        s_ref[...] = s0_ref[...]

    _retention_body(q_ref, k_ref, v_ref, g_ref, cos_ref, sin_ref, gn_ref, y_ref, s_ref, intra_ref,
                    slice(0, q_ref.shape[0]))

    @pl.when(c == pl.num_programs(1) - 1)
    def _():
        sfin_ref[...] = s_ref[...]


def _retention_branch(proj, cosf, sinf, s0, gn_g, layer, *, bsz, length, chunk):
    m = proj.shape[0]
    nc = length // chunk
    assert length % chunk == 0
    col = lambda c: pl.BlockSpec((chunk, BRANCH_W), lambda b, i: (b * nc + i, c))
    state = pl.BlockSpec((None, RET_HEADS, RET_DK, RET_DK), lambda b, i: (b, 0, 0, 0))
    return pl.pallas_call(
        _retention_kernel,
        grid=(bsz, nc),
        in_specs=[
            col(COL_RQ), col(COL_RK), col(COL_RV), col(COL_RG),
            pl.BlockSpec((chunk, RET_DK), lambda b, i: (i, 0)),
            pl.BlockSpec((chunk, RET_DK), lambda b, i: (i, 0)),
            state,
            pl.BlockSpec((None, 1, BRANCH_W), lambda b, i: (layer, 0, 0)),
        ],
        out_specs=[pl.BlockSpec((chunk, BRANCH_W), lambda b, i: (b * nc + i, 0)), state],
        out_shape=[
            jax.ShapeDtypeStruct((m, BRANCH_W), BF16),
            jax.ShapeDtypeStruct((bsz, RET_HEADS, RET_DK, RET_DK), F32),
        ],
        scratch_shapes=[pltpu.VMEM((RET_HEADS, RET_DK, RET_DK), F32),
                        pltpu.VMEM((RET_HEADS, chunk, chunk), F32)],
        compiler_params=_params(("arbitrary", "arbitrary"), 32),
        name="retention_branch",
    )(proj, proj, proj, proj, cosf, sinf, s0, gn_g)


def _mem_head(q, mk, mv):
    s = _dot_nt(q.astype(BF16), mk.astype(BF16)) * (MEM_HD ** -0.5)
    p = jnp.exp(s - jnp.max(s, -1, keepdims=True))
    den = jnp.sum(p, -1, keepdims=True)
    return _dot(p.astype(BF16), mv.astype(BF16)) / den


def _local_mixers_kernel(x_ref, mq_ref, w_ref, prev_ref, cos_ref, sin_ref, s0_ref, gn_ref, mk_ref, mv_ref,
                         ya_ref, yb_ref, yd_ref, conv_state_ref, sfin_ref,
                         carry_ref, s_ref, intra_ref):
    cb_ref, cc_ref, ch_ref, rq_ref, rk_ref, rv_ref, rg_ref = (
        x_ref.at[:, pl.ds(col * BRANCH_W, BRANCH_W)]
        for col in (COL_CB, COL_CC, COL_CH, COL_RQ, COL_RK, COL_RV, COL_RG))
    c = pl.program_id(1)

    @pl.when(jnp.logical_and(pl.program_id(0) == 0, c == 0))
    def _():
        _ret_fill_decay(intra_ref)

    @pl.when(c == 0)
    def _():
        carry_ref[...] = prev_ref[...]
        s_ref[...] = s0_ref[...]

    tail = _conv_body(cb_ref, cc_ref, ch_ref, w_ref, carry_ref, ya_ref)
    chunk = intra_ref.shape[1]
    for r in range(0, rq_ref.shape[0], chunk):
        _retention_body(rq_ref, rk_ref, rv_ref, rg_ref, cos_ref, sin_ref, gn_ref, yb_ref, s_ref, intra_ref,
                        slice(r, r + chunk))
    for h in range(MEM_HEADS):
        cols = slice(h * MEM_HD, (h + 1) * MEM_HD)
        yd_ref[:, cols] = _mem_head(mq_ref[:, cols], mk_ref[:, cols], mv_ref[:, cols]).astype(BF16)

    @pl.when(c == pl.num_programs(1) - 1)
    def _():
        conv_state_ref[...] = tail
        sfin_ref[...] = s_ref[...]


def _local_mixers(proj, conv_w, conv_prev, cosf, sinf, s0, gn_g, mkv, layer, *, bsz, length, rows, chunk):
    m = proj.shape[0]
    nc = length // rows
    assert length % rows == 0 and rows % chunk == 0 and chunk >= F32_SUBLANES
    tokens = mkv.shape[1]
    col = lambda c: pl.BlockSpec((rows, BRANCH_W), lambda b, i: (b * nc + i, c))
    out = pl.BlockSpec((rows, BRANCH_W), lambda b, i: (b * nc + i, 0))
    state = pl.BlockSpec((None, RET_HEADS, RET_DK, RET_DK), lambda b, i: (b, 0, 0, 0))
    conv_state = pl.BlockSpec((None, CONV_WIDTH - 1, BRANCH_W), lambda b, i: (b, 0, 0))
    rope = pl.BlockSpec((rows, RET_DK), lambda b, i: (i, 0))
    y_shape = jax.ShapeDtypeStruct((m, BRANCH_W), BF16)
    return pl.pallas_call(
        _local_mixers_kernel,
        grid=(bsz, nc),
        in_specs=[
            pl.BlockSpec((rows, (COL_RG + 1) * BRANCH_W), lambda b, i: (b * nc + i, 0)), col(COL_MQ),
            pl.BlockSpec((None, CONV_WIDTH, BRANCH_W), lambda b, i: (layer, 0, 0)),
            conv_state, rope, rope, state,
            pl.BlockSpec((None, 1, BRANCH_W), lambda b, i: (layer, 0, 0)),
            pl.BlockSpec((None, tokens, BRANCH_W), lambda b, i: (b, 0, 0)),
            pl.BlockSpec((None, tokens, BRANCH_W), lambda b, i: (b, 0, 1)),
        ],
        out_specs=[out, out, out, conv_state, state],
        out_shape=[
            y_shape, y_shape, y_shape,
            jax.ShapeDtypeStruct((bsz, CONV_WIDTH - 1, BRANCH_W), F32),
            jax.ShapeDtypeStruct((bsz, RET_HEADS, RET_DK, RET_DK), F32),
        ],
        scratch_shapes=[pltpu.VMEM((CONV_WIDTH - 1, BRANCH_W), F32),
                        pltpu.VMEM((RET_HEADS, RET_DK, RET_DK), F32),
                        pltpu.VMEM((RET_HEADS, chunk, chunk), F32)],
        compiler_params=_params(("arbitrary", "arbitrary"), 40),
        name="local_mixers",
    )(proj, proj, conv_w, conv_prev, cosf, sinf, s0, gn_g, mkv, mkv)


def _diff_lambda(lam_ref, lam_init):
    e0 = jnp.exp(jnp.sum(lam_ref[0:1, :] * lam_ref[1:2, :], -1, keepdims=True))
    e1 = jnp.exp(jnp.sum(lam_ref[2:3, :] * lam_ref[3:4, :], -1, keepdims=True))
    return e0 - e1 + lam_init


def _diff_prompt_kernel(q_ref, k_ref, v_ref, bias_ref, lam_ref, g_ref, *rest, lam_init, n_cast):
    cast_src, o_ref, cast_dst = rest[:n_cast], rest[n_cast], rest[n_cast + 1:2 * n_cast + 1]
    (k16_ref, vt_ref, sa_ref, sb_ref, m0_ref, m1_ref, l0_ref, l1_ref, acc0_ref,
     acc1_ref) = rest[2 * n_cast + 1:]
    _side_cast(cast_src, cast_dst)
    i = pl.program_id(2)
    t = q_ref.shape[0]
    nt = k_ref.shape[0] // t
    m_refs, l_refs, acc_refs = (m0_ref, m1_ref), (l0_ref, l1_ref), (acc0_ref, acc1_ref)

    @pl.when(i == 0)
    def _():
        k16_ref[...] = k_ref[...].astype(BF16)
        for j in range(nt):
            vt_ref[j] = jnp.transpose(v_ref[j * t:(j + 1) * t, :]).astype(BF16)

    qt = jnp.transpose(q_ref[...] * (DIFF_D ** -0.5 * LOG2E))
    feat = lax.broadcasted_iota(jnp.int32, qt.shape, 0)
    qts = (jnp.where(feat < DIFF_D, qt, 0.0).astype(BF16), jnp.where(feat >= DIFF_D, qt, 0.0).astype(BF16))
    for c in range(2):
        m_refs[c][...] = jnp.full(m_refs[c].shape, NEG_INF, F32)
        l_refs[c][...] = jnp.zeros(l_refs[c].shape, F32)
        acc_refs[c][...] = jnp.zeros(acc_refs[c].shape, F32)

    def scores(j, kind, dst_ref):
        kt = k16_ref[j * t:(j + 1) * t, :]
        for c in range(2):
            s = _dot(kt, qts[c])
            dst_ref[c] = s if kind == FAR_TILE else s + bias_ref[kind]

    def consume(j, src_ref):
        vt = vt_ref[j]
        s = [src_ref[c] for c in range(2)]
        m_prev = [m_refs[c][...] for c in range(2)]
        m_new = [jnp.maximum(m_prev[c], jnp.max(s[c], 0, keepdims=True)) for c in range(2)]
        p = [jnp.exp2(s[c] - m_new[c]) for c in range(2)]
        a = [jnp.exp2(m_prev[c] - m_new[c]) for c in range(2)]
        for c in range(2):
            l_refs[c][...] = a[c] * l_refs[c][...] + jnp.sum(p[c], 0, keepdims=True)
            acc_refs[c][...] = a[c] * acc_refs[c][...] + _dot(vt, p[c].astype(BF16))
            m_refs[c][...] = m_new[c]

    bufs = (sa_ref, sb_ref)
    for qi in range(nt):
        @pl.when(i == qi)
        def _(qi=qi):
            kind = lambda j: min(qi - j, FAR_TILE)
            scores(0, kind(0), bufs[0])
            for j in range(qi + 1):
                if j < qi:
                    scores(j + 1, kind(j + 1), bufs[(j + 1) % 2])
                consume(j, bufs[j % 2])

    lam = _diff_lambda(lam_ref, lam_init)
    o = acc0_ref[...] / l0_ref[...] - lam * (acc1_ref[...] / l1_ref[...])
    o = o * lax.rsqrt(jnp.mean(o * o, 0, keepdims=True) + LN_EPS) * g_ref[...]
    o_ref[...] = jnp.transpose(o * (1.0 - lam_init)).astype(BF16)


def _diff_prompt(proj, bias_tiles, diff_lambda, subln_g, layer, *, bsz, length, t, lam_init, casts=()):
    m = proj.shape[0]
    nq = length // t
    assert length % t == 0 and t % CHUNK == 0
    grid = (bsz, DIFF_HEADS, nq)
    kv = lambda c: pl.BlockSpec((length, HEAD_W), lambda b, h, i: (b, c * HEADS_PER_BRANCH + h))
    stat = pltpu.VMEM((1, t), F32)
    acc = pltpu.VMEM((DIFF_DV, t), F32)
    cast_in, cast_out, cast_shapes = _side_cast_specs(casts, grid)
    out = pl.pallas_call(
        functools.partial(_diff_prompt_kernel, lam_init=lam_init, n_cast=len(casts)),
        grid=grid,
        in_specs=[
            pl.BlockSpec((t, HEAD_W), lambda b, h, i: (b * nq + i, COL_DQ * HEADS_PER_BRANCH + h)),
            kv(COL_DK), kv(COL_DV),
            pl.BlockSpec((None, FAR_TILE, t, t), lambda b, h, i: (h, 0, 0, 0)),
            pl.BlockSpec((None, 4, DIFF_D), lambda b, h, i: (layer, 0, 0)),
            pl.BlockSpec((None, DIFF_DV, 1), lambda b, h, i: (layer, 0, 0)),
        ] + cast_in,
        out_specs=[pl.BlockSpec((t, HEAD_W), lambda b, h, i: (b * nq + i, h))] + cast_out,
        out_shape=[jax.ShapeDtypeStruct((m, BRANCH_W), BF16)] + cast_shapes,
        scratch_shapes=[pltpu.VMEM((length, HEAD_W), BF16), pltpu.VMEM((nq, DIFF_DV, t), BF16),
                        pltpu.VMEM((2, t, t), F32), pltpu.VMEM((2, t, t), F32),
                        stat, stat, stat, stat, acc, acc],
        compiler_params=_params(("parallel", "parallel", "arbitrary"), 40),
        name="diff_attention_prompt",
    )(proj, proj, proj, bias_tiles, diff_lambda, subln_g, *(c[0] for c in casts))
    return out[0], out[1:]


def _head_rows(ref, h, n):
    return ref[pl.ds(h, n, stride=HEADS_PER_BRANCH), :]


def _sample_attn_kernel(q_ref, kn_ref, vn_ref, mq_ref, kp_ref, vp_ref, mk_ref, mv_ref, bp_ref, bn_ref,
                        lam_ref, g_ref, yc_ref, yd_ref, *, lam_init):
    past = kp_ref.shape[0] // HEADS_PER_BRANCH
    tokens = mk_ref.shape[0] // HEADS_PER_BRANCH
    lam = _diff_lambda(lam_ref, lam_init)
    for h in range(HEADS_PER_BRANCH):
        cols = slice(h * HEAD_W, (h + 1) * HEAD_W)
        q = q_ref[:, cols] * (DIFF_D ** -0.5 * LOG2E)
        lane = lax.broadcasted_iota(jnp.int32, q.shape, 1)
        qs = (jnp.where(lane < DIFF_D, q, 0.0).astype(BF16), jnp.where(lane >= DIFF_D, q, 0.0).astype(BF16))
        kp = _head_rows(kp_ref, h, past).astype(BF16)
        vp = _head_rows(vp_ref, h, past).astype(BF16)
        kn = kn_ref[:, cols].astype(BF16)
        vn = vn_ref[:, cols].astype(BF16)
        outs = []
        for qc in qs:
            sp = _dot_nt(qc, kp) + bp_ref[h]
            sn = _dot_nt(qc, kn) + bn_ref[h]
            mx = jnp.maximum(jnp.max(sp, -1, keepdims=True), jnp.max(sn, -1, keepdims=True))
            pp = jnp.exp2(sp - mx)
            pn = jnp.exp2(sn - mx)
            den = jnp.sum(pp, -1, keepdims=True) + jnp.sum(pn, -1, keepdims=True)
            outs.append((_dot(pp.astype(BF16), vp) + _dot(pn.astype(BF16), vn)) / den)
        o = outs[0] - lam * outs[1]
        o = o * lax.rsqrt(jnp.mean(o * o, -1, keepdims=True) + LN_EPS) * g_ref[...]
        yc_ref[:, cols] = (o * (1.0 - lam_init)).astype(BF16)

        mk = _head_rows(mk_ref, h, tokens).astype(BF16)
        mv = _head_rows(mv_ref, h, tokens).astype(BF16)
        s = _dot_nt(mq_ref[:, cols].astype(BF16), mk) * (MEM_HD ** -0.5)
        p = jnp.exp(s - jnp.max(s, -1, keepdims=True))
        den = jnp.sum(p, -1, keepdims=True)
        yd_ref[:, cols] = (_dot(p.astype(BF16), mv) / den).astype(BF16)


def _sample_attention(proj, k_past, v_past, mem_k, mem_v, bias_past, bias_new, diff_lambda, subln_g, layer,
                      *, bsz, length, lam_init):
    m = proj.shape[0]
    col = lambda c: pl.BlockSpec((length, BRANCH_W), lambda b: (b, c))
    cache = lambda a: pl.BlockSpec((None, None) + a.shape[2:], lambda b: (layer, b, 0, 0))
    whole = lambda a: pl.BlockSpec(a.shape, lambda b: (0,) * a.ndim)
    out = pl.BlockSpec((length, BRANCH_W), lambda b: (b, 0))
    return pl.pallas_call(
        functools.partial(_sample_attn_kernel, lam_init=lam_init),
        grid=(bsz,),
        in_specs=[
            col(COL_DQ), col(COL_DK), col(COL_DV), col(COL_MQ),
            cache(k_past), cache(v_past), cache(mem_k), cache(mem_v),
            whole(bias_past), whole(bias_new),
            pl.BlockSpec((None, 4, DIFF_D), lambda b: (layer, 0, 0)),
            pl.BlockSpec((None, 1, DIFF_DV), lambda b: (layer, 0, 0)),
        ],
        out_specs=[out, out],
        out_shape=[jax.ShapeDtypeStruct((m, BRANCH_W), BF16), jax.ShapeDtypeStruct((m, BRANCH_W), BF16)],
        compiler_params=_params(("parallel",), 40),
        name="sample_attention",
    )(proj, proj, proj, proj, k_past, v_past, mem_k, mem_v, bias_past, bias_new, diff_lambda, subln_g)


def _merge_kernel(h_ref, ya_ref, yb_ref, yc_ref, yd_ref, wg_ref, wb_ref, bg_ref, o_ref):
    h = h_ref[...]
    acc = None
    for i, y_ref in enumerate((ya_ref, yb_ref, yc_ref, yd_ref)):
        gate = jax.nn.sigmoid(_dot(h, wg_ref[i]) + bg_ref[i])
        term = gate * _dot(y_ref[...], wb_ref[i])
        acc = term if acc is None else acc + term
    o_ref[...] = acc.astype(BF16)


def _merge(h16, ys, w_gate, w_branch, b_gate, layer, *, tm, tn):
    m, d = h16.shape
    assert m % tm == 0 and d % tn == 0
    y_spec = pl.BlockSpec((tm, BRANCH_W), lambda i, j: (i, 0))
    return pl.pallas_call(
        _merge_kernel,
        grid=(m // tm, d // tn),
        in_specs=[
            pl.BlockSpec((tm, d), lambda i, j: (i, 0)), y_spec, y_spec, y_spec, y_spec,
            pl.BlockSpec((None, N_BRANCH, d, tn), lambda i, j: (0, 0, 0, j)),
            pl.BlockSpec((None, N_BRANCH, BRANCH_W, tn), lambda i, j: (0, 0, 0, j)),
            pl.BlockSpec((None, N_BRANCH, 1, tn), lambda i, j: (layer, 0, 0, j)),
        ],
        out_specs=pl.BlockSpec((tm, tn), lambda i, j: (i, j)),
        out_shape=jax.ShapeDtypeStruct((m, d), BF16),
        compiler_params=_params(("parallel", "parallel"), 56),
        name="gated_merge",
    )(h16, *ys, w_gate, w_branch, b_gate)


def _proj_ln_kernel(m_ref, x_ref, w_ref, g_ref, b_ref, o_ref, *, alpha):
    for r in range(0, o_ref.shape[0], PROJ_ROW_CHUNK):
        rows = slice(r, r + PROJ_ROW_CHUNK)
        y = alpha * x_ref[rows, :] + _dot(m_ref[rows, :], w_ref[...])
        o_ref[rows, :] = _layer_norm(y, g_ref[...], b_ref[...])


def _proj_ln(merged, x, w_o, g, b, layer, *, alpha, tm):
    m, d = x.shape
    assert m % tm == 0
    row = pl.BlockSpec((tm, d), lambda i: (i, 0))
    vec = pl.BlockSpec((None, 1, d), lambda i: (layer, 0, 0))
    return pl.pallas_call(
        functools.partial(_proj_ln_kernel, alpha=alpha),
        grid=(m // tm,),
        in_specs=[row, row, pl.BlockSpec((None, d, d), lambda i: (0, 0, 0)), vec, vec],
        out_specs=row,
        out_shape=jax.ShapeDtypeStruct((m, d), F32),
        compiler_params=_params(("parallel",), 48),
        name="proj_ln",
    )(merged, x, w_o, g, b)


def _rope_tables(pos):
    half = RET_DK // 2
    inv = ROPE_BASE ** (-jnp.arange(half, dtype=F32) / half)
    ang = pos.astype(F32)[:, None] * inv[None, :]
    cos = jnp.cos(ang)
    sin = jnp.sin(ang)
    return jnp.concatenate([cos, cos], -1), jnp.concatenate([-sin, sin], -1)


def _t5_bucket(rel):
    nb = REL_BUCKETS // 2
    max_exact = nb // 2
    n = jnp.abs(rel)
    nf = jnp.maximum(n, 1).astype(F32)
    large = max_exact + (jnp.log(nf / max_exact) / math.log(REL_MAX_DIST / max_exact)
                         * (nb - max_exact)).astype(jnp.int32)
    large = jnp.minimum(large, nb - 1)
    return jnp.where(rel > 0, nb, 0) + jnp.where(n < max_exact, n, large)


def _masked_bias(q_pos, k_pos, rel_bias):
    bucket = _t5_bucket(k_pos[None, :] - q_pos[:, None])
    onehot = bucket[None, None] == jnp.arange(REL_BUCKETS, dtype=bucket.dtype)[None, :, None, None]
    bias = jnp.sum(jnp.where(onehot, rel_bias.astype(F32).T[:, :, None, None], 0.0), axis=1)
    allowed = (k_pos[None, :] // CHUNK) <= (q_pos[:, None] // CHUNK)
    return jnp.where(allowed[None], bias * LOG2E, NEG_INF)


def _prompt_bias_tiles(rel_bias, t):
    base = BIAS_BASE
    assert base % CHUNK == 0 and base + 1 >= REL_MAX_DIST and t % base == 0
    n = t // base
    pos = jnp.arange(base, dtype=jnp.int32)
    diag, sub, far = (jnp.swapaxes(_masked_bias(pos + d * base, pos, rel_bias), -1, -2) for d in range(3))
    shift = far[:, :1, :1]
    masked = jnp.full_like(diag, NEG_INF)
    pick = lambda delta: masked if delta > 0 else (diag - shift, sub - shift, far - shift)[min(-delta, 2)]
    tile = lambda d: jnp.concatenate(
        [jnp.concatenate([pick(kk - qq - d * n) for qq in range(n)], axis=-1) for kk in range(n)], axis=-2)
    return jnp.stack([tile(d) for d in range(FAR_TILE)], axis=1)


def _kv_export_kernel(k_ref, v_ref, *rest):
    ok_ref, ov_ref = rest[-2:]
    n = k_ref.shape[0]
    for h in range(HEADS_PER_BRANCH):
        cols = slice(h * HEAD_W, (h + 1) * HEAD_W)
        rows = pl.ds(h, n, stride=HEADS_PER_BRANCH)
        ok_ref[rows, :] = k_ref[:, cols]
        ov_ref[rows, :] = v_ref[:, cols]


def _kv_export(proj, prev, layer, depth, *, tm):
    m = proj.shape[0]
    assert m % tm == 0
    col = lambda c: pl.BlockSpec((tm, BRANCH_W), lambda i: (i, c))
    out = pl.BlockSpec((None, tm * HEADS_PER_BRANCH, HEAD_W), lambda i: (layer, i, 0))
    shape = jax.ShapeDtypeStruct((depth, m * HEADS_PER_BRANCH, HEAD_W), F32)
    keep = [pl.BlockSpec(memory_space=pl.ANY)] * len(prev)
    return pl.pallas_call(
        _kv_export_kernel,
        grid=(m // tm,),
        in_specs=[col(COL_DK), col(COL_DV)] + keep,
        out_specs=[out, out],
        out_shape=[shape, shape],
        input_output_aliases={2 + n: n for n in range(len(prev))},
        compiler_params=_params(("parallel",), 32),
        name="kv_export",
    )(proj, proj, *prev)


CAST_IN_FFN1 = (('w_in', 'rc'), ('ffn2_w_up', 'rc'), ('ffn2_w_down', 'cr'))
CAST_IN_ATTN = (('w_gate', 'flat'), ('w_branch', 'flat'), ('w_o', 'flat'))
CAST_IN_FFN2 = (('ffn1_w_up', 'rc'), ('ffn1_w_down', 'cr'))


def _encoder_layer(x, w16, p, layer, *, alpha, cfg, mixers_fn, w32=None, w16_next=None):
    d = x.shape[1]
    casts = lambda names, l: tuple((w32[n], l, split) for n, split in names) if w32 is not None else ()
    x1, done = _ffn_ln(x, w16['ffn1_w_up'], w16['ffn1_w_down'], p['ln1_g'], p['ln1_b'], layer,
                       alpha=alpha, tm=cfg['tm_ffn'], tf=cfg['tf'], casts=casts(CAST_IN_FFN1, layer))
    w16.update(zip((n for n, _ in CAST_IN_FFN1), done))
    proj, x1b, _ = _in_proj(x1, w16['w_in'], tm=cfg['tm_in'], tn=cfg['tn_in'])
    y_a, y_b, y_c, y_d, conv_new, ret_new, done = mixers_fn(
        proj, 0.8 - 0.6 * math.exp(-0.3 * layer), casts(CAST_IN_ATTN, layer))
    w16.update(zip((n for n, _ in CAST_IN_ATTN), done))
    merged = _merge(x1b, (y_a, y_b, y_c, y_d), w16['w_gate'].reshape(1, N_BRANCH, d, d),
                    w16['w_branch'].reshape(1, N_BRANCH, BRANCH_W, d), p['b_gate'], layer,
                    tm=cfg['tm_merge'], tn=cfg['tn_merge'])
    x2 = _proj_ln(merged, x1, w16['w_o'], p['ln2_g'], p['ln2_b'], layer, alpha=alpha, tm=cfg['tm_proj'])
    x3, done = _ffn_ln(x2, w16['ffn2_w_up'], w16['ffn2_w_down'], p['ln3_g'], p['ln3_b'], layer,
                       alpha=alpha, tm=cfg['tm_ffn'], tf=cfg['tf'],
                       casts=casts(CAST_IN_FFN2, layer + 1) if w16_next is not None else ())
    if w16_next is not None:
        w16_next.update(zip((n for n, _ in CAST_IN_FFN2), done))
    return x3, proj, conv_new, ret_new


PROMPT_CFG = dict(tm_ffn=1024, tf=512, tm_in=1024, tn_in=1408, mix_rows=256, ret_chunk=256, t_diff=512,
                  tm_merge=1024, tn_merge=512, tm_proj=512, tm_export=1024)
SAMPLE_CFG = dict(tm_ffn=256, tf=512, tm_in=256, tn_in=512, t_conv=16, ret_chunk=16,
                  tm_merge=256, tn_merge=256, tm_proj=256, tm_export=256)


def kernel(x_prompt, x_sample, state_conv, state_ret, cache_diff_k, cache_diff_v, cache_mem_k, cache_mem_v,
           mem_prompt, ffn1_w_up, ffn1_w_down, ln1_g, ln1_b, w_in, conv_w, ret_gn_g, diff_lambda,
           diff_subln_g, w_mem_kv, w_branch, w_gate, b_gate, w_o, ln2_g, ln2_b, ffn2_w_up, ffn2_w_down,
           ln3_g, ln3_b, rel_bias):
    bp, lp, d = x_prompt.shape
    bs, ls, _ = x_sample.shape
    depth = w_in.shape[0]
    past = cache_diff_k.shape[2]
    mem_tokens = mem_prompt.shape[1]
    alpha = (2 * depth) ** 0.25
    half = MEM_HEADS * MEM_HD

    pos_p = jnp.arange(lp, dtype=jnp.int32)
    pos_s = past + jnp.arange(ls, dtype=jnp.int32)
    past_pos = jnp.arange(past, dtype=jnp.int32)
    rope_p = _rope_tables(pos_p)
    rope_s = _rope_tables(pos_s)
    bias_p = _prompt_bias_tiles(rel_bias, PROMPT_CFG['t_diff'])
    bias_s_past = _masked_bias(pos_s, past_pos, rel_bias)
    bias_s_new = _masked_bias(pos_s, pos_s, rel_bias)

    vec = lambda v: v.reshape(depth, 1, v.shape[-1])
    p = {'ln1_g': vec(ln1_g), 'ln1_b': vec(ln1_b), 'ln2_g': vec(ln2_g), 'ln2_b': vec(ln2_b),
         'ln3_g': vec(ln3_g), 'ln3_b': vec(ln3_b), 'b_gate': b_gate.reshape(depth, N_BRANCH, 1, d)}
    gn_g = vec(ret_gn_g)
    w32 = {'ffn1_w_up': ffn1_w_up, 'ffn1_w_down': ffn1_w_down, 'w_in': w_in,
           'w_gate': w_gate.reshape(depth, N_BRANCH * d, d), 'w_branch': w_branch.reshape(depth, N_BRANCH * BRANCH_W, d),
           'w_o': w_o, 'ffn2_w_up': ffn2_w_up, 'ffn2_w_down': ffn2_w_down}
    w16 = [dict() for _ in range(depth)]
    w16[0].update({n: w32[n][:1].astype(BF16) for n, _ in CAST_IN_FFN2})
    w_mem16 = w_mem_kv.astype(BF16)
    subln_col = diff_subln_g.reshape(depth, DIFF_DV, 1)
    subln_row = diff_subln_g.reshape(depth, 1, DIFF_DV)
    head_major = lambda c: c.reshape(depth, bs, c.shape[2] * c.shape[3], c.shape[4])
    kpast, vpast = head_major(cache_diff_k), head_major(cache_diff_v)
    mem_k_s, mem_v_s = head_major(cache_mem_k), head_major(cache_mem_v)

    yp = x_prompt.reshape(bp * lp, d)
    ys = x_sample.reshape(bs * ls, d)
    mem16 = mem_prompt.reshape(bp * mem_tokens, d).astype(BF16)
    zero_conv = jnp.zeros((bp, CONV_WIDTH - 1, BRANCH_W), F32)
    zero_ret = jnp.zeros((bp, RET_HEADS, RET_DK, RET_DK), F32)

    conv_p, ret_p, mk_p, mv_p, conv_s, ret_s = [], [], [], [], [], []
    kv_p, kv_s = [], []
    for l in range(depth):
        mkv = _matmul(mem16, w_mem16, l, tm=bp * mem_tokens, tn=512).reshape(bp, mem_tokens, 2 * half)

        def mixers_p(proj, lam_init, casts):
            y_a, y_b, y_d, c_new, r_new = _local_mixers(
                proj, conv_w, zero_conv, rope_p[0], rope_p[1], zero_ret, gn_g, mkv, l,
                bsz=bp, length=lp, rows=PROMPT_CFG['mix_rows'], chunk=PROMPT_CFG['ret_chunk'])
            y_c, done = _diff_prompt(proj, bias_p, diff_lambda, subln_col, l, bsz=bp, length=lp,
                                     t=PROMPT_CFG['t_diff'], lam_init=lam_init, casts=casts)
            return y_a, y_b, y_c, y_d, c_new, r_new, done

        yp, proj, c_new, r_new = _encoder_layer(
            yp, w16[l], p, l, alpha=alpha, cfg=PROMPT_CFG, mixers_fn=mixers_p,
            w32=w32, w16_next=w16[l + 1] if l + 1 < depth else None)
        kv_p = _kv_export(proj, kv_p, l, depth, tm=PROMPT_CFG['tm_export'])
        conv_p.append(c_new)
        ret_p.append(r_new)
        mk_p.append(mkv[:, :, :half].reshape(bp, mem_tokens, MEM_HEADS, MEM_HD))
        mv_p.append(mkv[:, :, half:].reshape(bp, mem_tokens, MEM_HEADS, MEM_HD))

        def mixers_s(proj, lam_init, casts):
            assert not casts
            y_a, c_new = _conv_branch(proj, conv_w, state_conv[l], l, bsz=bs, length=ls, tl=SAMPLE_CFG['t_conv'])
            y_b, r_new = _retention_branch(proj, rope_s[0], rope_s[1], state_ret[l], gn_g, l,
                                           bsz=bs, length=ls, chunk=SAMPLE_CFG['ret_chunk'])
            y_c, y_d = _sample_attention(proj, kpast, vpast, mem_k_s, mem_v_s, bias_s_past, bias_s_new,
                                         diff_lambda, subln_row, l, bsz=bs, length=ls, lam_init=lam_init)
            return y_a, y_b, y_c, y_d, c_new, r_new, ()

        ys, proj, c_new, r_new = _encoder_layer(ys, w16[l], p, l, alpha=alpha, cfg=SAMPLE_CFG, mixers_fn=mixers_s)
        kv_s = _kv_export(proj, kv_s, l, depth, tm=SAMPLE_CFG['tm_export'])
        conv_s.append(c_new)
        ret_s.append(r_new)

    kv_shape = lambda b, n: (depth, b, n, DIFF_HEADS, DIFF_DV)
    return (yp.reshape(bp, lp, d), ys.reshape(bs, ls, d), jnp.stack(conv_p), jnp.stack(ret_p),
            kv_p[0].reshape(kv_shape(bp, lp)), kv_p[1].reshape(kv_shape(bp, lp)),
            jnp.stack(mk_p), jnp.stack(mv_p), jnp.stack(conv_s), jnp.stack(ret_s),
            kv_s[0].reshape(kv_shape(bs, ls)), kv_s[1].reshape(kv_shape(bs, ls)))
```

```python
import functools
import math

import jax
import jax.numpy as jnp
from jax import lax
from jax.experimental import pallas as pl
from jax.experimental.pallas import tpu as pltpu

F32 = jnp.float32
BF16 = jnp.bfloat16

CHUNK = 64
CONV_WIDTH = 3
RET_HEADS = 4
RET_DK = 128
DIFF_HEADS = 4
DIFF_D = 64
DIFF_DV = 128
MEM_HEADS = 4
MEM_HD = 128
BRANCH_W = 512
N_BRANCH = 4
REL_BUCKETS = 32
REL_MAX_DIST = 128
LN_EPS = 1e-5
ROPE_BASE = 10000.0
NEG_INF = -1e30
HEAD_W = 128
LOG2E = math.log2(math.e)
LN_ROW_CHUNK = 128
PROJ_ROW_CHUNK = 256
BIAS_BASE = 256
FAR_TILE = 2
FFN_UP_CHUNK = 256
FFN_DOWN_CHUNK = 512

W_IN_BLOCKS = 11
W_IN_DK, W_IN_DV, W_IN_MQ = 8, 9, 10
COL_CB, COL_CC, COL_CH, COL_RQ, COL_RK, COL_RV, COL_RG, COL_DQ, COL_MQ = range(9)
MAIN_BLOCKS = 9
KV_K, KV_V = range(2)
KV_BLOCKS = 2
HEADS_PER_BRANCH = BRANCH_W // HEAD_W

LANES = 128
F32_SUBLANES = 8
BF16_SUBLANES = 16
V7X_VMEM_BYTES = 64 * 1024 * 1024
MIB = 1024 * 1024


def _params(semantics, vmem_mib):
    assert vmem_mib * MIB < V7X_VMEM_BYTES
    return pltpu.CompilerParams(dimension_semantics=semantics, vmem_limit_bytes=vmem_mib * MIB)


def _layer_norm(y, g, b):
    mu = jnp.mean(y, -1, keepdims=True)
    d = y - mu
    var = jnp.mean(d * d, -1, keepdims=True)
    return d * lax.rsqrt(var + LN_EPS) * g + b


def _dot(a, b):
    return jnp.dot(a, b, preferred_element_type=F32)


def _dot_nt(a, b):
    return lax.dot_general(a, b, (((1,), (1,)), ((), ())), preferred_element_type=F32)


def _dot_tn(a, b):
    return lax.dot_general(a, b, (((0,), (0,)), ((), ())), preferred_element_type=F32)


def _side_cast_specs(casts, grid):
    in_specs, out_specs, out_shapes = [], [], []
    for src, layer, split in casts:
        _, r, c = src.shape
        if split == 'flat':
            gr, gc = math.prod(grid), 1

            def pick(*ids, grid=grid):
                flat = ids[0]
                for size, idx in zip(grid[1:], ids[1:]):
                    flat = flat * size + idx
                return (flat, 0)
        elif split == 'cr':
            gr, gc = grid[1], grid[0]
            pick = lambda i, j: (j, i)
        else:
            gr, gc = grid
            pick = lambda i, j: (i, j)
        assert r % gr == 0 and c % gc == 0
        blk = (None, r // gr, c // gc)
        assert blk[1] % BF16_SUBLANES == 0 and blk[2] % LANES == 0
        src_pick, dst_pick, c_out = pick, pick, c
        if isinstance(split, tuple):
            kind, src_col, dst_col, n_dst = split
            assert kind == 'rc'
            src_pick = lambda i, j, col=src_col: (i, col(j))
            dst_pick = lambda i, j, col=dst_col: (i, col(j))
            c_out = n_dst * blk[2]
        else:
            assert split in ('rc', 'cr', 'flat')
        in_specs.append(pl.BlockSpec(blk, lambda *ids, layer=layer, pick=src_pick: (layer,) + pick(*ids)))
        out_specs.append(pl.BlockSpec(blk, lambda *ids, pick=dst_pick: (0,) + pick(*ids)))
        out_shapes.append(jax.ShapeDtypeStruct((1, r, c_out), BF16))
    return in_specs, out_specs, out_shapes


def _side_cast(src_refs, dst_refs):
    for src_ref, dst_ref in zip(src_refs, dst_refs):
        dst_ref[...] = src_ref[...].astype(BF16)


def _ffn_ln_kernel(x_ref, wa_ref, wb_ref, wd_ref, g_ref, b_ref, *rest, alpha, n_cast):
    cast_src, o_ref, cast_dst = rest[:n_cast], rest[n_cast], rest[n_cast + 1:2 * n_cast + 1]
    xb_ref, h_ref = rest[2 * n_cast + 1:]
    j = pl.program_id(1)

    @pl.when(j == 0)
    def _():
        xb_ref[...] = x_ref[...].astype(BF16)
        o_ref[...] = jnp.zeros(o_ref.shape, F32)

    xb = xb_ref[...]
    for c in range(0, h_ref.shape[1], FFN_UP_CHUNK):
        cols = slice(c, c + FFN_UP_CHUNK)
        a = _dot(xb, wa_ref[:, cols])
        b = _dot(xb, wb_ref[:, cols])
        h_ref[:, cols] = (a * jax.nn.sigmoid(a) * b).astype(BF16)
    h = h_ref[...]
    for c in range(0, o_ref.shape[1], FFN_DOWN_CHUNK):
        cols = slice(c, c + FFN_DOWN_CHUNK)
        o_ref[:, cols] += _dot(h, wd_ref[:, cols])
    _side_cast(cast_src, cast_dst)

    @pl.when(j == pl.num_programs(1) - 1)
    def _():
        for r in range(0, o_ref.shape[0], LN_ROW_CHUNK):
            rows = slice(r, r + LN_ROW_CHUNK)
            o_ref[rows, :] = _layer_norm(alpha * x_ref[rows, :] + 0.5 * o_ref[rows, :], g_ref[...], b_ref[...])


def _ffn_ln(x, w_up, w_down, g, b, layer, *, alpha, tm, tf, casts=()):
    m, d = x.shape
    d_ff = w_down.shape[1]
    nj = d_ff // tf
    assert m % tm == 0 and d_ff % tf == 0
    grid = (m // tm, nj)
    row = lambda i, j: (i, 0)
    vec = pl.BlockSpec((None, 1, d), lambda i, j: (layer, 0, 0))
    cast_in, cast_out, cast_shapes = _side_cast_specs(casts, grid)
    out = pl.pallas_call(
        functools.partial(_ffn_ln_kernel, alpha=alpha, n_cast=len(casts)),
        grid=grid,
        in_specs=[
            pl.BlockSpec((tm, d), row),
            pl.BlockSpec((None, d, tf), lambda i, j: (0, 0, j)),
            pl.BlockSpec((None, d, tf), lambda i, j: (0, 0, j + nj)),
            pl.BlockSpec((None, tf, d), lambda i, j: (0, j, 0)),
            vec, vec,
        ] + cast_in,
        out_specs=[pl.BlockSpec((tm, d), row)] + cast_out,
        out_shape=[jax.ShapeDtypeStruct((m, d), F32)] + cast_shapes,
        scratch_shapes=[pltpu.VMEM((tm, d), BF16), pltpu.VMEM((tm, tf), BF16)],
        compiler_params=_params(("parallel", "arbitrary"), 60),
        name="ffn_ln",
    )(x, w_up, w_up, w_down, g, b, *(c[0] for c in casts))
    return out[0], out[1:]


def _in_proj_kernel(x_ref, w_ref, *rest, n_cast):
    cast_src, o_ref, xb_ref, cast_dst = rest[:n_cast], rest[n_cast], rest[n_cast + 1], rest[n_cast + 2:]

    @pl.when(pl.program_id(1) == 0)
    def _():
        xb_ref[...] = x_ref[...].astype(BF16)

    o_ref[...] = _dot(xb_ref[...], w_ref[...])
    _side_cast(cast_src, cast_dst)


def _in_proj(x, w, *, tm, tn, casts=()):
    m, k = x.shape
    n = w.shape[2]
    assert m % tm == 0 and n % tn == 0
    grid = (m // tm, n // tn)
    cast_in, cast_out, cast_shapes = _side_cast_specs(casts, grid)
    out = pl.pallas_call(
        functools.partial(_in_proj_kernel, n_cast=len(casts)),
        grid=grid,
        in_specs=[pl.BlockSpec((tm, k), lambda i, j: (i, 0)),
                  pl.BlockSpec((None, k, tn), lambda i, j: (0, 0, j))] + cast_in,
        out_specs=[pl.BlockSpec((tm, tn), lambda i, j: (i, j)), pl.BlockSpec((tm, k), lambda i, j: (i, 0))] + cast_out,
        out_shape=[jax.ShapeDtypeStruct((m, n), F32), jax.ShapeDtypeStruct((m, k), BF16)] + cast_shapes,
        compiler_params=_params(("parallel", "arbitrary"), 60),
        name="in_proj",
    )(x, w, *(c[0] for c in casts))
    return out[0], out[1], out[2:]


def _matmul_kernel(x_ref, w_ref, o_ref):
    o_ref[...] = _dot(x_ref[...], w_ref[...])


def _matmul(x, w, layer, *, tm, tn):
    m, k = x.shape
    n = w.shape[2]
    assert m % tm == 0 and n % tn == 0
    return pl.pallas_call(
        _matmul_kernel,
        grid=(m // tm, n // tn),
        in_specs=[pl.BlockSpec((tm, k), lambda i, j: (i, 0)),
                  pl.BlockSpec((None, k, tn), lambda i, j: (layer, 0, j))],
        out_specs=pl.BlockSpec((tm, tn), lambda i, j: (i, j)),
        out_shape=jax.ShapeDtypeStruct((m, n), F32),
        compiler_params=_params(("parallel", "parallel"), 48),
        name="matmul",
    )(x, w)


def _conv_body(cb_ref, cc_ref, ch_ref, w_ref, carry_ref, y_ref):
    u = cc_ref[...] * ch_ref[...]
    row = lax.broadcasted_iota(jnp.int32, u.shape, 0)
    c0 = carry_ref[0:1, :]
    c1 = carry_ref[1:2, :]
    u1 = jnp.where(row == 0, c1, pltpu.roll(u, 1, 0))
    u2 = jnp.where(row == 0, c0, jnp.where(row == 1, c1, pltpu.roll(u, 2, 0)))
    z = w_ref[0:1, :] * u2 + w_ref[1:2, :] * u1 + w_ref[2:3, :] * u
    y_ref[...] = (cb_ref[...] * z).astype(BF16)
    tail = u[u.shape[0] - F32_SUBLANES:, :][F32_SUBLANES - (CONV_WIDTH - 1):, :]
    carry_ref[...] = tail
    return tail


def _conv_kernel(cb_ref, cc_ref, ch_ref, w_ref, prev_ref, y_ref, state_ref, carry_ref):
    l = pl.program_id(1)

    @pl.when(l == 0)
    def _():
        carry_ref[...] = prev_ref[...]

    tail = _conv_body(cb_ref, cc_ref, ch_ref, w_ref, carry_ref, y_ref)

    @pl.when(l == pl.num_programs(1) - 1)
    def _():
        state_ref[...] = tail


def _conv_branch(proj, conv_w, prev, layer, *, bsz, length, tl):
    m = proj.shape[0]
    nl = length // tl
    assert length % tl == 0 and tl >= F32_SUBLANES
    col = lambda c: pl.BlockSpec((tl, BRANCH_W), lambda b, l: (b * nl + l, c))
    return pl.pallas_call(
        _conv_kernel,
        grid=(bsz, nl),
        in_specs=[
            col(COL_CB), col(COL_CC), col(COL_CH),
            pl.BlockSpec((None, CONV_WIDTH, BRANCH_W), lambda b, l: (layer, 0, 0)),
            pl.BlockSpec((None, CONV_WIDTH - 1, BRANCH_W), lambda b, l: (b, 0, 0)),
        ],
        out_specs=[
            pl.BlockSpec((tl, BRANCH_W), lambda b, l: (b * nl + l, 0)),
            pl.BlockSpec((None, CONV_WIDTH - 1, BRANCH_W), lambda b, l: (b, 0, 0)),
        ],
        out_shape=[
            jax.ShapeDtypeStruct((m, BRANCH_W), BF16),
            jax.ShapeDtypeStruct((bsz, CONV_WIDTH - 1, BRANCH_W), F32),
        ],
        scratch_shapes=[pltpu.VMEM((CONV_WIDTH - 1, BRANCH_W), F32)],
        compiler_params=_params(("parallel", "arbitrary"), 32),
        name="conv_branch",
    )(proj, proj, proj, conv_w, prev)


def _ret_log_gamma(h):
    return math.log1p(-(2.0 ** (-5.0 - h)))


def _ret_fill_decay(intra_ref):
    chunk = intra_ref.shape[1]
    ri = lax.broadcasted_iota(jnp.int32, (chunk, chunk), 0)
    ci = lax.broadcasted_iota(jnp.int32, (chunk, chunk), 1)
    rel = (ri - ci).astype(F32)
    for h in range(RET_HEADS):
        intra_ref[h] = jnp.where(rel >= 0.0, jnp.exp(jnp.maximum(rel, 0.0) * _ret_log_gamma(h)), 0.0)


def _retention_body(q_ref, k_ref, v_ref, g_ref, cos_ref, sin_ref, gn_ref, y_ref, s_ref, intra_ref, rows):
    chunk = intra_ref.shape[1]
    cosf = cos_ref[rows, :]
    sinf = sin_ref[rows, :]
    idx = lax.broadcasted_iota(jnp.int32, (chunk, 1), 0).astype(F32)
    for h in range(RET_HEADS):
        log_g = _ret_log_gamma(h)
        cols = slice(h * RET_DK, (h + 1) * RET_DK)
        q = q_ref[rows, cols]
        k = k_ref[rows, cols]
        q = q * cosf + pltpu.roll(q, RET_DK // 2, 1) * sinf
        k = (k * cosf + pltpu.roll(k, RET_DK // 2, 1) * sinf) * (RET_DK ** -0.5)
        vb = v_ref[rows, cols].astype(BF16)
        q_dec = jnp.exp((idx + 1.0) * log_g)
        k_dec = jnp.exp((chunk - 1.0 - idx) * log_g)
        c_dec = math.exp(chunk * log_g)
        qb = q.astype(BF16)
        s_prev = s_ref[h]
        att = _dot_nt(qb, k.astype(BF16)) * intra_ref[h]
        o = _dot(att.astype(BF16), vb) + _dot(qb, s_prev.astype(BF16)) * q_dec
        s_ref[h] = s_prev * c_dec + _dot_tn((k * k_dec).astype(BF16), vb)
        mu = jnp.mean(o, -1, keepdims=True)
        d = o - mu
        var = jnp.mean(d * d, -1, keepdims=True)
        ro = d * lax.rsqrt(var + LN_EPS) * gn_ref[:, cols]
        gate = g_ref[rows, cols]
        y_ref[rows, cols] = (gate * jax.nn.sigmoid(gate) * ro).astype(BF16)


def _retention_kernel(q_ref, k_ref, v_ref, g_ref, cos_ref, sin_ref, s0_ref, gn_ref, y_ref, sfin_ref,
                      s_ref, intra_ref):
    c = pl.program_id(1)

    @pl.when(jnp.logical_and(pl.program_id(0) == 0, c == 0))
    def _():
        _ret_fill_decay(intra_ref)

    @pl.when(c == 0)
    def _():
        s_ref[...] = s0_ref[...]

    _retention_body(q_ref, k_ref, v_ref, g_ref, cos_ref, sin_ref, gn_ref, y_ref, s_ref, intra_ref,
                    slice(0, q_ref.shape[0]))

    @pl.when(c == pl.num_programs(1) - 1)
    def _():
        sfin_ref[...] = s_ref[...]


def _retention_branch(proj, cosf, sinf, s0, gn_g, layer, *, bsz, length, chunk):
    m = proj.shape[0]
    nc = length // chunk
    assert length % chunk == 0
    col = lambda c: pl.BlockSpec((chunk, BRANCH_W), lambda b, i: (b * nc + i, c))
    state = pl.BlockSpec((None, RET_HEADS, RET_DK, RET_DK), lambda b, i: (b, 0, 0, 0))
    return pl.pallas_call(
        _retention_kernel,
        grid=(bsz, nc),
        in_specs=[
            col(COL_RQ), col(COL_RK), col(COL_RV), col(COL_RG),
            pl.BlockSpec((chunk, RET_DK), lambda b, i: (i, 0)),
            pl.BlockSpec((chunk, RET_DK), lambda b, i: (i, 0)),
            state,
            pl.BlockSpec((None, 1, BRANCH_W), lambda b, i: (layer, 0, 0)),
        ],
        out_specs=[pl.BlockSpec((chunk, BRANCH_W), lambda b, i: (b * nc + i, 0)), state],
        out_shape=[
            jax.ShapeDtypeStruct((m, BRANCH_W), BF16),
            jax.ShapeDtypeStruct((bsz, RET_HEADS, RET_DK, RET_DK), F32),
        ],
        scratch_shapes=[pltpu.VMEM((RET_HEADS, RET_DK, RET_DK), F32),
                        pltpu.VMEM((RET_HEADS, chunk, chunk), F32)],
        compiler_params=_params(("arbitrary", "arbitrary"), 32),
        name="retention_branch",
    )(proj, proj, proj, proj, cosf, sinf, s0, gn_g)


def _mem_head(q, mk, mv):
    s = _dot_nt(q.astype(BF16), mk.astype(BF16)) * (MEM_HD ** -0.5)
    p = jnp.exp(s - jnp.max(s, -1, keepdims=True))
    den = jnp.sum(p, -1, keepdims=True)
    return _dot(p.astype(BF16), mv.astype(BF16)) / den


def _local_mixers_kernel(x_ref, mq_ref, w_ref, prev_ref, cos_ref, sin_ref, s0_ref, gn_ref, mk_ref, mv_ref,
                         ya_ref, yb_ref, yd_ref, conv_state_ref, sfin_ref,
                         carry_ref, s_ref, intra_ref):
    cb_ref, cc_ref, ch_ref, rq_ref, rk_ref, rv_ref, rg_ref = (
        x_ref.at[:, pl.ds(col * BRANCH_W, BRANCH_W)]
        for col in (COL_CB, COL_CC, COL_CH, COL_RQ, COL_RK, COL_RV, COL_RG))
    c = pl.program_id(1)

    @pl.when(jnp.logical_and(pl.program_id(0) == 0, c == 0))
    def _():
        _ret_fill_decay(intra_ref)

    @pl.when(c == 0)
    def _():
        carry_ref[...] = prev_ref[...]
        s_ref[...] = s0_ref[...]

    tail = _conv_body(cb_ref, cc_ref, ch_ref, w_ref, carry_ref, ya_ref)
    chunk = intra_ref.shape[1]
    for r in range(0, rq_ref.shape[0], chunk):
        _retention_body(rq_ref, rk_ref, rv_ref, rg_ref, cos_ref, sin_ref, gn_ref, yb_ref, s_ref, intra_ref,
                        slice(r, r + chunk))
    for h in range(MEM_HEADS):
        cols = slice(h * MEM_HD, (h + 1) * MEM_HD)
        yd_ref[:, cols] = _mem_head(mq_ref[:, cols], mk_ref[:, cols], mv_ref[:, cols]).astype(BF16)

    @pl.when(c == pl.num_programs(1) - 1)
    def _():
        conv_state_ref[...] = tail
        sfin_ref[...] = s_ref[...]


def _local_mixers(proj, conv_w, conv_prev, cosf, sinf, s0, gn_g, mkv, layer, *, bsz, length, rows, chunk):
    m = proj.shape[0]
    nc = length // rows
    assert length % rows == 0 and rows % chunk == 0 and chunk >= F32_SUBLANES
    tokens = mkv.shape[1]
    col = lambda c: pl.BlockSpec((rows, BRANCH_W), lambda b, i: (b * nc + i, c))
    out = pl.BlockSpec((rows, BRANCH_W), lambda b, i: (b * nc + i, 0))
    state = pl.BlockSpec((None, RET_HEADS, RET_DK, RET_DK), lambda b, i: (b, 0, 0, 0))
    conv_state = pl.BlockSpec((None, CONV_WIDTH - 1, BRANCH_W), lambda b, i: (b, 0, 0))
    rope = pl.BlockSpec((rows, RET_DK), lambda b, i: (i, 0))
    y_shape = jax.ShapeDtypeStruct((m, BRANCH_W), BF16)
    return pl.pallas_call(
        _local_mixers_kernel,
        grid=(bsz, nc),
        in_specs=[
            pl.BlockSpec((rows, (COL_RG + 1) * BRANCH_W), lambda b, i: (b * nc + i, 0)), col(COL_MQ),
            pl.BlockSpec((None, CONV_WIDTH, BRANCH_W), lambda b, i: (layer, 0, 0)),
            conv_state, rope, rope, state,
            pl.BlockSpec((None, 1, BRANCH_W), lambda b, i: (layer, 0, 0)),
            pl.BlockSpec((None, tokens, BRANCH_W), lambda b, i: (b, 0, 0)),
            pl.BlockSpec((None, tokens, BRANCH_W), lambda b, i: (b, 0, 1)),
        ],
        out_specs=[out, out, out, conv_state, state],
        out_shape=[
            y_shape, y_shape, y_shape,
            jax.ShapeDtypeStruct((bsz, CONV_WIDTH - 1, BRANCH_W), F32),
            jax.ShapeDtypeStruct((bsz, RET_HEADS, RET_DK, RET_DK), F32),
        ],
        scratch_shapes=[pltpu.VMEM((CONV_WIDTH - 1, BRANCH_W), F32),
                        pltpu.VMEM((RET_HEADS, RET_DK, RET_DK), F32),
                        pltpu.VMEM((RET_HEADS, chunk, chunk), F32)],
        compiler_params=_params(("arbitrary", "arbitrary"), 40),
        name="local_mixers",
    )(proj, proj, conv_w, conv_prev, cosf, sinf, s0, gn_g, mkv, mkv)


def _diff_lambda(lam_ref, lam_init):
    e0 = jnp.exp(jnp.sum(lam_ref[0:1, :] * lam_ref[1:2, :], -1, keepdims=True))
    e1 = jnp.exp(jnp.sum(lam_ref[2:3, :] * lam_ref[3:4, :], -1, keepdims=True))
    return e0 - e1 + lam_init


def _diff_prompt_kernel(q_ref, k_ref, v_ref, bias_ref, lam_ref, g_ref, *rest, lam_init, n_cast):
    cast_src, o_ref, cast_dst = rest[:n_cast], rest[n_cast], rest[n_cast + 1:2 * n_cast + 1]
    (k16_ref, vt_ref, sa_ref, sb_ref, m0_ref, m1_ref, l0_ref, l1_ref, acc0_ref,
     acc1_ref) = rest[2 * n_cast + 1:]
    _side_cast(cast_src, cast_dst)
    i = pl.program_id(2)
    t = q_ref.shape[0]
    nt = k_ref.shape[0] // t
    m_refs, l_refs, acc_refs = (m0_ref, m1_ref), (l0_ref, l1_ref), (acc0_ref, acc1_ref)

    @pl.when(i == 0)
    def _():
        k16_ref[...] = k_ref[...].astype(BF16)
        for j in range(nt):
            vt_ref[j] = jnp.transpose(v_ref[j * t:(j + 1) * t, :]).astype(BF16)

    qt = jnp.transpose(q_ref[...] * (DIFF_D ** -0.5 * LOG2E))
    feat = lax.broadcasted_iota(jnp.int32, qt.shape, 0)
    qts = (jnp.where(feat < DIFF_D, qt, 0.0).astype(BF16), jnp.where(feat >= DIFF_D, qt, 0.0).astype(BF16))
    for c in range(2):
        m_refs[c][...] = jnp.full(m_refs[c].shape, NEG_INF, F32)
        l_refs[c][...] = jnp.zeros(l_refs[c].shape, F32)
        acc_refs[c][...] = jnp.zeros(acc_refs[c].shape, F32)

    def scores(j, kind, dst_ref):
        kt = k16_ref[j * t:(j + 1) * t, :]
        for c in range(2):
            s = _dot(kt, qts[c])
            dst_ref[c] = s if kind == FAR_TILE else s + bias_ref[kind]

    def consume(j, src_ref):
        vt = vt_ref[j]
        s = [src_ref[c] for c in range(2)]
        m_prev = [m_refs[c][...] for c in range(2)]
        m_new = [jnp.maximum(m_prev[c], jnp.max(s[c], 0, keepdims=True)) for c in range(2)]
        p = [jnp.exp2(s[c] - m_new[c]) for c in range(2)]
        a = [jnp.exp2(m_prev[c] - m_new[c]) for c in range(2)]
        for c in range(2):
            l_refs[c][...] = a[c] * l_refs[c][...] + jnp.sum(p[c], 0, keepdims=True)
            acc_refs[c][...] = a[c] * acc_refs[c][...] + _dot(vt, p[c].astype(BF16))
            m_refs[c][...] = m_new[c]

    bufs = (sa_ref, sb_ref)
    for qi in range(nt):
        @pl.when(i == qi)
        def _(qi=qi):
            kind = lambda j: min(qi - j, FAR_TILE)
            scores(0, kind(0), bufs[0])
            for j in range(qi + 1):
                if j < qi:
                    scores(j + 1, kind(j + 1), bufs[(j + 1) % 2])
                consume(j, bufs[j % 2])

    lam = _diff_lambda(lam_ref, lam_init)
    o = acc0_ref[...] / l0_ref[...] - lam * (acc1_ref[...] / l1_ref[...])
    o = o * lax.rsqrt(jnp.mean(o * o, 0, keepdims=True) + LN_EPS) * g_ref[...]
    o_ref[...] = jnp.transpose(o * (1.0 - lam_init)).astype(BF16)


def _diff_prompt(proj, pkv, bias_tiles, diff_lambda, subln_g, layer, *, bsz, length, t, lam_init, casts=()):
    m = proj.shape[0]
    nq = length // t
    assert length % t == 0 and t % CHUNK == 0
    grid = (bsz, DIFF_HEADS, nq)
    kv = lambda c: pl.BlockSpec((length, HEAD_W), lambda b, h, i: (b, c * HEADS_PER_BRANCH + h))
    stat = pltpu.VMEM((1, t), F32)
    acc = pltpu.VMEM((DIFF_DV, t), F32)
    cast_in, cast_out, cast_shapes = _side_cast_specs(casts, grid)
    out = pl.pallas_call(
        functools.partial(_diff_prompt_kernel, lam_init=lam_init, n_cast=len(casts)),
        grid=grid,
        in_specs=[
            pl.BlockSpec((t, HEAD_W), lambda b, h, i: (b * nq + i, COL_DQ * HEADS_PER_BRANCH + h)),
            kv(KV_K), kv(KV_V),
            pl.BlockSpec((None, FAR_TILE, t, t), lambda b, h, i: (h, 0, 0, 0)),
            pl.BlockSpec((None, 4, DIFF_D), lambda b, h, i: (layer, 0, 0)),
            pl.BlockSpec((None, DIFF_DV, 1), lambda b, h, i: (layer, 0, 0)),
        ] + cast_in,
        out_specs=[pl.BlockSpec((t, HEAD_W), lambda b, h, i: (b * nq + i, h))] + cast_out,
        out_shape=[jax.ShapeDtypeStruct((m, BRANCH_W), BF16)] + cast_shapes,
        scratch_shapes=[pltpu.VMEM((length, HEAD_W), BF16), pltpu.VMEM((nq, DIFF_DV, t), BF16),
                        pltpu.VMEM((2, t, t), F32), pltpu.VMEM((2, t, t), F32),
                        stat, stat, stat, stat, acc, acc],
        compiler_params=_params(("parallel", "parallel", "arbitrary"), 40),
        name="diff_attention_prompt",
    )(proj, pkv, pkv, bias_tiles, diff_lambda, subln_g, *(c[0] for c in casts))
    return out[0], out[1:]


def _head_rows(ref, h, n):
    return ref[pl.ds(h, n, stride=HEADS_PER_BRANCH), :]


def _sample_attn_kernel(q_ref, kn_ref, vn_ref, mq_ref, kp_ref, vp_ref, mk_ref, mv_ref, bp_ref, bn_ref,
                        lam_ref, g_ref, yc_ref, yd_ref, *, lam_init):
    past = kp_ref.shape[0] // HEADS_PER_BRANCH
    tokens = mk_ref.shape[0] // HEADS_PER_BRANCH
    lam = _diff_lambda(lam_ref, lam_init)
    for h in range(HEADS_PER_BRANCH):
        cols = slice(h * HEAD_W, (h + 1) * HEAD_W)
        q = q_ref[:, cols] * (DIFF_D ** -0.5 * LOG2E)
        lane = lax.broadcasted_iota(jnp.int32, q.shape, 1)
        qs = (jnp.where(lane < DIFF_D, q, 0.0).astype(BF16), jnp.where(lane >= DIFF_D, q, 0.0).astype(BF16))
        kp = _head_rows(kp_ref, h, past).astype(BF16)
        vp = _head_rows(vp_ref, h, past).astype(BF16)
        kn = kn_ref[:, cols].astype(BF16)
        vn = vn_ref[:, cols].astype(BF16)
        outs = []
        for qc in qs:
            sp = _dot_nt(qc, kp) + bp_ref[h]
            sn = _dot_nt(qc, kn) + bn_ref[h]
            mx = jnp.maximum(jnp.max(sp, -1, keepdims=True), jnp.max(sn, -1, keepdims=True))
            pp = jnp.exp2(sp - mx)
            pn = jnp.exp2(sn - mx)
            den = jnp.sum(pp, -1, keepdims=True) + jnp.sum(pn, -1, keepdims=True)
            outs.append((_dot(pp.astype(BF16), vp) + _dot(pn.astype(BF16), vn)) / den)
        o = outs[0] - lam * outs[1]
        o = o * lax.rsqrt(jnp.mean(o * o, -1, keepdims=True) + LN_EPS) * g_ref[...]
        yc_ref[:, cols] = (o * (1.0 - lam_init)).astype(BF16)

        mk = _head_rows(mk_ref, h, tokens).astype(BF16)
        mv = _head_rows(mv_ref, h, tokens).astype(BF16)
        s = _dot_nt(mq_ref[:, cols].astype(BF16), mk) * (MEM_HD ** -0.5)
        p = jnp.exp(s - jnp.max(s, -1, keepdims=True))
        den = jnp.sum(p, -1, keepdims=True)
        yd_ref[:, cols] = (_dot(p.astype(BF16), mv) / den).astype(BF16)


def _sample_attention(proj, pkv, k_past, v_past, mem_k, mem_v, bias_past, bias_new, diff_lambda, subln_g, layer,
                      *, bsz, length, lam_init):
    m = proj.shape[0]
    col = lambda c: pl.BlockSpec((length, BRANCH_W), lambda b: (b, c))
    cache = lambda a: pl.BlockSpec((None, None) + a.shape[2:], lambda b: (layer, b, 0, 0))
    whole = lambda a: pl.BlockSpec(a.shape, lambda b: (0,) * a.ndim)
    out = pl.BlockSpec((length, BRANCH_W), lambda b: (b, 0))
    return pl.pallas_call(
        functools.partial(_sample_attn_kernel, lam_init=lam_init),
        grid=(bsz,),
        in_specs=[
            col(COL_DQ), col(KV_K), col(KV_V), col(COL_MQ),
            cache(k_past), cache(v_past), cache(mem_k), cache(mem_v),
            whole(bias_past), whole(bias_new),
            pl.BlockSpec((None, 4, DIFF_D), lambda b: (layer, 0, 0)),
            pl.BlockSpec((None, 1, DIFF_DV), lambda b: (layer, 0, 0)),
        ],
        out_specs=[out, out],
        out_shape=[jax.ShapeDtypeStruct((m, BRANCH_W), BF16), jax.ShapeDtypeStruct((m, BRANCH_W), BF16)],
        compiler_params=_params(("parallel",), 40),
        name="sample_attention",
    )(proj, pkv, pkv, proj, k_past, v_past, mem_k, mem_v, bias_past, bias_new, diff_lambda, subln_g)


def _merge_kernel(h_ref, ya_ref, yb_ref, yc_ref, yd_ref, wg_ref, wb_ref, bg_ref, *rest, n_cast):
    cast_src, o_ref, cast_dst = rest[:n_cast], rest[n_cast], rest[n_cast + 1:]
    _side_cast(cast_src, cast_dst)
    h = h_ref[...]
    acc = None
    for i, y_ref in enumerate((ya_ref, yb_ref, yc_ref, yd_ref)):
        gate = jax.nn.sigmoid(_dot(h, wg_ref[i]) + bg_ref[i])
        term = gate * _dot(y_ref[...], wb_ref[i])
        acc = term if acc is None else acc + term
    o_ref[...] = acc.astype(BF16)


def _merge(h16, ys, w_gate, w_branch, b_gate, layer, *, tm, tn, casts=()):
    m, d = h16.shape
    assert m % tm == 0 and d % tn == 0
    grid = (m // tm, d // tn)
    y_spec = pl.BlockSpec((tm, BRANCH_W), lambda i, j: (i, 0))
    cast_in, cast_out, cast_shapes = _side_cast_specs(casts, grid)
    out = pl.pallas_call(
        functools.partial(_merge_kernel, n_cast=len(casts)),
        grid=grid,
        in_specs=[
            pl.BlockSpec((tm, d), lambda i, j: (i, 0)), y_spec, y_spec, y_spec, y_spec,
            pl.BlockSpec((None, N_BRANCH, d, tn), lambda i, j: (0, 0, 0, j)),
            pl.BlockSpec((None, N_BRANCH, BRANCH_W, tn), lambda i, j: (0, 0, 0, j)),
            pl.BlockSpec((None, N_BRANCH, 1, tn), lambda i, j: (layer, 0, 0, j)),
        ] + cast_in,
        out_specs=[pl.BlockSpec((tm, tn), lambda i, j: (i, j))] + cast_out,
        out_shape=[jax.ShapeDtypeStruct((m, d), BF16)] + cast_shapes,
        compiler_params=_params(("parallel", "parallel"), 60),
        name="gated_merge",
    )(h16, *ys, w_gate, w_branch, b_gate, *(c[0] for c in casts))
    return out[0], out[1:]


def _proj_ln_kernel(m_ref, x_ref, w_ref, g_ref, b_ref, o_ref, *, alpha):
    for r in range(0, o_ref.shape[0], PROJ_ROW_CHUNK):
        rows = slice(r, r + PROJ_ROW_CHUNK)
        y = alpha * x_ref[rows, :] + _dot(m_ref[rows, :], w_ref[...])
        o_ref[rows, :] = _layer_norm(y, g_ref[...], b_ref[...])


def _proj_ln(merged, x, w_o, g, b, layer, *, alpha, tm):
    m, d = x.shape
    assert m % tm == 0
    row = pl.BlockSpec((tm, d), lambda i: (i, 0))
    vec = pl.BlockSpec((None, 1, d), lambda i: (layer, 0, 0))
    return pl.pallas_call(
        functools.partial(_proj_ln_kernel, alpha=alpha),
        grid=(m // tm,),
        in_specs=[row, row, pl.BlockSpec((None, d, d), lambda i: (0, 0, 0)), vec, vec],
        out_specs=row,
        out_shape=jax.ShapeDtypeStruct((m, d), F32),
        compiler_params=_params(("parallel",), 48),
        name="proj_ln",
    )(merged, x, w_o, g, b)


def _rope_tables(pos):
    half = RET_DK // 2
    inv = ROPE_BASE ** (-jnp.arange(half, dtype=F32) / half)
    ang = pos.astype(F32)[:, None] * inv[None, :]
    cos = jnp.cos(ang)
    sin = jnp.sin(ang)
    return jnp.concatenate([cos, cos], -1), jnp.concatenate([-sin, sin], -1)


def _t5_bucket(rel):
    nb = REL_BUCKETS // 2
    max_exact = nb // 2
    n = jnp.abs(rel)
    nf = jnp.maximum(n, 1).astype(F32)
    large = max_exact + (jnp.log(nf / max_exact) / math.log(REL_MAX_DIST / max_exact)
                         * (nb - max_exact)).astype(jnp.int32)
    large = jnp.minimum(large, nb - 1)
    return jnp.where(rel > 0, nb, 0) + jnp.where(n < max_exact, n, large)


def _masked_bias(q_pos, k_pos, rel_bias):
    bucket = _t5_bucket(k_pos[None, :] - q_pos[:, None])
    onehot = bucket[None, None] == jnp.arange(REL_BUCKETS, dtype=bucket.dtype)[None, :, None, None]
    bias = jnp.sum(jnp.where(onehot, rel_bias.astype(F32).T[:, :, None, None], 0.0), axis=1)
    allowed = (k_pos[None, :] // CHUNK) <= (q_pos[:, None] // CHUNK)
    return jnp.where(allowed[None], bias * LOG2E, NEG_INF)


def _prompt_bias_tiles(rel_bias, t):
    base = BIAS_BASE
    assert base % CHUNK == 0 and base + 1 >= REL_MAX_DIST and t % base == 0
    n = t // base
    pos = jnp.arange(base, dtype=jnp.int32)
    diag, sub, far = (jnp.swapaxes(_masked_bias(pos + d * base, pos, rel_bias), -1, -2) for d in range(3))
    shift = far[:, :1, :1]
    masked = jnp.full_like(diag, NEG_INF)
    pick = lambda delta: masked if delta > 0 else (diag - shift, sub - shift, far - shift)[min(-delta, 2)]
    tile = lambda d: jnp.concatenate(
        [jnp.concatenate([pick(kk - qq - d * n) for qq in range(n)], axis=-1) for kk in range(n)], axis=-2)
    return jnp.stack([tile(d) for d in range(FAR_TILE)], axis=1)


def _kv_proj_kernel(x_ref, w_ref, *rest):
    o_ref, ok_ref, ov_ref = rest[-3:]
    y = _dot(x_ref[...], w_ref[...])
    o_ref[...] = y
    n = x_ref.shape[0]
    for h in range(HEADS_PER_BRANCH):
        rows = pl.ds(h, n, stride=HEADS_PER_BRANCH)
        ok_ref[rows, :] = y[:, KV_K * BRANCH_W + h * HEAD_W:KV_K * BRANCH_W + (h + 1) * HEAD_W]
        ov_ref[rows, :] = y[:, KV_V * BRANCH_W + h * HEAD_W:KV_V * BRANCH_W + (h + 1) * HEAD_W]


def _kv_proj(xb, w_kv, prev, layer, depth, *, tm):
    m, d = xb.shape
    n = w_kv.shape[2]
    assert m % tm == 0 and n == KV_BLOCKS * BRANCH_W
    out = pl.BlockSpec((None, tm * HEADS_PER_BRANCH, HEAD_W), lambda i: (layer, i, 0))
    shape = jax.ShapeDtypeStruct((depth, m * HEADS_PER_BRANCH, HEAD_W), F32)
    keep = [pl.BlockSpec(memory_space=pl.ANY)] * len(prev)
    res = pl.pallas_call(
        _kv_proj_kernel,
        grid=(m // tm,),
        in_specs=[pl.BlockSpec((tm, d), lambda i: (i, 0)), pl.BlockSpec((None, d, n), lambda i: (0, 0, 0))] + keep,
        out_specs=[pl.BlockSpec((tm, n), lambda i: (i, 0)), out, out],
        out_shape=[jax.ShapeDtypeStruct((m, n), F32), shape, shape],
        input_output_aliases={2 + k: 1 + k for k in range(len(prev))},
        compiler_params=_params(("parallel",), 40),
        name="kv_proj",
    )(xb, w_kv, *prev)
    return res[0], res[1:]


CAST_IN_FFN1 = (
    ('w_in_main', 'w_in', ('rc', lambda j: jnp.where(j < W_IN_DK, j, W_IN_MQ),
                           lambda j: jnp.minimum(j, MAIN_BLOCKS - 1), MAIN_BLOCKS)),
    ('w_in_kv', 'w_in', ('rc', lambda j: jnp.where(j == 0, W_IN_DK, W_IN_DV),
                         lambda j: jnp.minimum(j, KV_BLOCKS - 1), KV_BLOCKS)))
CAST_IN_ATTN = (('w_gate', 'w_gate', 'flat'), ('w_branch', 'w_branch', 'flat'), ('w_o', 'w_o', 'flat'))
CAST_IN_MERGE = (('ffn2_w_up', 'ffn2_w_up', 'rc'), ('ffn2_w_down', 'ffn2_w_down', 'rc'))
CAST_IN_FFN2 = (('ffn1_w_up', 'ffn1_w_up', 'rc'), ('ffn1_w_down', 'ffn1_w_down', 'cr'))


def _encoder_layer(x, w16, p, layer, *, alpha, cfg, mixers_fn, kv_prev, depth, w32=None, w16_next=None):
    d = x.shape[1]
    casts = lambda names, l: tuple((w32[src], l, split) for _, src, split in names) if w32 is not None else ()
    if w32 is not None:
        assert w16['ffn1_w_down'].shape[1] // cfg['tf'] == W_IN_BLOCKS == w32['w_in'].shape[2] // BRANCH_W
    x1, done = _ffn_ln(x, w16['ffn1_w_up'], w16['ffn1_w_down'], p['ln1_g'], p['ln1_b'], layer,
                       alpha=alpha, tm=cfg['tm_ffn'], tf=cfg['tf'], casts=casts(CAST_IN_FFN1, layer))
    w16.update(zip((n for n, _, _ in CAST_IN_FFN1), done))
    proj, x1b, _ = _in_proj(x1, w16['w_in_main'], tm=cfg['tm_in'], tn=cfg['tn_in'])
    pkv, kv_new = _kv_proj(x1b, w16['w_in_kv'], kv_prev, layer, depth, tm=cfg['tm_kv'])
    y_a, y_b, y_c, y_d, conv_new, ret_new, done = mixers_fn(
        proj, pkv, 0.8 - 0.6 * math.exp(-0.3 * layer), casts(CAST_IN_ATTN, layer))
    w16.update(zip((n for n, _, _ in CAST_IN_ATTN), done))
    merged, done = _merge(x1b, (y_a, y_b, y_c, y_d), w16['w_gate'].reshape(1, N_BRANCH, d, d),
                          w16['w_branch'].reshape(1, N_BRANCH, BRANCH_W, d), p['b_gate'], layer,
                          tm=cfg['tm_merge'], tn=cfg['tn_merge'], casts=casts(CAST_IN_MERGE, layer))
    w16.update(zip((n for n, _, _ in CAST_IN_MERGE), done))
    x2 = _proj_ln(merged, x1, w16['w_o'], p['ln2_g'], p['ln2_b'], layer, alpha=alpha, tm=cfg['tm_proj'])
    x3, done = _ffn_ln(x2, w16['ffn2_w_up'], w16['ffn2_w_down'], p['ln3_g'], p['ln3_b'], layer,
                       alpha=alpha, tm=cfg['tm_ffn'], tf=cfg['tf'],
                       casts=casts(CAST_IN_FFN2, layer + 1) if w16_next is not None else ())
    if w16_next is not None:
        w16_next.update(zip((n for n, _, _ in CAST_IN_FFN2), done))
    return x3, kv_new, conv_new, ret_new


PROMPT_CFG = dict(tm_ffn=1024, tf=512, tm_in=1024, tn_in=1536, tm_kv=1024, mix_rows=256, ret_chunk=256, t_diff=512,
                  tm_merge=1024, tn_merge=512, tm_proj=512)
SAMPLE_CFG = dict(tm_ffn=256, tf=512, tm_in=256, tn_in=512, tm_kv=256, t_conv=16, ret_chunk=16,
                  tm_merge=256, tn_merge=256, tm_proj=256)


def kernel(x_prompt, x_sample, state_conv, state_ret, cache_diff_k, cache_diff_v, cache_mem_k, cache_mem_v,
           mem_prompt, ffn1_w_up, ffn1_w_down, ln1_g, ln1_b, w_in, conv_w, ret_gn_g, diff_lambda,
           diff_subln_g, w_mem_kv, w_branch, w_gate, b_gate, w_o, ln2_g, ln2_b, ffn2_w_up, ffn2_w_down,
           ln3_g, ln3_b, rel_bias):
    bp, lp, d = x_prompt.shape
    bs, ls, _ = x_sample.shape
    depth = w_in.shape[0]
    past = cache_diff_k.shape[2]
    mem_tokens = mem_prompt.shape[1]
    alpha = (2 * depth) ** 0.25
    half = MEM_HEADS * MEM_HD

    pos_p = jnp.arange(lp, dtype=jnp.int32)
    pos_s = past + jnp.arange(ls, dtype=jnp.int32)
    past_pos = jnp.arange(past, dtype=jnp.int32)
    rope_p = _rope_tables(pos_p)
    rope_s = _rope_tables(pos_s)
    bias_p = _prompt_bias_tiles(rel_bias, PROMPT_CFG['t_diff'])
    bias_s_past = _masked_bias(pos_s, past_pos, rel_bias)
    bias_s_new = _masked_bias(pos_s, pos_s, rel_bias)

    vec = lambda v: v.reshape(depth, 1, v.shape[-1])
    p = {'ln1_g': vec(ln1_g), 'ln1_b': vec(ln1_b), 'ln2_g': vec(ln2_g), 'ln2_b': vec(ln2_b),
         'ln3_g': vec(ln3_g), 'ln3_b': vec(ln3_b), 'b_gate': b_gate.reshape(depth, N_BRANCH, 1, d)}
    gn_g = vec(ret_gn_g)
    w32 = {'ffn1_w_up': ffn1_w_up, 'ffn1_w_down': ffn1_w_down, 'w_in': w_in,
           'w_gate': w_gate.reshape(depth, N_BRANCH * d, d), 'w_branch': w_branch.reshape(depth, N_BRANCH * BRANCH_W, d),
           'w_o': w_o, 'ffn2_w_up': ffn2_w_up, 'ffn2_w_down': ffn2_w_down}
    w16 = [dict() for _ in range(depth)]
    w16[0].update({n: w32[src][:1].astype(BF16) for n, src, _ in CAST_IN_FFN2})
    w_mem16 = w_mem_kv.astype(BF16)
    subln_col = diff_subln_g.reshape(depth, DIFF_DV, 1)
    subln_row = diff_subln_g.reshape(depth, 1, DIFF_DV)
    head_major = lambda c: c.reshape(depth, bs, c.shape[2] * c.shape[3], c.shape[4])
    kpast, vpast = head_major(cache_diff_k), head_major(cache_diff_v)
    mem_k_s, mem_v_s = head_major(cache_mem_k), head_major(cache_mem_v)

    yp = x_prompt.reshape(bp * lp, d)
    ys = x_sample.reshape(bs * ls, d)
    mem16 = mem_prompt.reshape(bp * mem_tokens, d).astype(BF16)
    zero_conv = jnp.zeros((bp, CONV_WIDTH - 1, BRANCH_W), F32)
    zero_ret = jnp.zeros((bp, RET_HEADS, RET_DK, RET_DK), F32)

    conv_p, ret_p, mk_p, mv_p, conv_s, ret_s = [], [], [], [], [], []
    kv_p, kv_s = [], []
    for l in range(depth):
        mkv = _matmul(mem16, w_mem16, l, tm=bp * mem_tokens, tn=512).reshape(bp, mem_tokens, 2 * half)

        def mixers_p(proj, pkv, lam_init, casts):
            y_a, y_b, y_d, c_new, r_new = _local_mixers(
                proj, conv_w, zero_conv, rope_p[0], rope_p[1], zero_ret, gn_g, mkv, l,
                bsz=bp, length=lp, rows=PROMPT_CFG['mix_rows'], chunk=PROMPT_CFG['ret_chunk'])
            y_c, done = _diff_prompt(proj, pkv, bias_p, diff_lambda, subln_col, l, bsz=bp, length=lp,
                                     t=PROMPT_CFG['t_diff'], lam_init=lam_init, casts=casts)
            return y_a, y_b, y_c, y_d, c_new, r_new, done

        yp, kv_p, c_new, r_new = _encoder_layer(
            yp, w16[l], p, l, alpha=alpha, cfg=PROMPT_CFG, mixers_fn=mixers_p, kv_prev=kv_p, depth=depth,
            w32=w32, w16_next=w16[l + 1] if l + 1 < depth else None)
        conv_p.append(c_new)
        ret_p.append(r_new)
        mk_p.append(mkv[:, :, :half].reshape(bp, mem_tokens, MEM_HEADS, MEM_HD))
        mv_p.append(mkv[:, :, half:].reshape(bp, mem_tokens, MEM_HEADS, MEM_HD))

        def mixers_s(proj, pkv, lam_init, casts):
            assert not casts
            y_a, c_new = _conv_branch(proj, conv_w, state_conv[l], l, bsz=bs, length=ls, tl=SAMPLE_CFG['t_conv'])
            y_b, r_new = _retention_branch(proj, rope_s[0], rope_s[1], state_ret[l], gn_g, l,
                                           bsz=bs, length=ls, chunk=SAMPLE_CFG['ret_chunk'])
            y_c, y_d = _sample_attention(proj, pkv, kpast, vpast, mem_k_s, mem_v_s, bias_s_past, bias_s_new,
                                         diff_lambda, subln_row, l, bsz=bs, length=ls, lam_init=lam_init)
            return y_a, y_b, y_c, y_d, c_new, r_new, ()

        ys, kv_s, c_new, r_new = _encoder_layer(
            ys, w16[l], p, l, alpha=alpha, cfg=SAMPLE_CFG, mixers_fn=mixers_s, kv_prev=kv_s, depth=depth)
        conv_s.append(c_new)
        ret_s.append(r_new)

    kv_shape = lambda b, n: (depth, b, n, DIFF_HEADS, DIFF_DV)
    return (yp.reshape(bp, lp, d), ys.reshape(bs, ls, d), jnp.stack(conv_p), jnp.stack(ret_p),
            kv_p[0].reshape(kv_shape(bp, lp)), kv_p[1].reshape(kv_shape(bp, lp)),
            jnp.stack(mk_p), jnp.stack(mv_p), jnp.stack(conv_s), jnp.stack(ret_s),
            kv_s[0].reshape(kv_shape(bs, ls)), kv_s[1].reshape(kv_shape(bs, ls)))
```

```python
import functools
import math

import jax
import jax.numpy as jnp
from jax import lax
from jax.experimental import pallas as pl
from jax.experimental.pallas import tpu as pltpu

F32 = jnp.float32
BF16 = jnp.bfloat16

CHUNK = 64
CONV_WIDTH = 3
RET_HEADS = 4
RET_DK = 128
DIFF_HEADS = 4
DIFF_D = 64
DIFF_DV = 128
MEM_HEADS = 4
MEM_HD = 128
BRANCH_W = 512
N_BRANCH = 4
REL_BUCKETS = 32
REL_MAX_DIST = 128
LN_EPS = 1e-5
ROPE_BASE = 10000.0
NEG_INF = -1e30
HEAD_W = 128
LOG2E = math.log2(math.e)
LN_ROW_CHUNK = 128
PROJ_ROW_CHUNK = 256
BIAS_BASE = 256
FAR_TILE = 2
FFN_UP_CHUNK = 256
FFN_DOWN_CHUNK = 512

W_IN_BLOCKS = 11
W_IN_DK, W_IN_DV, W_IN_MQ = 8, 9, 10
COL_CB, COL_CC, COL_CH, COL_RQ, COL_RK, COL_RV, COL_RG, COL_DQ, COL_MQ = range(9)
MAIN_BLOCKS = 9
KV_K, KV_V = range(2)
KV_BLOCKS = 2
HEADS_PER_BRANCH = BRANCH_W // HEAD_W

LANES = 128
F32_SUBLANES = 8
BF16_SUBLANES = 16
V7X_VMEM_BYTES = 64 * 1024 * 1024
MIB = 1024 * 1024


def _params(semantics, vmem_mib):
    assert vmem_mib * MIB < V7X_VMEM_BYTES
    return pltpu.CompilerParams(dimension_semantics=semantics, vmem_limit_bytes=vmem_mib * MIB)


def _layer_norm(y, g, b):
    mu = jnp.mean(y, -1, keepdims=True)
    d = y - mu
    var = jnp.mean(d * d, -1, keepdims=True)
    return d * lax.rsqrt(var + LN_EPS) * g + b


def _dot(a, b):
    return jnp.dot(a, b, preferred_element_type=F32)


def _dot_nt(a, b):
    return lax.dot_general(a, b, (((1,), (1,)), ((), ())), preferred_element_type=F32)


def _dot_tn(a, b):
    return lax.dot_general(a, b, (((0,), (0,)), ((), ())), preferred_element_type=F32)


def _side_cast_specs(casts, grid):
    in_specs, out_specs, out_shapes = [], [], []
    for src, layer, split in casts:
        _, r, c = src.shape
        if split == 'flat':
            gr, gc = math.prod(grid), 1

            def pick(*ids, grid=grid):
                flat = ids[0]
                for size, idx in zip(grid[1:], ids[1:]):
                    flat = flat * size + idx
                return (flat, 0)
        elif split == 'cr':
            gr, gc = grid[1], grid[0]
            pick = lambda i, j: (j, i)
        else:
            gr, gc = grid
            pick = lambda i, j: (i, j)
        assert r % gr == 0 and c % gc == 0
        blk = (None, r // gr, c // gc)
        assert blk[1] % BF16_SUBLANES == 0 and blk[2] % LANES == 0
        src_pick, dst_pick, c_out = pick, pick, c
        if isinstance(split, tuple):
            kind, src_col, dst_col, n_dst = split
            assert kind == 'rc'
            src_pick = lambda i, j, col=src_col: (i, col(j))
            dst_pick = lambda i, j, col=dst_col: (i, col(j))
            c_out = n_dst * blk[2]
        else:
            assert split in ('rc', 'cr', 'flat')
        in_specs.append(pl.BlockSpec(blk, lambda *ids, layer=layer, pick=src_pick: (layer,) + pick(*ids)))
        out_specs.append(pl.BlockSpec(blk, lambda *ids, pick=dst_pick: (0,) + pick(*ids)))
        out_shapes.append(jax.ShapeDtypeStruct((1, r, c_out), BF16))
    return in_specs, out_specs, out_shapes


def _side_cast(src_refs, dst_refs):
    for src_ref, dst_ref in zip(src_refs, dst_refs):
        dst_ref[...] = src_ref[...].astype(BF16)


def _ffn_ln_kernel(x_ref, wa_ref, wb_ref, wd_ref, g_ref, b_ref, *rest, alpha, n_cast):
    cast_src, o_ref, cast_dst = rest[:n_cast], rest[n_cast], rest[n_cast + 1:2 * n_cast + 1]
    xb_ref, h_ref = rest[2 * n_cast + 1:]
    j = pl.program_id(1)

    @pl.when(j == 0)
    def _():
        xb_ref[...] = x_ref[...].astype(BF16)
        o_ref[...] = jnp.zeros(o_ref.shape, F32)

    xb = xb_ref[...]
    for c in range(0, h_ref.shape[1], FFN_UP_CHUNK):
        cols = slice(c, c + FFN_UP_CHUNK)
        a = _dot(xb, wa_ref[:, cols])
        b = _dot(xb, wb_ref[:, cols])
        h_ref[:, cols] = (a * jax.nn.sigmoid(a) * b).astype(BF16)
    h = h_ref[...]
    for c in range(0, o_ref.shape[1], FFN_DOWN_CHUNK):
        cols = slice(c, c + FFN_DOWN_CHUNK)
        o_ref[:, cols] += _dot(h, wd_ref[:, cols])
    _side_cast(cast_src, cast_dst)

    @pl.when(j == pl.num_programs(1) - 1)
    def _():
        for r in range(0, o_ref.shape[0], LN_ROW_CHUNK):
            rows = slice(r, r + LN_ROW_CHUNK)
            o_ref[rows, :] = _layer_norm(alpha * x_ref[rows, :] + 0.5 * o_ref[rows, :], g_ref[...], b_ref[...])


def _ffn_ln(x, w_up, w_down, g, b, layer, *, alpha, tm, tf, casts=()):
    m, d = x.shape
    d_ff = w_down.shape[1]
    nj = d_ff // tf
    assert m % tm == 0 and d_ff % tf == 0
    grid = (m // tm, nj)
    row = lambda i, j: (i, 0)
    vec = pl.BlockSpec((None, 1, d), lambda i, j: (layer, 0, 0))
    cast_in, cast_out, cast_shapes = _side_cast_specs(casts, grid)
    out = pl.pallas_call(
        functools.partial(_ffn_ln_kernel, alpha=alpha, n_cast=len(casts)),
        grid=grid,
        in_specs=[
            pl.BlockSpec((tm, d), row),
            pl.BlockSpec((None, d, tf), lambda i, j: (0, 0, j)),
            pl.BlockSpec((None, d, tf), lambda i, j: (0, 0, j + nj)),
            pl.BlockSpec((None, tf, d), lambda i, j: (0, j, 0)),
            vec, vec,
        ] + cast_in,
        out_specs=[pl.BlockSpec((tm, d), row)] + cast_out,
        out_shape=[jax.ShapeDtypeStruct((m, d), F32)] + cast_shapes,
        scratch_shapes=[pltpu.VMEM((tm, d), BF16), pltpu.VMEM((tm, tf), BF16)],
        compiler_params=_params(("parallel", "arbitrary"), 60),
        name="ffn_ln",
    )(x, w_up, w_up, w_down, g, b, *(c[0] for c in casts))
    return out[0], out[1:]


def _in_proj_kernel(x_ref, w_ref, *rest, n_cast):
    cast_src, o_ref, xb_ref, cast_dst = rest[:n_cast], rest[n_cast], rest[n_cast + 1], rest[n_cast + 2:]

    @pl.when(pl.program_id(1) == 0)
    def _():
        xb_ref[...] = x_ref[...].astype(BF16)

    o_ref[...] = _dot(xb_ref[...], w_ref[...]).astype(o_ref.dtype)
    _side_cast(cast_src, cast_dst)


def _in_proj(x, w, *, tm, tn, casts=()):
    m, k = x.shape
    n = w.shape[2]
    assert m % tm == 0 and n % tn == 0
    grid = (m // tm, n // tn)
    cast_in, cast_out, cast_shapes = _side_cast_specs(casts, grid)
    out = pl.pallas_call(
        functools.partial(_in_proj_kernel, n_cast=len(casts)),
        grid=grid,
        in_specs=[pl.BlockSpec((tm, k), lambda i, j: (i, 0)),
                  pl.BlockSpec((None, k, tn), lambda i, j: (0, 0, j))] + cast_in,
        out_specs=[pl.BlockSpec((tm, tn), lambda i, j: (i, j)), pl.BlockSpec((tm, k), lambda i, j: (i, 0))] + cast_out,
        out_shape=[jax.ShapeDtypeStruct((m, n), BF16), jax.ShapeDtypeStruct((m, k), BF16)] + cast_shapes,
        compiler_params=_params(("parallel", "arbitrary"), 60),
        name="in_proj",
    )(x, w, *(c[0] for c in casts))
    return out[0], out[1], out[2:]


def _matmul_kernel(x_ref, w_ref, o_ref):
    o_ref[...] = _dot(x_ref[...], w_ref[...])


def _matmul(x, w, layer, *, tm, tn):
    m, k = x.shape
    n = w.shape[2]
    assert m % tm == 0 and n % tn == 0
    return pl.pallas_call(
        _matmul_kernel,
        grid=(m // tm, n // tn),
        in_specs=[pl.BlockSpec((tm, k), lambda i, j: (i, 0)),
                  pl.BlockSpec((None, k, tn), lambda i, j: (layer, 0, j))],
        out_specs=pl.BlockSpec((tm, tn), lambda i, j: (i, j)),
        out_shape=jax.ShapeDtypeStruct((m, n), F32),
        compiler_params=_params(("parallel", "parallel"), 48),
        name="matmul",
    )(x, w)


def _conv_body(cb_ref, cc_ref, ch_ref, w_ref, carry_ref, y_ref):
    u = cc_ref[...].astype(F32) * ch_ref[...].astype(F32)
    row = lax.broadcasted_iota(jnp.int32, u.shape, 0)
    c0 = carry_ref[0:1, :]
    c1 = carry_ref[1:2, :]
    u1 = jnp.where(row == 0, c1, pltpu.roll(u, 1, 0))
    u2 = jnp.where(row == 0, c0, jnp.where(row == 1, c1, pltpu.roll(u, 2, 0)))
    z = w_ref[0:1, :] * u2 + w_ref[1:2, :] * u1 + w_ref[2:3, :] * u
    y_ref[...] = (cb_ref[...].astype(F32) * z).astype(BF16)
    tail = u[u.shape[0] - F32_SUBLANES:, :][F32_SUBLANES - (CONV_WIDTH - 1):, :]
    carry_ref[...] = tail
    return tail


def _conv_kernel(cb_ref, cc_ref, ch_ref, w_ref, prev_ref, y_ref, state_ref, carry_ref):
    l = pl.program_id(1)

    @pl.when(l == 0)
    def _():
        carry_ref[...] = prev_ref[...]

    tail = _conv_body(cb_ref, cc_ref, ch_ref, w_ref, carry_ref, y_ref)

    @pl.when(l == pl.num_programs(1) - 1)
    def _():
        state_ref[...] = tail


def _conv_branch(proj, conv_w, prev, layer, *, bsz, length, tl):
    m = proj.shape[0]
    nl = length // tl
    assert length % tl == 0 and tl >= F32_SUBLANES
    col = lambda c: pl.BlockSpec((tl, BRANCH_W), lambda b, l: (b * nl + l, c))
    return pl.pallas_call(
        _conv_kernel,
        grid=(bsz, nl),
        in_specs=[
            col(COL_CB), col(COL_CC), col(COL_CH),
            pl.BlockSpec((None, CONV_WIDTH, BRANCH_W), lambda b, l: (layer, 0, 0)),
            pl.BlockSpec((None, CONV_WIDTH - 1, BRANCH_W), lambda b, l: (b, 0, 0)),
        ],
        out_specs=[
            pl.BlockSpec((tl, BRANCH_W), lambda b, l: (b * nl + l, 0)),
            pl.BlockSpec((None, CONV_WIDTH - 1, BRANCH_W), lambda b, l: (b, 0, 0)),
        ],
        out_shape=[
            jax.ShapeDtypeStruct((m, BRANCH_W), BF16),
            jax.ShapeDtypeStruct((bsz, CONV_WIDTH - 1, BRANCH_W), F32),
        ],
        scratch_shapes=[pltpu.VMEM((CONV_WIDTH - 1, BRANCH_W), F32)],
        compiler_params=_params(("parallel", "arbitrary"), 32),
        name="conv_branch",
    )(proj, proj, proj, conv_w, prev)


def _ret_log_gamma(h):
    return math.log1p(-(2.0 ** (-5.0 - h)))


def _ret_fill_decay(intra_ref):
    chunk = intra_ref.shape[1]
    ri = lax.broadcasted_iota(jnp.int32, (chunk, chunk), 0)
    ci = lax.broadcasted_iota(jnp.int32, (chunk, chunk), 1)
    rel = (ri - ci).astype(F32)
    for h in range(RET_HEADS):
        intra_ref[h] = jnp.where(rel >= 0.0, jnp.exp(jnp.maximum(rel, 0.0) * _ret_log_gamma(h)), 0.0)


def _retention_body(q_ref, k_ref, v_ref, g_ref, cos_ref, sin_ref, gn_ref, y_ref, s_ref, intra_ref, rows):
    chunk = intra_ref.shape[1]
    cosf = cos_ref[rows, :]
    sinf = sin_ref[rows, :]
    idx = lax.broadcasted_iota(jnp.int32, (chunk, 1), 0).astype(F32)
    for h in range(RET_HEADS):
        log_g = _ret_log_gamma(h)
        cols = slice(h * RET_DK, (h + 1) * RET_DK)
        q = q_ref[rows, cols].astype(F32)
        k = k_ref[rows, cols].astype(F32)
        q = q * cosf + pltpu.roll(q, RET_DK // 2, 1) * sinf
        k = (k * cosf + pltpu.roll(k, RET_DK // 2, 1) * sinf) * (RET_DK ** -0.5)
        vb = v_ref[rows, cols].astype(BF16)
        q_dec = jnp.exp((idx + 1.0) * log_g)
        k_dec = jnp.exp((chunk - 1.0 - idx) * log_g)
        c_dec = math.exp(chunk * log_g)
        qb = q.astype(BF16)
        s_prev = s_ref[h]
        att = _dot_nt(qb, k.astype(BF16)) * intra_ref[h]
        o = _dot(att.astype(BF16), vb) + _dot(qb, s_prev.astype(BF16)) * q_dec
        s_ref[h] = s_prev * c_dec + _dot_tn((k * k_dec).astype(BF16), vb)
        mu = jnp.mean(o, -1, keepdims=True)
        d = o - mu
        var = jnp.mean(d * d, -1, keepdims=True)
        ro = d * lax.rsqrt(var + LN_EPS) * gn_ref[:, cols]
        gate = g_ref[rows, cols].astype(F32)
        y_ref[rows, cols] = (gate * jax.nn.sigmoid(gate) * ro).astype(BF16)


def _retention_kernel(q_ref, k_ref, v_ref, g_ref, cos_ref, sin_ref, s0_ref, gn_ref, y_ref, sfin_ref,
                      s_ref, intra_ref):
    c = pl.program_id(1)

    @pl.when(jnp.logical_and(pl.program_id(0) == 0, c == 0))
    def _():
        _ret_fill_decay(intra_ref)

    @pl.when(c == 0)
    def _():
        s_ref[...] = s0_ref[...]

    _retention_body(q_ref, k_ref, v_ref, g_ref, cos_ref, sin_ref, gn_ref, y_ref, s_ref, intra_ref,
                    slice(0, q_ref.shape[0]))

    @pl.when(c == pl.num_programs(1) - 1)
    def _():
        sfin_ref[...] = s_ref[...]


def _retention_branch(proj, cosf, sinf, s0, gn_g, layer, *, bsz, length, chunk):
    m = proj.shape[0]
    nc = length // chunk
    assert length % chunk == 0
    col = lambda c: pl.BlockSpec((chunk, BRANCH_W), lambda b, i: (b * nc + i, c))
    state = pl.BlockSpec((None, RET_HEADS, RET_DK, RET_DK), lambda b, i: (b, 0, 0, 0))
    return pl.pallas_call(
        _retention_kernel,
        grid=(bsz, nc),
        in_specs=[
            col(COL_RQ), col(COL_RK), col(COL_RV), col(COL_RG),
            pl.BlockSpec((chunk, RET_DK), lambda b, i: (i, 0)),
            pl.BlockSpec((chunk, RET_DK), lambda b, i: (i, 0)),
            state,
            pl.BlockSpec((None, 1, BRANCH_W), lambda b, i: (layer, 0, 0)),
        ],
        out_specs=[pl.BlockSpec((chunk, BRANCH_W), lambda b, i: (b * nc + i, 0)), state],
        out_shape=[
            jax.ShapeDtypeStruct((m, BRANCH_W), BF16),
            jax.ShapeDtypeStruct((bsz, RET_HEADS, RET_DK, RET_DK), F32),
        ],
        scratch_shapes=[pltpu.VMEM((RET_HEADS, RET_DK, RET_DK), F32),
                        pltpu.VMEM((RET_HEADS, chunk, chunk), F32)],
        compiler_params=_params(("arbitrary", "arbitrary"), 32),
        name="retention_branch",
    )(proj, proj, proj, proj, cosf, sinf, s0, gn_g)


def _mem_head(q, mk, mv):
    s = _dot_nt(q.astype(BF16), mk.astype(BF16)) * (MEM_HD ** -0.5)
    p = jnp.exp(s - jnp.max(s, -1, keepdims=True))
    den = jnp.sum(p, -1, keepdims=True)
    return _dot(p.astype(BF16), mv.astype(BF16)) / den


def _local_mixers_kernel(x_ref, mq_ref, w_ref, prev_ref, cos_ref, sin_ref, s0_ref, gn_ref, mk_ref, mv_ref,
                         ya_ref, yb_ref, yd_ref, conv_state_ref, sfin_ref,
                         carry_ref, s_ref, intra_ref):
    cb_ref, cc_ref, ch_ref, rq_ref, rk_ref, rv_ref, rg_ref = (
        x_ref.at[:, pl.ds(col * BRANCH_W, BRANCH_W)]
        for col in (COL_CB, COL_CC, COL_CH, COL_RQ, COL_RK, COL_RV, COL_RG))
    c = pl.program_id(1)

    @pl.when(jnp.logical_and(pl.program_id(0) == 0, c == 0))
    def _():
        _ret_fill_decay(intra_ref)

    @pl.when(c == 0)
    def _():
        carry_ref[...] = prev_ref[...]
        s_ref[...] = s0_ref[...]

    tail = _conv_body(cb_ref, cc_ref, ch_ref, w_ref, carry_ref, ya_ref)
    chunk = intra_ref.shape[1]
    for r in range(0, rq_ref.shape[0], chunk):
        _retention_body(rq_ref, rk_ref, rv_ref, rg_ref, cos_ref, sin_ref, gn_ref, yb_ref, s_ref, intra_ref,
                        slice(r, r + chunk))
    for h in range(MEM_HEADS):
        cols = slice(h * MEM_HD, (h + 1) * MEM_HD)
        yd_ref[:, cols] = _mem_head(mq_ref[:, cols], mk_ref[:, cols], mv_ref[:, cols]).astype(BF16)

    @pl.when(c == pl.num_programs(1) - 1)
    def _():
        conv_state_ref[...] = tail
        sfin_ref[...] = s_ref[...]


def _local_mixers(proj, conv_w, conv_prev, cosf, sinf, s0, gn_g, mkv, layer, *, bsz, length, rows, chunk):
    m = proj.shape[0]
    nc = length // rows
    assert length % rows == 0 and rows % chunk == 0 and chunk >= F32_SUBLANES
    tokens = mkv.shape[1]
    col = lambda c: pl.BlockSpec((rows, BRANCH_W), lambda b, i: (b * nc + i, c))
    out = pl.BlockSpec((rows, BRANCH_W), lambda b, i: (b * nc + i, 0))
    state = pl.BlockSpec((None, RET_HEADS, RET_DK, RET_DK), lambda b, i: (b, 0, 0, 0))
    conv_state = pl.BlockSpec((None, CONV_WIDTH - 1, BRANCH_W), lambda b, i: (b, 0, 0))
    rope = pl.BlockSpec((rows, RET_DK), lambda b, i: (i, 0))
    y_shape = jax.ShapeDtypeStruct((m, BRANCH_W), BF16)
    return pl.pallas_call(
        _local_mixers_kernel,
        grid=(bsz, nc),
        in_specs=[
            pl.BlockSpec((rows, (COL_RG + 1) * BRANCH_W), lambda b, i: (b * nc + i, 0)), col(COL_MQ),
            pl.BlockSpec((None, CONV_WIDTH, BRANCH_W), lambda b, i: (layer, 0, 0)),
            conv_state, rope, rope, state,
            pl.BlockSpec((None, 1, BRANCH_W), lambda b, i: (layer, 0, 0)),
            pl.BlockSpec((None, tokens, BRANCH_W), lambda b, i: (b, 0, 0)),
            pl.BlockSpec((None, tokens, BRANCH_W), lambda b, i: (b, 0, 1)),
        ],
        out_specs=[out, out, out, conv_state, state],
        out_shape=[
            y_shape, y_shape, y_shape,
            jax.ShapeDtypeStruct((bsz, CONV_WIDTH - 1, BRANCH_W), F32),
            jax.ShapeDtypeStruct((bsz, RET_HEADS, RET_DK, RET_DK), F32),
        ],
        scratch_shapes=[pltpu.VMEM((CONV_WIDTH - 1, BRANCH_W), F32),
                        pltpu.VMEM((RET_HEADS, RET_DK, RET_DK), F32),
                        pltpu.VMEM((RET_HEADS, chunk, chunk), F32)],
        compiler_params=_params(("arbitrary", "arbitrary"), 40),
        name="local_mixers",
    )(proj, proj, conv_w, conv_prev, cosf, sinf, s0, gn_g, mkv, mkv)


def _diff_lambda(lam_ref, lam_init):
    e0 = jnp.exp(jnp.sum(lam_ref[0:1, :] * lam_ref[1:2, :], -1, keepdims=True))
    e1 = jnp.exp(jnp.sum(lam_ref[2:3, :] * lam_ref[3:4, :], -1, keepdims=True))
    return e0 - e1 + lam_init


def _diff_prompt_kernel(q_ref, k_ref, v_ref, bias_ref, lam_ref, g_ref, *rest, lam_init, n_cast):
    cast_src, o_ref, cast_dst = rest[:n_cast], rest[n_cast], rest[n_cast + 1:2 * n_cast + 1]
    (k16_ref, vt_ref, sa_ref, sb_ref, m0_ref, m1_ref, l0_ref, l1_ref, acc0_ref,
     acc1_ref) = rest[2 * n_cast + 1:]
    _side_cast(cast_src, cast_dst)
    i = pl.program_id(2)
    t = q_ref.shape[0]
    nt = k_ref.shape[0] // t
    m_refs, l_refs, acc_refs = (m0_ref, m1_ref), (l0_ref, l1_ref), (acc0_ref, acc1_ref)

    @pl.when(i == 0)
    def _():
        k16_ref[...] = k_ref[...].astype(BF16)
        for j in range(nt):
            vt_ref[j] = jnp.transpose(v_ref[j * t:(j + 1) * t, :]).astype(BF16)

    qt = jnp.transpose(q_ref[...].astype(F32) * (DIFF_D ** -0.5 * LOG2E))
    feat = lax.broadcasted_iota(jnp.int32, qt.shape, 0)
    qts = (jnp.where(feat < DIFF_D, qt, 0.0).astype(BF16), jnp.where(feat >= DIFF_D, qt, 0.0).astype(BF16))

    def scores(j, kind, dst_ref):
        kt = k16_ref[j * t:(j + 1) * t, :]
        for c in range(2):
            s = _dot(kt, qts[c])
            dst_ref[c] = s if kind == FAR_TILE else s + bias_ref[kind]

    def consume(j, src_ref):
        vt = vt_ref[j]
        s = [src_ref[c] for c in range(2)]
        if j == 0:
            m_new = [jnp.max(s[c], 0, keepdims=True) for c in range(2)]
        else:
            m_prev = [m_refs[c][...] for c in range(2)]
            m_new = [jnp.maximum(m_prev[c], jnp.max(s[c], 0, keepdims=True)) for c in range(2)]
            a = [jnp.exp2(m_prev[c] - m_new[c]) for c in range(2)]
        p = [jnp.exp2(s[c] - m_new[c]) for c in range(2)]
        for c in range(2):
            l_new = jnp.sum(p[c], 0, keepdims=True)
            acc_new = _dot(vt, p[c].astype(BF16))
            if j > 0:
                l_new = a[c] * l_refs[c][...] + l_new
                acc_new = a[c] * acc_refs[c][...] + acc_new
            l_refs[c][...] = l_new
            acc_refs[c][...] = acc_new
            m_refs[c][...] = m_new[c]

    bufs = (sa_ref, sb_ref)
    for qi in range(nt):
        @pl.when(i == qi)
        def _(qi=qi):
            kind = lambda j: min(qi - j, FAR_TILE)
            scores(0, kind(0), bufs[0])
            for j in range(qi + 1):
                if j < qi:
                    scores(j + 1, kind(j + 1), bufs[(j + 1) % 2])
                consume(j, bufs[j % 2])

    lam = _diff_lambda(lam_ref, lam_init)
    o = acc0_ref[...] / l0_ref[...] - lam * (acc1_ref[...] / l1_ref[...])
    o = o * lax.rsqrt(jnp.mean(o * o, 0, keepdims=True) + LN_EPS) * g_ref[...]
    o_ref[...] = jnp.transpose(o * (1.0 - lam_init)).astype(BF16)


def _diff_prompt(proj, pkv, bias_tiles, diff_lambda, subln_g, layer, *, bsz, length, t, lam_init, casts=()):
    m = proj.shape[0]
    nq = length // t
    assert length % t == 0 and t % CHUNK == 0
    grid = (bsz, DIFF_HEADS, nq)
    kv = lambda c: pl.BlockSpec((length, HEAD_W), lambda b, h, i: (b, c * HEADS_PER_BRANCH + h))
    stat = pltpu.VMEM((1, t), F32)
    acc = pltpu.VMEM((DIFF_DV, t), F32)
    cast_in, cast_out, cast_shapes = _side_cast_specs(casts, grid)
    out = pl.pallas_call(
        functools.partial(_diff_prompt_kernel, lam_init=lam_init, n_cast=len(casts)),
        grid=grid,
        in_specs=[
            pl.BlockSpec((t, HEAD_W), lambda b, h, i: (b * nq + i, COL_DQ * HEADS_PER_BRANCH + h)),
            kv(KV_K), kv(KV_V),
            pl.BlockSpec((None, FAR_TILE, t, t), lambda b, h, i: (h, 0, 0, 0)),
            pl.BlockSpec((None, 4, DIFF_D), lambda b, h, i: (layer, 0, 0)),
            pl.BlockSpec((None, DIFF_DV, 1), lambda b, h, i: (layer, 0, 0)),
        ] + cast_in,
        out_specs=[pl.BlockSpec((t, HEAD_W), lambda b, h, i: (b * nq + i, h))] + cast_out,
        out_shape=[jax.ShapeDtypeStruct((m, BRANCH_W), BF16)] + cast_shapes,
        scratch_shapes=[pltpu.VMEM((length, HEAD_W), BF16), pltpu.VMEM((nq, DIFF_DV, t), BF16),
                        pltpu.VMEM((2, t, t), F32), pltpu.VMEM((2, t, t), F32),
                        stat, stat, stat, stat, acc, acc],
        compiler_params=_params(("parallel", "parallel", "arbitrary"), 40),
        name="diff_attention_prompt",
    )(proj, pkv, pkv, bias_tiles, diff_lambda, subln_g, *(c[0] for c in casts))
    return out[0], out[1:]


def _head_rows(ref, h, n):
    return ref[pl.ds(h, n, stride=HEADS_PER_BRANCH), :]


def _sample_attn_kernel(q_ref, kn_ref, vn_ref, mq_ref, kp_ref, vp_ref, mk_ref, mv_ref, bp_ref, bn_ref,
                        lam_ref, g_ref, yc_ref, yd_ref, *, lam_init):
    past = kp_ref.shape[0] // HEADS_PER_BRANCH
    tokens = mk_ref.shape[0] // HEADS_PER_BRANCH
    lam = _diff_lambda(lam_ref, lam_init)
    for h in range(HEADS_PER_BRANCH):
        cols = slice(h * HEAD_W, (h + 1) * HEAD_W)
        q = q_ref[:, cols].astype(F32) * (DIFF_D ** -0.5 * LOG2E)
        lane = lax.broadcasted_iota(jnp.int32, q.shape, 1)
        qs = (jnp.where(lane < DIFF_D, q, 0.0).astype(BF16), jnp.where(lane >= DIFF_D, q, 0.0).astype(BF16))
        kp = _head_rows(kp_ref, h, past).astype(BF16)
        vp = _head_rows(vp_ref, h, past).astype(BF16)
        kn = kn_ref[:, cols].astype(BF16)
        vn = vn_ref[:, cols].astype(BF16)
        outs = []
        for qc in qs:
            sp = _dot_nt(qc, kp) + bp_ref[h]
            sn = _dot_nt(qc, kn) + bn_ref[h]
            mx = jnp.maximum(jnp.max(sp, -1, keepdims=True), jnp.max(sn, -1, keepdims=True))
            pp = jnp.exp2(sp - mx)
            pn = jnp.exp2(sn - mx)
            den = jnp.sum(pp, -1, keepdims=True) + jnp.sum(pn, -1, keepdims=True)
            outs.append((_dot(pp.astype(BF16), vp) + _dot(pn.astype(BF16), vn)) / den)
        o = outs[0] - lam * outs[1]
        o = o * lax.rsqrt(jnp.mean(o * o, -1, keepdims=True) + LN_EPS) * g_ref[...]
        yc_ref[:, cols] = (o * (1.0 - lam_init)).astype(BF16)

        mk = _head_rows(mk_ref, h, tokens).astype(BF16)
        mv = _head_rows(mv_ref, h, tokens).astype(BF16)
        s = _dot_nt(mq_ref[:, cols].astype(BF16), mk) * (MEM_HD ** -0.5)
        p = jnp.exp(s - jnp.max(s, -1, keepdims=True))
        den = jnp.sum(p, -1, keepdims=True)
        yd_ref[:, cols] = (_dot(p.astype(BF16), mv) / den).astype(BF16)


def _sample_attention(proj, pkv, k_past, v_past, mem_k, mem_v, bias_past, bias_new, diff_lambda, subln_g, layer,
                      *, bsz, length, lam_init):
    m = proj.shape[0]
    col = lambda c: pl.BlockSpec((length, BRANCH_W), lambda b: (b, c))
    cache = lambda a: pl.BlockSpec((None, None) + a.shape[2:], lambda b: (layer, b, 0, 0))
    whole = lambda a: pl.BlockSpec(a.shape, lambda b: (0,) * a.ndim)
    out = pl.BlockSpec((length, BRANCH_W), lambda b: (b, 0))
    return pl.pallas_call(
        functools.partial(_sample_attn_kernel, lam_init=lam_init),
        grid=(bsz,),
        in_specs=[
            col(COL_DQ), col(KV_K), col(KV_V), col(COL_MQ),
            cache(k_past), cache(v_past), cache(mem_k), cache(mem_v),
            whole(bias_past), whole(bias_new),
            pl.BlockSpec((None, 4, DIFF_D), lambda b: (layer, 0, 0)),
            pl.BlockSpec((None, 1, DIFF_DV), lambda b: (layer, 0, 0)),
        ],
        out_specs=[out, out],
        out_shape=[jax.ShapeDtypeStruct((m, BRANCH_W), BF16), jax.ShapeDtypeStruct((m, BRANCH_W), BF16)],
        compiler_params=_params(("parallel",), 40),
        name="sample_attention",
    )(proj, pkv, pkv, proj, k_past, v_past, mem_k, mem_v, bias_past, bias_new, diff_lambda, subln_g)


def _merge_kernel(h_ref, ya_ref, yb_ref, yc_ref, yd_ref, wg_ref, wb_ref, bg_ref, *rest, n_cast):
    cast_src, o_ref, cast_dst = rest[:n_cast], rest[n_cast], rest[n_cast + 1:]
    _side_cast(cast_src, cast_dst)
    h = h_ref[...]
    acc = None
    for i, y_ref in enumerate((ya_ref, yb_ref, yc_ref, yd_ref)):
        gate = jax.nn.sigmoid(_dot(h, wg_ref[i]) + bg_ref[i])
        term = gate * _dot(y_ref[...], wb_ref[i])
        acc = term if acc is None else acc + term
    o_ref[...] = acc.astype(BF16)


def _merge(h16, ys, w_gate, w_branch, b_gate, layer, *, tm, tn, casts=()):
    m, d = h16.shape
    assert m % tm == 0 and d % tn == 0
    grid = (m // tm, d // tn)
    y_spec = pl.BlockSpec((tm, BRANCH_W), lambda i, j: (i, 0))
    cast_in, cast_out, cast_shapes = _side_cast_specs(casts, grid)
    out = pl.pallas_call(
        functools.partial(_merge_kernel, n_cast=len(casts)),
        grid=grid,
        in_specs=[
            pl.BlockSpec((tm, d), lambda i, j: (i, 0)), y_spec, y_spec, y_spec, y_spec,
            pl.BlockSpec((None, N_BRANCH, d, tn), lambda i, j: (0, 0, 0, j)),
            pl.BlockSpec((None, N_BRANCH, BRANCH_W, tn), lambda i, j: (0, 0, 0, j)),
            pl.BlockSpec((None, N_BRANCH, 1, tn), lambda i, j: (layer, 0, 0, j)),
        ] + cast_in,
        out_specs=[pl.BlockSpec((tm, tn), lambda i, j: (i, j))] + cast_out,
        out_shape=[jax.ShapeDtypeStruct((m, d), BF16)] + cast_shapes,
        compiler_params=_params(("parallel", "parallel"), 60),
        name="gated_merge",
    )(h16, *ys, w_gate, w_branch, b_gate, *(c[0] for c in casts))
    return out[0], out[1:]


def _proj_ln_kernel(m_ref, x_ref, w_ref, g_ref, b_ref, o_ref, *, alpha):
    for r in range(0, o_ref.shape[0], PROJ_ROW_CHUNK):
        rows = slice(r, r + PROJ_ROW_CHUNK)
        y = alpha * x_ref[rows, :] + _dot(m_ref[rows, :], w_ref[...])
        o_ref[rows, :] = _layer_norm(y, g_ref[...], b_ref[...])


def _proj_ln(merged, x, w_o, g, b, layer, *, alpha, tm):
    m, d = x.shape
    assert m % tm == 0
    row = pl.BlockSpec((tm, d), lambda i: (i, 0))
    vec = pl.BlockSpec((None, 1, d), lambda i: (layer, 0, 0))
    return pl.pallas_call(
        functools.partial(_proj_ln_kernel, alpha=alpha),
        grid=(m // tm,),
        in_specs=[row, row, pl.BlockSpec((None, d, d), lambda i: (0, 0, 0)), vec, vec],
        out_specs=row,
        out_shape=jax.ShapeDtypeStruct((m, d), F32),
        compiler_params=_params(("parallel",), 48),
        name="proj_ln",
    )(merged, x, w_o, g, b)


def _rope_tables(pos):
    half = RET_DK // 2
    inv = ROPE_BASE ** (-jnp.arange(half, dtype=F32) / half)
    ang = pos.astype(F32)[:, None] * inv[None, :]
    cos = jnp.cos(ang)
    sin = jnp.sin(ang)
    return jnp.concatenate([cos, cos], -1), jnp.concatenate([-sin, sin], -1)


def _t5_bucket(rel):
    nb = REL_BUCKETS // 2
    max_exact = nb // 2
    n = jnp.abs(rel)
    nf = jnp.maximum(n, 1).astype(F32)
    large = max_exact + (jnp.log(nf / max_exact) / math.log(REL_MAX_DIST / max_exact)
                         * (nb - max_exact)).astype(jnp.int32)
    large = jnp.minimum(large, nb - 1)
    return jnp.where(rel > 0, nb, 0) + jnp.where(n < max_exact, n, large)


def _masked_bias(q_pos, k_pos, rel_bias):
    bucket = _t5_bucket(k_pos[None, :] - q_pos[:, None])
    onehot = bucket[None, None] == jnp.arange(REL_BUCKETS, dtype=bucket.dtype)[None, :, None, None]
    bias = jnp.sum(jnp.where(onehot, rel_bias.astype(F32).T[:, :, None, None], 0.0), axis=1)
    allowed = (k_pos[None, :] // CHUNK) <= (q_pos[:, None] // CHUNK)
    return jnp.where(allowed[None], bias * LOG2E, NEG_INF)


def _prompt_bias_tiles(rel_bias, t):
    base = BIAS_BASE
    assert base % CHUNK == 0 and base + 1 >= REL_MAX_DIST and t % base == 0
    n = t // base
    pos = jnp.arange(base, dtype=jnp.int32)
    diag, sub, far = (jnp.swapaxes(_masked_bias(pos + d * base, pos, rel_bias), -1, -2) for d in range(3))
    shift = far[:, :1, :1]
    masked = jnp.full_like(diag, NEG_INF)
    pick = lambda delta: masked if delta > 0 else (diag - shift, sub - shift, far - shift)[min(-delta, 2)]
    tile = lambda d: jnp.concatenate(
        [jnp.concatenate([pick(kk - qq - d * n) for qq in range(n)], axis=-1) for kk in range(n)], axis=-2)
    return jnp.stack([tile(d) for d in range(FAR_TILE)], axis=1)


def _kv_proj_kernel(x_ref, w_ref, *rest):
    o_ref, ok_ref, ov_ref = rest[-3:]
    y = _dot(x_ref[...], w_ref[...])
    o_ref[...] = y
    n = x_ref.shape[0]
    for h in range(HEADS_PER_BRANCH):
        rows = pl.ds(h, n, stride=HEADS_PER_BRANCH)
        ok_ref[rows, :] = y[:, KV_K * BRANCH_W + h * HEAD_W:KV_K * BRANCH_W + (h + 1) * HEAD_W]
        ov_ref[rows, :] = y[:, KV_V * BRANCH_W + h * HEAD_W:KV_V * BRANCH_W + (h + 1) * HEAD_W]


def _kv_proj(xb, w_kv, prev, layer, depth, *, tm):
    m, d = xb.shape
    n = w_kv.shape[2]
    assert m % tm == 0 and n == KV_BLOCKS * BRANCH_W
    out = pl.BlockSpec((None, tm * HEADS_PER_BRANCH, HEAD_W), lambda i: (layer, i, 0))
    shape = jax.ShapeDtypeStruct((depth, m * HEADS_PER_BRANCH, HEAD_W), F32)
    keep = [pl.BlockSpec(memory_space=pl.ANY)] * len(prev)
    res = pl.pallas_call(
        _kv_proj_kernel,
        grid=(m // tm,),
        in_specs=[pl.BlockSpec((tm, d), lambda i: (i, 0)), pl.BlockSpec((None, d, n), lambda i: (0, 0, 0))] + keep,
        out_specs=[pl.BlockSpec((tm, n), lambda i: (i, 0)), out, out],
        out_shape=[jax.ShapeDtypeStruct((m, n), F32), shape, shape],
        input_output_aliases={2 + k: 1 + k for k in range(len(prev))},
        compiler_params=_params(("parallel",), 40),
        name="kv_proj",
    )(xb, w_kv, *prev)
    return res[0], res[1:]


CAST_IN_FFN1 = (
    ('w_in_main', 'w_in', ('rc', lambda j: jnp.where(j < W_IN_DK, j, W_IN_MQ),
                           lambda j: jnp.minimum(j, MAIN_BLOCKS - 1), MAIN_BLOCKS)),
    ('w_in_kv', 'w_in', ('rc', lambda j: jnp.where(j == 0, W_IN_DK, W_IN_DV),
                         lambda j: jnp.minimum(j, KV_BLOCKS - 1), KV_BLOCKS)))
CAST_IN_ATTN = (('w_gate', 'w_gate', 'flat'), ('w_branch', 'w_branch', 'flat'), ('w_o', 'w_o', 'flat'))
CAST_IN_MERGE = (('ffn2_w_up', 'ffn2_w_up', 'rc'), ('ffn2_w_down', 'ffn2_w_down', 'rc'))
CAST_IN_FFN2 = (('ffn1_w_up', 'ffn1_w_up', 'rc'), ('ffn1_w_down', 'ffn1_w_down', 'cr'))


def _encoder_layer(x, w16, p, layer, *, alpha, cfg, mixers_fn, kv_prev, depth, w32=None, w16_next=None):
    d = x.shape[1]
    casts = lambda names, l: tuple((w32[src], l, split) for _, src, split in names) if w32 is not None else ()
    if w32 is not None:
        assert w16['ffn1_w_down'].shape[1] // cfg['tf'] == W_IN_BLOCKS == w32['w_in'].shape[2] // BRANCH_W
    x1, done = _ffn_ln(x, w16['ffn1_w_up'], w16['ffn1_w_down'], p['ln1_g'], p['ln1_b'], layer,
                       alpha=alpha, tm=cfg['tm_ffn'], tf=cfg['tf'], casts=casts(CAST_IN_FFN1, layer))
    w16.update(zip((n for n, _, _ in CAST_IN_FFN1), done))
    proj, x1b, _ = _in_proj(x1, w16['w_in_main'], tm=cfg['tm_in'], tn=cfg['tn_in'])
    pkv, kv_new = _kv_proj(x1b, w16['w_in_kv'], kv_prev, layer, depth, tm=cfg['tm_kv'])
    y_a, y_b, y_c, y_d, conv_new, ret_new, done = mixers_fn(
        proj, pkv, 0.8 - 0.6 * math.exp(-0.3 * layer), casts(CAST_IN_ATTN, layer))
    w16.update(zip((n for n, _, _ in CAST_IN_ATTN), done))
    merged, done = _merge(x1b, (y_a, y_b, y_c, y_d), w16['w_gate'].reshape(1, N_BRANCH, d, d),
                          w16['w_branch'].reshape(1, N_BRANCH, BRANCH_W, d), p['b_gate'], layer,
                          tm=cfg['tm_merge'], tn=cfg['tn_merge'], casts=casts(CAST_IN_MERGE, layer))
    w16.update(zip((n for n, _, _ in CAST_IN_MERGE), done))
    x2 = _proj_ln(merged, x1, w16['w_o'], p['ln2_g'], p['ln2_b'], layer, alpha=alpha, tm=cfg['tm_proj'])
    x3, done = _ffn_ln(x2, w16['ffn2_w_up'], w16['ffn2_w_down'], p['ln3_g'], p['ln3_b'], layer,
                       alpha=alpha, tm=cfg['tm_ffn'], tf=cfg['tf'],
                       casts=casts(CAST_IN_FFN2, layer + 1) if w16_next is not None else ())
    if w16_next is not None:
        w16_next.update(zip((n for n, _, _ in CAST_IN_FFN2), done))
    return x3, kv_new, conv_new, ret_new


PROMPT_CFG = dict(tm_ffn=1024, tf=512, tm_in=1024, tn_in=1536, tm_kv=1024, mix_rows=256, ret_chunk=256, t_diff=512,
                  tm_merge=1024, tn_merge=512, tm_proj=512)
SAMPLE_CFG = dict(tm_ffn=256, tf=512, tm_in=256, tn_in=512, tm_kv=256, t_conv=16, ret_chunk=16,
                  tm_merge=256, tn_merge=256, tm_proj=256)


def kernel(x_prompt, x_sample, state_conv, state_ret, cache_diff_k, cache_diff_v, cache_mem_k, cache_mem_v,
           mem_prompt, ffn1_w_up, ffn1_w_down, ln1_g, ln1_b, w_in, conv_w, ret_gn_g, diff_lambda,
           diff_subln_g, w_mem_kv, w_branch, w_gate, b_gate, w_o, ln2_g, ln2_b, ffn2_w_up, ffn2_w_down,
           ln3_g, ln3_b, rel_bias):
    bp, lp, d = x_prompt.shape
    bs, ls, _ = x_sample.shape
    depth = w_in.shape[0]
    past = cache_diff_k.shape[2]
    mem_tokens = mem_prompt.shape[1]
    alpha = (2 * depth) ** 0.25
    half = MEM_HEADS * MEM_HD

    pos_p = jnp.arange(lp, dtype=jnp.int32)
    pos_s = past + jnp.arange(ls, dtype=jnp.int32)
    past_pos = jnp.arange(past, dtype=jnp.int32)
    rope_p = _rope_tables(pos_p)
    rope_s = _rope_tables(pos_s)
    bias_p = _prompt_bias_tiles(rel_bias, PROMPT_CFG['t_diff'])
    bias_s_past = _masked_bias(pos_s, past_pos, rel_bias)
    bias_s_new = _masked_bias(pos_s, pos_s, rel_bias)

    vec = lambda v: v.reshape(depth, 1, v.shape[-1])
    p = {'ln1_g': vec(ln1_g), 'ln1_b': vec(ln1_b), 'ln2_g': vec(ln2_g), 'ln2_b': vec(ln2_b),
         'ln3_g': vec(ln3_g), 'ln3_b': vec(ln3_b), 'b_gate': b_gate.reshape(depth, N_BRANCH, 1, d)}
    gn_g = vec(ret_gn_g)
    w32 = {'ffn1_w_up': ffn1_w_up, 'ffn1_w_down': ffn1_w_down, 'w_in': w_in,
           'w_gate': w_gate.reshape(depth, N_BRANCH * d, d), 'w_branch': w_branch.reshape(depth, N_BRANCH * BRANCH_W, d),
           'w_o': w_o, 'ffn2_w_up': ffn2_w_up, 'ffn2_w_down': ffn2_w_down}
    w16 = [dict() for _ in range(depth)]
    w16[0].update({n: w32[src][:1].astype(BF16) for n, src, _ in CAST_IN_FFN2})
    w_mem16 = w_mem_kv.astype(BF16)
    subln_col = diff_subln_g.reshape(depth, DIFF_DV, 1)
    subln_row = diff_subln_g.reshape(depth, 1, DIFF_DV)
    head_major = lambda c: c.reshape(depth, bs, c.shape[2] * c.shape[3], c.shape[4])
    kpast, vpast = head_major(cache_diff_k), head_major(cache_diff_v)
    mem_k_s, mem_v_s = head_major(cache_mem_k), head_major(cache_mem_v)

    yp = x_prompt.reshape(bp * lp, d)
    ys = x_sample.reshape(bs * ls, d)
    mem16 = mem_prompt.reshape(bp * mem_tokens, d).astype(BF16)
    zero_conv = jnp.zeros((bp, CONV_WIDTH - 1, BRANCH_W), F32)
    zero_ret = jnp.zeros((bp, RET_HEADS, RET_DK, RET_DK), F32)

    conv_p, ret_p, mk_p, mv_p, conv_s, ret_s = [], [], [], [], [], []
    kv_p, kv_s = [], []
    for l in range(depth):
        mkv = _matmul(mem16, w_mem16, l, tm=bp * mem_tokens, tn=512).reshape(bp, mem_tokens, 2 * half)

        def mixers_p(proj, pkv, lam_init, casts):
            y_a, y_b, y_d, c_new, r_new = _local_mixers(
                proj, conv_w, zero_conv, rope_p[0], rope_p[1], zero_ret, gn_g, mkv, l,
                bsz=bp, length=lp, rows=PROMPT_CFG['mix_rows'], chunk=PROMPT_CFG['ret_chunk'])
            y_c, done = _diff_prompt(proj, pkv, bias_p, diff_lambda, subln_col, l, bsz=bp, length=lp,
                                     t=PROMPT_CFG['t_diff'], lam_init=lam_init, casts=casts)
            return y_a, y_b, y_c, y_d, c_new, r_new, done

        yp, kv_p, c_new, r_new = _encoder_layer(
            yp, w16[l], p, l, alpha=alpha, cfg=PROMPT_CFG, mixers_fn=mixers_p, kv_prev=kv_p, depth=depth,
            w32=w32, w16_next=w16[l + 1] if l + 1 < depth else None)
        conv_p.append(c_new)
        ret_p.append(r_new)
        mk_p.append(mkv[:, :, :half].reshape(bp, mem_tokens, MEM_HEADS, MEM_HD))
        mv_p.append(mkv[:, :, half:].reshape(bp, mem_tokens, MEM_HEADS, MEM_HD))

        def mixers_s(proj, pkv, lam_init, casts):
            assert not casts
            y_a, c_new = _conv_branch(proj, conv_w, state_conv[l], l, bsz=bs, length=ls, tl=SAMPLE_CFG['t_conv'])
            y_b, r_new = _retention_branch(proj, rope_s[0], rope_s[1], state_ret[l], gn_g, l,
                                           bsz=bs, length=ls, chunk=SAMPLE_CFG['ret_chunk'])
            y_c, y_d = _sample_attention(proj, pkv, kpast, vpast, mem_k_s, mem_v_s, bias_s_past, bias_s_new,
                                         diff_lambda, subln_row, l, bsz=bs, length=ls, lam_init=lam_init)
            return y_a, y_b, y_c, y_d, c_new, r_new, ()

        ys, kv_s, c_new, r_new = _encoder_layer(
            ys, w16[l], p, l, alpha=alpha, cfg=SAMPLE_CFG, mixers_fn=mixers_s, kv_prev=kv_s, depth=depth)
        conv_s.append(c_new)
        ret_s.append(r_new)

    kv_shape = lambda b, n: (depth, b, n, DIFF_HEADS, DIFF_DV)
    return (yp.reshape(bp, lp, d), ys.reshape(bs, ls, d), jnp.stack(conv_p), jnp.stack(ret_p),
            kv_p[0].reshape(kv_shape(bp, lp)), kv_p[1].reshape(kv_shape(bp, lp)),
            jnp.stack(mk_p), jnp.stack(mv_p), jnp.stack(conv_s), jnp.stack(ret_s),
            kv_s[0].reshape(kv_shape(bs, ls)), kv_s[1].reshape(kv_shape(bs, ls)))
```

```python
import functools
import math

import jax
import jax.numpy as jnp
from jax import lax
from jax.experimental import pallas as pl
from jax.experimental.pallas import tpu as pltpu

F32 = jnp.float32
BF16 = jnp.bfloat16

CHUNK = 64
CONV_WIDTH = 3
RET_HEADS = 4
RET_DK = 128
DIFF_HEADS = 4
DIFF_D = 64
DIFF_DV = 128
MEM_HEADS = 4
MEM_HD = 128
BRANCH_W = 512
N_BRANCH = 4
REL_BUCKETS = 32
REL_MAX_DIST = 128
LN_EPS = 1e-5
ROPE_BASE = 10000.0
NEG_INF = -1e30
HEAD_W = 128
LOG2E = math.log2(math.e)
LN_ROW_CHUNK = 128
PROJ_ROW_CHUNK = 256
BIAS_BASE = 256
FAR_TILE = 2
FFN_UP_CHUNK = 256
FFN_DOWN_CHUNK = 512

W_IN_BLOCKS = 11
W_IN_DK, W_IN_DV, W_IN_MQ = 8, 9, 10
COL_CB, COL_CC, COL_CH, COL_RQ, COL_RK, COL_RV, COL_RG, COL_DQ, COL_MQ = range(9)
MAIN_BLOCKS = 9
KV_K, KV_V = range(2)
KV_BLOCKS = 2
HEADS_PER_BRANCH = BRANCH_W // HEAD_W

LANES = 128
F32_SUBLANES = 8
BF16_SUBLANES = 16
V7X_VMEM_BYTES = 64 * 1024 * 1024
MIB = 1024 * 1024


def _params(semantics, vmem_mib):
    assert vmem_mib * MIB < V7X_VMEM_BYTES
    return pltpu.CompilerParams(dimension_semantics=semantics, vmem_limit_bytes=vmem_mib * MIB)


def _layer_norm(y, g, b):
    mu = jnp.mean(y, -1, keepdims=True)
    d = y - mu
    var = jnp.mean(d * d, -1, keepdims=True)
    return d * lax.rsqrt(var + LN_EPS) * g + b


def _dot(a, b):
    return jnp.dot(a, b, preferred_element_type=F32)


def _dot_nt(a, b):
    return lax.dot_general(a, b, (((1,), (1,)), ((), ())), preferred_element_type=F32)


def _dot_tn(a, b):
    return lax.dot_general(a, b, (((0,), (0,)), ((), ())), preferred_element_type=F32)


def _side_cast_specs(casts, grid):
    in_specs, out_specs, out_shapes = [], [], []
    for src, layer, split in casts:
        _, r, c = src.shape
        if split == 'flat':
            gr, gc = math.prod(grid), 1

            def pick(*ids, grid=grid):
                flat = ids[0]
                for size, idx in zip(grid[1:], ids[1:]):
                    flat = flat * size + idx
                return (flat, 0)
        elif split == 'cr':
            gr, gc = grid[1], grid[0]
            pick = lambda i, j: (j, i)
        else:
            gr, gc = grid
            pick = lambda i, j: (i, j)
        assert r % gr == 0 and c % gc == 0
        blk = (None, r // gr, c // gc)
        assert blk[1] % BF16_SUBLANES == 0 and blk[2] % LANES == 0
        src_pick, dst_pick, c_out = pick, pick, c
        if isinstance(split, tuple):
            kind, src_col, dst_col, n_dst = split
            assert kind == 'rc'
            src_pick = lambda i, j, col=src_col: (i, col(j))
            dst_pick = lambda i, j, col=dst_col: (i, col(j))
            c_out = n_dst * blk[2]
        else:
            assert split in ('rc', 'cr', 'flat')
        in_specs.append(pl.BlockSpec(blk, lambda *ids, layer=layer, pick=src_pick: (layer,) + pick(*ids)))
        out_specs.append(pl.BlockSpec(blk, lambda *ids, pick=dst_pick: (0,) + pick(*ids)))
        out_shapes.append(jax.ShapeDtypeStruct((1, r, c_out), BF16))
    return in_specs, out_specs, out_shapes


def _side_cast(src_refs, dst_refs):
    for src_ref, dst_ref in zip(src_refs, dst_refs):
        dst_ref[...] = src_ref[...].astype(BF16)


def _ffn_ln_kernel(x_ref, wa_ref, wb_ref, wd_ref, g_ref, b_ref, *rest, alpha, n_cast):
    cast_src, o_ref, cast_dst = rest[:n_cast], rest[n_cast], rest[n_cast + 1:2 * n_cast + 1]
    xb_ref, h_ref = rest[2 * n_cast + 1:]
    j = pl.program_id(1)

    @pl.when(j == 0)
    def _():
        xb_ref[...] = x_ref[...].astype(BF16)
        o_ref[...] = jnp.zeros(o_ref.shape, F32)

    xb = xb_ref[...]
    for c in range(0, h_ref.shape[1], FFN_UP_CHUNK):
        cols = slice(c, c + FFN_UP_CHUNK)
        a = _dot(xb, wa_ref[:, cols])
        b = _dot(xb, wb_ref[:, cols])
        h_ref[:, cols] = (a * jax.nn.sigmoid(a) * b).astype(BF16)
    h = h_ref[...]
    for c in range(0, o_ref.shape[1], FFN_DOWN_CHUNK):
        cols = slice(c, c + FFN_DOWN_CHUNK)
        o_ref[:, cols] += _dot(h, wd_ref[:, cols])
    _side_cast(cast_src, cast_dst)

    @pl.when(j == pl.num_programs(1) - 1)
    def _():
        for r in range(0, o_ref.shape[0], LN_ROW_CHUNK):
            rows = slice(r, r + LN_ROW_CHUNK)
            o_ref[rows, :] = _layer_norm(alpha * x_ref[rows, :] + 0.5 * o_ref[rows, :], g_ref[...], b_ref[...])


def _ffn_ln(x, w_up, w_down, g, b, layer, *, alpha, tm, tf, casts=()):
    m, d = x.shape
    d_ff = w_down.shape[1]
    nj = d_ff // tf
    assert m % tm == 0 and d_ff % tf == 0
    grid = (m // tm, nj)
    row = lambda i, j: (i, 0)
    vec = pl.BlockSpec((None, 1, d), lambda i, j: (layer, 0, 0))
    cast_in, cast_out, cast_shapes = _side_cast_specs(casts, grid)
    out = pl.pallas_call(
        functools.partial(_ffn_ln_kernel, alpha=alpha, n_cast=len(casts)),
        grid=grid,
        in_specs=[
            pl.BlockSpec((tm, d), row),
            pl.BlockSpec((None, d, tf), lambda i, j: (0, 0, j)),
            pl.BlockSpec((None, d, tf), lambda i, j: (0, 0, j + nj)),
            pl.BlockSpec((None, tf, d), lambda i, j: (0, j, 0)),
            vec, vec,
        ] + cast_in,
        out_specs=[pl.BlockSpec((tm, d), row)] + cast_out,
        out_shape=[jax.ShapeDtypeStruct((m, d), F32)] + cast_shapes,
        scratch_shapes=[pltpu.VMEM((tm, d), BF16), pltpu.VMEM((tm, tf), BF16)],
        compiler_params=_params(("parallel", "arbitrary"), 60),
        name="ffn_ln",
    )(x, w_up, w_up, w_down, g, b, *(c[0] for c in casts))
    return out[0], out[1:]


def _in_proj_kernel(x_ref, w_ref, *rest, n_cast):
    cast_src, o_ref, xb_ref, cast_dst = rest[:n_cast], rest[n_cast], rest[n_cast + 1], rest[n_cast + 2:]

    @pl.when(pl.program_id(1) == 0)
    def _():
        xb_ref[...] = x_ref[...].astype(BF16)

    o_ref[...] = _dot(xb_ref[...], w_ref[...]).astype(o_ref.dtype)
    _side_cast(cast_src, cast_dst)


def _in_proj(x, w, *, tm, tn, casts=()):
    m, k = x.shape
    n = w.shape[2]
    assert m % tm == 0 and n % tn == 0
    grid = (m // tm, n // tn)
    cast_in, cast_out, cast_shapes = _side_cast_specs(casts, grid)
    out = pl.pallas_call(
        functools.partial(_in_proj_kernel, n_cast=len(casts)),
        grid=grid,
        in_specs=[pl.BlockSpec((tm, k), lambda i, j: (i, 0)),
                  pl.BlockSpec((None, k, tn), lambda i, j: (0, 0, j))] + cast_in,
        out_specs=[pl.BlockSpec((tm, tn), lambda i, j: (i, j)), pl.BlockSpec((tm, k), lambda i, j: (i, 0))] + cast_out,
        out_shape=[jax.ShapeDtypeStruct((m, n), BF16), jax.ShapeDtypeStruct((m, k), BF16)] + cast_shapes,
        compiler_params=_params(("parallel", "arbitrary"), 60),
        name="in_proj",
    )(x, w, *(c[0] for c in casts))
    return out[0], out[1], out[2:]


def _matmul_kernel(x_ref, w_ref, o_ref):
    o_ref[...] = _dot(x_ref[...], w_ref[...])


def _matmul(x, w, layer, *, tm, tn):
    m, k = x.shape
    n = w.shape[2]
    assert m % tm == 0 and n % tn == 0
    return pl.pallas_call(
        _matmul_kernel,
        grid=(m // tm, n // tn),
        in_specs=[pl.BlockSpec((tm, k), lambda i, j: (i, 0)),
                  pl.BlockSpec((None, k, tn), lambda i, j: (layer, 0, j))],
        out_specs=pl.BlockSpec((tm, tn), lambda i, j: (i, j)),
        out_shape=jax.ShapeDtypeStruct((m, n), F32),
        compiler_params=_params(("parallel", "parallel"), 48),
        name="matmul",
    )(x, w)


def _conv_body(cb_ref, cc_ref, ch_ref, w_ref, carry_ref, y_ref):
    u = cc_ref[...].astype(F32) * ch_ref[...].astype(F32)
    row = lax.broadcasted_iota(jnp.int32, u.shape, 0)
    c0 = carry_ref[0:1, :]
    c1 = carry_ref[1:2, :]
    u1 = jnp.where(row == 0, c1, pltpu.roll(u, 1, 0))
    u2 = jnp.where(row == 0, c0, jnp.where(row == 1, c1, pltpu.roll(u, 2, 0)))
    z = w_ref[0:1, :] * u2 + w_ref[1:2, :] * u1 + w_ref[2:3, :] * u
    y_ref[...] = (cb_ref[...].astype(F32) * z).astype(BF16)
    tail = u[u.shape[0] - F32_SUBLANES:, :][F32_SUBLANES - (CONV_WIDTH - 1):, :]
    carry_ref[...] = tail
    return tail


def _conv_kernel(cb_ref, cc_ref, ch_ref, w_ref, prev_ref, y_ref, state_ref, carry_ref):
    l = pl.program_id(1)

    @pl.when(l == 0)
    def _():
        carry_ref[...] = prev_ref[...]

    tail = _conv_body(cb_ref, cc_ref, ch_ref, w_ref, carry_ref, y_ref)

    @pl.when(l == pl.num_programs(1) - 1)
    def _():
        state_ref[...] = tail


def _conv_branch(proj, conv_w, prev, layer, *, bsz, length, tl):
    m = proj.shape[0]
    nl = length // tl
    assert length % tl == 0 and tl >= F32_SUBLANES
    col = lambda c: pl.BlockSpec((tl, BRANCH_W), lambda b, l: (b * nl + l, c))
    return pl.pallas_call(
        _conv_kernel,
        grid=(bsz, nl),
        in_specs=[
            col(COL_CB), col(COL_CC), col(COL_CH),
            pl.BlockSpec((None, CONV_WIDTH, BRANCH_W), lambda b, l: (layer, 0, 0)),
            pl.BlockSpec((None, CONV_WIDTH - 1, BRANCH_W), lambda b, l: (b, 0, 0)),
        ],
        out_specs=[
            pl.BlockSpec((tl, BRANCH_W), lambda b, l: (b * nl + l, 0)),
            pl.BlockSpec((None, CONV_WIDTH - 1, BRANCH_W), lambda b, l: (b, 0, 0)),
        ],
        out_shape=[
            jax.ShapeDtypeStruct((m, BRANCH_W), BF16),
            jax.ShapeDtypeStruct((bsz, CONV_WIDTH - 1, BRANCH_W), F32),
        ],
        scratch_shapes=[pltpu.VMEM((CONV_WIDTH - 1, BRANCH_W), F32)],
        compiler_params=_params(("parallel", "arbitrary"), 32),
        name="conv_branch",
    )(proj, proj, proj, conv_w, prev)


def _ret_log_gamma(h):
    return math.log1p(-(2.0 ** (-5.0 - h)))


def _ret_fill_decay(intra_ref):
    chunk = intra_ref.shape[1]
    ri = lax.broadcasted_iota(jnp.int32, (chunk, chunk), 0)
    ci = lax.broadcasted_iota(jnp.int32, (chunk, chunk), 1)
    rel = (ri - ci).astype(F32)
    for h in range(RET_HEADS):
        intra_ref[h] = jnp.where(rel >= 0.0, jnp.exp(jnp.maximum(rel, 0.0) * _ret_log_gamma(h)), 0.0)


def _retention_body(q_ref, k_ref, v_ref, g_ref, cos_ref, sin_ref, gn_ref, y_ref, s_ref, intra_ref, rows):
    chunk = intra_ref.shape[1]
    cosf = cos_ref[rows, :]
    sinf = sin_ref[rows, :]
    idx = lax.broadcasted_iota(jnp.int32, (chunk, 1), 0).astype(F32)
    for h in range(RET_HEADS):
        log_g = _ret_log_gamma(h)
        cols = slice(h * RET_DK, (h + 1) * RET_DK)
        q = q_ref[rows, cols].astype(F32)
        k = k_ref[rows, cols].astype(F32)
        q = q * cosf + pltpu.roll(q, RET_DK // 2, 1) * sinf
        k = (k * cosf + pltpu.roll(k, RET_DK // 2, 1) * sinf) * (RET_DK ** -0.5)
        vb = v_ref[rows, cols].astype(BF16)
        q_dec = jnp.exp((idx + 1.0) * log_g)
        k_dec = jnp.exp((chunk - 1.0 - idx) * log_g)
        c_dec = math.exp(chunk * log_g)
        qb = q.astype(BF16)
        s_prev = s_ref[h]
        att = _dot_nt(qb, k.astype(BF16)) * intra_ref[h]
        o = _dot(att.astype(BF16), vb) + _dot(qb, s_prev.astype(BF16)) * q_dec
        s_ref[h] = s_prev * c_dec + _dot_tn((k * k_dec).astype(BF16), vb)
        mu = jnp.mean(o, -1, keepdims=True)
        d = o - mu
        var = jnp.mean(d * d, -1, keepdims=True)
        ro = d * lax.rsqrt(var + LN_EPS) * gn_ref[:, cols]
        gate = g_ref[rows, cols].astype(F32)
        y_ref[rows, cols] = (gate * jax.nn.sigmoid(gate) * ro).astype(BF16)


def _retention_kernel(q_ref, k_ref, v_ref, g_ref, cos_ref, sin_ref, s0_ref, gn_ref, y_ref, sfin_ref,
                      s_ref, intra_ref):
    c = pl.program_id(1)

    @pl.when(jnp.logical_and(pl.program_id(0) == 0, c == 0))
    def _():
        _ret_fill_decay(intra_ref)

    @pl.when(c == 0)
    def _():
        s_ref[...] = s0_ref[...]

    _retention_body(q_ref, k_ref, v_ref, g_ref, cos_ref, sin_ref, gn_ref, y_ref, s_ref, intra_ref,
                    slice(0, q_ref.shape[0]))

    @pl.when(c == pl.num_programs(1) - 1)
    def _():
        sfin_ref[...] = s_ref[...]


def _retention_branch(proj, cosf, sinf, s0, gn_g, layer, *, bsz, length, chunk):
    m = proj.shape[0]
    nc = length // chunk
    assert length % chunk == 0
    col = lambda c: pl.BlockSpec((chunk, BRANCH_W), lambda b, i: (b * nc + i, c))
    state = pl.BlockSpec((None, RET_HEADS, RET_DK, RET_DK), lambda b, i: (b, 0, 0, 0))
    return pl.pallas_call(
        _retention_kernel,
        grid=(bsz, nc),
        in_specs=[
            col(COL_RQ), col(COL_RK), col(COL_RV), col(COL_RG),
            pl.BlockSpec((chunk, RET_DK), lambda b, i: (i, 0)),
            pl.BlockSpec((chunk, RET_DK), lambda b, i: (i, 0)),
            state,
            pl.BlockSpec((None, 1, BRANCH_W), lambda b, i: (layer, 0, 0)),
        ],
        out_specs=[pl.BlockSpec((chunk, BRANCH_W), lambda b, i: (b * nc + i, 0)), state],
        out_shape=[
            jax.ShapeDtypeStruct((m, BRANCH_W), BF16),
            jax.ShapeDtypeStruct((bsz, RET_HEADS, RET_DK, RET_DK), F32),
        ],
        scratch_shapes=[pltpu.VMEM((RET_HEADS, RET_DK, RET_DK), F32),
                        pltpu.VMEM((RET_HEADS, chunk, chunk), F32)],
        compiler_params=_params(("arbitrary", "arbitrary"), 32),
        name="retention_branch",
    )(proj, proj, proj, proj, cosf, sinf, s0, gn_g)


def _local_mixers_kernel(x_ref, mq_ref, w_ref, prev_ref, cos_ref, sin_ref, s0_ref, gn_ref, mk_ref, mv_ref,
                         ya_ref, yb_ref, yd_ref, conv_state_ref, sfin_ref,
                         carry_ref, s_ref, intra_ref):
    cb_ref, cc_ref, ch_ref, rq_ref, rk_ref, rv_ref, rg_ref = (
        x_ref.at[:, pl.ds(col * BRANCH_W, BRANCH_W)]
        for col in (COL_CB, COL_CC, COL_CH, COL_RQ, COL_RK, COL_RV, COL_RG))
    c = pl.program_id(1)

    @pl.when(jnp.logical_and(pl.program_id(0) == 0, c == 0))
    def _():
        _ret_fill_decay(intra_ref)

    @pl.when(c == 0)
    def _():
        carry_ref[...] = prev_ref[...]
        s_ref[...] = s0_ref[...]

    tail = _conv_body(cb_ref, cc_ref, ch_ref, w_ref, carry_ref, ya_ref)
    chunk = intra_ref.shape[1]
    for r in range(0, rq_ref.shape[0], chunk):
        _retention_body(rq_ref, rk_ref, rv_ref, rg_ref, cos_ref, sin_ref, gn_ref, yb_ref, s_ref, intra_ref,
                        slice(r, r + chunk))
    head_cols = [slice(h * MEM_HD, (h + 1) * MEM_HD) for h in range(MEM_HEADS)]
    scores = [_dot_nt(mq_ref[:, hc].astype(BF16), mk_ref[:, hc].astype(BF16)) * (MEM_HD ** -0.5)
              for hc in head_cols]
    probs = [jnp.exp(s - jnp.max(s, -1, keepdims=True)) for s in scores]
    for hc, p in zip(head_cols, probs):
        den = jnp.sum(p, -1, keepdims=True)
        yd_ref[:, hc] = (_dot(p.astype(BF16), mv_ref[:, hc].astype(BF16)) / den).astype(BF16)

    @pl.when(c == pl.num_programs(1) - 1)
    def _():
        conv_state_ref[...] = tail
        sfin_ref[...] = s_ref[...]


def _local_mixers(proj, conv_w, conv_prev, cosf, sinf, s0, gn_g, mkv, layer, *, bsz, length, rows, chunk):
    m = proj.shape[0]
    nc = length // rows
    assert length % rows == 0 and rows % chunk == 0 and chunk >= F32_SUBLANES
    tokens = mkv.shape[1]
    col = lambda c: pl.BlockSpec((rows, BRANCH_W), lambda b, i: (b * nc + i, c))
    out = pl.BlockSpec((rows, BRANCH_W), lambda b, i: (b * nc + i, 0))
    state = pl.BlockSpec((None, RET_HEADS, RET_DK, RET_DK), lambda b, i: (b, 0, 0, 0))
    conv_state = pl.BlockSpec((None, CONV_WIDTH - 1, BRANCH_W), lambda b, i: (b, 0, 0))
    rope = pl.BlockSpec((rows, RET_DK), lambda b, i: (i, 0))
    y_shape = jax.ShapeDtypeStruct((m, BRANCH_W), BF16)
    return pl.pallas_call(
        _local_mixers_kernel,
        grid=(bsz, nc),
        in_specs=[
            pl.BlockSpec((rows, (COL_RG + 1) * BRANCH_W), lambda b, i: (b * nc + i, 0)), col(COL_MQ),
            pl.BlockSpec((None, CONV_WIDTH, BRANCH_W), lambda b, i: (layer, 0, 0)),
            conv_state, rope, rope, state,
            pl.BlockSpec((None, 1, BRANCH_W), lambda b, i: (layer, 0, 0)),
            pl.BlockSpec((None, tokens, BRANCH_W), lambda b, i: (b, 0, 0)),
            pl.BlockSpec((None, tokens, BRANCH_W), lambda b, i: (b, 0, 1)),
        ],
        out_specs=[out, out, out, conv_state, state],
        out_shape=[
            y_shape, y_shape, y_shape,
            jax.ShapeDtypeStruct((bsz, CONV_WIDTH - 1, BRANCH_W), F32),
            jax.ShapeDtypeStruct((bsz, RET_HEADS, RET_DK, RET_DK), F32),
        ],
        scratch_shapes=[pltpu.VMEM((CONV_WIDTH - 1, BRANCH_W), F32),
                        pltpu.VMEM((RET_HEADS, RET_DK, RET_DK), F32),
                        pltpu.VMEM((RET_HEADS, chunk, chunk), F32)],
        compiler_params=_params(("arbitrary", "arbitrary"), 40),
        name="local_mixers",
    )(proj, proj, conv_w, conv_prev, cosf, sinf, s0, gn_g, mkv, mkv)


def _diff_lambda(lam_ref, lam_init):
    e0 = jnp.exp(jnp.sum(lam_ref[0:1, :] * lam_ref[1:2, :], -1, keepdims=True))
    e1 = jnp.exp(jnp.sum(lam_ref[2:3, :] * lam_ref[3:4, :], -1, keepdims=True))
    return e0 - e1 + lam_init


def _diff_prompt_kernel(q_ref, k_ref, v_ref, bias_ref, lam_ref, g_ref, *rest, lam_init, n_cast):
    cast_src, o_ref, cast_dst = rest[:n_cast], rest[n_cast], rest[n_cast + 1:2 * n_cast + 1]
    (k16_ref, vt_ref, sa_ref, sb_ref, m0_ref, m1_ref, l0_ref, l1_ref, acc0_ref,
     acc1_ref) = rest[2 * n_cast + 1:]
    _side_cast(cast_src, cast_dst)
    i = pl.program_id(2)
    t = q_ref.shape[0]
    nt = k_ref.shape[0] // t
    m_refs, l_refs, acc_refs = (m0_ref, m1_ref), (l0_ref, l1_ref), (acc0_ref, acc1_ref)

    @pl.when(i == 0)
    def _():
        k16_ref[...] = k_ref[...].astype(BF16)
        for j in range(nt):
            vt_ref[j] = jnp.transpose(v_ref[j * t:(j + 1) * t, :]).astype(BF16)

    qt = jnp.transpose(q_ref[...].astype(F32) * (DIFF_D ** -0.5 * LOG2E))
    feat = lax.broadcasted_iota(jnp.int32, qt.shape, 0)
    qts = (jnp.where(feat < DIFF_D, qt, 0.0).astype(BF16), jnp.where(feat >= DIFF_D, qt, 0.0).astype(BF16))

    def scores(j, kind, dst_ref):
        kt = k16_ref[j * t:(j + 1) * t, :]
        for c in range(2):
            s = _dot(kt, qts[c])
            dst_ref[c] = s if kind == FAR_TILE else s + bias_ref[kind]

    def consume(j, src_ref):
        vt = vt_ref[j]
        s = [src_ref[c] for c in range(2)]
        if j == 0:
            m_new = [jnp.max(s[c], 0, keepdims=True) for c in range(2)]
        else:
            m_prev = [m_refs[c][...] for c in range(2)]
            m_new = [jnp.maximum(m_prev[c], jnp.max(s[c], 0, keepdims=True)) for c in range(2)]
            a = [jnp.exp2(m_prev[c] - m_new[c]) for c in range(2)]
        p = [jnp.exp2(s[c] - m_new[c]) for c in range(2)]
        for c in range(2):
            l_new = jnp.sum(p[c], 0, keepdims=True)
            acc_new = _dot(vt, p[c].astype(BF16))
            if j > 0:
                l_new = a[c] * l_refs[c][...] + l_new
                acc_new = a[c] * acc_refs[c][...] + acc_new
            l_refs[c][...] = l_new
            acc_refs[c][...] = acc_new
            m_refs[c][...] = m_new[c]

    bufs = (sa_ref, sb_ref)
    for qi in range(nt):
        @pl.when(i == qi)
        def _(qi=qi):
            kind = lambda j: min(qi - j, FAR_TILE)
            scores(0, kind(0), bufs[0])
            for j in range(qi + 1):
                if j < qi:
                    scores(j + 1, kind(j + 1), bufs[(j + 1) % 2])
                consume(j, bufs[j % 2])

    lam = _diff_lambda(lam_ref, lam_init)
    o = acc0_ref[...] / l0_ref[...] - lam * (acc1_ref[...] / l1_ref[...])
    o = o * lax.rsqrt(jnp.mean(o * o, 0, keepdims=True) + LN_EPS) * g_ref[...]
    o_ref[...] = jnp.transpose(o * (1.0 - lam_init)).astype(BF16)


def _diff_prompt(proj, pkv, bias_tiles, diff_lambda, subln_g, layer, *, bsz, length, t, lam_init, casts=()):
    m = proj.shape[0]
    nq = length // t
    assert length % t == 0 and t % CHUNK == 0
    grid = (bsz, DIFF_HEADS, nq)
    kv = lambda c: pl.BlockSpec((length, HEAD_W), lambda b, h, i: (b, c * HEADS_PER_BRANCH + h))
    stat = pltpu.VMEM((1, t), F32)
    acc = pltpu.VMEM((DIFF_DV, t), F32)
    cast_in, cast_out, cast_shapes = _side_cast_specs(casts, grid)
    out = pl.pallas_call(
        functools.partial(_diff_prompt_kernel, lam_init=lam_init, n_cast=len(casts)),
        grid=grid,
        in_specs=[
            pl.BlockSpec((t, HEAD_W), lambda b, h, i: (b * nq + i, COL_DQ * HEADS_PER_BRANCH + h)),
            kv(KV_K), kv(KV_V),
            pl.BlockSpec((None, FAR_TILE, t, t), lambda b, h, i: (h, 0, 0, 0)),
            pl.BlockSpec((None, 4, DIFF_D), lambda b, h, i: (layer, 0, 0)),
            pl.BlockSpec((None, DIFF_DV, 1), lambda b, h, i: (layer, 0, 0)),
        ] + cast_in,
        out_specs=[pl.BlockSpec((t, HEAD_W), lambda b, h, i: (b * nq + i, h))] + cast_out,
        out_shape=[jax.ShapeDtypeStruct((m, BRANCH_W), BF16)] + cast_shapes,
        scratch_shapes=[pltpu.VMEM((length, HEAD_W), BF16), pltpu.VMEM((nq, DIFF_DV, t), BF16),
                        pltpu.VMEM((2, t, t), F32), pltpu.VMEM((2, t, t), F32),
                        stat, stat, stat, stat, acc, acc],
        compiler_params=_params(("parallel", "parallel", "arbitrary"), 40),
        name="diff_attention_prompt",
    )(proj, pkv, pkv, bias_tiles, diff_lambda, subln_g, *(c[0] for c in casts))
    return out[0], out[1:]


def _head_rows(ref, h, n):
    return ref[pl.ds(h, n, stride=HEADS_PER_BRANCH), :]


def _sample_attn_kernel(q_ref, kn_ref, vn_ref, mq_ref, kp_ref, vp_ref, mk_ref, mv_ref, bp_ref, bn_ref,
                        lam_ref, g_ref, yc_ref, yd_ref, *, lam_init):
    past = kp_ref.shape[0] // HEADS_PER_BRANCH
    tokens = mk_ref.shape[0] // HEADS_PER_BRANCH
    lam = _diff_lambda(lam_ref, lam_init)
    heads = range(HEADS_PER_BRANCH)
    cols = [slice(h * HEAD_W, (h + 1) * HEAD_W) for h in heads]
    kp = [_head_rows(kp_ref, h, past).astype(BF16) for h in heads]
    kn = [kn_ref[:, cols[h]].astype(BF16) for h in heads]
    mk = [_head_rows(mk_ref, h, tokens).astype(BF16) for h in heads]
    sp, sn = {}, {}
    for h in heads:
        q = q_ref[:, cols[h]].astype(F32) * (DIFF_D ** -0.5 * LOG2E)
        lane = lax.broadcasted_iota(jnp.int32, q.shape, 1)
        qs = (jnp.where(lane < DIFF_D, q, 0.0).astype(BF16), jnp.where(lane >= DIFF_D, q, 0.0).astype(BF16))
        for c, qc in enumerate(qs):
            sp[h, c] = _dot_nt(qc, kp[h]) + bp_ref[h]
            sn[h, c] = _dot_nt(qc, kn[h]) + bn_ref[h]
    sm = [_dot_nt(mq_ref[:, cols[h]].astype(BF16), mk[h]) * (MEM_HD ** -0.5) for h in heads]

    pp, pn, den = {}, {}, {}
    for key in sp:
        mx = jnp.maximum(jnp.max(sp[key], -1, keepdims=True), jnp.max(sn[key], -1, keepdims=True))
        pp[key] = jnp.exp2(sp[key] - mx)
        pn[key] = jnp.exp2(sn[key] - mx)
        den[key] = jnp.sum(pp[key], -1, keepdims=True) + jnp.sum(pn[key], -1, keepdims=True)
    pm = [jnp.exp(sm[h] - jnp.max(sm[h], -1, keepdims=True)) for h in heads]

    vp = [_head_rows(vp_ref, h, past).astype(BF16) for h in heads]
    vn = [vn_ref[:, cols[h]].astype(BF16) for h in heads]
    mv = [_head_rows(mv_ref, h, tokens).astype(BF16) for h in heads]
    y_c, y_d = [], []
    for h in heads:
        outs = [(_dot(pp[h, c].astype(BF16), vp[h]) + _dot(pn[h, c].astype(BF16), vn[h])) / den[h, c]
                for c in range(2)]
        o = outs[0] - lam * outs[1]
        o = o * lax.rsqrt(jnp.mean(o * o, -1, keepdims=True) + LN_EPS) * g_ref[...]
        y_c.append((o * (1.0 - lam_init)).astype(BF16))
        y_d.append((_dot(pm[h].astype(BF16), mv[h]) / jnp.sum(pm[h], -1, keepdims=True)).astype(BF16))
    yc_ref[...] = jnp.concatenate(y_c, axis=-1)
    yd_ref[...] = jnp.concatenate(y_d, axis=-1)


def _sample_attention(proj, pkv, k_past, v_past, mem_k, mem_v, bias_past, bias_new, diff_lambda, subln_g, layer,
                      *, bsz, length, lam_init):
    m = proj.shape[0]
    col = lambda c: pl.BlockSpec((length, BRANCH_W), lambda b: (b, c))
    cache = lambda a: pl.BlockSpec((None, None) + a.shape[2:], lambda b: (layer, b, 0, 0))
    whole = lambda a: pl.BlockSpec(a.shape, lambda b: (0,) * a.ndim)
    out = pl.BlockSpec((length, BRANCH_W), lambda b: (b, 0))
    return pl.pallas_call(
        functools.partial(_sample_attn_kernel, lam_init=lam_init),
        grid=(bsz,),
        in_specs=[
            col(COL_DQ), col(KV_K), col(KV_V), col(COL_MQ),
            cache(k_past), cache(v_past), cache(mem_k), cache(mem_v),
            whole(bias_past), whole(bias_new),
            pl.BlockSpec((None, 4, DIFF_D), lambda b: (layer, 0, 0)),
            pl.BlockSpec((None, 1, DIFF_DV), lambda b: (layer, 0, 0)),
        ],
        out_specs=[out, out],
        out_shape=[jax.ShapeDtypeStruct((m, BRANCH_W), BF16), jax.ShapeDtypeStruct((m, BRANCH_W), BF16)],
        compiler_params=_params(("parallel",), 40),
        name="sample_attention",
    )(proj, pkv, pkv, proj, k_past, v_past, mem_k, mem_v, bias_past, bias_new, diff_lambda, subln_g)


def _merge_kernel(h_ref, ya_ref, yb_ref, yc_ref, yd_ref, wg_ref, wb_ref, bg_ref, *rest, n_cast):
    cast_src, o_ref, cast_dst = rest[:n_cast], rest[n_cast], rest[n_cast + 1:]
    _side_cast(cast_src, cast_dst)
    h = h_ref[...]
    acc = None
    for i, y_ref in enumerate((ya_ref, yb_ref, yc_ref, yd_ref)):
        gate = jax.nn.sigmoid(_dot(h, wg_ref[i]) + bg_ref[i])
        term = gate * _dot(y_ref[...], wb_ref[i])
        acc = term if acc is None else acc + term
    o_ref[...] = acc.astype(BF16)


def _merge(h16, ys, w_gate, w_branch, b_gate, layer, *, tm, tn, casts=()):
    m, d = h16.shape
    assert m % tm == 0 and d % tn == 0
    grid = (m // tm, d // tn)
    y_spec = pl.BlockSpec((tm, BRANCH_W), lambda i, j: (i, 0))
    cast_in, cast_out, cast_shapes = _side_cast_specs(casts, grid)
    out = pl.pallas_call(
        functools.partial(_merge_kernel, n_cast=len(casts)),
        grid=grid,
        in_specs=[
            pl.BlockSpec((tm, d), lambda i, j: (i, 0)), y_spec, y_spec, y_spec, y_spec,
            pl.BlockSpec((None, N_BRANCH, d, tn), lambda i, j: (0, 0, 0, j)),
            pl.BlockSpec((None, N_BRANCH, BRANCH_W, tn), lambda i, j: (0, 0, 0, j)),
            pl.BlockSpec((None, N_BRANCH, 1, tn), lambda i, j: (layer, 0, 0, j)),
        ] + cast_in,
        out_specs=[pl.BlockSpec((tm, tn), lambda i, j: (i, j))] + cast_out,
        out_shape=[jax.ShapeDtypeStruct((m, d), BF16)] + cast_shapes,
        compiler_params=_params(("parallel", "parallel"), 60),
        name="gated_merge",
    )(h16, *ys, w_gate, w_branch, b_gate, *(c[0] for c in casts))
    return out[0], out[1:]


def _proj_ln_kernel(m_ref, x_ref, w_ref, g_ref, b_ref, o_ref, *, alpha):
    for r in range(0, o_ref.shape[0], PROJ_ROW_CHUNK):
        rows = slice(r, r + PROJ_ROW_CHUNK)
        y = alpha * x_ref[rows, :] + _dot(m_ref[rows, :], w_ref[...])
        o_ref[rows, :] = _layer_norm(y, g_ref[...], b_ref[...])


def _proj_ln(merged, x, w_o, g, b, layer, *, alpha, tm):
    m, d = x.shape
    assert m % tm == 0
    row = pl.BlockSpec((tm, d), lambda i: (i, 0))
    vec = pl.BlockSpec((None, 1, d), lambda i: (layer, 0, 0))
    return pl.pallas_call(
        functools.partial(_proj_ln_kernel, alpha=alpha),
        grid=(m // tm,),
        in_specs=[row, row, pl.BlockSpec((None, d, d), lambda i: (0, 0, 0)), vec, vec],
        out_specs=row,
        out_shape=jax.ShapeDtypeStruct((m, d), F32),
        compiler_params=_params(("parallel",), 48),
        name="proj_ln",
    )(merged, x, w_o, g, b)


def _rope_tables(pos):
    half = RET_DK // 2
    inv = ROPE_BASE ** (-jnp.arange(half, dtype=F32) / half)
    ang = pos.astype(F32)[:, None] * inv[None, :]
    cos = jnp.cos(ang)
    sin = jnp.sin(ang)
    return jnp.concatenate([cos, cos], -1), jnp.concatenate([-sin, sin], -1)


def _t5_bucket(rel):
    nb = REL_BUCKETS // 2
    max_exact = nb // 2
    n = jnp.abs(rel)
    nf = jnp.maximum(n, 1).astype(F32)
    large = max_exact + (jnp.log(nf / max_exact) / math.log(REL_MAX_DIST / max_exact)
                         * (nb - max_exact)).astype(jnp.int32)
    large = jnp.minimum(large, nb - 1)
    return jnp.where(rel > 0, nb, 0) + jnp.where(n < max_exact, n, large)


def _masked_bias(q_pos, k_pos, rel_bias):
    bucket = _t5_bucket(k_pos[None, :] - q_pos[:, None])
    onehot = bucket[None, None] == jnp.arange(REL_BUCKETS, dtype=bucket.dtype)[None, :, None, None]
    bias = jnp.sum(jnp.where(onehot, rel_bias.astype(F32).T[:, :, None, None], 0.0), axis=1)
    allowed = (k_pos[None, :] // CHUNK) <= (q_pos[:, None] // CHUNK)
    return jnp.where(allowed[None], bias * LOG2E, NEG_INF)


def _prompt_bias_tiles(rel_bias, t):
    base = BIAS_BASE
    assert base % CHUNK == 0 and base + 1 >= REL_MAX_DIST and t % base == 0
    n = t // base
    pos = jnp.arange(base, dtype=jnp.int32)
    diag, sub, far = (jnp.swapaxes(_masked_bias(pos + d * base, pos, rel_bias), -1, -2) for d in range(3))
    shift = far[:, :1, :1]
    masked = jnp.full_like(diag, NEG_INF)
    pick = lambda delta: masked if delta > 0 else (diag - shift, sub - shift, far - shift)[min(-delta, 2)]
    tile = lambda d: jnp.concatenate(
        [jnp.concatenate([pick(kk - qq - d * n) for qq in range(n)], axis=-1) for kk in range(n)], axis=-2)
    return jnp.stack([tile(d) for d in range(FAR_TILE)], axis=1)


def _kv_proj_kernel(x_ref, w_ref, *rest):
    o_ref, ok_ref, ov_ref = rest[-3:]
    y = _dot(x_ref[...], w_ref[...])
    o_ref[...] = y
    n = x_ref.shape[0]
    for h in range(HEADS_PER_BRANCH):
        rows = pl.ds(h, n, stride=HEADS_PER_BRANCH)
        ok_ref[rows, :] = y[:, KV_K * BRANCH_W + h * HEAD_W:KV_K * BRANCH_W + (h + 1) * HEAD_W]
        ov_ref[rows, :] = y[:, KV_V * BRANCH_W + h * HEAD_W:KV_V * BRANCH_W + (h + 1) * HEAD_W]


def _kv_proj(xb, w_kv, prev, layer, depth, *, tm):
    m, d = xb.shape
    n = w_kv.shape[2]
    assert m % tm == 0 and n == KV_BLOCKS * BRANCH_W
    out = pl.BlockSpec((None, tm * HEADS_PER_BRANCH, HEAD_W), lambda i: (layer, i, 0))
    shape = jax.ShapeDtypeStruct((depth, m * HEADS_PER_BRANCH, HEAD_W), F32)
    keep = [pl.BlockSpec(memory_space=pl.ANY)] * len(prev)
    res = pl.pallas_call(
        _kv_proj_kernel,
        grid=(m // tm,),
        in_specs=[pl.BlockSpec((tm, d), lambda i: (i, 0)), pl.BlockSpec((None, d, n), lambda i: (0, 0, 0))] + keep,
        out_specs=[pl.BlockSpec((tm, n), lambda i: (i, 0)), out, out],
        out_shape=[jax.ShapeDtypeStruct((m, n), F32), shape, shape],
        input_output_aliases={2 + k: 1 + k for k in range(len(prev))},
        compiler_params=_params(("parallel",), 40),
        name="kv_proj",
    )(xb, w_kv, *prev)
    return res[0], res[1:]


CAST_IN_FFN1 = (
    ('w_in_main', 'w_in', ('rc', lambda j: jnp.where(j < W_IN_DK, j, W_IN_MQ),
                           lambda j: jnp.minimum(j, MAIN_BLOCKS - 1), MAIN_BLOCKS)),
    ('w_in_kv', 'w_in', ('rc', lambda j: jnp.where(j == 0, W_IN_DK, W_IN_DV),
                         lambda j: jnp.minimum(j, KV_BLOCKS - 1), KV_BLOCKS)))
CAST_IN_ATTN = (('w_gate', 'w_gate', 'flat'), ('w_branch', 'w_branch', 'flat'), ('w_o', 'w_o', 'flat'))
CAST_IN_MERGE = (('ffn2_w_up', 'ffn2_w_up', 'rc'), ('ffn2_w_down', 'ffn2_w_down', 'rc'))
CAST_IN_FFN2 = (('ffn1_w_up', 'ffn1_w_up', 'rc'), ('ffn1_w_down', 'ffn1_w_down', 'cr'))


def _encoder_layer(x, w16, p, layer, *, alpha, cfg, mixers_fn, kv_prev, depth, w32=None, w16_next=None):
    d = x.shape[1]
    casts = lambda names, l: tuple((w32[src], l, split) for _, src, split in names) if w32 is not None else ()
    if w32 is not None:
        assert w16['ffn1_w_down'].shape[1] // cfg['tf'] == W_IN_BLOCKS == w32['w_in'].shape[2] // BRANCH_W
    x1, done = _ffn_ln(x, w16['ffn1_w_up'], w16['ffn1_w_down'], p['ln1_g'], p['ln1_b'], layer,
                       alpha=alpha, tm=cfg['tm_ffn'], tf=cfg['tf'], casts=casts(CAST_IN_FFN1, layer))
    w16.update(zip((n for n, _, _ in CAST_IN_FFN1), done))
    proj, x1b, _ = _in_proj(x1, w16['w_in_main'], tm=cfg['tm_in'], tn=cfg['tn_in'])
    pkv, kv_new = _kv_proj(x1b, w16['w_in_kv'], kv_prev, layer, depth, tm=cfg['tm_kv'])
    y_a, y_b, y_c, y_d, conv_new, ret_new, done = mixers_fn(
        proj, pkv, 0.8 - 0.6 * math.exp(-0.3 * layer), casts(CAST_IN_ATTN, layer))
    w16.update(zip((n for n, _, _ in CAST_IN_ATTN), done))
    merged, done = _merge(x1b, (y_a, y_b, y_c, y_d), w16['w_gate'].reshape(1, N_BRANCH, d, d),
                          w16['w_branch'].reshape(1, N_BRANCH, BRANCH_W, d), p['b_gate'], layer,
                          tm=cfg['tm_merge'], tn=cfg['tn_merge'], casts=casts(CAST_IN_MERGE, layer))
    w16.update(zip((n for n, _, _ in CAST_IN_MERGE), done))
    x2 = _proj_ln(merged, x1, w16['w_o'], p['ln2_g'], p['ln2_b'], layer, alpha=alpha, tm=cfg['tm_proj'])
    x3, done = _ffn_ln(x2, w16['ffn2_w_up'], w16['ffn2_w_down'], p['ln3_g'], p['ln3_b'], layer,
                       alpha=alpha, tm=cfg['tm_ffn'], tf=cfg['tf'],
                       casts=casts(CAST_IN_FFN2, layer + 1) if w16_next is not None else ())
    if w16_next is not None:
        w16_next.update(zip((n for n, _, _ in CAST_IN_FFN2), done))
    return x3, kv_new, conv_new, ret_new


PROMPT_CFG = dict(tm_ffn=1024, tf=512, tm_in=1024, tn_in=1536, tm_kv=1024, mix_rows=256, ret_chunk=256, t_diff=512,
                  tm_merge=1024, tn_merge=512, tm_proj=512)
SAMPLE_CFG = dict(tm_ffn=256, tf=512, tm_in=256, tn_in=512, tm_kv=256, t_conv=16, ret_chunk=16,
                  tm_merge=256, tn_merge=256, tm_proj=256)


def kernel(x_prompt, x_sample, state_conv, state_ret, cache_diff_k, cache_diff_v, cache_mem_k, cache_mem_v,
           mem_prompt, ffn1_w_up, ffn1_w_down, ln1_g, ln1_b, w_in, conv_w, ret_gn_g, diff_lambda,
           diff_subln_g, w_mem_kv, w_branch, w_gate, b_gate, w_o, ln2_g, ln2_b, ffn2_w_up, ffn2_w_down,
           ln3_g, ln3_b, rel_bias):
    bp, lp, d = x_prompt.shape
    bs, ls, _ = x_sample.shape
    depth = w_in.shape[0]
    past = cache_diff_k.shape[2]
    mem_tokens = mem_prompt.shape[1]
    alpha = (2 * depth) ** 0.25
    half = MEM_HEADS * MEM_HD

    pos_p = jnp.arange(lp, dtype=jnp.int32)
    pos_s = past + jnp.arange(ls, dtype=jnp.int32)
    past_pos = jnp.arange(past, dtype=jnp.int32)
    rope_p = _rope_tables(pos_p)
    rope_s = _rope_tables(pos_s)
    bias_p = _prompt_bias_tiles(rel_bias, PROMPT_CFG['t_diff'])
    bias_s_past = _masked_bias(pos_s, past_pos, rel_bias)
    bias_s_new = _masked_bias(pos_s, pos_s, rel_bias)

    vec = lambda v: v.reshape(depth, 1, v.shape[-1])
    p = {'ln1_g': vec(ln1_g), 'ln1_b': vec(ln1_b), 'ln2_g': vec(ln2_g), 'ln2_b': vec(ln2_b),
         'ln3_g': vec(ln3_g), 'ln3_b': vec(ln3_b), 'b_gate': b_gate.reshape(depth, N_BRANCH, 1, d)}
    gn_g = vec(ret_gn_g)
    w32 = {'ffn1_w_up': ffn1_w_up, 'ffn1_w_down': ffn1_w_down, 'w_in': w_in,
           'w_gate': w_gate.reshape(depth, N_BRANCH * d, d), 'w_branch': w_branch.reshape(depth, N_BRANCH * BRANCH_W, d),
           'w_o': w_o, 'ffn2_w_up': ffn2_w_up, 'ffn2_w_down': ffn2_w_down}
    w16 = [dict() for _ in range(depth)]
    w16[0].update({n: w32[src][:1].astype(BF16) for n, src, _ in CAST_IN_FFN2})
    w_mem16 = w_mem_kv.astype(BF16)
    subln_col = diff_subln_g.reshape(depth, DIFF_DV, 1)
    subln_row = diff_subln_g.reshape(depth, 1, DIFF_DV)
    head_major = lambda c: c.reshape(depth, bs, c.shape[2] * c.shape[3], c.shape[4])
    kpast, vpast = head_major(cache_diff_k), head_major(cache_diff_v)
    mem_k_s, mem_v_s = head_major(cache_mem_k), head_major(cache_mem_v)

    yp = x_prompt.reshape(bp * lp, d)
    ys = x_sample.reshape(bs * ls, d)
    mem16 = mem_prompt.reshape(bp * mem_tokens, d).astype(BF16)
    zero_conv = jnp.zeros((bp, CONV_WIDTH - 1, BRANCH_W), F32)
    zero_ret = jnp.zeros((bp, RET_HEADS, RET_DK, RET_DK), F32)

    conv_p, ret_p, mk_p, mv_p, conv_s, ret_s = [], [], [], [], [], []
    kv_p, kv_s = [], []
    for l in range(depth):
        mkv = _matmul(mem16, w_mem16, l, tm=bp * mem_tokens, tn=512).reshape(bp, mem_tokens, 2 * half)

        def mixers_p(proj, pkv, lam_init, casts):
            y_a, y_b, y_d, c_new, r_new = _local_mixers(
                proj, conv_w, zero_conv, rope_p[0], rope_p[1], zero_ret, gn_g, mkv, l,
                bsz=bp, length=lp, rows=PROMPT_CFG['mix_rows'], chunk=PROMPT_CFG['ret_chunk'])
            y_c, done = _diff_prompt(proj, pkv, bias_p, diff_lambda, subln_col, l, bsz=bp, length=lp,
                                     t=PROMPT_CFG['t_diff'], lam_init=lam_init, casts=casts)
            return y_a, y_b, y_c, y_d, c_new, r_new, done

        yp, kv_p, c_new, r_new = _encoder_layer(
            yp, w16[l], p, l, alpha=alpha, cfg=PROMPT_CFG, mixers_fn=mixers_p, kv_prev=kv_p, depth=depth,
            w32=w32, w16_next=w16[l + 1] if l + 1 < depth else None)
        conv_p.append(c_new)
        ret_p.append(r_new)
        mk_p.append(mkv[:, :, :half].reshape(bp, mem_tokens, MEM_HEADS, MEM_HD))
        mv_p.append(mkv[:, :, half:].reshape(bp, mem_tokens, MEM_HEADS, MEM_HD))

        def mixers_s(proj, pkv, lam_init, casts):
            assert not casts
            y_a, c_new = _conv_branch(proj, conv_w, state_conv[l], l, bsz=bs, length=ls, tl=SAMPLE_CFG['t_conv'])
            y_b, r_new = _retention_branch(proj, rope_s[0], rope_s[1], state_ret[l], gn_g, l,
                                           bsz=bs, length=ls, chunk=SAMPLE_CFG['ret_chunk'])
            y_c, y_d = _sample_attention(proj, pkv, kpast, vpast, mem_k_s, mem_v_s, bias_s_past, bias_s_new,
                                         diff_lambda, subln_row, l, bsz=bs, length=ls, lam_init=lam_init)
            return y_a, y_b, y_c, y_d, c_new, r_new, ()

        ys, kv_s, c_new, r_new = _encoder_layer(
            ys, w16[l], p, l, alpha=alpha, cfg=SAMPLE_CFG, mixers_fn=mixers_s, kv_prev=kv_s, depth=depth)
        conv_s.append(c_new)
        ret_s.append(r_new)

    kv_shape = lambda b, n: (depth, b, n, DIFF_HEADS, DIFF_DV)
    return (yp.reshape(bp, lp, d), ys.reshape(bs, ls, d), jnp.stack(conv_p), jnp.stack(ret_p),
            kv_p[0].reshape(kv_shape(bp, lp)), kv_p[1].reshape(kv_shape(bp, lp)),
            jnp.stack(mk_p), jnp.stack(mv_p), jnp.stack(conv_s), jnp.stack(ret_s),
            kv_s[0].reshape(kv_shape(bs, ls)), kv_s[1].reshape(kv_shape(bs, ls)))
```

```python
import functools
import math

import jax
import jax.numpy as jnp
from jax import lax
from jax.experimental import pallas as pl
from jax.experimental.pallas import tpu as pltpu

F32 = jnp.float32
BF16 = jnp.bfloat16

CHUNK = 64
CONV_WIDTH = 3
RET_HEADS = 4
RET_DK = 128
DIFF_HEADS = 4
DIFF_D = 64
DIFF_DV = 128
MEM_HEADS = 4
MEM_HD = 128
BRANCH_W = 512
N_BRANCH = 4
REL_BUCKETS = 32
REL_MAX_DIST = 128
LN_EPS = 1e-5
ROPE_BASE = 10000.0
NEG_INF = -1e30
HEAD_W = 128
LOG2E = math.log2(math.e)
LN_ROW_CHUNK = 128
PROJ_ROW_CHUNK = 256
BIAS_BASE = 256
FAR_TILE = 2
FFN_UP_CHUNK = 256
FFN_DOWN_CHUNK = 512

W_IN_BLOCKS = 11
W_IN_DK, W_IN_DV, W_IN_MQ = 8, 9, 10
COL_CB, COL_CC, COL_CH, COL_RQ, COL_RK, COL_RV, COL_RG, COL_DQ, COL_MQ = range(9)
MAIN_BLOCKS = 9
KV_K, KV_V = range(2)
KV_BLOCKS = 2
HEADS_PER_BRANCH = BRANCH_W // HEAD_W

LANES = 128
F32_SUBLANES = 8
BF16_SUBLANES = 16
V7X_VMEM_BYTES = 64 * 1024 * 1024
MIB = 1024 * 1024


def _params(semantics, vmem_mib):
    assert vmem_mib * MIB < V7X_VMEM_BYTES
    return pltpu.CompilerParams(dimension_semantics=semantics, vmem_limit_bytes=vmem_mib * MIB)


def _layer_norm(y, g, b):
    mu = jnp.mean(y, -1, keepdims=True)
    d = y - mu
    var = jnp.mean(d * d, -1, keepdims=True)
    return d * lax.rsqrt(var + LN_EPS) * g + b


def _dot(a, b):
    return jnp.dot(a, b, preferred_element_type=F32)


def _dot_nt(a, b):
    return lax.dot_general(a, b, (((1,), (1,)), ((), ())), preferred_element_type=F32)


def _dot_tn(a, b):
    return lax.dot_general(a, b, (((0,), (0,)), ((), ())), preferred_element_type=F32)


def _side_cast_specs(casts, grid):
    in_specs, out_specs, out_shapes = [], [], []
    for src, layer, split in casts:
        _, r, c = src.shape
        if split == 'flat':
            gr, gc = math.prod(grid), 1

            def pick(*ids, grid=grid):
                flat = ids[0]
                for size, idx in zip(grid[1:], ids[1:]):
                    flat = flat * size + idx
                return (flat, 0)
        elif split == 'cr':
            gr, gc = grid[1], grid[0]
            pick = lambda i, j: (j, i)
        else:
            gr, gc = grid
            pick = lambda i, j: (i, j)
        assert r % gr == 0 and c % gc == 0
        blk = (None, r // gr, c // gc)
        assert blk[1] % BF16_SUBLANES == 0 and blk[2] % LANES == 0
        src_pick, dst_pick, c_out = pick, pick, c
        if isinstance(split, tuple):
            kind, src_col, dst_col, n_dst = split
            assert kind == 'rc'
            src_pick = lambda i, j, col=src_col: (i, col(j))
            dst_pick = lambda i, j, col=dst_col: (i, col(j))
            c_out = n_dst * blk[2]
        else:
            assert split in ('rc', 'cr', 'flat')
        in_specs.append(pl.BlockSpec(blk, lambda *ids, layer=layer, pick=src_pick: (layer,) + pick(*ids)))
        out_specs.append(pl.BlockSpec(blk, lambda *ids, pick=dst_pick: (0,) + pick(*ids)))
        out_shapes.append(jax.ShapeDtypeStruct((1, r, c_out), BF16))
    return in_specs, out_specs, out_shapes


def _side_cast(src_refs, dst_refs):
    for src_ref, dst_ref in zip(src_refs, dst_refs):
        dst_ref[...] = src_ref[...].astype(BF16)


def _ffn_ln_kernel(x_ref, wa_ref, wb_ref, wd_ref, g_ref, b_ref, *rest, alpha, n_cast):
    cast_src, o_ref, cast_dst = rest[:n_cast], rest[n_cast], rest[n_cast + 1:2 * n_cast + 1]
    xb_ref, h_ref = rest[2 * n_cast + 1:]
    j = pl.program_id(1)

    @pl.when(j == 0)
    def _():
        xb_ref[...] = x_ref[...].astype(BF16)
        o_ref[...] = jnp.zeros(o_ref.shape, F32)

    xb = xb_ref[...]
    for c in range(0, h_ref.shape[1], FFN_UP_CHUNK):
        cols = slice(c, c + FFN_UP_CHUNK)
        a = _dot(xb, wa_ref[:, cols])
        b = _dot(xb, wb_ref[:, cols])
        h_ref[:, cols] = (a * jax.nn.sigmoid(a) * b).astype(BF16)
    h = h_ref[...]
    for c in range(0, o_ref.shape[1], FFN_DOWN_CHUNK):
        cols = slice(c, c + FFN_DOWN_CHUNK)
        o_ref[:, cols] += _dot(h, wd_ref[:, cols])
    _side_cast(cast_src, cast_dst)

    @pl.when(j == pl.num_programs(1) - 1)
    def _():
        for r in range(0, o_ref.shape[0], LN_ROW_CHUNK):
            rows = slice(r, r + LN_ROW_CHUNK)
            o_ref[rows, :] = _layer_norm(alpha * x_ref[rows, :] + 0.5 * o_ref[rows, :], g_ref[...], b_ref[...])


def _ffn_ln(x, w_up, w_down, g, b, layer, *, alpha, tm, tf, casts=()):
    m, d = x.shape
    d_ff = w_down.shape[1]
    nj = d_ff // tf
    assert m % tm == 0 and d_ff % tf == 0
    grid = (m // tm, nj)
    row = lambda i, j: (i, 0)
    vec = pl.BlockSpec((None, 1, d), lambda i, j: (layer, 0, 0))
    cast_in, cast_out, cast_shapes = _side_cast_specs(casts, grid)
    out = pl.pallas_call(
        functools.partial(_ffn_ln_kernel, alpha=alpha, n_cast=len(casts)),
        grid=grid,
        in_specs=[
            pl.BlockSpec((tm, d), row),
            pl.BlockSpec((None, d, tf), lambda i, j: (0, 0, j)),
            pl.BlockSpec((None, d, tf), lambda i, j: (0, 0, j + nj)),
            pl.BlockSpec((None, tf, d), lambda i, j: (0, j, 0)),
            vec, vec,
        ] + cast_in,
        out_specs=[pl.BlockSpec((tm, d), row)] + cast_out,
        out_shape=[jax.ShapeDtypeStruct((m, d), F32)] + cast_shapes,
        scratch_shapes=[pltpu.VMEM((tm, d), BF16), pltpu.VMEM((tm, tf), BF16)],
        compiler_params=_params(("parallel", "arbitrary"), 60),
        name="ffn_ln",
    )(x, w_up, w_up, w_down, g, b, *(c[0] for c in casts))
    return out[0], out[1:]


def _in_proj_kernel(x_ref, w_ref, o_ref, xb_ref):
    @pl.when(pl.program_id(1) == 0)
    def _():
        xb_ref[...] = x_ref[...].astype(BF16)

    o_ref[...] = _dot(xb_ref[...], w_ref[...]).astype(o_ref.dtype)


def _in_proj(x, w, *, tm, tn):
    m, k = x.shape
    n = w.shape[2]
    assert m % tm == 0 and n % tn == 0
    return pl.pallas_call(
        _in_proj_kernel,
        grid=(m // tm, n // tn),
        in_specs=[pl.BlockSpec((tm, k), lambda i, j: (i, 0)),
                  pl.BlockSpec((None, k, tn), lambda i, j: (0, 0, j))],
        out_specs=[pl.BlockSpec((tm, tn), lambda i, j: (i, j)), pl.BlockSpec((tm, k), lambda i, j: (i, 0))],
        out_shape=[jax.ShapeDtypeStruct((m, n), BF16), jax.ShapeDtypeStruct((m, k), BF16)],
        compiler_params=_params(("parallel", "arbitrary"), 60),
        name="in_proj",
    )(x, w)


def _matmul_kernel(x_ref, w_ref, o_ref):
    o_ref[...] = _dot(x_ref[...], w_ref[...])


def _matmul(x, w, layer, *, tm, tn):
    m, k = x.shape
    n = w.shape[2]
    assert m % tm == 0 and n % tn == 0
    return pl.pallas_call(
        _matmul_kernel,
        grid=(m // tm, n // tn),
        in_specs=[pl.BlockSpec((tm, k), lambda i, j: (i, 0)),
                  pl.BlockSpec((None, k, tn), lambda i, j: (layer, 0, j))],
        out_specs=pl.BlockSpec((tm, tn), lambda i, j: (i, j)),
        out_shape=jax.ShapeDtypeStruct((m, n), F32),
        compiler_params=_params(("parallel", "parallel"), 48),
        name="matmul",
    )(x, w)


def _conv_body(cb_ref, cc_ref, ch_ref, w_ref, carry_ref, y_ref):
    u = cc_ref[...].astype(F32) * ch_ref[...].astype(F32)
    row = lax.broadcasted_iota(jnp.int32, u.shape, 0)
    c0 = carry_ref[0:1, :]
    c1 = carry_ref[1:2, :]
    u1 = jnp.where(row == 0, c1, pltpu.roll(u, 1, 0))
    u2 = jnp.where(row == 0, c0, jnp.where(row == 1, c1, pltpu.roll(u, 2, 0)))
    z = w_ref[0:1, :] * u2 + w_ref[1:2, :] * u1 + w_ref[2:3, :] * u
    y_ref[...] = (cb_ref[...].astype(F32) * z).astype(BF16)
    tail = u[u.shape[0] - F32_SUBLANES:, :][F32_SUBLANES - (CONV_WIDTH - 1):, :]
    carry_ref[...] = tail
    return tail


def _conv_kernel(cb_ref, cc_ref, ch_ref, w_ref, prev_ref, y_ref, state_ref, carry_ref):
    l = pl.program_id(1)

    @pl.when(l == 0)
    def _():
        carry_ref[...] = prev_ref[...]

    tail = _conv_body(cb_ref, cc_ref, ch_ref, w_ref, carry_ref, y_ref)

    @pl.when(l == pl.num_programs(1) - 1)
    def _():
        state_ref[...] = tail


def _conv_branch(proj, conv_w, prev, layer, *, bsz, length, tl):
    m = proj.shape[0]
    nl = length // tl
    assert length % tl == 0 and tl >= F32_SUBLANES
    col = lambda c: pl.BlockSpec((tl, BRANCH_W), lambda b, l: (b * nl + l, c))
    return pl.pallas_call(
        _conv_kernel,
        grid=(bsz, nl),
        in_specs=[
            col(COL_CB), col(COL_CC), col(COL_CH),
            pl.BlockSpec((None, CONV_WIDTH, BRANCH_W), lambda b, l: (layer, 0, 0)),
            pl.BlockSpec((None, CONV_WIDTH - 1, BRANCH_W), lambda b, l: (b, 0, 0)),
        ],
        out_specs=[
            pl.BlockSpec((tl, BRANCH_W), lambda b, l: (b * nl + l, 0)),
            pl.BlockSpec((None, CONV_WIDTH - 1, BRANCH_W), lambda b, l: (b, 0, 0)),
        ],
        out_shape=[
            jax.ShapeDtypeStruct((m, BRANCH_W), BF16),
            jax.ShapeDtypeStruct((bsz, CONV_WIDTH - 1, BRANCH_W), F32),
        ],
        scratch_shapes=[pltpu.VMEM((CONV_WIDTH - 1, BRANCH_W), F32)],
        compiler_params=_params(("parallel", "arbitrary"), 32),
        name="conv_branch",
    )(proj, proj, proj, conv_w, prev)


def _ret_log_gamma(h):
    return math.log1p(-(2.0 ** (-5.0 - h)))


def _ret_fill_decay(intra_ref):
    chunk = intra_ref.shape[1]
    ri = lax.broadcasted_iota(jnp.int32, (chunk, chunk), 0)
    ci = lax.broadcasted_iota(jnp.int32, (chunk, chunk), 1)
    rel = (ri - ci).astype(F32)
    for h in range(RET_HEADS):
        intra_ref[h] = jnp.where(rel >= 0.0, jnp.exp(jnp.maximum(rel, 0.0) * _ret_log_gamma(h)), 0.0)


def _retention_body(q_ref, k_ref, v_ref, g_ref, cos_ref, sin_ref, gn_ref, y_ref, s_ref, intra_ref, rows):
    chunk = intra_ref.shape[1]
    cosf = cos_ref[rows, :]
    sinf = sin_ref[rows, :]
    idx = lax.broadcasted_iota(jnp.int32, (chunk, 1), 0).astype(F32)
    for h in range(RET_HEADS):
        log_g = _ret_log_gamma(h)
        cols = slice(h * RET_DK, (h + 1) * RET_DK)
        q = q_ref[rows, cols].astype(F32)
        k = k_ref[rows, cols].astype(F32)
        q = q * cosf + pltpu.roll(q, RET_DK // 2, 1) * sinf
        k = (k * cosf + pltpu.roll(k, RET_DK // 2, 1) * sinf) * (RET_DK ** -0.5)
        vb = v_ref[rows, cols].astype(BF16)
        q_dec = jnp.exp((idx + 1.0) * log_g)
        k_dec = jnp.exp((chunk - 1.0 - idx) * log_g)
        c_dec = math.exp(chunk * log_g)
        qb = q.astype(BF16)
        s_prev = s_ref[h]
        att = _dot_nt(qb, k.astype(BF16)) * intra_ref[h]
        o = _dot(att.astype(BF16), vb) + _dot(qb, s_prev.astype(BF16)) * q_dec
        s_ref[h] = s_prev * c_dec + _dot_tn((k * k_dec).astype(BF16), vb)
        mu = jnp.mean(o, -1, keepdims=True)
        d = o - mu
        var = jnp.mean(d * d, -1, keepdims=True)
        ro = d * lax.rsqrt(var + LN_EPS) * gn_ref[:, cols]
        gate = g_ref[rows, cols].astype(F32)
        y_ref[rows, cols] = (gate * jax.nn.sigmoid(gate) * ro).astype(BF16)


def _retention_kernel(q_ref, k_ref, v_ref, g_ref, cos_ref, sin_ref, s0_ref, gn_ref, y_ref, sfin_ref,
                      s_ref, intra_ref):
    c = pl.program_id(1)

    @pl.when(jnp.logical_and(pl.program_id(0) == 0, c == 0))
    def _():
        _ret_fill_decay(intra_ref)

    @pl.when(c == 0)
    def _():
        s_ref[...] = s0_ref[...]

    _retention_body(q_ref, k_ref, v_ref, g_ref, cos_ref, sin_ref, gn_ref, y_ref, s_ref, intra_ref,
                    slice(0, q_ref.shape[0]))

    @pl.when(c == pl.num_programs(1) - 1)
    def _():
        sfin_ref[...] = s_ref[...]


def _retention_branch(proj, cosf, sinf, s0, gn_g, layer, *, bsz, length, chunk):
    m = proj.shape[0]
    nc = length // chunk
    assert length % chunk == 0
    col = lambda c: pl.BlockSpec((chunk, BRANCH_W), lambda b, i: (b * nc + i, c))
    state = pl.BlockSpec((None, RET_HEADS, RET_DK, RET_DK), lambda b, i: (b, 0, 0, 0))
    return pl.pallas_call(
        _retention_kernel,
        grid=(bsz, nc),
        in_specs=[
            col(COL_RQ), col(COL_RK), col(COL_RV), col(COL_RG),
            pl.BlockSpec((chunk, RET_DK), lambda b, i: (i, 0)),
            pl.BlockSpec((chunk, RET_DK), lambda b, i: (i, 0)),
            state,
            pl.BlockSpec((None, 1, BRANCH_W), lambda b, i: (layer, 0, 0)),
        ],
        out_specs=[pl.BlockSpec((chunk, BRANCH_W), lambda b, i: (b * nc + i, 0)), state],
        out_shape=[
            jax.ShapeDtypeStruct((m, BRANCH_W), BF16),
            jax.ShapeDtypeStruct((bsz, RET_HEADS, RET_DK, RET_DK), F32),
        ],
        scratch_shapes=[pltpu.VMEM((RET_HEADS, RET_DK, RET_DK), F32),
                        pltpu.VMEM((RET_HEADS, chunk, chunk), F32)],
        compiler_params=_params(("arbitrary", "arbitrary"), 32),
        name="retention_branch",
    )(proj, proj, proj, proj, cosf, sinf, s0, gn_g)


def _local_mixers_kernel(x_ref, mq_ref, w_ref, prev_ref, cos_ref, sin_ref, s0_ref, gn_ref, mk_ref, mv_ref,
                         ya_ref, yb_ref, yd_ref, conv_state_ref, sfin_ref,
                         carry_ref, s_ref, intra_ref):
    cb_ref, cc_ref, ch_ref, rq_ref, rk_ref, rv_ref, rg_ref = (
        x_ref.at[:, pl.ds(col * BRANCH_W, BRANCH_W)]
        for col in (COL_CB, COL_CC, COL_CH, COL_RQ, COL_RK, COL_RV, COL_RG))
    c = pl.program_id(1)

    @pl.when(jnp.logical_and(pl.program_id(0) == 0, c == 0))
    def _():
        _ret_fill_decay(intra_ref)

    @pl.when(c == 0)
    def _():
        carry_ref[...] = prev_ref[...]
        s_ref[...] = s0_ref[...]

    tail = _conv_body(cb_ref, cc_ref, ch_ref, w_ref, carry_ref, ya_ref)
    chunk = intra_ref.shape[1]
    for r in range(0, rq_ref.shape[0], chunk):
        _retention_body(rq_ref, rk_ref, rv_ref, rg_ref, cos_ref, sin_ref, gn_ref, yb_ref, s_ref, intra_ref,
                        slice(r, r + chunk))
    head_cols = [slice(h * MEM_HD, (h + 1) * MEM_HD) for h in range(MEM_HEADS)]
    scores = [_dot_nt(mq_ref[:, hc].astype(BF16), mk_ref[:, hc].astype(BF16)) * (MEM_HD ** -0.5)
              for hc in head_cols]
    probs = [jnp.exp(s - jnp.max(s, -1, keepdims=True)) for s in scores]
    for hc, p in zip(head_cols, probs):
        den = jnp.sum(p, -1, keepdims=True)
        yd_ref[:, hc] = (_dot(p.astype(BF16), mv_ref[:, hc].astype(BF16)) / den).astype(BF16)

    @pl.when(c == pl.num_programs(1) - 1)
    def _():
        conv_state_ref[...] = tail
        sfin_ref[...] = s_ref[...]


def _local_mixers(proj, conv_w, conv_prev, cosf, sinf, s0, gn_g, mkv, layer, *, bsz, length, rows, chunk):
    m = proj.shape[0]
    nc = length // rows
    assert length % rows == 0 and rows % chunk == 0 and chunk >= F32_SUBLANES
    tokens = mkv.shape[1]
    col = lambda c: pl.BlockSpec((rows, BRANCH_W), lambda b, i: (b * nc + i, c))
    out = pl.BlockSpec((rows, BRANCH_W), lambda b, i: (b * nc + i, 0))
    state = pl.BlockSpec((None, RET_HEADS, RET_DK, RET_DK), lambda b, i: (b, 0, 0, 0))
    conv_state = pl.BlockSpec((None, CONV_WIDTH - 1, BRANCH_W), lambda b, i: (b, 0, 0))
    rope = pl.BlockSpec((rows, RET_DK), lambda b, i: (i, 0))
    y_shape = jax.ShapeDtypeStruct((m, BRANCH_W), BF16)
    return pl.pallas_call(
        _local_mixers_kernel,
        grid=(bsz, nc),
        in_specs=[
            pl.BlockSpec((rows, (COL_RG + 1) * BRANCH_W), lambda b, i: (b * nc + i, 0)), col(COL_MQ),
            pl.BlockSpec((None, CONV_WIDTH, BRANCH_W), lambda b, i: (layer, 0, 0)),
            conv_state, rope, rope, state,
            pl.BlockSpec((None, 1, BRANCH_W), lambda b, i: (layer, 0, 0)),
            pl.BlockSpec((None, tokens, BRANCH_W), lambda b, i: (b, 0, 0)),
            pl.BlockSpec((None, tokens, BRANCH_W), lambda b, i: (b, 0, 1)),
        ],
        out_specs=[out, out, out, conv_state, state],
        out_shape=[
            y_shape, y_shape, y_shape,
            jax.ShapeDtypeStruct((bsz, CONV_WIDTH - 1, BRANCH_W), F32),
            jax.ShapeDtypeStruct((bsz, RET_HEADS, RET_DK, RET_DK), F32),
        ],
        scratch_shapes=[pltpu.VMEM((CONV_WIDTH - 1, BRANCH_W), F32),
                        pltpu.VMEM((RET_HEADS, RET_DK, RET_DK), F32),
                        pltpu.VMEM((RET_HEADS, chunk, chunk), F32)],
        compiler_params=_params(("arbitrary", "arbitrary"), 40),
        name="local_mixers",
    )(proj, proj, conv_w, conv_prev, cosf, sinf, s0, gn_g, mkv, mkv)


def _diff_lambda(lam_ref, lam_init):
    e0 = jnp.exp(jnp.sum(lam_ref[0:1, :] * lam_ref[1:2, :], -1, keepdims=True))
    e1 = jnp.exp(jnp.sum(lam_ref[2:3, :] * lam_ref[3:4, :], -1, keepdims=True))
    return e0 - e1 + lam_init


def _diff_prompt_kernel(q_ref, k_ref, v_ref, bias_ref, lam_ref, g_ref, *rest, lam_init, n_cast):
    cast_src, o_ref, cast_dst = rest[:n_cast], rest[n_cast], rest[n_cast + 1:2 * n_cast + 1]
    (k16_ref, vt_ref, sa_ref, sb_ref, m0_ref, m1_ref, l0_ref, l1_ref, acc0_ref,
     acc1_ref) = rest[2 * n_cast + 1:]
    _side_cast(cast_src, cast_dst)
    i = pl.program_id(2)
    t = q_ref.shape[0]
    nt = k_ref.shape[0] // t
    m_refs, l_refs, acc_refs = (m0_ref, m1_ref), (l0_ref, l1_ref), (acc0_ref, acc1_ref)

    @pl.when(i == 0)
    def _():
        k16_ref[...] = k_ref[...].astype(BF16)
        for j in range(nt):
            vt_ref[j] = jnp.transpose(v_ref[j * t:(j + 1) * t, :]).astype(BF16)

    qt = jnp.transpose(q_ref[...].astype(F32) * (DIFF_D ** -0.5 * LOG2E))
    feat = lax.broadcasted_iota(jnp.int32, qt.shape, 0)
    qts = (jnp.where(feat < DIFF_D, qt, 0.0).astype(BF16), jnp.where(feat >= DIFF_D, qt, 0.0).astype(BF16))

    def scores(j, kind, dst_ref):
        kt = k16_ref[j * t:(j + 1) * t, :]
        for c in range(2):
            s = _dot(kt, qts[c])
            dst_ref[c] = s if kind == FAR_TILE else s + bias_ref[kind]

    def consume(j, src_ref):
        vt = vt_ref[j]
        s = [src_ref[c] for c in range(2)]
        if j == 0:
            m_new = [jnp.max(s[c], 0, keepdims=True) for c in range(2)]
        else:
            m_prev = [m_refs[c][...] for c in range(2)]
            m_new = [jnp.maximum(m_prev[c], jnp.max(s[c], 0, keepdims=True)) for c in range(2)]
            a = [jnp.exp2(m_prev[c] - m_new[c]) for c in range(2)]
        p = [jnp.exp2(s[c] - m_new[c]) for c in range(2)]
        for c in range(2):
            l_new = jnp.sum(p[c], 0, keepdims=True)
            acc_new = _dot(vt, p[c].astype(BF16))
            if j > 0:
                l_new = a[c] * l_refs[c][...] + l_new
                acc_new = a[c] * acc_refs[c][...] + acc_new
            l_refs[c][...] = l_new
            acc_refs[c][...] = acc_new
            m_refs[c][...] = m_new[c]

    bufs = (sa_ref, sb_ref)
    for qi in range(nt):
        @pl.when(i == qi)
        def _(qi=qi):
            kind = lambda j: min(qi - j, FAR_TILE)
            scores(0, kind(0), bufs[0])
            for j in range(qi + 1):
                if j < qi:
                    scores(j + 1, kind(j + 1), bufs[(j + 1) % 2])
                consume(j, bufs[j % 2])

    lam = _diff_lambda(lam_ref, lam_init)
    o = acc0_ref[...] / l0_ref[...] - lam * (acc1_ref[...] / l1_ref[...])
    o = o * lax.rsqrt(jnp.mean(o * o, 0, keepdims=True) + LN_EPS) * g_ref[...]
    o_ref[...] = jnp.transpose(o * (1.0 - lam_init)).astype(BF16)


def _diff_prompt(proj, pkv, bias_tiles, diff_lambda, subln_g, layer, *, bsz, length, t, lam_init, casts=()):
    m = proj.shape[0]
    nq = length // t
    assert length % t == 0 and t % CHUNK == 0
    grid = (bsz, DIFF_HEADS, nq)
    kv = lambda c: pl.BlockSpec((length, HEAD_W), lambda b, h, i: (b, c * HEADS_PER_BRANCH + h))
    stat = pltpu.VMEM((1, t), F32)
    acc = pltpu.VMEM((DIFF_DV, t), F32)
    cast_in, cast_out, cast_shapes = _side_cast_specs(casts, grid)
    out = pl.pallas_call(
        functools.partial(_diff_prompt_kernel, lam_init=lam_init, n_cast=len(casts)),
        grid=grid,
        in_specs=[
            pl.BlockSpec((t, HEAD_W), lambda b, h, i: (b * nq + i, COL_DQ * HEADS_PER_BRANCH + h)),
            kv(KV_K), kv(KV_V),
            pl.BlockSpec((None, FAR_TILE, t, t), lambda b, h, i: (h, 0, 0, 0)),
            pl.BlockSpec((None, 4, DIFF_D), lambda b, h, i: (layer, 0, 0)),
            pl.BlockSpec((None, DIFF_DV, 1), lambda b, h, i: (layer, 0, 0)),
        ] + cast_in,
        out_specs=[pl.BlockSpec((t, HEAD_W), lambda b, h, i: (b * nq + i, h))] + cast_out,
        out_shape=[jax.ShapeDtypeStruct((m, BRANCH_W), BF16)] + cast_shapes,
        scratch_shapes=[pltpu.VMEM((length, HEAD_W), BF16), pltpu.VMEM((nq, DIFF_DV, t), BF16),
                        pltpu.VMEM((2, t, t), F32), pltpu.VMEM((2, t, t), F32),
                        stat, stat, stat, stat, acc, acc],
        compiler_params=_params(("parallel", "parallel", "arbitrary"), 40),
        name="diff_attention_prompt",
    )(proj, pkv, pkv, bias_tiles, diff_lambda, subln_g, *(c[0] for c in casts))
    return out[0], out[1:]


def _head_rows(ref, h, n):
    return ref[pl.ds(h, n, stride=HEADS_PER_BRANCH), :]


def _sample_attn_kernel(q_ref, kn_ref, vn_ref, mq_ref, kp_ref, vp_ref, mk_ref, mv_ref, bp_ref, bn_ref,
                        lam_ref, g_ref, yc_ref, yd_ref, *, lam_init):
    past = kp_ref.shape[0] // HEADS_PER_BRANCH
    tokens = mk_ref.shape[0] // HEADS_PER_BRANCH
    lam = _diff_lambda(lam_ref, lam_init)
    heads = range(HEADS_PER_BRANCH)
    cols = [slice(h * HEAD_W, (h + 1) * HEAD_W) for h in heads]
    kp = [_head_rows(kp_ref, h, past).astype(BF16) for h in heads]
    kn = [kn_ref[:, cols[h]].astype(BF16) for h in heads]
    mk = [_head_rows(mk_ref, h, tokens).astype(BF16) for h in heads]
    sp, sn = {}, {}
    for h in heads:
        q = q_ref[:, cols[h]].astype(F32) * (DIFF_D ** -0.5 * LOG2E)
        lane = lax.broadcasted_iota(jnp.int32, q.shape, 1)
        qs = (jnp.where(lane < DIFF_D, q, 0.0).astype(BF16), jnp.where(lane >= DIFF_D, q, 0.0).astype(BF16))
        for c, qc in enumerate(qs):
            sp[h, c] = _dot_nt(qc, kp[h]) + bp_ref[h]
            sn[h, c] = _dot_nt(qc, kn[h]) + bn_ref[h]
    sm = [_dot_nt(mq_ref[:, cols[h]].astype(BF16), mk[h]) * (MEM_HD ** -0.5) for h in heads]

    pp, pn, den = {}, {}, {}
    for key in sp:
        mx = jnp.maximum(jnp.max(sp[key], -1, keepdims=True), jnp.max(sn[key], -1, keepdims=True))
        pp[key] = jnp.exp2(sp[key] - mx)
        pn[key] = jnp.exp2(sn[key] - mx)
        den[key] = jnp.sum(pp[key], -1, keepdims=True) + jnp.sum(pn[key], -1, keepdims=True)
    pm = [jnp.exp(sm[h] - jnp.max(sm[h], -1, keepdims=True)) for h in heads]

    vp = [_head_rows(vp_ref, h, past).astype(BF16) for h in heads]
    vn = [vn_ref[:, cols[h]].astype(BF16) for h in heads]
    mv = [_head_rows(mv_ref, h, tokens).astype(BF16) for h in heads]
    y_c, y_d = [], []
    for h in heads:
        outs = [(_dot(pp[h, c].astype(BF16), vp[h]) + _dot(pn[h, c].astype(BF16), vn[h])) / den[h, c]
                for c in range(2)]
        o = outs[0] - lam * outs[1]
        o = o * lax.rsqrt(jnp.mean(o * o, -1, keepdims=True) + LN_EPS) * g_ref[...]
        y_c.append((o * (1.0 - lam_init)).astype(BF16))
        y_d.append((_dot(pm[h].astype(BF16), mv[h]) / jnp.sum(pm[h], -1, keepdims=True)).astype(BF16))
    yc_ref[...] = jnp.concatenate(y_c, axis=-1)
    yd_ref[...] = jnp.concatenate(y_d, axis=-1)


def _sample_attention(proj, pkv, k_past, v_past, mem_k, mem_v, bias_past, bias_new, diff_lambda, subln_g, layer,
                      *, bsz, length, lam_init):
    m = proj.shape[0]
    col = lambda c: pl.BlockSpec((length, BRANCH_W), lambda b: (b, c))
    cache = lambda a: pl.BlockSpec((None, None) + a.shape[2:], lambda b: (layer, b, 0, 0))
    whole = lambda a: pl.BlockSpec(a.shape, lambda b: (0,) * a.ndim)
    out = pl.BlockSpec((length, BRANCH_W), lambda b: (b, 0))
    return pl.pallas_call(
        functools.partial(_sample_attn_kernel, lam_init=lam_init),
        grid=(bsz,),
        in_specs=[
            col(COL_DQ), col(KV_K), col(KV_V), col(COL_MQ),
            cache(k_past), cache(v_past), cache(mem_k), cache(mem_v),
            whole(bias_past), whole(bias_new),
            pl.BlockSpec((None, 4, DIFF_D), lambda b: (layer, 0, 0)),
            pl.BlockSpec((None, 1, DIFF_DV), lambda b: (layer, 0, 0)),
        ],
        out_specs=[out, out],
        out_shape=[jax.ShapeDtypeStruct((m, BRANCH_W), BF16), jax.ShapeDtypeStruct((m, BRANCH_W), BF16)],
        compiler_params=_params(("parallel",), 40),
        name="sample_attention",
    )(proj, pkv, pkv, proj, k_past, v_past, mem_k, mem_v, bias_past, bias_new, diff_lambda, subln_g)


def _merge_kernel(h_ref, ya_ref, yb_ref, yc_ref, yd_ref, wg_ref, wb_ref, bg_ref, *rest, n_cast):
    cast_src, o_ref, cast_dst = rest[:n_cast], rest[n_cast], rest[n_cast + 1:]
    _side_cast(cast_src, cast_dst)
    h = h_ref[...]
    acc = None
    for i, y_ref in enumerate((ya_ref, yb_ref, yc_ref, yd_ref)):
        gate = jax.nn.sigmoid(_dot(h, wg_ref[i]) + bg_ref[i])
        term = gate * _dot(y_ref[...], wb_ref[i])
        acc = term if acc is None else acc + term
    o_ref[...] = acc.astype(BF16)


def _merge(h16, ys, w_gate, w_branch, b_gate, layer, *, tm, tn, casts=()):
    m, d = h16.shape
    assert m % tm == 0 and d % tn == 0
    grid = (m // tm, d // tn)
    y_spec = pl.BlockSpec((tm, BRANCH_W), lambda i, j: (i, 0))
    cast_in, cast_out, cast_shapes = _side_cast_specs(casts, grid)
    out = pl.pallas_call(
        functools.partial(_merge_kernel, n_cast=len(casts)),
        grid=grid,
        in_specs=[
            pl.BlockSpec((tm, d), lambda i, j: (i, 0)), y_spec, y_spec, y_spec, y_spec,
            pl.BlockSpec((None, N_BRANCH, d, tn), lambda i, j: (0, 0, 0, j)),
            pl.BlockSpec((None, N_BRANCH, BRANCH_W, tn), lambda i, j: (0, 0, 0, j)),
            pl.BlockSpec((None, N_BRANCH, 1, tn), lambda i, j: (layer, 0, 0, j)),
        ] + cast_in,
        out_specs=[pl.BlockSpec((tm, tn), lambda i, j: (i, j))] + cast_out,
        out_shape=[jax.ShapeDtypeStruct((m, d), BF16)] + cast_shapes,
        compiler_params=_params(("parallel", "parallel"), 60),
        name="gated_merge",
    )(h16, *ys, w_gate, w_branch, b_gate, *(c[0] for c in casts))
    return out[0], out[1:]


def _proj_ln_kernel(m_ref, x_ref, w_ref, g_ref, b_ref, o_ref, *, alpha):
    for r in range(0, o_ref.shape[0], PROJ_ROW_CHUNK):
        rows = slice(r, r + PROJ_ROW_CHUNK)
        y = alpha * x_ref[rows, :] + _dot(m_ref[rows, :], w_ref[...])
        o_ref[rows, :] = _layer_norm(y, g_ref[...], b_ref[...])


def _proj_ln(merged, x, w_o, g, b, layer, *, alpha, tm):
    m, d = x.shape
    assert m % tm == 0
    row = pl.BlockSpec((tm, d), lambda i: (i, 0))
    vec = pl.BlockSpec((None, 1, d), lambda i: (layer, 0, 0))
    return pl.pallas_call(
        functools.partial(_proj_ln_kernel, alpha=alpha),
        grid=(m // tm,),
        in_specs=[row, row, pl.BlockSpec((None, d, d), lambda i: (0, 0, 0)), vec, vec],
        out_specs=row,
        out_shape=jax.ShapeDtypeStruct((m, d), F32),
        compiler_params=_params(("parallel",), 48),
        name="proj_ln",
    )(merged, x, w_o, g, b)


def _rope_tables(pos):
    half = RET_DK // 2
    inv = ROPE_BASE ** (-jnp.arange(half, dtype=F32) / half)
    ang = pos.astype(F32)[:, None] * inv[None, :]
    cos = jnp.cos(ang)
    sin = jnp.sin(ang)
    return jnp.concatenate([cos, cos], -1), jnp.concatenate([-sin, sin], -1)


def _t5_bucket(rel):
    nb = REL_BUCKETS // 2
    max_exact = nb // 2
    n = jnp.abs(rel)
    nf = jnp.maximum(n, 1).astype(F32)
    large = max_exact + (jnp.log(nf / max_exact) / math.log(REL_MAX_DIST / max_exact)
                         * (nb - max_exact)).astype(jnp.int32)
    large = jnp.minimum(large, nb - 1)
    return jnp.where(rel > 0, nb, 0) + jnp.where(n < max_exact, n, large)


def _masked_bias(q_pos, k_pos, rel_bias):
    bucket = _t5_bucket(k_pos[None, :] - q_pos[:, None])
    onehot = bucket[None, None] == jnp.arange(REL_BUCKETS, dtype=bucket.dtype)[None, :, None, None]
    bias = jnp.sum(jnp.where(onehot, rel_bias.astype(F32).T[:, :, None, None], 0.0), axis=1)
    allowed = (k_pos[None, :] // CHUNK) <= (q_pos[:, None] // CHUNK)
    return jnp.where(allowed[None], bias * LOG2E, NEG_INF)


def _prompt_bias_tiles(rel_bias, t):
    base = BIAS_BASE
    assert base % CHUNK == 0 and base + 1 >= REL_MAX_DIST and t % base == 0
    n = t // base
    pos = jnp.arange(base, dtype=jnp.int32)
    diag, sub, far = (jnp.swapaxes(_masked_bias(pos + d * base, pos, rel_bias), -1, -2) for d in range(3))
    shift = far[:, :1, :1]
    masked = jnp.full_like(diag, NEG_INF)
    pick = lambda delta: masked if delta > 0 else (diag - shift, sub - shift, far - shift)[min(-delta, 2)]
    tile = lambda d: jnp.concatenate(
        [jnp.concatenate([pick(kk - qq - d * n) for qq in range(n)], axis=-1) for kk in range(n)], axis=-2)
    return jnp.stack([tile(d) for d in range(FAR_TILE)], axis=1)


def _kv_proj_kernel(x_ref, w_ref, *rest):
    o_ref, ok_ref, ov_ref = rest[-3:]
    y = _dot(x_ref[...], w_ref[...])
    o_ref[...] = y
    n = x_ref.shape[0]
    for h in range(HEADS_PER_BRANCH):
        rows = pl.ds(h, n, stride=HEADS_PER_BRANCH)
        ok_ref[rows, :] = y[:, KV_K * BRANCH_W + h * HEAD_W:KV_K * BRANCH_W + (h + 1) * HEAD_W]
        ov_ref[rows, :] = y[:, KV_V * BRANCH_W + h * HEAD_W:KV_V * BRANCH_W + (h + 1) * HEAD_W]


def _kv_proj(xb, w_kv, prev, layer, depth, *, tm):
    m, d = xb.shape
    n = w_kv.shape[2]
    assert m % tm == 0 and n == KV_BLOCKS * BRANCH_W
    out = pl.BlockSpec((None, tm * HEADS_PER_BRANCH, HEAD_W), lambda i: (layer, i, 0))
    shape = jax.ShapeDtypeStruct((depth, m * HEADS_PER_BRANCH, HEAD_W), F32)
    keep = [pl.BlockSpec(memory_space=pl.ANY)] * len(prev)
    res = pl.pallas_call(
        _kv_proj_kernel,
        grid=(m // tm,),
        in_specs=[pl.BlockSpec((tm, d), lambda i: (i, 0)), pl.BlockSpec((None, d, n), lambda i: (0, 0, 0))] + keep,
        out_specs=[pl.BlockSpec((tm, n), lambda i: (i, 0)), out, out],
        out_shape=[jax.ShapeDtypeStruct((m, n), F32), shape, shape],
        input_output_aliases={2 + k: 1 + k for k in range(len(prev))},
        compiler_params=_params(("parallel",), 40),
        name="kv_proj",
    )(xb, w_kv, *prev)
    return res[0], res[1:]


CAST_IN_FFN1 = (
    ('w_in_main', 'w_in', ('rc', lambda j: jnp.where(j < W_IN_DK, j, W_IN_MQ),
                           lambda j: jnp.minimum(j, MAIN_BLOCKS - 1), MAIN_BLOCKS)),
    ('w_in_kv', 'w_in', ('rc', lambda j: jnp.where(j == 0, W_IN_DK, W_IN_DV),
                         lambda j: jnp.minimum(j, KV_BLOCKS - 1), KV_BLOCKS)))
CAST_IN_ATTN = (('w_gate', 'w_gate', 'flat'), ('w_branch', 'w_branch', 'flat'), ('w_o', 'w_o', 'flat'))
CAST_IN_MERGE = (('ffn2_w_up', 'ffn2_w_up', 'rc'), ('ffn2_w_down', 'ffn2_w_down', 'rc'))
CAST_IN_FFN2 = (('ffn1_w_up', 'ffn1_w_up', 'rc'), ('ffn1_w_down', 'ffn1_w_down', 'cr'))


def _encoder_layer(x, w16, p, layer, *, alpha, cfg, mixers_fn, kv_prev, depth, w32=None, w16_next=None):
    d = x.shape[1]
    casts = lambda names, l: tuple((w32[src], l, split) for _, src, split in names) if w32 is not None else ()
    if w32 is not None:
        assert w16['ffn1_w_down'].shape[1] // cfg['tf'] == W_IN_BLOCKS == w32['w_in'].shape[2] // BRANCH_W
    x1, done = _ffn_ln(x, w16['ffn1_w_up'], w16['ffn1_w_down'], p['ln1_g'], p['ln1_b'], layer,
                       alpha=alpha, tm=cfg['tm_ffn'], tf=cfg['tf'], casts=casts(CAST_IN_FFN1, layer))
    w16.update(zip((n for n, _, _ in CAST_IN_FFN1), done))
    proj, x1b = _in_proj(x1, w16['w_in_main'], tm=cfg['tm_in'], tn=cfg['tn_in'])
    pkv, kv_new = _kv_proj(x1b, w16['w_in_kv'], kv_prev, layer, depth, tm=cfg['tm_kv'])
    y_a, y_b, y_c, y_d, conv_new, ret_new, done = mixers_fn(
        proj, pkv, 0.8 - 0.6 * math.exp(-0.3 * layer), casts(CAST_IN_ATTN, layer))
    w16.update(zip((n for n, _, _ in CAST_IN_ATTN), done))
    merged, done = _merge(x1b, (y_a, y_b, y_c, y_d), w16['w_gate'].reshape(1, N_BRANCH, d, d),
                          w16['w_branch'].reshape(1, N_BRANCH, BRANCH_W, d), p['b_gate'], layer,
                          tm=cfg['tm_merge'], tn=cfg['tn_merge'], casts=casts(CAST_IN_MERGE, layer))
    w16.update(zip((n for n, _, _ in CAST_IN_MERGE), done))
    x2 = _proj_ln(merged, x1, w16['w_o'], p['ln2_g'], p['ln2_b'], layer, alpha=alpha, tm=cfg['tm_proj'])
    x3, done = _ffn_ln(x2, w16['ffn2_w_up'], w16['ffn2_w_down'], p['ln3_g'], p['ln3_b'], layer,
                       alpha=alpha, tm=cfg['tm_ffn'], tf=cfg['tf'],
                       casts=casts(CAST_IN_FFN2, layer + 1) if w16_next is not None else ())
    if w16_next is not None:
        w16_next.update(zip((n for n, _, _ in CAST_IN_FFN2), done))
    return x3, kv_new, conv_new, ret_new


PROMPT_CFG = dict(tm_ffn=1024, tf=512, tm_in=1024, tn_in=1536, tm_kv=1024, mix_rows=256, ret_chunk=256, t_diff=512,
                  tm_merge=1024, tn_merge=512, tm_proj=512)
SAMPLE_CFG = dict(tm_ffn=256, tf=512, tm_in=256, tn_in=512, tm_kv=256, t_conv=16, ret_chunk=16,
                  tm_merge=256, tn_merge=256, tm_proj=256)


def kernel(x_prompt, x_sample, state_conv, state_ret, cache_diff_k, cache_diff_v, cache_mem_k, cache_mem_v,
           mem_prompt, ffn1_w_up, ffn1_w_down, ln1_g, ln1_b, w_in, conv_w, ret_gn_g, diff_lambda,
           diff_subln_g, w_mem_kv, w_branch, w_gate, b_gate, w_o, ln2_g, ln2_b, ffn2_w_up, ffn2_w_down,
           ln3_g, ln3_b, rel_bias):
    bp, lp, d = x_prompt.shape
    bs, ls, _ = x_sample.shape
    depth = w_in.shape[0]
    past = cache_diff_k.shape[2]
    mem_tokens = mem_prompt.shape[1]
    alpha = (2 * depth) ** 0.25
    half = MEM_HEADS * MEM_HD

    pos_p = jnp.arange(lp, dtype=jnp.int32)
    pos_s = past + jnp.arange(ls, dtype=jnp.int32)
    past_pos = jnp.arange(past, dtype=jnp.int32)
    rope_p = _rope_tables(pos_p)
    rope_s = _rope_tables(pos_s)
    bias_p = _prompt_bias_tiles(rel_bias, PROMPT_CFG['t_diff'])
    bias_s_past = _masked_bias(pos_s, past_pos, rel_bias)
    bias_s_new = _masked_bias(pos_s, pos_s, rel_bias)

    vec = lambda v: v.reshape(depth, 1, v.shape[-1])
    p = {'ln1_g': vec(ln1_g), 'ln1_b': vec(ln1_b), 'ln2_g': vec(ln2_g), 'ln2_b': vec(ln2_b),
         'ln3_g': vec(ln3_g), 'ln3_b': vec(ln3_b), 'b_gate': b_gate.reshape(depth, N_BRANCH, 1, d)}
    gn_g = vec(ret_gn_g)
    w32 = {'ffn1_w_up': ffn1_w_up, 'ffn1_w_down': ffn1_w_down, 'w_in': w_in,
           'w_gate': w_gate.reshape(depth, N_BRANCH * d, d), 'w_branch': w_branch.reshape(depth, N_BRANCH * BRANCH_W, d),
           'w_o': w_o, 'ffn2_w_up': ffn2_w_up, 'ffn2_w_down': ffn2_w_down}
    w16 = [dict() for _ in range(depth)]
    w16[0].update({n: w32[src][:1].astype(BF16) for n, src, _ in CAST_IN_FFN2})
    w_mem16 = w_mem_kv.astype(BF16)
    subln_col = diff_subln_g.reshape(depth, DIFF_DV, 1)
    subln_row = diff_subln_g.reshape(depth, 1, DIFF_DV)
    head_major = lambda c: c.reshape(depth, bs, c.shape[2] * c.shape[3], c.shape[4])
    kpast, vpast = head_major(cache_diff_k), head_major(cache_diff_v)
    mem_k_s, mem_v_s = head_major(cache_mem_k), head_major(cache_mem_v)

    yp = x_prompt.reshape(bp * lp, d)
    ys = x_sample.reshape(bs * ls, d)
    mem16 = mem_prompt.reshape(bp * mem_tokens, d).astype(BF16)
    zero_conv = jnp.zeros((bp, CONV_WIDTH - 1, BRANCH_W), F32)
    zero_ret = jnp.zeros((bp, RET_HEADS, RET_DK, RET_DK), F32)

    conv_p, ret_p, mk_p, mv_p, conv_s, ret_s = [], [], [], [], [], []
    kv_p, kv_s = [], []
    for l in range(depth):
        mkv = _matmul(mem16, w_mem16, l, tm=bp * mem_tokens, tn=512).reshape(bp, mem_tokens, 2 * half)

        def mixers_p(proj, pkv, lam_init, casts):
            y_a, y_b, y_d, c_new, r_new = _local_mixers(
                proj, conv_w, zero_conv, rope_p[0], rope_p[1], zero_ret, gn_g, mkv, l,
                bsz=bp, length=lp, rows=PROMPT_CFG['mix_rows'], chunk=PROMPT_CFG['ret_chunk'])
            y_c, done = _diff_prompt(proj, pkv, bias_p, diff_lambda, subln_col, l, bsz=bp, length=lp,
                                     t=PROMPT_CFG['t_diff'], lam_init=lam_init, casts=casts)
            return y_a, y_b, y_c, y_d, c_new, r_new, done

        yp, kv_p, c_new, r_new = _encoder_layer(
            yp, w16[l], p, l, alpha=alpha, cfg=PROMPT_CFG, mixers_fn=mixers_p, kv_prev=kv_p, depth=depth,
            w32=w32, w16_next=w16[l + 1] if l + 1 < depth else None)
        conv_p.append(c_new)
        ret_p.append(r_new)
        mk_p.append(mkv[:, :, :half].reshape(bp, mem_tokens, MEM_HEADS, MEM_HD))
        mv_p.append(mkv[:, :, half:].reshape(bp, mem_tokens, MEM_HEADS, MEM_HD))

        def mixers_s(proj, pkv, lam_init, casts):
            assert not casts
            y_a, c_new = _conv_branch(proj, conv_w, state_conv[l], l, bsz=bs, length=ls, tl=SAMPLE_CFG['t_conv'])
            y_b, r_new = _retention_branch(proj, rope_s[0], rope_s[1], state_ret[l], gn_g, l,
                                           bsz=bs, length=ls, chunk=SAMPLE_CFG['ret_chunk'])
            y_c, y_d = _sample_attention(proj, pkv, kpast, vpast, mem_k_s, mem_v_s, bias_s_past, bias_s_new,
                                         diff_lambda, subln_row, l, bsz=bs, length=ls, lam_init=lam_init)
            return y_a, y_b, y_c, y_d, c_new, r_new, ()

        ys, kv_s, c_new, r_new = _encoder_layer(
            ys, w16[l], p, l, alpha=alpha, cfg=SAMPLE_CFG, mixers_fn=mixers_s, kv_prev=kv_s, depth=depth)
        conv_s.append(c_new)
        ret_s.append(r_new)

    kv_shape = lambda b, n: (depth, b, n, DIFF_HEADS, DIFF_DV)
    return (yp.reshape(bp, lp, d), ys.reshape(bs, ls, d), jnp.stack(conv_p), jnp.stack(ret_p),
            kv_p[0].reshape(kv_shape(bp, lp)), kv_p[1].reshape(kv_shape(bp, lp)),
            jnp.stack(mk_p), jnp.stack(mv_p), jnp.stack(conv_s), jnp.stack(ret_s),
            kv_s[0].reshape(kv_shape(bs, ls)), kv_s[1].reshape(kv_shape(bs, ls)))
```

```python
import functools
import math

import jax
import jax.numpy as jnp
from jax import lax
from jax.experimental import pallas as pl
from jax.experimental.pallas import tpu as pltpu

F32 = jnp.float32
BF16 = jnp.bfloat16

CHUNK = 64
CONV_WIDTH = 3
RET_HEADS = 4
RET_DK = 128
DIFF_HEADS = 4
DIFF_D = 64
DIFF_DV = 128
MEM_HEADS = 4
MEM_HD = 128
BRANCH_W = 512
N_BRANCH = 4
REL_BUCKETS = 32
REL_MAX_DIST = 128
LN_EPS = 1e-5
ROPE_BASE = 10000.0
NEG_INF = -1e30
HEAD_W = 128
LOG2E = math.log2(math.e)
LN_ROW_CHUNK = 128
PROJ_ROW_CHUNK = 256
BIAS_BASE = 256
DIFF_QUERY_GROUP = 256
FAR_TILE = 2
FFN_UP_CHUNK = 256
FFN_DOWN_CHUNK = 512

W_IN_BLOCKS = 11
W_IN_DK, W_IN_DV, W_IN_MQ = 8, 9, 10
COL_CB, COL_CC, COL_CH, COL_RQ, COL_RK, COL_RV, COL_RG, COL_DQ, COL_MQ = range(9)
MAIN_BLOCKS = 9
KV_K, KV_V = range(2)
KV_BLOCKS = 2
HEADS_PER_BRANCH = BRANCH_W // HEAD_W

LANES = 128
F32_SUBLANES = 8
BF16_SUBLANES = 16
V7X_VMEM_BYTES = 64 * 1024 * 1024
MIB = 1024 * 1024


def _params(semantics, vmem_mib):
    assert vmem_mib * MIB < V7X_VMEM_BYTES
    return pltpu.CompilerParams(dimension_semantics=semantics, vmem_limit_bytes=vmem_mib * MIB)


def _layer_norm(y, g, b):
    mu = jnp.mean(y, -1, keepdims=True)
    d = y - mu
    var = jnp.mean(d * d, -1, keepdims=True)
    return d * lax.rsqrt(var + LN_EPS) * g + b


def _dot(a, b):
    return jnp.dot(a, b, preferred_element_type=F32)


def _dot_nt(a, b):
    return lax.dot_general(a, b, (((1,), (1,)), ((), ())), preferred_element_type=F32)


def _dot_tn(a, b):
    return lax.dot_general(a, b, (((0,), (0,)), ((), ())), preferred_element_type=F32)


def _side_cast_specs(casts, grid):
    in_specs, out_specs, out_shapes = [], [], []
    for src, layer, split in casts:
        _, r, c = src.shape
        if split == 'flat':
            gr, gc = math.prod(grid), 1

            def pick(*ids, grid=grid):
                flat = ids[0]
                for size, idx in zip(grid[1:], ids[1:]):
                    flat = flat * size + idx
                return (flat, 0)
        elif split == 'cr':
            gr, gc = grid[1], grid[0]
            pick = lambda i, j: (j, i)
        else:
            gr, gc = grid
            pick = lambda i, j: (i, j)
        assert r % gr == 0 and c % gc == 0
        blk = (None, r // gr, c // gc)
        assert blk[1] % BF16_SUBLANES == 0 and blk[2] % LANES == 0
        src_pick, dst_pick, c_out = pick, pick, c
        if isinstance(split, tuple):
            kind, src_col, dst_col, n_dst = split
            assert kind == 'rc'
            src_pick = lambda i, j, col=src_col: (i, col(j))
            dst_pick = lambda i, j, col=dst_col: (i, col(j))
            c_out = n_dst * blk[2]
        else:
            assert split in ('rc', 'cr', 'flat')
        in_specs.append(pl.BlockSpec(blk, lambda *ids, layer=layer, pick=src_pick: (layer,) + pick(*ids)))
        out_specs.append(pl.BlockSpec(blk, lambda *ids, pick=dst_pick: (0,) + pick(*ids)))
        out_shapes.append(jax.ShapeDtypeStruct((1, r, c_out), BF16))
    return in_specs, out_specs, out_shapes


def _side_cast(src_refs, dst_refs):
    for src_ref, dst_ref in zip(src_refs, dst_refs):
        dst_ref[...] = src_ref[...].astype(BF16)


def _ffn_ln_kernel(x_ref, wa_ref, wb_ref, wd_ref, g_ref, b_ref, *rest, alpha, n_cast):
    cast_src, o_ref, cast_dst = rest[:n_cast], rest[n_cast], rest[n_cast + 1:2 * n_cast + 1]
    xb_ref, h_ref = rest[2 * n_cast + 1:]
    j = pl.program_id(1)

    @pl.when(j == 0)
    def _():
        xb_ref[...] = x_ref[...].astype(BF16)
        o_ref[...] = jnp.zeros(o_ref.shape, F32)

    xb = xb_ref[...]
    for c in range(0, h_ref.shape[1], FFN_UP_CHUNK):
        cols = slice(c, c + FFN_UP_CHUNK)
        a = _dot(xb, wa_ref[:, cols])
        b = _dot(xb, wb_ref[:, cols])
        h_ref[:, cols] = (a * jax.nn.sigmoid(a) * b).astype(BF16)
    h = h_ref[...]
    for c in range(0, o_ref.shape[1], FFN_DOWN_CHUNK):
        cols = slice(c, c + FFN_DOWN_CHUNK)
        o_ref[:, cols] += _dot(h, wd_ref[:, cols])
    _side_cast(cast_src, cast_dst)

    @pl.when(j == pl.num_programs(1) - 1)
    def _():
        for r in range(0, o_ref.shape[0], LN_ROW_CHUNK):
            rows = slice(r, r + LN_ROW_CHUNK)
            o_ref[rows, :] = _layer_norm(alpha * x_ref[rows, :] + 0.5 * o_ref[rows, :], g_ref[...], b_ref[...])


def _ffn_ln(x, w_up, w_down, g, b, layer, *, alpha, tm, tf, casts=()):
    m, d = x.shape
    d_ff = w_down.shape[1]
    nj = d_ff // tf
    assert m % tm == 0 and d_ff % tf == 0
    grid = (m // tm, nj)
    row = lambda i, j: (i, 0)
    vec = pl.BlockSpec((None, 1, d), lambda i, j: (layer, 0, 0))
    cast_in, cast_out, cast_shapes = _side_cast_specs(casts, grid)
    out = pl.pallas_call(
        functools.partial(_ffn_ln_kernel, alpha=alpha, n_cast=len(casts)),
        grid=grid,
        in_specs=[
            pl.BlockSpec((tm, d), row),
            pl.BlockSpec((None, d, tf), lambda i, j: (0, 0, j)),
            pl.BlockSpec((None, d, tf), lambda i, j: (0, 0, j + nj)),
            pl.BlockSpec((None, tf, d), lambda i, j: (0, j, 0)),
            vec, vec,
        ] + cast_in,
        out_specs=[pl.BlockSpec((tm, d), row)] + cast_out,
        out_shape=[jax.ShapeDtypeStruct((m, d), F32)] + cast_shapes,
        scratch_shapes=[pltpu.VMEM((tm, d), BF16), pltpu.VMEM((tm, tf), BF16)],
        compiler_params=_params(("parallel", "arbitrary"), 60),
        name="ffn_ln",
    )(x, w_up, w_up, w_down, g, b, *(c[0] for c in casts))
    return out[0], out[1:]


def _in_proj_kernel(x_ref, w_ref, o_ref, xb_ref):
    @pl.when(pl.program_id(1) == 0)
    def _():
        xb_ref[...] = x_ref[...].astype(BF16)

    o_ref[...] = _dot(xb_ref[...], w_ref[...]).astype(o_ref.dtype)


def _in_proj(x, w, *, tm, tn):
    m, k = x.shape
    n = w.shape[2]
    assert m % tm == 0 and n % tn == 0
    return pl.pallas_call(
        _in_proj_kernel,
        grid=(m // tm, n // tn),
        in_specs=[pl.BlockSpec((tm, k), lambda i, j: (i, 0)),
                  pl.BlockSpec((None, k, tn), lambda i, j: (0, 0, j))],
        out_specs=[pl.BlockSpec((tm, tn), lambda i, j: (i, j)), pl.BlockSpec((tm, k), lambda i, j: (i, 0))],
        out_shape=[jax.ShapeDtypeStruct((m, n), BF16), jax.ShapeDtypeStruct((m, k), BF16)],
        compiler_params=_params(("parallel", "arbitrary"), 60),
        name="in_proj",
    )(x, w)


def _matmul_kernel(x_ref, w_ref, o_ref):
    o_ref[...] = _dot(x_ref[...], w_ref[...])


def _matmul(x, w, layer, *, tm, tn):
    m, k = x.shape
    n = w.shape[2]
    assert m % tm == 0 and n % tn == 0
    return pl.pallas_call(
        _matmul_kernel,
        grid=(m // tm, n // tn),
        in_specs=[pl.BlockSpec((tm, k), lambda i, j: (i, 0)),
                  pl.BlockSpec((None, k, tn), lambda i, j: (layer, 0, j))],
        out_specs=pl.BlockSpec((tm, tn), lambda i, j: (i, j)),
        out_shape=jax.ShapeDtypeStruct((m, n), F32),
        compiler_params=_params(("parallel", "parallel"), 48),
        name="matmul",
    )(x, w)


def _conv_body(cb_ref, cc_ref, ch_ref, w_ref, carry_ref, y_ref):
    u = cc_ref[...].astype(F32) * ch_ref[...].astype(F32)
    row = lax.broadcasted_iota(jnp.int32, u.shape, 0)
    c0 = carry_ref[0:1, :]
    c1 = carry_ref[1:2, :]
    u1 = jnp.where(row == 0, c1, pltpu.roll(u, 1, 0))
    u2 = jnp.where(row == 0, c0, jnp.where(row == 1, c1, pltpu.roll(u, 2, 0)))
    z = w_ref[0:1, :] * u2 + w_ref[1:2, :] * u1 + w_ref[2:3, :] * u
    y_ref[...] = (cb_ref[...].astype(F32) * z).astype(BF16)
    tail = u[u.shape[0] - F32_SUBLANES:, :][F32_SUBLANES - (CONV_WIDTH - 1):, :]
    carry_ref[...] = tail
    return tail


def _conv_kernel(cb_ref, cc_ref, ch_ref, w_ref, prev_ref, y_ref, state_ref, carry_ref):
    l = pl.program_id(1)

    @pl.when(l == 0)
    def _():
        carry_ref[...] = prev_ref[...]

    tail = _conv_body(cb_ref, cc_ref, ch_ref, w_ref, carry_ref, y_ref)

    @pl.when(l == pl.num_programs(1) - 1)
    def _():
        state_ref[...] = tail


def _conv_branch(proj, conv_w, prev, layer, *, bsz, length, tl):
    m = proj.shape[0]
    nl = length // tl
    assert length % tl == 0 and tl >= F32_SUBLANES
    col = lambda c: pl.BlockSpec((tl, BRANCH_W), lambda b, l: (b * nl + l, c))
    return pl.pallas_call(
        _conv_kernel,
        grid=(bsz, nl),
        in_specs=[
            col(COL_CB), col(COL_CC), col(COL_CH),
            pl.BlockSpec((None, CONV_WIDTH, BRANCH_W), lambda b, l: (layer, 0, 0)),
            pl.BlockSpec((None, CONV_WIDTH - 1, BRANCH_W), lambda b, l: (b, 0, 0)),
        ],
        out_specs=[
            pl.BlockSpec((tl, BRANCH_W), lambda b, l: (b * nl + l, 0)),
            pl.BlockSpec((None, CONV_WIDTH - 1, BRANCH_W), lambda b, l: (b, 0, 0)),
        ],
        out_shape=[
            jax.ShapeDtypeStruct((m, BRANCH_W), BF16),
            jax.ShapeDtypeStruct((bsz, CONV_WIDTH - 1, BRANCH_W), F32),
        ],
        scratch_shapes=[pltpu.VMEM((CONV_WIDTH - 1, BRANCH_W), F32)],
        compiler_params=_params(("parallel", "arbitrary"), 32),
        name="conv_branch",
    )(proj, proj, proj, conv_w, prev)


def _ret_log_gamma(h):
    return math.log1p(-(2.0 ** (-5.0 - h)))


def _ret_fill_decay(intra_ref):
    chunk = intra_ref.shape[1]
    ri = lax.broadcasted_iota(jnp.int32, (chunk, chunk), 0)
    ci = lax.broadcasted_iota(jnp.int32, (chunk, chunk), 1)
    rel = (ri - ci).astype(F32)
    for h in range(RET_HEADS):
        intra_ref[h] = jnp.where(rel >= 0.0, jnp.exp(jnp.maximum(rel, 0.0) * _ret_log_gamma(h)), 0.0)


def _retention_body(q_ref, k_ref, v_ref, g_ref, cos_ref, sin_ref, gn_ref, y_ref, s_ref, intra_ref, rows):
    chunk = intra_ref.shape[1]
    cosf = cos_ref[rows, :]
    sinf = sin_ref[rows, :]
    idx = lax.broadcasted_iota(jnp.int32, (chunk, 1), 0).astype(F32)
    for h in range(RET_HEADS):
        log_g = _ret_log_gamma(h)
        cols = slice(h * RET_DK, (h + 1) * RET_DK)
        q = q_ref[rows, cols].astype(F32)
        k = k_ref[rows, cols].astype(F32)
        q = q * cosf + pltpu.roll(q, RET_DK // 2, 1) * sinf
        k = (k * cosf + pltpu.roll(k, RET_DK // 2, 1) * sinf) * (RET_DK ** -0.5)
        vb = v_ref[rows, cols].astype(BF16)
        q_dec = jnp.exp((idx + 1.0) * log_g)
        k_dec = jnp.exp((chunk - 1.0 - idx) * log_g)
        c_dec = math.exp(chunk * log_g)
        qb = q.astype(BF16)
        s_prev = s_ref[h]
        att = _dot_nt(qb, k.astype(BF16)) * intra_ref[h]
        o = _dot(att.astype(BF16), vb) + _dot(qb, s_prev.astype(BF16)) * q_dec
        s_ref[h] = s_prev * c_dec + _dot_tn((k * k_dec).astype(BF16), vb)
        mu = jnp.mean(o, -1, keepdims=True)
        d = o - mu
        var = jnp.mean(d * d, -1, keepdims=True)
        ro = d * lax.rsqrt(var + LN_EPS) * gn_ref[:, cols]
        gate = g_ref[rows, cols].astype(F32)
        y_ref[rows, cols] = (gate * jax.nn.sigmoid(gate) * ro).astype(BF16)


def _retention_kernel(q_ref, k_ref, v_ref, g_ref, cos_ref, sin_ref, s0_ref, gn_ref, y_ref, sfin_ref,
                      s_ref, intra_ref):
    c = pl.program_id(1)

    @pl.when(jnp.logical_and(pl.program_id(0) == 0, c == 0))
    def _():
        _ret_fill_decay(intra_ref)

    @pl.when(c == 0)
    def _():
        s_ref[...] = s0_ref[...]

    _retention_body(q_ref, k_ref, v_ref, g_ref, cos_ref, sin_ref, gn_ref, y_ref, s_ref, intra_ref,
                    slice(0, q_ref.shape[0]))

    @pl.when(c == pl.num_programs(1) - 1)
    def _():
        sfin_ref[...] = s_ref[...]


def _retention_branch(proj, cosf, sinf, s0, gn_g, layer, *, bsz, length, chunk):
    m = proj.shape[0]
    nc = length // chunk
    assert length % chunk == 0
    col = lambda c: pl.BlockSpec((chunk, BRANCH_W), lambda b, i: (b * nc + i, c))
    state = pl.BlockSpec((None, RET_HEADS, RET_DK, RET_DK), lambda b, i: (b, 0, 0, 0))
    return pl.pallas_call(
        _retention_kernel,
        grid=(bsz, nc),
        in_specs=[
            col(COL_RQ), col(COL_RK), col(COL_RV), col(COL_RG),
            pl.BlockSpec((chunk, RET_DK), lambda b, i: (i, 0)),
            pl.BlockSpec((chunk, RET_DK), lambda b, i: (i, 0)),
            state,
            pl.BlockSpec((None, 1, BRANCH_W), lambda b, i: (layer, 0, 0)),
        ],
        out_specs=[pl.BlockSpec((chunk, BRANCH_W), lambda b, i: (b * nc + i, 0)), state],
        out_shape=[
            jax.ShapeDtypeStruct((m, BRANCH_W), BF16),
            jax.ShapeDtypeStruct((bsz, RET_HEADS, RET_DK, RET_DK), F32),
        ],
        scratch_shapes=[pltpu.VMEM((RET_HEADS, RET_DK, RET_DK), F32),
                        pltpu.VMEM((RET_HEADS, chunk, chunk), F32)],
        compiler_params=_params(("arbitrary", "arbitrary"), 32),
        name="retention_branch",
    )(proj, proj, proj, proj, cosf, sinf, s0, gn_g)


def _local_mixers_kernel(x_ref, mq_ref, w_ref, prev_ref, cos_ref, sin_ref, s0_ref, gn_ref, mk_ref, mv_ref,
                         ya_ref, yb_ref, yd_ref, conv_state_ref, sfin_ref,
                         carry_ref, s_ref, intra_ref):
    cb_ref, cc_ref, ch_ref, rq_ref, rk_ref, rv_ref, rg_ref = (
        x_ref.at[:, pl.ds(col * BRANCH_W, BRANCH_W)]
        for col in (COL_CB, COL_CC, COL_CH, COL_RQ, COL_RK, COL_RV, COL_RG))
    c = pl.program_id(1)

    @pl.when(jnp.logical_and(pl.program_id(0) == 0, c == 0))
    def _():
        _ret_fill_decay(intra_ref)

    @pl.when(c == 0)
    def _():
        carry_ref[...] = prev_ref[...]
        s_ref[...] = s0_ref[...]

    tail = _conv_body(cb_ref, cc_ref, ch_ref, w_ref, carry_ref, ya_ref)
    chunk = intra_ref.shape[1]
    for r in range(0, rq_ref.shape[0], chunk):
        _retention_body(rq_ref, rk_ref, rv_ref, rg_ref, cos_ref, sin_ref, gn_ref, yb_ref, s_ref, intra_ref,
                        slice(r, r + chunk))
    head_cols = [slice(h * MEM_HD, (h + 1) * MEM_HD) for h in range(MEM_HEADS)]
    scores = [_dot_nt(mq_ref[:, hc].astype(BF16), mk_ref[:, hc].astype(BF16)) * (MEM_HD ** -0.5)
              for hc in head_cols]
    probs = [jnp.exp(s - jnp.max(s, -1, keepdims=True)) for s in scores]
    for hc, p in zip(head_cols, probs):
        den = jnp.sum(p, -1, keepdims=True)
        yd_ref[:, hc] = (_dot(p.astype(BF16), mv_ref[:, hc].astype(BF16)) / den).astype(BF16)

    @pl.when(c == pl.num_programs(1) - 1)
    def _():
        conv_state_ref[...] = tail
        sfin_ref[...] = s_ref[...]


def _local_mixers(proj, conv_w, conv_prev, cosf, sinf, s0, gn_g, mkv, layer, *, bsz, length, rows, chunk):
    m = proj.shape[0]
    nc = length // rows
    assert length % rows == 0 and rows % chunk == 0 and chunk >= F32_SUBLANES
    tokens = mkv.shape[1]
    col = lambda c: pl.BlockSpec((rows, BRANCH_W), lambda b, i: (b * nc + i, c))
    out = pl.BlockSpec((rows, BRANCH_W), lambda b, i: (b * nc + i, 0))
    state = pl.BlockSpec((None, RET_HEADS, RET_DK, RET_DK), lambda b, i: (b, 0, 0, 0))
    conv_state = pl.BlockSpec((None, CONV_WIDTH - 1, BRANCH_W), lambda b, i: (b, 0, 0))
    rope = pl.BlockSpec((rows, RET_DK), lambda b, i: (i, 0))
    y_shape = jax.ShapeDtypeStruct((m, BRANCH_W), BF16)
    return pl.pallas_call(
        _local_mixers_kernel,
        grid=(bsz, nc),
        in_specs=[
            pl.BlockSpec((rows, (COL_RG + 1) * BRANCH_W), lambda b, i: (b * nc + i, 0)), col(COL_MQ),
            pl.BlockSpec((None, CONV_WIDTH, BRANCH_W), lambda b, i: (layer, 0, 0)),
            conv_state, rope, rope, state,
            pl.BlockSpec((None, 1, BRANCH_W), lambda b, i: (layer, 0, 0)),
            pl.BlockSpec((None, tokens, BRANCH_W), lambda b, i: (b, 0, 0)),
            pl.BlockSpec((None, tokens, BRANCH_W), lambda b, i: (b, 0, 1)),
        ],
        out_specs=[out, out, out, conv_state, state],
        out_shape=[
            y_shape, y_shape, y_shape,
            jax.ShapeDtypeStruct((bsz, CONV_WIDTH - 1, BRANCH_W), F32),
            jax.ShapeDtypeStruct((bsz, RET_HEADS, RET_DK, RET_DK), F32),
        ],
        scratch_shapes=[pltpu.VMEM((CONV_WIDTH - 1, BRANCH_W), F32),
                        pltpu.VMEM((RET_HEADS, RET_DK, RET_DK), F32),
                        pltpu.VMEM((RET_HEADS, chunk, chunk), F32)],
        compiler_params=_params(("arbitrary", "arbitrary"), 40),
        name="local_mixers",
    )(proj, proj, conv_w, conv_prev, cosf, sinf, s0, gn_g, mkv, mkv)


def _diff_lambda(lam_ref, lam_init):
    e0 = jnp.exp(jnp.sum(lam_ref[0:1, :] * lam_ref[1:2, :], -1, keepdims=True))
    e1 = jnp.exp(jnp.sum(lam_ref[2:3, :] * lam_ref[3:4, :], -1, keepdims=True))
    return e0 - e1 + lam_init


def _diff_prompt_kernel(q_ref, k_ref, v_ref, bias_ref, lam_ref, g_ref, *rest, lam_init, n_cast):
    cast_src, o_ref, cast_dst = rest[:n_cast], rest[n_cast], rest[n_cast + 1:2 * n_cast + 1]
    (k16_ref, vt_ref, sa_ref, sb_ref, m0_ref, m1_ref, l0_ref, l1_ref, acc0_ref,
     acc1_ref) = rest[2 * n_cast + 1:]
    _side_cast(cast_src, cast_dst)
    i = pl.program_id(2)
    t = q_ref.shape[0]
    nt = k_ref.shape[0] // t
    m_refs, l_refs, acc_refs = (m0_ref, m1_ref), (l0_ref, l1_ref), (acc0_ref, acc1_ref)

    @pl.when(i == 0)
    def _():
        k16_ref[...] = k_ref[...].astype(BF16)
        for j in range(nt):
            vt_ref[j] = jnp.transpose(v_ref[j * t:(j + 1) * t, :]).astype(BF16)

    qt = jnp.transpose(q_ref[...].astype(F32) * (DIFF_D ** -0.5 * LOG2E))
    feat = lax.broadcasted_iota(jnp.int32, qt.shape, 0)
    qts = (jnp.where(feat < DIFF_D, qt, 0.0).astype(BF16), jnp.where(feat >= DIFF_D, qt, 0.0).astype(BF16))

    def scores(j, kind, dst_ref):
        kt = k16_ref[j * t:(j + 1) * t, :]
        for c in range(2):
            s = _dot(kt, qts[c])
            dst_ref[c] = s if kind == FAR_TILE else s + bias_ref[kind]

    def consume(j, src_ref):
        vt = vt_ref[j]
        for c in range(2):
            for q0 in range(0, t, min(t, DIFF_QUERY_GROUP)):
                qc = slice(q0, q0 + min(t, DIFF_QUERY_GROUP))
                s = src_ref[c, :, qc]
                m_new = jnp.max(s, 0, keepdims=True)
                if j > 0:
                    m_prev = m_refs[c][:, qc]
                    m_new = jnp.maximum(m_prev, m_new)
                    a = jnp.exp2(m_prev - m_new)
                p = jnp.exp2(s - m_new)
                l_new = jnp.sum(p, 0, keepdims=True)
                acc_new = _dot(vt, p.astype(BF16))
                if j > 0:
                    l_new = a * l_refs[c][:, qc] + l_new
                    acc_new = a * acc_refs[c][:, qc] + acc_new
                l_refs[c][:, qc] = l_new
                acc_refs[c][:, qc] = acc_new
                m_refs[c][:, qc] = m_new

    bufs = (sa_ref, sb_ref)
    for qi in range(nt):
        @pl.when(i == qi)
        def _(qi=qi):
            kind = lambda j: min(qi - j, FAR_TILE)
            scores(0, kind(0), bufs[0])
            for j in range(qi + 1):
                if j < qi:
                    scores(j + 1, kind(j + 1), bufs[(j + 1) % 2])
                consume(j, bufs[j % 2])

    lam = _diff_lambda(lam_ref, lam_init)
    o = acc0_ref[...] / l0_ref[...] - lam * (acc1_ref[...] / l1_ref[...])
    o = o * lax.rsqrt(jnp.mean(o * o, 0, keepdims=True) + LN_EPS) * g_ref[...]
    o_ref[...] = jnp.transpose(o * (1.0 - lam_init)).astype(BF16)


def _diff_prompt(proj, pkv, bias_tiles, diff_lambda, subln_g, layer, *, bsz, length, t, lam_init, casts=()):
    m = proj.shape[0]
    nq = length // t
    assert length % t == 0 and t % CHUNK == 0
    grid = (bsz, DIFF_HEADS, nq)
    kv = lambda c: pl.BlockSpec((length, HEAD_W), lambda b, h, i: (b, c * HEADS_PER_BRANCH + h))
    stat = pltpu.VMEM((1, t), F32)
    acc = pltpu.VMEM((DIFF_DV, t), F32)
    cast_in, cast_out, cast_shapes = _side_cast_specs(casts, grid)
    out = pl.pallas_call(
        functools.partial(_diff_prompt_kernel, lam_init=lam_init, n_cast=len(casts)),
        grid=grid,
        in_specs=[
            pl.BlockSpec((t, HEAD_W), lambda b, h, i: (b * nq + i, COL_DQ * HEADS_PER_BRANCH + h)),
            kv(KV_K), kv(KV_V),
            pl.BlockSpec((None, FAR_TILE, t, t), lambda b, h, i: (h, 0, 0, 0)),
            pl.BlockSpec((None, 4, DIFF_D), lambda b, h, i: (layer, 0, 0)),
            pl.BlockSpec((None, DIFF_DV, 1), lambda b, h, i: (layer, 0, 0)),
        ] + cast_in,
        out_specs=[pl.BlockSpec((t, HEAD_W), lambda b, h, i: (b * nq + i, h))] + cast_out,
        out_shape=[jax.ShapeDtypeStruct((m, BRANCH_W), BF16)] + cast_shapes,
        scratch_shapes=[pltpu.VMEM((length, HEAD_W), BF16), pltpu.VMEM((nq, DIFF_DV, t), BF16),
                        pltpu.VMEM((2, t, t), F32), pltpu.VMEM((2, t, t), F32),
                        stat, stat, stat, stat, acc, acc],
        compiler_params=_params(("parallel", "parallel", "arbitrary"), 40),
        name="diff_attention_prompt",
    )(proj, pkv, pkv, bias_tiles, diff_lambda, subln_g, *(c[0] for c in casts))
    return out[0], out[1:]


def _head_rows(ref, h, n):
    return ref[pl.ds(h, n, stride=HEADS_PER_BRANCH), :]


def _sample_attn_kernel(q_ref, kn_ref, vn_ref, mq_ref, kp_ref, vp_ref, mk_ref, mv_ref, bp_ref, bn_ref,
                        lam_ref, g_ref, yc_ref, yd_ref, *, lam_init):
    past = kp_ref.shape[0] // HEADS_PER_BRANCH
    tokens = mk_ref.shape[0] // HEADS_PER_BRANCH
    lam = _diff_lambda(lam_ref, lam_init)
    heads = range(HEADS_PER_BRANCH)
    cols = [slice(h * HEAD_W, (h + 1) * HEAD_W) for h in heads]
    kp = [_head_rows(kp_ref, h, past).astype(BF16) for h in heads]
    kn = [kn_ref[:, cols[h]].astype(BF16) for h in heads]
    mk = [_head_rows(mk_ref, h, tokens).astype(BF16) for h in heads]
    sp, sn = {}, {}
    for h in heads:
        q = q_ref[:, cols[h]].astype(F32) * (DIFF_D ** -0.5 * LOG2E)
        lane = lax.broadcasted_iota(jnp.int32, q.shape, 1)
        qs = (jnp.where(lane < DIFF_D, q, 0.0).astype(BF16), jnp.where(lane >= DIFF_D, q, 0.0).astype(BF16))
        for c, qc in enumerate(qs):
            sp[h, c] = _dot_nt(qc, kp[h]) + bp_ref[h]
            sn[h, c] = _dot_nt(qc, kn[h]) + bn_ref[h]
    sm = [_dot_nt(mq_ref[:, cols[h]].astype(BF16), mk[h]) * (MEM_HD ** -0.5) for h in heads]

    pp, pn, den = {}, {}, {}
    for key in sp:
        mx = jnp.maximum(jnp.max(sp[key], -1, keepdims=True), jnp.max(sn[key], -1, keepdims=True))
        pp[key] = jnp.exp2(sp[key] - mx)
        pn[key] = jnp.exp2(sn[key] - mx)
        den[key] = jnp.sum(pp[key], -1, keepdims=True) + jnp.sum(pn[key], -1, keepdims=True)
    pm = [jnp.exp(sm[h] - jnp.max(sm[h], -1, keepdims=True)) for h in heads]

    vp = [_head_rows(vp_ref, h, past).astype(BF16) for h in heads]
    vn = [vn_ref[:, cols[h]].astype(BF16) for h in heads]
    mv = [_head_rows(mv_ref, h, tokens).astype(BF16) for h in heads]
    y_c, y_d = [], []
    for h in heads:
        outs = [(_dot(pp[h, c].astype(BF16), vp[h]) + _dot(pn[h, c].astype(BF16), vn[h])) / den[h, c]
                for c in range(2)]
        o = outs[0] - lam * outs[1]
        o = o * lax.rsqrt(jnp.mean(o * o, -1, keepdims=True) + LN_EPS) * g_ref[...]
        y_c.append((o * (1.0 - lam_init)).astype(BF16))
        y_d.append((_dot(pm[h].astype(BF16), mv[h]) / jnp.sum(pm[h], -1, keepdims=True)).astype(BF16))
    yc_ref[...] = jnp.concatenate(y_c, axis=-1)
    yd_ref[...] = jnp.concatenate(y_d, axis=-1)


def _sample_attention(proj, pkv, k_past, v_past, mem_k, mem_v, bias_past, bias_new, diff_lambda, subln_g, layer,
                      *, bsz, length, lam_init):
    m = proj.shape[0]
    col = lambda c: pl.BlockSpec((length, BRANCH_W), lambda b: (b, c))
    cache = lambda a: pl.BlockSpec((None, None) + a.shape[2:], lambda b: (layer, b, 0, 0))
    whole = lambda a: pl.BlockSpec(a.shape, lambda b: (0,) * a.ndim)
    out = pl.BlockSpec((length, BRANCH_W), lambda b: (b, 0))
    return pl.pallas_call(
        functools.partial(_sample_attn_kernel, lam_init=lam_init),
        grid=(bsz,),
        in_specs=[
            col(COL_DQ), col(KV_K), col(KV_V), col(COL_MQ),
            cache(k_past), cache(v_past), cache(mem_k), cache(mem_v),
            whole(bias_past), whole(bias_new),
            pl.BlockSpec((None, 4, DIFF_D), lambda b: (layer, 0, 0)),
            pl.BlockSpec((None, 1, DIFF_DV), lambda b: (layer, 0, 0)),
        ],
        out_specs=[out, out],
        out_shape=[jax.ShapeDtypeStruct((m, BRANCH_W), BF16), jax.ShapeDtypeStruct((m, BRANCH_W), BF16)],
        compiler_params=_params(("parallel",), 40),
        name="sample_attention",
    )(proj, pkv, pkv, proj, k_past, v_past, mem_k, mem_v, bias_past, bias_new, diff_lambda, subln_g)


def _merge_kernel(h_ref, ya_ref, yb_ref, yc_ref, yd_ref, wg_ref, wb_ref, bg_ref, *rest, n_cast):
    cast_src, o_ref, cast_dst = rest[:n_cast], rest[n_cast], rest[n_cast + 1:]
    _side_cast(cast_src, cast_dst)
    h = h_ref[...]
    acc = None
    for i, y_ref in enumerate((ya_ref, yb_ref, yc_ref, yd_ref)):
        gate = jax.nn.sigmoid(_dot(h, wg_ref[i]) + bg_ref[i])
        term = gate * _dot(y_ref[...], wb_ref[i])
        acc = term if acc is None else acc + term
    o_ref[...] = acc.astype(BF16)


def _merge(h16, ys, w_gate, w_branch, b_gate, layer, *, tm, tn, casts=()):
    m, d = h16.shape
    assert m % tm == 0 and d % tn == 0
    grid = (m // tm, d // tn)
    y_spec = pl.BlockSpec((tm, BRANCH_W), lambda i, j: (i, 0))
    cast_in, cast_out, cast_shapes = _side_cast_specs(casts, grid)
    out = pl.pallas_call(
        functools.partial(_merge_kernel, n_cast=len(casts)),
        grid=grid,
        in_specs=[
            pl.BlockSpec((tm, d), lambda i, j: (i, 0)), y_spec, y_spec, y_spec, y_spec,
            pl.BlockSpec((None, N_BRANCH, d, tn), lambda i, j: (0, 0, 0, j)),
            pl.BlockSpec((None, N_BRANCH, BRANCH_W, tn), lambda i, j: (0, 0, 0, j)),
            pl.BlockSpec((None, N_BRANCH, 1, tn), lambda i, j: (layer, 0, 0, j)),
        ] + cast_in,
        out_specs=[pl.BlockSpec((tm, tn), lambda i, j: (i, j))] + cast_out,
        out_shape=[jax.ShapeDtypeStruct((m, d), BF16)] + cast_shapes,
        compiler_params=_params(("parallel", "parallel"), 60),
        name="gated_merge",
    )(h16, *ys, w_gate, w_branch, b_gate, *(c[0] for c in casts))
    return out[0], out[1:]


def _proj_ln_kernel(m_ref, x_ref, w_ref, g_ref, b_ref, o_ref, *, alpha):
    for r in range(0, o_ref.shape[0], PROJ_ROW_CHUNK):
        rows = slice(r, r + PROJ_ROW_CHUNK)
        y = alpha * x_ref[rows, :] + _dot(m_ref[rows, :], w_ref[...])
        o_ref[rows, :] = _layer_norm(y, g_ref[...], b_ref[...])


def _proj_ln(merged, x, w_o, g, b, layer, *, alpha, tm):
    m, d = x.shape
    assert m % tm == 0
    row = pl.BlockSpec((tm, d), lambda i: (i, 0))
    vec = pl.BlockSpec((None, 1, d), lambda i: (layer, 0, 0))
    return pl.pallas_call(
        functools.partial(_proj_ln_kernel, alpha=alpha),
        grid=(m // tm,),
        in_specs=[row, row, pl.BlockSpec((None, d, d), lambda i: (0, 0, 0)), vec, vec],
        out_specs=row,
        out_shape=jax.ShapeDtypeStruct((m, d), F32),
        compiler_params=_params(("parallel",), 48),
        name="proj_ln",
    )(merged, x, w_o, g, b)


def _rope_tables(pos):
    half = RET_DK // 2
    inv = ROPE_BASE ** (-jnp.arange(half, dtype=F32) / half)
    ang = pos.astype(F32)[:, None] * inv[None, :]
    cos = jnp.cos(ang)
    sin = jnp.sin(ang)
    return jnp.concatenate([cos, cos], -1), jnp.concatenate([-sin, sin], -1)


def _t5_bucket(rel):
    nb = REL_BUCKETS // 2
    max_exact = nb // 2
    n = jnp.abs(rel)
    nf = jnp.maximum(n, 1).astype(F32)
    large = max_exact + (jnp.log(nf / max_exact) / math.log(REL_MAX_DIST / max_exact)
                         * (nb - max_exact)).astype(jnp.int32)
    large = jnp.minimum(large, nb - 1)
    return jnp.where(rel > 0, nb, 0) + jnp.where(n < max_exact, n, large)


def _masked_bias(q_pos, k_pos, rel_bias):
    bucket = _t5_bucket(k_pos[None, :] - q_pos[:, None])
    onehot = bucket[None, None] == jnp.arange(REL_BUCKETS, dtype=bucket.dtype)[None, :, None, None]
    bias = jnp.sum(jnp.where(onehot, rel_bias.astype(F32).T[:, :, None, None], 0.0), axis=1)
    allowed = (k_pos[None, :] // CHUNK) <= (q_pos[:, None] // CHUNK)
    return jnp.where(allowed[None], bias * LOG2E, NEG_INF)


def _prompt_bias_tiles(rel_bias, t):
    base = BIAS_BASE
    assert base % CHUNK == 0 and base + 1 >= REL_MAX_DIST and t % base == 0
    n = t // base
    pos = jnp.arange(base, dtype=jnp.int32)
    diag, sub, far = (jnp.swapaxes(_masked_bias(pos + d * base, pos, rel_bias), -1, -2) for d in range(3))
    shift = far[:, :1, :1]
    masked = jnp.full_like(diag, NEG_INF)
    pick = lambda delta: masked if delta > 0 else (diag - shift, sub - shift, far - shift)[min(-delta, 2)]
    tile = lambda d: jnp.concatenate(
        [jnp.concatenate([pick(kk - qq - d * n) for qq in range(n)], axis=-1) for kk in range(n)], axis=-2)
    return jnp.stack([tile(d) for d in range(FAR_TILE)], axis=1)


def _kv_proj_kernel(x_ref, w_ref, *rest):
    o_ref, ok_ref, ov_ref = rest[-3:]
    y = _dot(x_ref[...], w_ref[...])
    o_ref[...] = y
    n = x_ref.shape[0]
    for h in range(HEADS_PER_BRANCH):
        rows = pl.ds(h, n, stride=HEADS_PER_BRANCH)
        ok_ref[rows, :] = y[:, KV_K * BRANCH_W + h * HEAD_W:KV_K * BRANCH_W + (h + 1) * HEAD_W]
        ov_ref[rows, :] = y[:, KV_V * BRANCH_W + h * HEAD_W:KV_V * BRANCH_W + (h + 1) * HEAD_W]


def _kv_proj(xb, w_kv, prev, layer, depth, *, tm):
    m, d = xb.shape
    n = w_kv.shape[2]
    assert m % tm == 0 and n == KV_BLOCKS * BRANCH_W
    out = pl.BlockSpec((None, tm * HEADS_PER_BRANCH, HEAD_W), lambda i: (layer, i, 0))
    shape = jax.ShapeDtypeStruct((depth, m * HEADS_PER_BRANCH, HEAD_W), F32)
    keep = [pl.BlockSpec(memory_space=pl.ANY)] * len(prev)
    res = pl.pallas_call(
        _kv_proj_kernel,
        grid=(m // tm,),
        in_specs=[pl.BlockSpec((tm, d), lambda i: (i, 0)), pl.BlockSpec((None, d, n), lambda i: (0, 0, 0))] + keep,
        out_specs=[pl.BlockSpec((tm, n), lambda i: (i, 0)), out, out],
        out_shape=[jax.ShapeDtypeStruct((m, n), F32), shape, shape],
        input_output_aliases={2 + k: 1 + k for k in range(len(prev))},
        compiler_params=_params(("parallel",), 40),
        name="kv_proj",
    )(xb, w_kv, *prev)
    return res[0], res[1:]


CAST_IN_FFN1 = (
    ('w_in_main', 'w_in', ('rc', lambda j: jnp.where(j < W_IN_DK, j, W_IN_MQ),
                           lambda j: jnp.minimum(j, MAIN_BLOCKS - 1), MAIN_BLOCKS)),
    ('w_in_kv', 'w_in', ('rc', lambda j: jnp.where(j == 0, W_IN_DK, W_IN_DV),
                         lambda j: jnp.minimum(j, KV_BLOCKS - 1), KV_BLOCKS)))
CAST_IN_ATTN = (('w_gate', 'w_gate', 'flat'), ('w_branch', 'w_branch', 'flat'), ('w_o', 'w_o', 'flat'))
CAST_IN_MERGE = (('ffn2_w_up', 'ffn2_w_up', 'rc'), ('ffn2_w_down', 'ffn2_w_down', 'rc'))
CAST_IN_FFN2 = (('ffn1_w_up', 'ffn1_w_up', 'rc'), ('ffn1_w_down', 'ffn1_w_down', 'cr'))


def _encoder_layer(x, w16, p, layer, *, alpha, cfg, mixers_fn, kv_prev, depth, w32=None, w16_next=None):
    d = x.shape[1]
    casts = lambda names, l: tuple((w32[src], l, split) for _, src, split in names) if w32 is not None else ()
    if w32 is not None:
        assert w16['ffn1_w_down'].shape[1] // cfg['tf'] == W_IN_BLOCKS == w32['w_in'].shape[2] // BRANCH_W
    x1, done = _ffn_ln(x, w16['ffn1_w_up'], w16['ffn1_w_down'], p['ln1_g'], p['ln1_b'], layer,
                       alpha=alpha, tm=cfg['tm_ffn'], tf=cfg['tf'], casts=casts(CAST_IN_FFN1, layer))
    w16.update(zip((n for n, _, _ in CAST_IN_FFN1), done))
    proj, x1b = _in_proj(x1, w16['w_in_main'], tm=cfg['tm_in'], tn=cfg['tn_in'])
    pkv, kv_new = _kv_proj(x1b, w16['w_in_kv'], kv_prev, layer, depth, tm=cfg['tm_kv'])
    y_a, y_b, y_c, y_d, conv_new, ret_new, done = mixers_fn(
        proj, pkv, 0.8 - 0.6 * math.exp(-0.3 * layer), casts(CAST_IN_ATTN, layer))
    w16.update(zip((n for n, _, _ in CAST_IN_ATTN), done))
    merged, done = _merge(x1b, (y_a, y_b, y_c, y_d), w16['w_gate'].reshape(1, N_BRANCH, d, d),
                          w16['w_branch'].reshape(1, N_BRANCH, BRANCH_W, d), p['b_gate'], layer,
                          tm=cfg['tm_merge'], tn=cfg['tn_merge'], casts=casts(CAST_IN_MERGE, layer))
    w16.update(zip((n for n, _, _ in CAST_IN_MERGE), done))
    x2 = _proj_ln(merged, x1, w16['w_o'], p['ln2_g'], p['ln2_b'], layer, alpha=alpha, tm=cfg['tm_proj'])
    x3, done = _ffn_ln(x2, w16['ffn2_w_up'], w16['ffn2_w_down'], p['ln3_g'], p['ln3_b'], layer,
                       alpha=alpha, tm=cfg['tm_ffn'], tf=cfg['tf'],
                       casts=casts(CAST_IN_FFN2, layer + 1) if w16_next is not None else ())
    if w16_next is not None:
        w16_next.update(zip((n for n, _, _ in CAST_IN_FFN2), done))
    return x3, kv_new, conv_new, ret_new


PROMPT_CFG = dict(tm_ffn=1024, tf=512, tm_in=1024, tn_in=1536, tm_kv=1024, mix_rows=256, ret_chunk=256, t_diff=512,
                  tm_merge=1024, tn_merge=512, tm_proj=512)
SAMPLE_CFG = dict(tm_ffn=256, tf=512, tm_in=256, tn_in=512, tm_kv=256, t_conv=16, ret_chunk=16,
                  tm_merge=256, tn_merge=256, tm_proj=256)


def kernel(x_prompt, x_sample, state_conv, state_ret, cache_diff_k, cache_diff_v, cache_mem_k, cache_mem_v,
           mem_prompt, ffn1_w_up, ffn1_w_down, ln1_g, ln1_b, w_in, conv_w, ret_gn_g, diff_lambda,
           diff_subln_g, w_mem_kv, w_branch, w_gate, b_gate, w_o, ln2_g, ln2_b, ffn2_w_up, ffn2_w_down,
           ln3_g, ln3_b, rel_bias):
    bp, lp, d = x_prompt.shape
    bs, ls, _ = x_sample.shape
    depth = w_in.shape[0]
    past = cache_diff_k.shape[2]
    mem_tokens = mem_prompt.shape[1]
    alpha = (2 * depth) ** 0.25
    half = MEM_HEADS * MEM_HD

    pos_p = jnp.arange(lp, dtype=jnp.int32)
    pos_s = past + jnp.arange(ls, dtype=jnp.int32)
    past_pos = jnp.arange(past, dtype=jnp.int32)
    rope_p = _rope_tables(pos_p)
    rope_s = _rope_tables(pos_s)
    bias_p = _prompt_bias_tiles(rel_bias, PROMPT_CFG['t_diff'])
    bias_s_past = _masked_bias(pos_s, past_pos, rel_bias)
    bias_s_new = _masked_bias(pos_s, pos_s, rel_bias)

    vec = lambda v: v.reshape(depth, 1, v.shape[-1])
    p = {'ln1_g': vec(ln1_g), 'ln1_b': vec(ln1_b), 'ln2_g': vec(ln2_g), 'ln2_b': vec(ln2_b),
         'ln3_g': vec(ln3_g), 'ln3_b': vec(ln3_b), 'b_gate': b_gate.reshape(depth, N_BRANCH, 1, d)}
    gn_g = vec(ret_gn_g)
    w32 = {'ffn1_w_up': ffn1_w_up, 'ffn1_w_down': ffn1_w_down, 'w_in': w_in,
           'w_gate': w_gate.reshape(depth, N_BRANCH * d, d), 'w_branch': w_branch.reshape(depth, N_BRANCH * BRANCH_W, d),
           'w_o': w_o, 'ffn2_w_up': ffn2_w_up, 'ffn2_w_down': ffn2_w_down}
    w16 = [dict() for _ in range(depth)]
    w16[0].update({n: w32[src][:1].astype(BF16) for n, src, _ in CAST_IN_FFN2})
    w_mem16 = w_mem_kv.astype(BF16)
    subln_col = diff_subln_g.reshape(depth, DIFF_DV, 1)
    subln_row = diff_subln_g.reshape(depth, 1, DIFF_DV)
    head_major = lambda c: c.reshape(depth, bs, c.shape[2] * c.shape[3], c.shape[4])
    kpast, vpast = head_major(cache_diff_k), head_major(cache_diff_v)
    mem_k_s, mem_v_s = head_major(cache_mem_k), head_major(cache_mem_v)

    yp = x_prompt.reshape(bp * lp, d)
    ys = x_sample.reshape(bs * ls, d)
    mem16 = mem_prompt.reshape(bp * mem_tokens, d).astype(BF16)
    zero_conv = jnp.zeros((bp, CONV_WIDTH - 1, BRANCH_W), F32)
    zero_ret = jnp.zeros((bp, RET_HEADS, RET_DK, RET_DK), F32)

    conv_p, ret_p, mk_p, mv_p, conv_s, ret_s = [], [], [], [], [], []
    kv_p, kv_s = [], []
    for l in range(depth):
        mkv = _matmul(mem16, w_mem16, l, tm=bp * mem_tokens, tn=512).reshape(bp, mem_tokens, 2 * half)

        def mixers_p(proj, pkv, lam_init, casts):
            y_a, y_b, y_d, c_new, r_new = _local_mixers(
                proj, conv_w, zero_conv, rope_p[0], rope_p[1], zero_ret, gn_g, mkv, l,
                bsz=bp, length=lp, rows=PROMPT_CFG['mix_rows'], chunk=PROMPT_CFG['ret_chunk'])
            y_c, done = _diff_prompt(proj, pkv, bias_p, diff_lambda, subln_col, l, bsz=bp, length=lp,
                                     t=PROMPT_CFG['t_diff'], lam_init=lam_init, casts=casts)
            return y_a, y_b, y_c, y_d, c_new, r_new, done

        yp, kv_p, c_new, r_new = _encoder_layer(
            yp, w16[l], p, l, alpha=alpha, cfg=PROMPT_CFG, mixers_fn=mixers_p, kv_prev=kv_p, depth=depth,
            w32=w32, w16_next=w16[l + 1] if l + 1 < depth else None)
        conv_p.append(c_new)
        ret_p.append(r_new)
        mk_p.append(mkv[:, :, :half].reshape(bp, mem_tokens, MEM_HEADS, MEM_HD))
        mv_p.append(mkv[:, :, half:].reshape(bp, mem_tokens, MEM_HEADS, MEM_HD))

        def mixers_s(proj, pkv, lam_init, casts):
            assert not casts
            y_a, c_new = _conv_branch(proj, conv_w, state_conv[l], l, bsz=bs, length=ls, tl=SAMPLE_CFG['t_conv'])
            y_b, r_new = _retention_branch(proj, rope_s[0], rope_s[1], state_ret[l], gn_g, l,
                                           bsz=bs, length=ls, chunk=SAMPLE_CFG['ret_chunk'])
            y_c, y_d = _sample_attention(proj, pkv, kpast, vpast, mem_k_s, mem_v_s, bias_s_past, bias_s_new,
                                         diff_lambda, subln_row, l, bsz=bs, length=ls, lam_init=lam_init)
            return y_a, y_b, y_c, y_d, c_new, r_new, ()

        ys, kv_s, c_new, r_new = _encoder_layer(
            ys, w16[l], p, l, alpha=alpha, cfg=SAMPLE_CFG, mixers_fn=mixers_s, kv_prev=kv_s, depth=depth)
        conv_s.append(c_new)
        ret_s.append(r_new)

    kv_shape = lambda b, n: (depth, b, n, DIFF_HEADS, DIFF_DV)
    return (yp.reshape(bp, lp, d), ys.reshape(bs, ls, d), jnp.stack(conv_p), jnp.stack(ret_p),
            kv_p[0].reshape(kv_shape(bp, lp)), kv_p[1].reshape(kv_shape(bp, lp)),
            jnp.stack(mk_p), jnp.stack(mv_p), jnp.stack(conv_s), jnp.stack(ret_s),
            kv_s[0].reshape(kv_shape(bs, ls)), kv_s[1].reshape(kv_shape(bs, ls)))
```

```python
import functools
import math

import jax
import jax.numpy as jnp
from jax import lax
from jax.experimental import pallas as pl
from jax.experimental.pallas import tpu as pltpu

F32 = jnp.float32
BF16 = jnp.bfloat16

CHUNK = 64
CONV_WIDTH = 3
RET_HEADS = 4
RET_DK = 128
DIFF_HEADS = 4
DIFF_D = 64
DIFF_DV = 128
MEM_HEADS = 4
MEM_HD = 128
BRANCH_W = 512
N_BRANCH = 4
REL_BUCKETS = 32
REL_MAX_DIST = 128
LN_EPS = 1e-5
ROPE_BASE = 10000.0
NEG_INF = -1e30
HEAD_W = 128
LOG2E = math.log2(math.e)
LN_ROW_CHUNK = 128
PROJ_ROW_CHUNK = 256
BIAS_BASE = 256
MERGE_ROW_CHUNK = 512
DIFF_QUERY_GROUP = 256
FAR_TILE = 2
FFN_UP_CHUNK = 256
FFN_DOWN_CHUNK = 512

W_IN_BLOCKS = 11
W_IN_DK, W_IN_DV, W_IN_MQ = 8, 9, 10
COL_CB, COL_CC, COL_CH, COL_RQ, COL_RK, COL_RV, COL_RG, COL_DQ, COL_MQ = range(9)
MAIN_BLOCKS = 9
KV_K, KV_V = range(2)
KV_BLOCKS = 2
HEADS_PER_BRANCH = BRANCH_W // HEAD_W

LANES = 128
F32_SUBLANES = 8
BF16_SUBLANES = 16
V7X_VMEM_BYTES = 64 * 1024 * 1024
MIB = 1024 * 1024


def _params(semantics, vmem_mib):
    assert vmem_mib * MIB < V7X_VMEM_BYTES
    return pltpu.CompilerParams(dimension_semantics=semantics, vmem_limit_bytes=vmem_mib * MIB)


def _layer_norm(y, g, b):
    mu = jnp.mean(y, -1, keepdims=True)
    d = y - mu
    var = jnp.mean(d * d, -1, keepdims=True)
    return d * lax.rsqrt(var + LN_EPS) * g + b


def _dot(a, b):
    return jnp.dot(a, b, preferred_element_type=F32)


def _dot_nt(a, b):
    return lax.dot_general(a, b, (((1,), (1,)), ((), ())), preferred_element_type=F32)


def _dot_tn(a, b):
    return lax.dot_general(a, b, (((0,), (0,)), ((), ())), preferred_element_type=F32)


def _side_cast_specs(casts, grid):
    in_specs, out_specs, out_shapes = [], [], []
    for src, layer, split in casts:
        _, r, c = src.shape
        if split == 'flat':
            gr, gc = math.prod(grid), 1

            def pick(*ids, grid=grid):
                flat = ids[0]
                for size, idx in zip(grid[1:], ids[1:]):
                    flat = flat * size + idx
                return (flat, 0)
        elif split == 'cr':
            gr, gc = grid[1], grid[0]
            pick = lambda i, j: (j, i)
        else:
            gr, gc = grid
            pick = lambda i, j: (i, j)
        assert r % gr == 0 and c % gc == 0
        blk = (None, r // gr, c // gc)
        assert blk[1] % BF16_SUBLANES == 0 and blk[2] % LANES == 0
        src_pick, dst_pick, c_out = pick, pick, c
        if isinstance(split, tuple):
            kind, src_col, dst_col, n_dst = split
            assert kind == 'rc'
            src_pick = lambda i, j, col=src_col: (i, col(j))
            dst_pick = lambda i, j, col=dst_col: (i, col(j))
            c_out = n_dst * blk[2]
        else:
            assert split in ('rc', 'cr', 'flat')
        in_specs.append(pl.BlockSpec(blk, lambda *ids, layer=layer, pick=src_pick: (layer,) + pick(*ids)))
        out_specs.append(pl.BlockSpec(blk, lambda *ids, pick=dst_pick: (0,) + pick(*ids)))
        out_shapes.append(jax.ShapeDtypeStruct((1, r, c_out), BF16))
    return in_specs, out_specs, out_shapes


def _side_cast(src_refs, dst_refs):
    for src_ref, dst_ref in zip(src_refs, dst_refs):
        dst_ref[...] = src_ref[...].astype(BF16)


def _ffn_ln_kernel(x_ref, wa_ref, wb_ref, wd_ref, g_ref, b_ref, *rest, alpha, n_cast):
    cast_src, o_ref, cast_dst = rest[:n_cast], rest[n_cast], rest[n_cast + 1:2 * n_cast + 1]
    xb_ref, h_ref = rest[2 * n_cast + 1:]
    j = pl.program_id(1)

    @pl.when(j == 0)
    def _():
        xb_ref[...] = x_ref[...].astype(BF16)
        o_ref[...] = jnp.zeros(o_ref.shape, F32)

    xb = xb_ref[...]
    for c in range(0, h_ref.shape[1], FFN_UP_CHUNK):
        cols = slice(c, c + FFN_UP_CHUNK)
        a = _dot(xb, wa_ref[:, cols])
        b = _dot(xb, wb_ref[:, cols])
        h_ref[:, cols] = (a * jax.nn.sigmoid(a) * b).astype(BF16)
    h = h_ref[...]
    for c in range(0, o_ref.shape[1], FFN_DOWN_CHUNK):
        cols = slice(c, c + FFN_DOWN_CHUNK)
        o_ref[:, cols] += _dot(h, wd_ref[:, cols])
    _side_cast(cast_src, cast_dst)

    @pl.when(j == pl.num_programs(1) - 1)
    def _():
        for r in range(0, o_ref.shape[0], LN_ROW_CHUNK):
            rows = slice(r, r + LN_ROW_CHUNK)
            o_ref[rows, :] = _layer_norm(alpha * x_ref[rows, :] + 0.5 * o_ref[rows, :], g_ref[...], b_ref[...])


def _ffn_ln(x, w_up, w_down, g, b, layer, *, alpha, tm, tf, casts=()):
    m, d = x.shape
    d_ff = w_down.shape[1]
    nj = d_ff // tf
    assert m % tm == 0 and d_ff % tf == 0
    grid = (m // tm, nj)
    row = lambda i, j: (i, 0)
    vec = pl.BlockSpec((None, 1, d), lambda i, j: (layer, 0, 0))
    cast_in, cast_out, cast_shapes = _side_cast_specs(casts, grid)
    out = pl.pallas_call(
        functools.partial(_ffn_ln_kernel, alpha=alpha, n_cast=len(casts)),
        grid=grid,
        in_specs=[
            pl.BlockSpec((tm, d), row),
            pl.BlockSpec((None, d, tf), lambda i, j: (0, 0, j)),
            pl.BlockSpec((None, d, tf), lambda i, j: (0, 0, j + nj)),
            pl.BlockSpec((None, tf, d), lambda i, j: (0, j, 0)),
            vec, vec,
        ] + cast_in,
        out_specs=[pl.BlockSpec((tm, d), row)] + cast_out,
        out_shape=[jax.ShapeDtypeStruct((m, d), F32)] + cast_shapes,
        scratch_shapes=[pltpu.VMEM((tm, d), BF16), pltpu.VMEM((tm, tf), BF16)],
        compiler_params=_params(("parallel", "arbitrary"), 60),
        name="ffn_ln",
    )(x, w_up, w_up, w_down, g, b, *(c[0] for c in casts))
    return out[0], out[1:]


def _in_proj_kernel(x_ref, w_ref, o_ref, xb_ref):
    @pl.when(pl.program_id(1) == 0)
    def _():
        xb_ref[...] = x_ref[...].astype(BF16)

    o_ref[...] = _dot(xb_ref[...], w_ref[...]).astype(o_ref.dtype)


def _in_proj(x, w, *, tm, tn):
    m, k = x.shape
    n = w.shape[2]
    assert m % tm == 0 and n % tn == 0
    return pl.pallas_call(
        _in_proj_kernel,
        grid=(m // tm, n // tn),
        in_specs=[pl.BlockSpec((tm, k), lambda i, j: (i, 0)),
                  pl.BlockSpec((None, k, tn), lambda i, j: (0, 0, j))],
        out_specs=[pl.BlockSpec((tm, tn), lambda i, j: (i, j)), pl.BlockSpec((tm, k), lambda i, j: (i, 0))],
        out_shape=[jax.ShapeDtypeStruct((m, n), BF16), jax.ShapeDtypeStruct((m, k), BF16)],
        compiler_params=_params(("parallel", "arbitrary"), 60),
        name="in_proj",
    )(x, w)


def _matmul_kernel(x_ref, w_ref, o_ref):
    o_ref[...] = _dot(x_ref[...], w_ref[...])


def _matmul(x, w, layer, *, tm, tn):
    m, k = x.shape
    n = w.shape[2]
    assert m % tm == 0 and n % tn == 0
    return pl.pallas_call(
        _matmul_kernel,
        grid=(m // tm, n // tn),
        in_specs=[pl.BlockSpec((tm, k), lambda i, j: (i, 0)),
                  pl.BlockSpec((None, k, tn), lambda i, j: (layer, 0, j))],
        out_specs=pl.BlockSpec((tm, tn), lambda i, j: (i, j)),
        out_shape=jax.ShapeDtypeStruct((m, n), F32),
        compiler_params=_params(("parallel", "parallel"), 48),
        name="matmul",
    )(x, w)


def _conv_body(cb_ref, cc_ref, ch_ref, w_ref, carry_ref, y_ref):
    u = cc_ref[...].astype(F32) * ch_ref[...].astype(F32)
    row = lax.broadcasted_iota(jnp.int32, u.shape, 0)
    c0 = carry_ref[0:1, :]
    c1 = carry_ref[1:2, :]
    u1 = jnp.where(row == 0, c1, pltpu.roll(u, 1, 0))
    u2 = jnp.where(row == 0, c0, jnp.where(row == 1, c1, pltpu.roll(u, 2, 0)))
    z = w_ref[0:1, :] * u2 + w_ref[1:2, :] * u1 + w_ref[2:3, :] * u
    y_ref[...] = (cb_ref[...].astype(F32) * z).astype(BF16)
    tail = u[u.shape[0] - F32_SUBLANES:, :][F32_SUBLANES - (CONV_WIDTH - 1):, :]
    carry_ref[...] = tail
    return tail


def _conv_kernel(cb_ref, cc_ref, ch_ref, w_ref, prev_ref, y_ref, state_ref, carry_ref):
    l = pl.program_id(1)

    @pl.when(l == 0)
    def _():
        carry_ref[...] = prev_ref[...]

    tail = _conv_body(cb_ref, cc_ref, ch_ref, w_ref, carry_ref, y_ref)

    @pl.when(l == pl.num_programs(1) - 1)
    def _():
        state_ref[...] = tail


def _conv_branch(proj, conv_w, prev, layer, *, bsz, length, tl):
    m = proj.shape[0]
    nl = length // tl
    assert length % tl == 0 and tl >= F32_SUBLANES
    col = lambda c: pl.BlockSpec((tl, BRANCH_W), lambda b, l: (b * nl + l, c))
    return pl.pallas_call(
        _conv_kernel,
        grid=(bsz, nl),
        in_specs=[
            col(COL_CB), col(COL_CC), col(COL_CH),
            pl.BlockSpec((None, CONV_WIDTH, BRANCH_W), lambda b, l: (layer, 0, 0)),
            pl.BlockSpec((None, CONV_WIDTH - 1, BRANCH_W), lambda b, l: (b, 0, 0)),
        ],
        out_specs=[
            pl.BlockSpec((tl, BRANCH_W), lambda b, l: (b * nl + l, 0)),
            pl.BlockSpec((None, CONV_WIDTH - 1, BRANCH_W), lambda b, l: (b, 0, 0)),
        ],
        out_shape=[
            jax.ShapeDtypeStruct((m, BRANCH_W), BF16),
            jax.ShapeDtypeStruct((bsz, CONV_WIDTH - 1, BRANCH_W), F32),
        ],
        scratch_shapes=[pltpu.VMEM((CONV_WIDTH - 1, BRANCH_W), F32)],
        compiler_params=_params(("parallel", "arbitrary"), 32),
        name="conv_branch",
    )(proj, proj, proj, conv_w, prev)


def _ret_log_gamma(h):
    return math.log1p(-(2.0 ** (-5.0 - h)))


def _ret_fill_decay(intra_ref):
    chunk = intra_ref.shape[1]
    ri = lax.broadcasted_iota(jnp.int32, (chunk, chunk), 0)
    ci = lax.broadcasted_iota(jnp.int32, (chunk, chunk), 1)
    rel = (ri - ci).astype(F32)
    for h in range(RET_HEADS):
        intra_ref[h] = jnp.where(rel >= 0.0, jnp.exp(jnp.maximum(rel, 0.0) * _ret_log_gamma(h)), 0.0)


def _retention_body(q_ref, k_ref, v_ref, g_ref, cos_ref, sin_ref, gn_ref, y_ref, s_ref, intra_ref, rows):
    chunk = intra_ref.shape[1]
    cosf = cos_ref[rows, :]
    sinf = sin_ref[rows, :]
    idx = lax.broadcasted_iota(jnp.int32, (chunk, 1), 0).astype(F32)
    for h in range(RET_HEADS):
        log_g = _ret_log_gamma(h)
        cols = slice(h * RET_DK, (h + 1) * RET_DK)
        q = q_ref[rows, cols].astype(F32)
        k = k_ref[rows, cols].astype(F32)
        q = q * cosf + pltpu.roll(q, RET_DK // 2, 1) * sinf
        k = (k * cosf + pltpu.roll(k, RET_DK // 2, 1) * sinf) * (RET_DK ** -0.5)
        vb = v_ref[rows, cols].astype(BF16)
        q_dec = jnp.exp((idx + 1.0) * log_g)
        k_dec = jnp.exp((chunk - 1.0 - idx) * log_g)
        c_dec = math.exp(chunk * log_g)
        qb = q.astype(BF16)
        s_prev = s_ref[h]
        att = _dot_nt(qb, k.astype(BF16)) * intra_ref[h]
        o = _dot(att.astype(BF16), vb) + _dot(qb, s_prev.astype(BF16)) * q_dec
        s_ref[h] = s_prev * c_dec + _dot_tn((k * k_dec).astype(BF16), vb)
        mu = jnp.mean(o, -1, keepdims=True)
        d = o - mu
        var = jnp.mean(d * d, -1, keepdims=True)
        ro = d * lax.rsqrt(var + LN_EPS) * gn_ref[:, cols]
        gate = g_ref[rows, cols].astype(F32)
        y_ref[rows, cols] = (gate * jax.nn.sigmoid(gate) * ro).astype(BF16)


def _retention_kernel(q_ref, k_ref, v_ref, g_ref, cos_ref, sin_ref, s0_ref, gn_ref, y_ref, sfin_ref,
                      s_ref, intra_ref):
    c = pl.program_id(1)

    @pl.when(jnp.logical_and(pl.program_id(0) == 0, c == 0))
    def _():
        _ret_fill_decay(intra_ref)

    @pl.when(c == 0)
    def _():
        s_ref[...] = s0_ref[...]

    _retention_body(q_ref, k_ref, v_ref, g_ref, cos_ref, sin_ref, gn_ref, y_ref, s_ref, intra_ref,
                    slice(0, q_ref.shape[0]))

    @pl.when(c == pl.num_programs(1) - 1)
    def _():
        sfin_ref[...] = s_ref[...]


def _retention_branch(proj, cosf, sinf, s0, gn_g, layer, *, bsz, length, chunk):
    m = proj.shape[0]
    nc = length // chunk
    assert length % chunk == 0
    col = lambda c: pl.BlockSpec((chunk, BRANCH_W), lambda b, i: (b * nc + i, c))
    state = pl.BlockSpec((None, RET_HEADS, RET_DK, RET_DK), lambda b, i: (b, 0, 0, 0))
    return pl.pallas_call(
        _retention_kernel,
        grid=(bsz, nc),
        in_specs=[
            col(COL_RQ), col(COL_RK), col(COL_RV), col(COL_RG),
            pl.BlockSpec((chunk, RET_DK), lambda b, i: (i, 0)),
            pl.BlockSpec((chunk, RET_DK), lambda b, i: (i, 0)),
            state,
            pl.BlockSpec((None, 1, BRANCH_W), lambda b, i: (layer, 0, 0)),
        ],
        out_specs=[pl.BlockSpec((chunk, BRANCH_W), lambda b, i: (b * nc + i, 0)), state],
        out_shape=[
            jax.ShapeDtypeStruct((m, BRANCH_W), BF16),
            jax.ShapeDtypeStruct((bsz, RET_HEADS, RET_DK, RET_DK), F32),
        ],
        scratch_shapes=[pltpu.VMEM((RET_HEADS, RET_DK, RET_DK), F32),
                        pltpu.VMEM((RET_HEADS, chunk, chunk), F32)],
        compiler_params=_params(("arbitrary", "arbitrary"), 32),
        name="retention_branch",
    )(proj, proj, proj, proj, cosf, sinf, s0, gn_g)


def _local_mixers_kernel(x_ref, mq_ref, w_ref, prev_ref, cos_ref, sin_ref, s0_ref, gn_ref, mk_ref, mv_ref,
                         ya_ref, yb_ref, yd_ref, conv_state_ref, sfin_ref,
                         carry_ref, s_ref, intra_ref):
    cb_ref, cc_ref, ch_ref, rq_ref, rk_ref, rv_ref, rg_ref = (
        x_ref.at[:, pl.ds(col * BRANCH_W, BRANCH_W)]
        for col in (COL_CB, COL_CC, COL_CH, COL_RQ, COL_RK, COL_RV, COL_RG))
    c = pl.program_id(1)

    @pl.when(jnp.logical_and(pl.program_id(0) == 0, c == 0))
    def _():
        _ret_fill_decay(intra_ref)

    @pl.when(c == 0)
    def _():
        carry_ref[...] = prev_ref[...]
        s_ref[...] = s0_ref[...]

    tail = _conv_body(cb_ref, cc_ref, ch_ref, w_ref, carry_ref, ya_ref)
    chunk = intra_ref.shape[1]
    for r in range(0, rq_ref.shape[0], chunk):
        _retention_body(rq_ref, rk_ref, rv_ref, rg_ref, cos_ref, sin_ref, gn_ref, yb_ref, s_ref, intra_ref,
                        slice(r, r + chunk))
    head_cols = [slice(h * MEM_HD, (h + 1) * MEM_HD) for h in range(MEM_HEADS)]
    scores = [_dot_nt(mq_ref[:, hc].astype(BF16), mk_ref[:, hc].astype(BF16)) * (MEM_HD ** -0.5)
              for hc in head_cols]
    probs = [jnp.exp(s - jnp.max(s, -1, keepdims=True)) for s in scores]
    for hc, p in zip(head_cols, probs):
        den = jnp.sum(p, -1, keepdims=True)
        yd_ref[:, hc] = (_dot(p.astype(BF16), mv_ref[:, hc].astype(BF16)) / den).astype(BF16)

    @pl.when(c == pl.num_programs(1) - 1)
    def _():
        conv_state_ref[...] = tail
        sfin_ref[...] = s_ref[...]


def _local_mixers(proj, conv_w, conv_prev, cosf, sinf, s0, gn_g, mkv, layer, *, bsz, length, rows, chunk):
    m = proj.shape[0]
    nc = length // rows
    assert length % rows == 0 and rows % chunk == 0 and chunk >= F32_SUBLANES
    tokens = mkv.shape[1]
    col = lambda c: pl.BlockSpec((rows, BRANCH_W), lambda b, i: (b * nc + i, c))
    out = pl.BlockSpec((rows, BRANCH_W), lambda b, i: (b * nc + i, 0))
    state = pl.BlockSpec((None, RET_HEADS, RET_DK, RET_DK), lambda b, i: (b, 0, 0, 0))
    conv_state = pl.BlockSpec((None, CONV_WIDTH - 1, BRANCH_W), lambda b, i: (b, 0, 0))
    rope = pl.BlockSpec((rows, RET_DK), lambda b, i: (i, 0))
    y_shape = jax.ShapeDtypeStruct((m, BRANCH_W), BF16)
    return pl.pallas_call(
        _local_mixers_kernel,
        grid=(bsz, nc),
        in_specs=[
            pl.BlockSpec((rows, (COL_RG + 1) * BRANCH_W), lambda b, i: (b * nc + i, 0)), col(COL_MQ),
            pl.BlockSpec((None, CONV_WIDTH, BRANCH_W), lambda b, i: (layer, 0, 0)),
            conv_state, rope, rope, state,
            pl.BlockSpec((None, 1, BRANCH_W), lambda b, i: (layer, 0, 0)),
            pl.BlockSpec((None, tokens, BRANCH_W), lambda b, i: (b, 0, 0)),
            pl.BlockSpec((None, tokens, BRANCH_W), lambda b, i: (b, 0, 1)),
        ],
        out_specs=[out, out, out, conv_state, state],
        out_shape=[
            y_shape, y_shape, y_shape,
            jax.ShapeDtypeStruct((bsz, CONV_WIDTH - 1, BRANCH_W), F32),
            jax.ShapeDtypeStruct((bsz, RET_HEADS, RET_DK, RET_DK), F32),
        ],
        scratch_shapes=[pltpu.VMEM((CONV_WIDTH - 1, BRANCH_W), F32),
                        pltpu.VMEM((RET_HEADS, RET_DK, RET_DK), F32),
                        pltpu.VMEM((RET_HEADS, chunk, chunk), F32)],
        compiler_params=_params(("arbitrary", "arbitrary"), 40),
        name="local_mixers",
    )(proj, proj, conv_w, conv_prev, cosf, sinf, s0, gn_g, mkv, mkv)


def _diff_lambda(lam_ref, lam_init):
    e0 = jnp.exp(jnp.sum(lam_ref[0:1, :] * lam_ref[1:2, :], -1, keepdims=True))
    e1 = jnp.exp(jnp.sum(lam_ref[2:3, :] * lam_ref[3:4, :], -1, keepdims=True))
    return e0 - e1 + lam_init


def _diff_prompt_kernel(q_ref, k_ref, v_ref, bias_ref, lam_ref, g_ref, *rest, lam_init, n_cast):
    cast_src, o_ref, cast_dst = rest[:n_cast], rest[n_cast], rest[n_cast + 1:2 * n_cast + 1]
    (k16_ref, vt_ref, sa_ref, sb_ref, m0_ref, m1_ref, l0_ref, l1_ref, acc0_ref,
     acc1_ref) = rest[2 * n_cast + 1:]
    _side_cast(cast_src, cast_dst)
    i = pl.program_id(2)
    t = q_ref.shape[0]
    nt = k_ref.shape[0] // t
    m_refs, l_refs, acc_refs = (m0_ref, m1_ref), (l0_ref, l1_ref), (acc0_ref, acc1_ref)

    @pl.when(i == 0)
    def _():
        k16_ref[...] = k_ref[...].astype(BF16)
        for j in range(nt):
            vt_ref[j] = jnp.transpose(v_ref[j * t:(j + 1) * t, :]).astype(BF16)

    qt = jnp.transpose(q_ref[...].astype(F32) * (DIFF_D ** -0.5 * LOG2E))
    feat = lax.broadcasted_iota(jnp.int32, qt.shape, 0)
    qts = (jnp.where(feat < DIFF_D, qt, 0.0).astype(BF16), jnp.where(feat >= DIFF_D, qt, 0.0).astype(BF16))

    def scores(j, kind, dst_ref):
        kt = k16_ref[j * t:(j + 1) * t, :]
        for c in range(2):
            s = _dot(kt, qts[c])
            dst_ref[c] = s if kind == FAR_TILE else s + bias_ref[kind]

    def consume(j, src_ref):
        vt = vt_ref[j]
        for c in range(2):
            for q0 in range(0, t, min(t, DIFF_QUERY_GROUP)):
                qc = slice(q0, q0 + min(t, DIFF_QUERY_GROUP))
                s = src_ref[c, :, qc]
                m_new = jnp.max(s, 0, keepdims=True)
                if j > 0:
                    m_prev = m_refs[c][:, qc]
                    m_new = jnp.maximum(m_prev, m_new)
                    a = jnp.exp2(m_prev - m_new)
                p = jnp.exp2(s - m_new)
                l_new = jnp.sum(p, 0, keepdims=True)
                acc_new = _dot(vt, p.astype(BF16))
                if j > 0:
                    l_new = a * l_refs[c][:, qc] + l_new
                    acc_new = a * acc_refs[c][:, qc] + acc_new
                l_refs[c][:, qc] = l_new
                acc_refs[c][:, qc] = acc_new
                m_refs[c][:, qc] = m_new

    bufs = (sa_ref, sb_ref)
    for qi in range(nt):
        @pl.when(i == qi)
        def _(qi=qi):
            kind = lambda j: min(qi - j, FAR_TILE)
            scores(0, kind(0), bufs[0])
            for j in range(qi + 1):
                if j < qi:
                    scores(j + 1, kind(j + 1), bufs[(j + 1) % 2])
                consume(j, bufs[j % 2])

    lam = _diff_lambda(lam_ref, lam_init)
    o = acc0_ref[...] / l0_ref[...] - lam * (acc1_ref[...] / l1_ref[...])
    o = o * lax.rsqrt(jnp.mean(o * o, 0, keepdims=True) + LN_EPS) * g_ref[...]
    o_ref[...] = jnp.transpose(o * (1.0 - lam_init)).astype(BF16)


def _diff_prompt(proj, pkv, bias_tiles, diff_lambda, subln_g, layer, *, bsz, length, t, lam_init, casts=()):
    m = proj.shape[0]
    nq = length // t
    assert length % t == 0 and t % CHUNK == 0
    grid = (bsz, DIFF_HEADS, nq)
    kv = lambda c: pl.BlockSpec((length, HEAD_W), lambda b, h, i: (b, c * HEADS_PER_BRANCH + h))
    stat = pltpu.VMEM((1, t), F32)
    acc = pltpu.VMEM((DIFF_DV, t), F32)
    cast_in, cast_out, cast_shapes = _side_cast_specs(casts, grid)
    out = pl.pallas_call(
        functools.partial(_diff_prompt_kernel, lam_init=lam_init, n_cast=len(casts)),
        grid=grid,
        in_specs=[
            pl.BlockSpec((t, HEAD_W), lambda b, h, i: (b * nq + i, COL_DQ * HEADS_PER_BRANCH + h)),
            kv(KV_K), kv(KV_V),
            pl.BlockSpec((None, FAR_TILE, t, t), lambda b, h, i: (h, 0, 0, 0)),
            pl.BlockSpec((None, 4, DIFF_D), lambda b, h, i: (layer, 0, 0)),
            pl.BlockSpec((None, DIFF_DV, 1), lambda b, h, i: (layer, 0, 0)),
        ] + cast_in,
        out_specs=[pl.BlockSpec((t, HEAD_W), lambda b, h, i: (b * nq + i, h))] + cast_out,
        out_shape=[jax.ShapeDtypeStruct((m, BRANCH_W), BF16)] + cast_shapes,
        scratch_shapes=[pltpu.VMEM((length, HEAD_W), BF16), pltpu.VMEM((nq, DIFF_DV, t), BF16),
                        pltpu.VMEM((2, t, t), F32), pltpu.VMEM((2, t, t), F32),
                        stat, stat, stat, stat, acc, acc],
        compiler_params=_params(("parallel", "parallel", "arbitrary"), 40),
        name="diff_attention_prompt",
    )(proj, pkv, pkv, bias_tiles, diff_lambda, subln_g, *(c[0] for c in casts))
    return out[0], out[1:]


def _head_rows(ref, h, n):
    return ref[pl.ds(h, n, stride=HEADS_PER_BRANCH), :]


def _sample_attn_kernel(q_ref, kn_ref, vn_ref, mq_ref, kp_ref, vp_ref, mk_ref, mv_ref, bp_ref, bn_ref,
                        lam_ref, g_ref, yc_ref, yd_ref, *, lam_init):
    past = kp_ref.shape[0] // HEADS_PER_BRANCH
    tokens = mk_ref.shape[0] // HEADS_PER_BRANCH
    lam = _diff_lambda(lam_ref, lam_init)
    heads = range(HEADS_PER_BRANCH)
    cols = [slice(h * HEAD_W, (h + 1) * HEAD_W) for h in heads]
    kp = [_head_rows(kp_ref, h, past).astype(BF16) for h in heads]
    kn = [kn_ref[:, cols[h]].astype(BF16) for h in heads]
    mk = [_head_rows(mk_ref, h, tokens).astype(BF16) for h in heads]
    sp, sn = {}, {}
    for h in heads:
        q = q_ref[:, cols[h]].astype(F32) * (DIFF_D ** -0.5 * LOG2E)
        lane = lax.broadcasted_iota(jnp.int32, q.shape, 1)
        qs = (jnp.where(lane < DIFF_D, q, 0.0).astype(BF16), jnp.where(lane >= DIFF_D, q, 0.0).astype(BF16))
        for c, qc in enumerate(qs):
            sp[h, c] = _dot_nt(qc, kp[h]) + bp_ref[h]
            sn[h, c] = _dot_nt(qc, kn[h]) + bn_ref[h]
    sm = [_dot_nt(mq_ref[:, cols[h]].astype(BF16), mk[h]) * (MEM_HD ** -0.5) for h in heads]

    pp, pn, den = {}, {}, {}
    for key in sp:
        mx = jnp.maximum(jnp.max(sp[key], -1, keepdims=True), jnp.max(sn[key], -1, keepdims=True))
        pp[key] = jnp.exp2(sp[key] - mx)
        pn[key] = jnp.exp2(sn[key] - mx)
        den[key] = jnp.sum(pp[key], -1, keepdims=True) + jnp.sum(pn[key], -1, keepdims=True)
    pm = [jnp.exp(sm[h] - jnp.max(sm[h], -1, keepdims=True)) for h in heads]

    vp = [_head_rows(vp_ref, h, past).astype(BF16) for h in heads]
    vn = [vn_ref[:, cols[h]].astype(BF16) for h in heads]
    mv = [_head_rows(mv_ref, h, tokens).astype(BF16) for h in heads]
    y_c, y_d = [], []
    for h in heads:
        outs = [(_dot(pp[h, c].astype(BF16), vp[h]) + _dot(pn[h, c].astype(BF16), vn[h])) / den[h, c]
                for c in range(2)]
        o = outs[0] - lam * outs[1]
        o = o * lax.rsqrt(jnp.mean(o * o, -1, keepdims=True) + LN_EPS) * g_ref[...]
        y_c.append((o * (1.0 - lam_init)).astype(BF16))
        y_d.append((_dot(pm[h].astype(BF16), mv[h]) / jnp.sum(pm[h], -1, keepdims=True)).astype(BF16))
    yc_ref[...] = jnp.concatenate(y_c, axis=-1)
    yd_ref[...] = jnp.concatenate(y_d, axis=-1)


def _sample_attention(proj, pkv, k_past, v_past, mem_k, mem_v, bias_past, bias_new, diff_lambda, subln_g, layer,
                      *, bsz, length, lam_init):
    m = proj.shape[0]
    col = lambda c: pl.BlockSpec((length, BRANCH_W), lambda b: (b, c))
    cache = lambda a: pl.BlockSpec((None, None) + a.shape[2:], lambda b: (layer, b, 0, 0))
    whole = lambda a: pl.BlockSpec(a.shape, lambda b: (0,) * a.ndim)
    out = pl.BlockSpec((length, BRANCH_W), lambda b: (b, 0))
    return pl.pallas_call(
        functools.partial(_sample_attn_kernel, lam_init=lam_init),
        grid=(bsz,),
        in_specs=[
            col(COL_DQ), col(KV_K), col(KV_V), col(COL_MQ),
            cache(k_past), cache(v_past), cache(mem_k), cache(mem_v),
            whole(bias_past), whole(bias_new),
            pl.BlockSpec((None, 4, DIFF_D), lambda b: (layer, 0, 0)),
            pl.BlockSpec((None, 1, DIFF_DV), lambda b: (layer, 0, 0)),
        ],
        out_specs=[out, out],
        out_shape=[jax.ShapeDtypeStruct((m, BRANCH_W), BF16), jax.ShapeDtypeStruct((m, BRANCH_W), BF16)],
        compiler_params=_params(("parallel",), 40),
        name="sample_attention",
    )(proj, pkv, pkv, proj, k_past, v_past, mem_k, mem_v, bias_past, bias_new, diff_lambda, subln_g)


def _merge_kernel(h_ref, ya_ref, yb_ref, yc_ref, yd_ref, wg_ref, wb_ref, bg_ref, *rest, n_cast):
    cast_src, o_ref, cast_dst = rest[:n_cast], rest[n_cast], rest[n_cast + 1:]
    _side_cast(cast_src, cast_dst)
    for r in range(0, o_ref.shape[0], MERGE_ROW_CHUNK):
        rows = slice(r, r + MERGE_ROW_CHUNK)
        h = h_ref[rows, :]
        acc = None
        for i, y_ref in enumerate((ya_ref, yb_ref, yc_ref, yd_ref)):
            gate = jax.nn.sigmoid(_dot(h, wg_ref[i]) + bg_ref[i])
            term = gate * _dot(y_ref[rows, :], wb_ref[i])
            acc = term if acc is None else acc + term
        o_ref[rows, :] = acc.astype(BF16)


def _merge(h16, ys, w_gate, w_branch, b_gate, layer, *, tm, tn, casts=()):
    m, d = h16.shape
    assert m % tm == 0 and d % tn == 0
    grid = (m // tm, d // tn)
    y_spec = pl.BlockSpec((tm, BRANCH_W), lambda i, j: (i, 0))
    cast_in, cast_out, cast_shapes = _side_cast_specs(casts, grid)
    out = pl.pallas_call(
        functools.partial(_merge_kernel, n_cast=len(casts)),
        grid=grid,
        in_specs=[
            pl.BlockSpec((tm, d), lambda i, j: (i, 0)), y_spec, y_spec, y_spec, y_spec,
            pl.BlockSpec((None, N_BRANCH, d, tn), lambda i, j: (0, 0, 0, j)),
            pl.BlockSpec((None, N_BRANCH, BRANCH_W, tn), lambda i, j: (0, 0, 0, j)),
            pl.BlockSpec((None, N_BRANCH, 1, tn), lambda i, j: (layer, 0, 0, j)),
        ] + cast_in,
        out_specs=[pl.BlockSpec((tm, tn), lambda i, j: (i, j))] + cast_out,
        out_shape=[jax.ShapeDtypeStruct((m, d), BF16)] + cast_shapes,
        compiler_params=_params(("parallel", "parallel"), 60),
        name="gated_merge",
    )(h16, *ys, w_gate, w_branch, b_gate, *(c[0] for c in casts))
    return out[0], out[1:]


def _proj_ln_kernel(m_ref, x_ref, w_ref, g_ref, b_ref, o_ref, *, alpha):
    for r in range(0, o_ref.shape[0], PROJ_ROW_CHUNK):
        rows = slice(r, r + PROJ_ROW_CHUNK)
        y = alpha * x_ref[rows, :] + _dot(m_ref[rows, :], w_ref[...])
        o_ref[rows, :] = _layer_norm(y, g_ref[...], b_ref[...])


def _proj_ln(merged, x, w_o, g, b, layer, *, alpha, tm):
    m, d = x.shape
    assert m % tm == 0
    row = pl.BlockSpec((tm, d), lambda i: (i, 0))
    vec = pl.BlockSpec((None, 1, d), lambda i: (layer, 0, 0))
    return pl.pallas_call(
        functools.partial(_proj_ln_kernel, alpha=alpha),
        grid=(m // tm,),
        in_specs=[row, row, pl.BlockSpec((None, d, d), lambda i: (0, 0, 0)), vec, vec],
        out_specs=row,
        out_shape=jax.ShapeDtypeStruct((m, d), F32),
        compiler_params=_params(("parallel",), 48),
        name="proj_ln",
    )(merged, x, w_o, g, b)


def _rope_tables(pos):
    half = RET_DK // 2
    inv = ROPE_BASE ** (-jnp.arange(half, dtype=F32) / half)
    ang = pos.astype(F32)[:, None] * inv[None, :]
    cos = jnp.cos(ang)
    sin = jnp.sin(ang)
    return jnp.concatenate([cos, cos], -1), jnp.concatenate([-sin, sin], -1)


def _t5_bucket(rel):
    nb = REL_BUCKETS // 2
    max_exact = nb // 2
    n = jnp.abs(rel)
    nf = jnp.maximum(n, 1).astype(F32)
    large = max_exact + (jnp.log(nf / max_exact) / math.log(REL_MAX_DIST / max_exact)
                         * (nb - max_exact)).astype(jnp.int32)
    large = jnp.minimum(large, nb - 1)
    return jnp.where(rel > 0, nb, 0) + jnp.where(n < max_exact, n, large)


def _masked_bias(q_pos, k_pos, rel_bias):
    bucket = _t5_bucket(k_pos[None, :] - q_pos[:, None])
    onehot = bucket[None, None] == jnp.arange(REL_BUCKETS, dtype=bucket.dtype)[None, :, None, None]
    bias = jnp.sum(jnp.where(onehot, rel_bias.astype(F32).T[:, :, None, None], 0.0), axis=1)
    allowed = (k_pos[None, :] // CHUNK) <= (q_pos[:, None] // CHUNK)
    return jnp.where(allowed[None], bias * LOG2E, NEG_INF)


def _prompt_bias_tiles(rel_bias, t):
    base = BIAS_BASE
    assert base % CHUNK == 0 and base + 1 >= REL_MAX_DIST and t % base == 0
    n = t // base
    pos = jnp.arange(base, dtype=jnp.int32)
    diag, sub, far = (jnp.swapaxes(_masked_bias(pos + d * base, pos, rel_bias), -1, -2) for d in range(3))
    shift = far[:, :1, :1]
    masked = jnp.full_like(diag, NEG_INF)
    pick = lambda delta: masked if delta > 0 else (diag - shift, sub - shift, far - shift)[min(-delta, 2)]
    tile = lambda d: jnp.concatenate(
        [jnp.concatenate([pick(kk - qq - d * n) for qq in range(n)], axis=-1) for kk in range(n)], axis=-2)
    return jnp.stack([tile(d) for d in range(FAR_TILE)], axis=1)


def _kv_proj_kernel(x_ref, w_ref, *rest):
    o_ref, ok_ref, ov_ref = rest[-3:]
    y = _dot(x_ref[...], w_ref[...])
    o_ref[...] = y
    n = x_ref.shape[0]
    for h in range(HEADS_PER_BRANCH):
        rows = pl.ds(h, n, stride=HEADS_PER_BRANCH)
        ok_ref[rows, :] = y[:, KV_K * BRANCH_W + h * HEAD_W:KV_K * BRANCH_W + (h + 1) * HEAD_W]
        ov_ref[rows, :] = y[:, KV_V * BRANCH_W + h * HEAD_W:KV_V * BRANCH_W + (h + 1) * HEAD_W]


def _kv_proj(xb, w_kv, prev, layer, depth, *, tm):
    m, d = xb.shape
    n = w_kv.shape[2]
    assert m % tm == 0 and n == KV_BLOCKS * BRANCH_W
    out = pl.BlockSpec((None, tm * HEADS_PER_BRANCH, HEAD_W), lambda i: (layer, i, 0))
    shape = jax.ShapeDtypeStruct((depth, m * HEADS_PER_BRANCH, HEAD_W), F32)
    keep = [pl.BlockSpec(memory_space=pl.ANY)] * len(prev)
    res = pl.pallas_call(
        _kv_proj_kernel,
        grid=(m // tm,),
        in_specs=[pl.BlockSpec((tm, d), lambda i: (i, 0)), pl.BlockSpec((None, d, n), lambda i: (0, 0, 0))] + keep,
        out_specs=[pl.BlockSpec((tm, n), lambda i: (i, 0)), out, out],
        out_shape=[jax.ShapeDtypeStruct((m, n), F32), shape, shape],
        input_output_aliases={2 + k: 1 + k for k in range(len(prev))},
        compiler_params=_params(("parallel",), 40),
        name="kv_proj",
    )(xb, w_kv, *prev)
    return res[0], res[1:]


CAST_IN_FFN1 = (
    ('w_in_main', 'w_in', ('rc', lambda j: jnp.where(j < W_IN_DK, j, W_IN_MQ),
                           lambda j: jnp.minimum(j, MAIN_BLOCKS - 1), MAIN_BLOCKS)),
    ('w_in_kv', 'w_in', ('rc', lambda j: jnp.where(j == 0, W_IN_DK, W_IN_DV),
                         lambda j: jnp.minimum(j, KV_BLOCKS - 1), KV_BLOCKS)))
CAST_IN_ATTN = (('w_gate', 'w_gate', 'flat'), ('w_branch', 'w_branch', 'flat'), ('w_o', 'w_o', 'flat'))
CAST_IN_MERGE = (('ffn2_w_up', 'ffn2_w_up', 'rc'), ('ffn2_w_down', 'ffn2_w_down', 'rc'))
CAST_IN_FFN2 = (('ffn1_w_up', 'ffn1_w_up', 'rc'), ('ffn1_w_down', 'ffn1_w_down', 'cr'))


def _encoder_layer(x, w16, p, layer, *, alpha, cfg, mixers_fn, kv_prev, depth, w32=None, w16_next=None):
    d = x.shape[1]
    casts = lambda names, l: tuple((w32[src], l, split) for _, src, split in names) if w32 is not None else ()
    if w32 is not None:
        assert w16['ffn1_w_down'].shape[1] // cfg['tf'] == W_IN_BLOCKS == w32['w_in'].shape[2] // BRANCH_W
    x1, done = _ffn_ln(x, w16['ffn1_w_up'], w16['ffn1_w_down'], p['ln1_g'], p['ln1_b'], layer,
                       alpha=alpha, tm=cfg['tm_ffn'], tf=cfg['tf'], casts=casts(CAST_IN_FFN1, layer))
    w16.update(zip((n for n, _, _ in CAST_IN_FFN1), done))
    proj, x1b = _in_proj(x1, w16['w_in_main'], tm=cfg['tm_in'], tn=cfg['tn_in'])
    pkv, kv_new = _kv_proj(x1b, w16['w_in_kv'], kv_prev, layer, depth, tm=cfg['tm_kv'])
    y_a, y_b, y_c, y_d, conv_new, ret_new, done = mixers_fn(
        proj, pkv, 0.8 - 0.6 * math.exp(-0.3 * layer), casts(CAST_IN_ATTN, layer))
    w16.update(zip((n for n, _, _ in CAST_IN_ATTN), done))
    merged, done = _merge(x1b, (y_a, y_b, y_c, y_d), w16['w_gate'].reshape(1, N_BRANCH, d, d),
                          w16['w_branch'].reshape(1, N_BRANCH, BRANCH_W, d), p['b_gate'], layer,
                          tm=cfg['tm_merge'], tn=cfg['tn_merge'], casts=casts(CAST_IN_MERGE, layer))
    w16.update(zip((n for n, _, _ in CAST_IN_MERGE), done))
    x2 = _proj_ln(merged, x1, w16['w_o'], p['ln2_g'], p['ln2_b'], layer, alpha=alpha, tm=cfg['tm_proj'])
    x3, done = _ffn_ln(x2, w16['ffn2_w_up'], w16['ffn2_w_down'], p['ln3_g'], p['ln3_b'], layer,
                       alpha=alpha, tm=cfg['tm_ffn'], tf=cfg['tf'],
                       casts=casts(CAST_IN_FFN2, layer + 1) if w16_next is not None else ())
    if w16_next is not None:
        w16_next.update(zip((n for n, _, _ in CAST_IN_FFN2), done))
    return x3, kv_new, conv_new, ret_new


PROMPT_CFG = dict(tm_ffn=1024, tf=512, tm_in=1024, tn_in=1536, tm_kv=1024, mix_rows=256, ret_chunk=256, t_diff=512,
                  tm_merge=1024, tn_merge=512, tm_proj=512)
SAMPLE_CFG = dict(tm_ffn=256, tf=512, tm_in=256, tn_in=512, tm_kv=256, t_conv=16, ret_chunk=16,
                  tm_merge=256, tn_merge=256, tm_proj=256)


def kernel(x_prompt, x_sample, state_conv, state_ret, cache_diff_k, cache_diff_v, cache_mem_k, cache_mem_v,
           mem_prompt, ffn1_w_up, ffn1_w_down, ln1_g, ln1_b, w_in, conv_w, ret_gn_g, diff_lambda,
           diff_subln_g, w_mem_kv, w_branch, w_gate, b_gate, w_o, ln2_g, ln2_b, ffn2_w_up, ffn2_w_down,
           ln3_g, ln3_b, rel_bias):
    bp, lp, d = x_prompt.shape
    bs, ls, _ = x_sample.shape
    depth = w_in.shape[0]
    past = cache_diff_k.shape[2]
    mem_tokens = mem_prompt.shape[1]
    alpha = (2 * depth) ** 0.25
    half = MEM_HEADS * MEM_HD

    pos_p = jnp.arange(lp, dtype=jnp.int32)
    pos_s = past + jnp.arange(ls, dtype=jnp.int32)
    past_pos = jnp.arange(past, dtype=jnp.int32)
    rope_p = _rope_tables(pos_p)
    rope_s = _rope_tables(pos_s)
    bias_p = _prompt_bias_tiles(rel_bias, PROMPT_CFG['t_diff'])
    bias_s_past = _masked_bias(pos_s, past_pos, rel_bias)
    bias_s_new = _masked_bias(pos_s, pos_s, rel_bias)

    vec = lambda v: v.reshape(depth, 1, v.shape[-1])
    p = {'ln1_g': vec(ln1_g), 'ln1_b': vec(ln1_b), 'ln2_g': vec(ln2_g), 'ln2_b': vec(ln2_b),
         'ln3_g': vec(ln3_g), 'ln3_b': vec(ln3_b), 'b_gate': b_gate.reshape(depth, N_BRANCH, 1, d)}
    gn_g = vec(ret_gn_g)
    w32 = {'ffn1_w_up': ffn1_w_up, 'ffn1_w_down': ffn1_w_down, 'w_in': w_in,
           'w_gate': w_gate.reshape(depth, N_BRANCH * d, d), 'w_branch': w_branch.reshape(depth, N_BRANCH * BRANCH_W, d),
           'w_o': w_o, 'ffn2_w_up': ffn2_w_up, 'ffn2_w_down': ffn2_w_down}
    w16 = [dict() for _ in range(depth)]
    w16[0].update({n: w32[src][:1].astype(BF16) for n, src, _ in CAST_IN_FFN2})
    w_mem16 = w_mem_kv.astype(BF16)
    subln_col = diff_subln_g.reshape(depth, DIFF_DV, 1)
    subln_row = diff_subln_g.reshape(depth, 1, DIFF_DV)
    head_major = lambda c: c.reshape(depth, bs, c.shape[2] * c.shape[3], c.shape[4])
    kpast, vpast = head_major(cache_diff_k), head_major(cache_diff_v)
    mem_k_s, mem_v_s = head_major(cache_mem_k), head_major(cache_mem_v)

    yp = x_prompt.reshape(bp * lp, d)
    ys = x_sample.reshape(bs * ls, d)
    mem16 = mem_prompt.reshape(bp * mem_tokens, d).astype(BF16)
    zero_conv = jnp.zeros((bp, CONV_WIDTH - 1, BRANCH_W), F32)
    zero_ret = jnp.zeros((bp, RET_HEADS, RET_DK, RET_DK), F32)

    conv_p, ret_p, mk_p, mv_p, conv_s, ret_s = [], [], [], [], [], []
    kv_p, kv_s = [], []
    for l in range(depth):
        mkv = _matmul(mem16, w_mem16, l, tm=bp * mem_tokens, tn=512).reshape(bp, mem_tokens, 2 * half)

        def mixers_p(proj, pkv, lam_init, casts):
            y_a, y_b, y_d, c_new, r_new = _local_mixers(
                proj, conv_w, zero_conv, rope_p[0], rope_p[1], zero_ret, gn_g, mkv, l,
                bsz=bp, length=lp, rows=PROMPT_CFG['mix_rows'], chunk=PROMPT_CFG['ret_chunk'])
            y_c, done = _diff_prompt(proj, pkv, bias_p, diff_lambda, subln_col, l, bsz=bp, length=lp,
                                     t=PROMPT_CFG['t_diff'], lam_init=lam_init, casts=casts)
            return y_a, y_b, y_c, y_d, c_new, r_new, done

        yp, kv_p, c_new, r_new = _encoder_layer(
            yp, w16[l], p, l, alpha=alpha, cfg=PROMPT_CFG, mixers_fn=mixers_p, kv_prev=kv_p, depth=depth,
            w32=w32, w16_next=w16[l + 1] if l + 1 < depth else None)
        conv_p.append(c_new)
        ret_p.append(r_new)
        mk_p.append(mkv[:, :, :half].reshape(bp, mem_tokens, MEM_HEADS, MEM_HD))
        mv_p.append(mkv[:, :, half:].reshape(bp, mem_tokens, MEM_HEADS, MEM_HD))

        def mixers_s(proj, pkv, lam_init, casts):
            assert not casts
            y_a, c_new = _conv_branch(proj, conv_w, state_conv[l], l, bsz=bs, length=ls, tl=SAMPLE_CFG['t_conv'])
            y_b, r_new = _retention_branch(proj, rope_s[0], rope_s[1], state_ret[l], gn_g, l,
                                           bsz=bs, length=ls, chunk=SAMPLE_CFG['ret_chunk'])
            y_c, y_d = _sample_attention(proj, pkv, kpast, vpast, mem_k_s, mem_v_s, bias_s_past, bias_s_new,
                                         diff_lambda, subln_row, l, bsz=bs, length=ls, lam_init=lam_init)
            return y_a, y_b, y_c, y_d, c_new, r_new, ()

        ys, kv_s, c_new, r_new = _encoder_layer(
            ys, w16[l], p, l, alpha=alpha, cfg=SAMPLE_CFG, mixers_fn=mixers_s, kv_prev=kv_s, depth=depth)
        conv_s.append(c_new)
        ret_s.append(r_new)

    kv_shape = lambda b, n: (depth, b, n, DIFF_HEADS, DIFF_DV)
    return (yp.reshape(bp, lp, d), ys.reshape(bs, ls, d), jnp.stack(conv_p), jnp.stack(ret_p),
            kv_p[0].reshape(kv_shape(bp, lp)), kv_p[1].reshape(kv_shape(bp, lp)),
            jnp.stack(mk_p), jnp.stack(mv_p), jnp.stack(conv_s), jnp.stack(ret_s),
            kv_s[0].reshape(kv_shape(bs, ls)), kv_s[1].reshape(kv_shape(bs, ls)))
```
